```python
import jax, jax.numpy as jnp
from jax import lax
import numpy as np

D_MODEL = 1024
BATCH = 16
SEQ = 4096
DEPTH = 1

CHUNK = 64
N_META = 16
Q_BLOCK = 128
EPS = 1e-6
GLA_HEADS = 4
GLA_DK = 64
GLA_DV = 128
GLA_GATE_RANK = 16
GLA_TAU = 16.0
GLA_QK = GLA_HEADS * GLA_DK
GLA_VW = GLA_HEADS * GLA_DV
MLA_HEADS = 4
MLA_Q_RANK = 256
MLA_KV_RANK = 128
MLA_NOPE = 128
MLA_ROPE = 64
MLA_V = 128
MLA_OUT = MLA_HEADS * MLA_V
ROPE_BASE = 10000.0
D_MIX = GLA_VW + MLA_OUT
IN_WIDTHS = (GLA_QK, GLA_QK, GLA_VW, GLA_VW, GLA_GATE_RANK, MLA_Q_RANK, MLA_KV_RANK, MLA_ROPE)
D_IN = sum(IN_WIDTHS)
N_GROUPS = 8
EXPERTS_PER_GROUP = 8
N_EXPERTS = N_GROUPS * EXPERTS_PER_GROUP
TOP_K = 2
D_EXPERT = 512
MOE_BLOCK = 128

kernel_name = 'hymba_gla_mla_hier_moe'


def rmsnorm(x, gain):
    x32 = x.astype(jnp.float32)
    y = x32 * lax.rsqrt(jnp.mean(x32 * x32, axis=-1, keepdims=True) + EPS)
    return (y * gain.astype(jnp.float32)).astype(x.dtype)


def split_points():
    pts, acc = [], 0
    for w in IN_WIDTHS[:-1]:
        acc += w
        pts.append(acc)
    return pts


def chunk_ids(length):
    p = jnp.arange(length)
    return jnp.where(p < N_META, 0, 1 + (p - N_META) // CHUNK)


def rope_tables(length):
    pos = jnp.arange(length, dtype=jnp.float32)
    inv = ROPE_BASE ** (-jnp.arange(0, MLA_ROPE, 2, dtype=jnp.float32) / MLA_ROPE)
    ang = pos[:, None] * inv[None, :]
    return jnp.cos(ang), jnp.sin(ang)


def apply_rope(x, cos, sin):
    half = x.shape[-1] // 2
    x1, x2 = x[..., :half].astype(jnp.float32), x[..., half:].astype(jnp.float32)
    return jnp.concatenate([x1 * cos - x2 * sin, x2 * cos + x1 * sin], axis=-1).astype(x.dtype)


def gla_group(q, k, v, r, a_low, w_a2, b_a, out_gain):
    f32 = jnp.float32
    B, L, _ = q.shape
    pad = (-L) % CHUNK
    log_a = jax.nn.log_sigmoid((a_low @ w_a2 + b_a).astype(f32)) / GLA_TAU

    def prep(t, d):
        t = jnp.pad(t.astype(f32), ((0, 0), (pad, 0), (0, 0)))
        n = t.shape[1] // CHUNK
        return t.reshape(B, n, CHUNK, GLA_HEADS, d).transpose(0, 3, 1, 2, 4)

    qc = prep(q, GLA_DK) * (GLA_DK ** -0.5)
    kc = prep(k, GLA_DK)
    vc = prep(v, GLA_DV)
    b = jnp.cumsum(prep(log_a, GLA_DK), axis=3)
    b_last = b[:, :, :, -1:, :]
    q_e = qc * jnp.exp(b)
    causal = jnp.tril(jnp.ones((CHUNK, CHUNK), dtype=bool))
    att = jnp.where(causal, jnp.einsum('bhncd,bhnsd->bhncs', q_e, kc * jnp.exp(-b)), 0.0)
    o = jnp.einsum('bhncs,bhnse->bhnce', att, vc)
    upd = jnp.einsum('bhncd,bhnce->nbhde', kc * jnp.exp(b_last - b), vc)
    decay = jnp.exp(b_last[:, :, :, 0, :]).transpose(2, 0, 1, 3)

    def step(s, inp):
        dec, u = inp
        return dec[..., None] * s + u, s

    s0 = jnp.zeros((B, GLA_HEADS, GLA_DK, GLA_DV), f32)
    _, s_prev = lax.scan(step, s0, (decay, upd))
    o = o + jnp.einsum('bhncd,nbhde->bhnce', q_e, s_prev)
    o = o.transpose(0, 2, 3, 1, 4).reshape(B, -1, GLA_HEADS, GLA_DV)[:, pad:]
    o = o * lax.rsqrt(jnp.mean(o * o, axis=-1, keepdims=True) + EPS)
    o = o.reshape(B, L, GLA_VW) * out_gain.astype(f32)
    return (o * jax.nn.silu(r.astype(f32))).astype(q.dtype)


def mla_group(q_lat, kv_lat, k_rope, q_norm, w_qb, kv_norm, w_kvb):
    B, L, _ = q_lat.shape
    q = (rmsnorm(q_lat, q_norm) @ w_qb).reshape(B, L, MLA_HEADS, MLA_NOPE + MLA_ROPE)
    q_nope, q_rope = q[..., :MLA_NOPE], q[..., MLA_NOPE:]
    kv = (rmsnorm(kv_lat, kv_norm) @ w_kvb).reshape(B, L, MLA_HEADS, MLA_NOPE + MLA_V)
    k_nope, v = kv[..., :MLA_NOPE], kv[..., MLA_NOPE:]
    cos, sin = rope_tables(L)
    q_rope = apply_rope(q_rope, cos[:, None, :], sin[:, None, :])
    k_rope = apply_rope(k_rope, cos, sin)
    key_chunk = chunk_ids(L)
    scale = (MLA_NOPE + MLA_ROPE) ** -0.5

    def attend(args):
        qn, qr, qcid = args
        s = (jnp.einsum('bqhd,bkhd->bhqk', qn, k_nope)
             + jnp.einsum('bqhd,bkd->bhqk', qr, k_rope)).astype(jnp.float32) * scale
        s = jnp.where(key_chunk[None, None, None, :] <= qcid[None, None, :, None], s, -jnp.inf)
        p = jax.nn.softmax(s, axis=-1).astype(v.dtype)
        return jnp.einsum('bhqk,bkhd->bqhd', p, v)

    meta_out = attend((q_nope[:, :N_META], q_rope[:, :N_META], key_chunk[:N_META]))
    nb = (L - N_META) // Q_BLOCK

    def blocks(t):
        return t[:, N_META:].reshape(B, nb, Q_BLOCK, *t.shape[2:]).swapaxes(0, 1)

    blk = lax.map(attend, (blocks(q_nope), blocks(q_rope), key_chunk[N_META:].reshape(nb, Q_BLOCK)))
    blk = blk.swapaxes(0, 1).reshape(B, L - N_META, MLA_HEADS, MLA_V)
    return jnp.concatenate([meta_out, blk], axis=1).reshape(B, L, MLA_OUT)


def hier_moe(u, wg, bg, we, be, w_gate, w_up, w_down):
    f32 = jnp.float32
    B, L, D = u.shape
    t = u.reshape(-1, D)
    n_tok = t.shape[0]
    g_prob = jax.nn.softmax((t @ wg + bg).astype(f32), axis=-1)
    g_p, g_idx = lax.top_k(g_prob, 1)
    e_logits = (t @ we + be).astype(f32).reshape(n_tok, N_GROUPS, EXPERTS_PER_GROUP)
    e_logits = jnp.take_along_axis(e_logits, g_idx[:, :, None], axis=1)[:, 0]
    e_p, e_idx = lax.top_k(jax.nn.softmax(e_logits, axis=-1), TOP_K)
    gates = g_p * e_p / jnp.sum(e_p, axis=-1, keepdims=True)
    expert = g_idx * EXPERTS_PER_GROUP + e_idx
    e_flat = expert.reshape(-1)
    tok_flat = jnp.repeat(jnp.arange(n_tok, dtype=jnp.int32), TOP_K)
    gate_flat = gates.reshape(-1)
    order = jnp.argsort(e_flat)
    e_sorted, tok_sorted, gate_sorted = e_flat[order], tok_flat[order], gate_flat[order]
    counts = jnp.bincount(e_flat, length=N_EXPERTS)
    padded = (counts + MOE_BLOCK - 1) // MOE_BLOCK * MOE_BLOCK
    pad_end = jnp.cumsum(padded)
    pad_start = pad_end - padded
    seg_start = jnp.cumsum(counts) - counts
    n_assign = n_tok * TOP_K
    dest = pad_start[e_sorted] + jnp.arange(n_assign) - seg_start[e_sorted]
    n_blocks = -(-(n_assign + N_EXPERTS * (MOE_BLOCK - 1)) // MOE_BLOCK)
    n_slots = n_blocks * MOE_BLOCK
    slot_tok = jnp.full((n_slots,), n_tok, jnp.int32).at[dest].set(tok_sorted)
    slot_gate = jnp.zeros((n_slots,), f32).at[dest].set(gate_sorted)
    block_expert = jnp.minimum(
        jnp.searchsorted(pad_end, jnp.arange(n_blocks) * MOE_BLOCK, side='right'), N_EXPERTS - 1)
    t_pad = jnp.concatenate([t, jnp.zeros((1, D), t.dtype)], axis=0)
    xs = t_pad[slot_tok].reshape(n_blocks, MOE_BLOCK, D)

    def expert_block(args):
        xb, e = args
        hdn = jax.nn.silu(xb @ w_gate[e]) * (xb @ w_up[e])
        return hdn @ w_down[e]

    ys = lax.map(expert_block, (xs, block_expert)).reshape(n_slots, D)
    out = jnp.zeros((n_tok + 1, D), f32).at[slot_tok].add(ys.astype(f32) * slot_gate[:, None])[:n_tok]
    return out.reshape(B, L, D).astype(u.dtype)


def setup_inputs(seed: int = 0) -> dict:
    key = jax.random.key(seed)
    ks = jax.random.split(key, 24)
    f32 = jnp.float32

    def nrm(k, shape, scale):
        return jax.random.normal(k, shape, f32) * scale

    def gain(k, shape):
        return 1.0 + 0.02 * jax.random.normal(k, shape, f32)

    return {
        'x': nrm(ks[0], (BATCH, SEQ, D_MODEL), 1.0),
        'meta_tokens': nrm(ks[1], (N_META, D_MODEL), 1.0),
        'mix_norm': gain(ks[2], (DEPTH, D_MODEL)),
        'w_in': nrm(ks[3], (DEPTH, D_MODEL, D_IN), D_MODEL ** -0.5),
        'gla_w_a2': nrm(ks[4], (DEPTH, GLA_GATE_RANK, GLA_QK), GLA_GATE_RANK ** -0.5),
        'gla_b_a': nrm(ks[5], (DEPTH, GLA_QK), 0.1),
        'gla_out_norm': gain(ks[6], (DEPTH, GLA_VW)),
        'mla_q_norm': gain(ks[7], (DEPTH, MLA_Q_RANK)),
        'mla_w_qb': nrm(ks[8], (DEPTH, MLA_Q_RANK, MLA_HEADS * (MLA_NOPE + MLA_ROPE)), MLA_Q_RANK ** -0.5),
        'mla_kv_norm': gain(ks[9], (DEPTH, MLA_KV_RANK)),
        'mla_w_kvb': nrm(ks[10], (DEPTH, MLA_KV_RANK, MLA_HEADS * (MLA_NOPE + MLA_V)), MLA_KV_RANK ** -0.5),
        'w_out': nrm(ks[11], (DEPTH, D_MIX, D_MODEL), D_MIX ** -0.5),
        'ffn_norm': gain(ks[12], (DEPTH, D_MODEL)),
        'router_group_w': nrm(ks[13], (DEPTH, D_MODEL, N_GROUPS), D_MODEL ** -0.5),
        'router_group_b': nrm(ks[14], (DEPTH, N_GROUPS), 0.01),
        'router_expert_w': nrm(ks[15], (DEPTH, D_MODEL, N_EXPERTS), D_MODEL ** -0.5),
        'router_expert_b': nrm(ks[16], (DEPTH, N_EXPERTS), 0.01),
        'expert_w_gate': nrm(ks[17], (DEPTH, N_EXPERTS, D_MODEL, D_EXPERT), D_MODEL ** -0.5),
        'expert_w_up': nrm(ks[18], (DEPTH, N_EXPERTS, D_MODEL, D_EXPERT), D_MODEL ** -0.5),
        'expert_w_down': nrm(ks[19], (DEPTH, N_EXPERTS, D_EXPERT, D_MODEL), D_EXPERT ** -0.5),
        'final_norm': gain(ks[20], (D_MODEL,)),
    }


def reference(x, meta_tokens, mix_norm, w_in, gla_w_a2, gla_b_a, gla_out_norm, mla_q_norm, mla_w_qb,
              mla_kv_norm, mla_w_kvb, w_out, ffn_norm, router_group_w, router_group_b, router_expert_w,
              router_expert_b, expert_w_gate, expert_w_up, expert_w_down, final_norm):
    B = x.shape[0]
    meta = jnp.broadcast_to(meta_tokens[None].astype(x.dtype), (B, N_META, x.shape[-1]))
    h = jnp.concatenate([meta, x], axis=1)
    pts = split_points()
    for l in range(DEPTH):
        u = rmsnorm(h, mix_norm[l])
        z = u @ w_in[l]
        q, k, v, r, a_low, q_lat, kv_lat, k_rope = jnp.split(z, pts, axis=-1)
        y_gla = gla_group(q, k, v, r, a_low, gla_w_a2[l], gla_b_a[l], gla_out_norm[l])
        y_mla = mla_group(q_lat, kv_lat, k_rope, mla_q_norm[l], mla_w_qb[l], mla_kv_norm[l], mla_w_kvb[l])
        h = h + jnp.concatenate([y_gla, y_mla], axis=-1) @ w_out[l]
        h = h + hier_moe(rmsnorm(h, ffn_norm[l]), router_group_w[l], router_group_b[l], router_expert_w[l],
                         router_expert_b[l], expert_w_gate[l], expert_w_up[l], expert_w_down[l])
    return rmsnorm(h, final_norm)[:, N_META:]
```

```python
import functools

import numpy as np
import jax
import jax.numpy as jnp
from jax import lax
from jax.experimental import pallas as pl
from jax.experimental.pallas import tpu as pltpu

f32 = jnp.float32
bf16 = jnp.bfloat16
i32 = jnp.int32

D_MODEL = 1024
CHUNK = 64
N_META = 16
EPS = 1e-6
GLA_HEADS = 4
GLA_DK = 64
GLA_DV = 128
GLA_GATE_RANK = 16
GLA_TAU = 16.0
GLA_QK = GLA_HEADS * GLA_DK
GLA_VW = GLA_HEADS * GLA_DV
MLA_HEADS = 4
MLA_Q_RANK = 256
MLA_KV_RANK = 128
MLA_NOPE = 128
MLA_ROPE = 64
MLA_V = 128
MLA_OUT = MLA_HEADS * MLA_V
MLA_QK_PAD = 256
ROPE_BASE = 10000.0
N_GROUPS = 8
EXPERTS_PER_GROUP = 8
N_EXPERTS = N_GROUPS * EXPERTS_PER_GROUP
D_EXPERT = 512
N_PAIRS = EXPERTS_PER_GROUP * (EXPERTS_PER_GROUP - 1) // 2
N_BUCKETS = N_GROUPS * N_PAIRS
BUCKET_LANES = 256
LANE = 128
META_W = LANE
ROW_W = D_MODEL + META_W

PREP_TILE = 512
GLA_TILE = 512
ATT_TILE = 512
OUT_TILE = 512
SCATTER_TILE = 256
FINAL_TILE = 256
MOE_BLOCK = 128
VMEM_LIMIT = 56 * 1024 * 1024

C_Q, C_K, C_V, C_R = 0, 256, 512, 1024
C_QLAT, C_KVLAT, C_KROPE, C_A, C_END = 1536, 1792, 1920, 2048, 2176

_PAIR_LO = np.array([lo for lo in range(8) for hi in range(lo + 1, 8)], np.int32)
_PAIR_HI = np.array([hi for lo in range(8) for hi in range(lo + 1, 8)], np.int32)


def _dot(a, b):
    return jnp.dot(a, b, preferred_element_type=f32)


def _dot_nt(a, b):
    return lax.dot_general(a, b, (((1,), (1,)), ((), ())), preferred_element_type=f32)


def _dot_tn(a, b):
    return lax.dot_general(a, b, (((0,), (0,)), ((), ())), preferred_element_type=f32)


def _rms(x, gain):
    return x * lax.rsqrt(jnp.mean(x * x, axis=-1, keepdims=True) + EPS) * gain


def _split3(x):
    hi = x.astype(bf16)
    r1 = x - hi.astype(f32)
    mid = r1.astype(bf16)
    lo = (r1 - mid.astype(f32)).astype(bf16)
    return hi, mid, lo


def _prep_kernel(x_ref, g_ref, win_ref, qn_ref, wqb_ref, kvn_ref, wkvb_ref, ct_ref, st_ref,
                 qg_ref, kg_ref, vg_ref, rg_ref, a_ref, qm_ref, km_ref, vm_ref):
    u = _rms(x_ref[...], g_ref[...]).astype(bf16)

    def proj(lo, hi):
        return _dot(u, win_ref[:, lo:hi])

    qg_ref[...] = proj(C_Q, C_K).astype(bf16)
    kg_ref[...] = proj(C_K, C_V).astype(bf16)
    vg_ref[...] = proj(C_V, C_R).astype(bf16)
    rg_ref[...] = proj(C_R, C_QLAT).astype(bf16)
    z = proj(C_QLAT, C_END)
    a_ref[...] = z[:, C_A - C_QLAT:].astype(bf16)
    ctab = ct_ref[...]
    stab = st_ref[...]

    def rope(seg):
        return seg * ctab + pltpu.roll(seg, 64, axis=1) * stab

    k_rope = rope(z[:, C_KROPE - C_QLAT:C_A - C_QLAT]).astype(bf16)
    qn = _rms(z[:, 0:MLA_Q_RANK], qn_ref[...]).astype(bf16)
    kvn = _rms(z[:, MLA_Q_RANK:MLA_Q_RANK + MLA_KV_RANK], kvn_ref[...]).astype(bf16)
    scale = (MLA_NOPE + MLA_ROPE) ** -0.5
    qf = _dot(qn, wqb_ref[...])
    kvf = _dot(kvn, wkvb_ref[...])
    for h in range(MLA_HEADS):
        c = h * MLA_QK_PAD
        qm_ref[:, c:c + LANE] = (qf[:, c:c + LANE] * scale).astype(bf16)
        qm_ref[:, c + LANE:c + 2 * LANE] = (rope(qf[:, c + LANE:c + 2 * LANE]) * scale).astype(bf16)
        km_ref[:, c:c + LANE] = kvf[:, h * LANE:(h + 1) * LANE].astype(bf16)
        km_ref[:, c + LANE:c + 2 * LANE] = k_rope
    vm_ref[...] = kvf[:, MLA_HEADS * MLA_NOPE:].astype(bf16)


def _prep_call(x2d, rows_per_seq, tile, gain, w_in_r, q_norm, w_qb_r, kv_norm, w_kvb_r, ctab, stab):
    t = x2d.shape[0]
    nj = rows_per_seq // tile
    grid = (t // rows_per_seq, nj)

    def row(b, j):
        return (b * nj + j, 0)

    def const(b, j):
        return (0, 0)

    def tab(b, j):
        return (j, 0)

    widths = (GLA_QK, GLA_QK, GLA_VW, GLA_VW, LANE, MLA_HEADS * MLA_QK_PAD, MLA_HEADS * MLA_QK_PAD, MLA_OUT)
    return pl.pallas_call(
        _prep_kernel,
        grid=grid,
        in_specs=[
            pl.BlockSpec((tile, D_MODEL), row),
            pl.BlockSpec((1, D_MODEL), const),
            pl.BlockSpec((D_MODEL, C_END), const),
            pl.BlockSpec((1, MLA_Q_RANK), const),
            pl.BlockSpec((MLA_Q_RANK, MLA_HEADS * MLA_QK_PAD), const),
            pl.BlockSpec((1, MLA_KV_RANK), const),
            pl.BlockSpec((MLA_KV_RANK, 2 * MLA_OUT), const),
            pl.BlockSpec((tile, LANE), tab),
            pl.BlockSpec((tile, LANE), tab),
        ],
        out_specs=[pl.BlockSpec((tile, w), row) for w in widths],
        out_shape=[jax.ShapeDtypeStruct((t, w), bf16) for w in widths],
        compiler_params=pltpu.CompilerParams(
            dimension_semantics=("parallel", "parallel"), vmem_limit_bytes=VMEM_LIMIT),
        name="prep",
    )(x2d, gain, w_in_r, q_norm, w_qb_r, kv_norm, w_kvb_r, ctab, stab)


def _gla_log_decay(a, wa2_ref, ba_ref):
    s = _dot(a, wa2_ref[...]) + ba_ref[...]
    return (jnp.minimum(s, 0.0) - jnp.log(1.0 + jnp.exp(-jnp.abs(s)))) * (1.0 / GLA_TAU)


def _gla_chunk(q, k, v, la, st_ref, want_out):
    c = CHUNK
    ri = lax.broadcasted_iota(i32, (c, c), 0)
    ci = lax.broadcasted_iota(i32, (c, c), 1)
    tri = jnp.where(ci <= ri, 1.0, 0.0).astype(bf16)
    hi, mid, lo = _split3(la)
    b = _dot(tri, hi) + _dot(tri, mid) + _dot(tri, lo)
    b_last = b[c - 1:c, :]
    kf = k.astype(f32)
    kd = (kf * jnp.exp(b_last - b)).astype(bf16)
    dec = jnp.exp(b_last)
    upd = _dot_tn(v, kd)
    rr = lax.broadcasted_iota(i32, (GLA_VW, GLA_QK), 0) // GLA_DV
    cc = lax.broadcasted_iota(i32, (GLA_VW, GLA_QK), 1) // GLA_DK
    o = None
    if want_out:
        qe = (q.astype(f32) * (GLA_DK ** -0.5) * jnp.exp(b)).astype(bf16)
        ke = kf * jnp.exp(-b)
        lane_h = lax.broadcasted_iota(i32, (c, GLA_QK), 1) // GLA_DK
        kbd = jnp.concatenate(
            [jnp.where(lane_h == h, ke, 0.0) for h in range(GLA_HEADS)], axis=0).astype(bf16)
        att = _dot_nt(qe, kbd)
        a_row = lax.broadcasted_iota(i32, (c, GLA_QK), 0)
        a_col = lax.broadcasted_iota(i32, (c, GLA_QK), 1) % c
        att = jnp.where(a_col <= a_row, att, 0.0).astype(bf16)
        vlane_h = lax.broadcasted_iota(i32, (c, GLA_VW), 1) // GLA_DV
        vf = v.astype(f32)
        vbd = jnp.concatenate(
            [jnp.where(vlane_h == h, vf, 0.0) for h in range(GLA_HEADS)], axis=0).astype(bf16)
        o = _dot(att, vbd) + _dot_nt(qe, st_ref[...].astype(bf16))
    st_ref[...] = st_ref[...] * dec + jnp.where(rr == cc, upd, 0.0)
    return o


def _gla_kernel(q_ref, k_ref, v_ref, r_ref, a_ref, km_ref, vm_ref, am_ref, wa2_ref, ba_ref, gain_ref,
                y_ref, st_ref):
    j = pl.program_id(1)

    @pl.when(j == 0)
    def _():
        st_ref[...] = jnp.zeros_like(st_ref)
        la = _gla_log_decay(am_ref[...], wa2_ref, ba_ref)
        row = lax.broadcasted_iota(i32, la.shape, 0)
        la = jnp.where(row >= CHUNK - N_META, la, 0.0)
        _gla_chunk(None, km_ref[...], vm_ref[...], la, st_ref, False)

    gain = gain_ref[...]

    def body(ci, carry):
        rows = pl.ds(pl.multiple_of(ci * CHUNK, CHUNK), CHUNK)
        la = _gla_log_decay(a_ref[rows, :], wa2_ref, ba_ref)
        o = _gla_chunk(q_ref[rows, :], k_ref[rows, :], v_ref[rows, :], la, st_ref, True)
        r = r_ref[rows, :].astype(f32)
        outs = []
        for h in range(GLA_HEADS):
            oh = o[:, h * GLA_DV:(h + 1) * GLA_DV]
            outs.append(oh * lax.rsqrt(jnp.mean(oh * oh, axis=-1, keepdims=True) + EPS))
        on = jnp.concatenate(outs, axis=1) * gain
        y_ref[rows, :] = (on * (r * jax.nn.sigmoid(r))).astype(bf16)
        return carry

    lax.fori_loop(0, GLA_TILE // CHUNK, body, 0)


def _gla_call(qg, kg, vg, rg, ag, km, vm, am, wa2_p, b_a, gain, batch, seq):
    nj = seq // GLA_TILE

    def row(b, j):
        return (b * nj + j, 0)

    def const(b, j):
        return (0, 0)

    return pl.pallas_call(
        _gla_kernel,
        grid=(batch, nj),
        in_specs=[
            pl.BlockSpec((GLA_TILE, GLA_QK), row),
            pl.BlockSpec((GLA_TILE, GLA_QK), row),
            pl.BlockSpec((GLA_TILE, GLA_VW), row),
            pl.BlockSpec((GLA_TILE, GLA_VW), row),
            pl.BlockSpec((GLA_TILE, LANE), row),
            pl.BlockSpec((CHUNK, GLA_QK), const),
            pl.BlockSpec((CHUNK, GLA_VW), const),
            pl.BlockSpec((CHUNK, LANE), const),
            pl.BlockSpec((LANE, GLA_QK), const),
            pl.BlockSpec((1, GLA_QK), const),
            pl.BlockSpec((1, GLA_VW), const),
        ],
        out_specs=pl.BlockSpec((GLA_TILE, GLA_VW), row),
        out_shape=jax.ShapeDtypeStruct((batch * seq, GLA_VW), bf16),
        scratch_shapes=[pltpu.VMEM((GLA_VW, GLA_QK), f32)],
        compiler_params=pltpu.CompilerParams(
            dimension_semantics=("parallel", "arbitrary"), vmem_limit_bytes=VMEM_LIMIT),
        name="gla",
    )(qg, kg, vg, rg, ag, km, vm, am, wa2_p, b_a, gain)


def _mla_kernel(q_ref, k_ref, v_ref, km_ref, vm_ref, o_ref):
    i = pl.program_id(2)
    tq = ATT_TILE
    q = q_ref[...]
    s0 = _dot_nt(q, km_ref[...])
    m0 = jnp.max(s0, axis=1, keepdims=True)
    p0 = jnp.exp(s0 - m0)
    l0 = jnp.sum(p0, axis=1, keepdims=True)
    acc0 = _dot(p0.astype(bf16), vm_ref[...])

    def step(s, vb, carry):
        m, l, acc = carry
        m_new = jnp.maximum(m, jnp.max(s, axis=1, keepdims=True))
        alpha = jnp.exp(m - m_new)
        p = jnp.exp(s - m_new)
        l = alpha * l + jnp.sum(p, axis=1, keepdims=True)
        acc = alpha * acc + _dot(p.astype(bf16), vb)
        return m_new, l, acc

    def body(j, carry):
        rows = pl.ds(pl.multiple_of(j * tq, tq), tq)
        return step(_dot_nt(q, k_ref[rows, :]), v_ref[rows, :], carry)

    carry = lax.fori_loop(0, i, body, (m0, l0, acc0))
    rows = pl.ds(pl.multiple_of(i * tq, tq), tq)
    s = _dot_nt(q, k_ref[rows, :])
    qc = lax.broadcasted_iota(i32, (tq, tq), 0) // CHUNK
    kc = lax.broadcasted_iota(i32, (tq, tq), 1) // CHUNK
    s = jnp.where(kc <= qc, s, -1e30)
    m, l, acc = step(s, v_ref[rows, :], carry)
    o_ref[...] = (acc / l).astype(bf16)


def _mla_call(qm, km, vm, km_meta, vm_meta, batch, seq):
    nq = seq // ATT_TILE
    qm3 = qm.reshape(batch, seq, MLA_HEADS * MLA_QK_PAD)
    km3 = km.reshape(batch, seq, MLA_HEADS * MLA_QK_PAD)
    vm3 = vm.reshape(batch, seq, MLA_OUT)
    out = pl.pallas_call(
        _mla_kernel,
        grid=(batch, MLA_HEADS, nq),
        in_specs=[
            pl.BlockSpec((None, ATT_TILE, MLA_QK_PAD), lambda b, h, i: (b, i, h)),
            pl.BlockSpec((None, seq, MLA_QK_PAD), lambda b, h, i: (b, 0, h)),
            pl.BlockSpec((None, seq, MLA_V), lambda b, h, i: (b, 0, h)),
            pl.BlockSpec((N_META, MLA_QK_PAD), lambda b, h, i: (0, h)),
            pl.BlockSpec((N_META, MLA_V), lambda b, h, i: (0, h)),
        ],
        out_specs=pl.BlockSpec((None, ATT_TILE, MLA_V), lambda b, h, i: (b, i, h)),
        out_shape=jax.ShapeDtypeStruct((batch, seq, MLA_OUT), bf16),
        compiler_params=pltpu.CompilerParams(
            dimension_semantics=("parallel", "parallel", "arbitrary"), vmem_limit_bytes=VMEM_LIMIT),
        name="mla",
    )(qm3, km3, vm3, km_meta, vm_meta)
    return out.reshape(batch * seq, MLA_OUT)


def _outproj_kernel(x_ref, yg_ref, ym_ref, wog_ref, wom_ref, gain_ref, whi_ref, wlo_ref, rb_ref,
                    h_ref, ux_ref, meta_ref, cnt_ref):
    t = OUT_TILE
    h1 = x_ref[...] + _dot(yg_ref[...], wog_ref[...]) + _dot(ym_ref[...], wom_ref[...])
    h_ref[...] = h1
    u2 = _rms(h1, gain_ref[...])
    ux_ref[:, 0:D_MODEL] = u2
    u_hi = u2.astype(bf16)
    u_lo = (u2 - u_hi.astype(f32)).astype(bf16)
    whi = whi_ref[...]
    logits = _dot(u_hi, whi) + _dot(u_lo, whi) + _dot(u_hi, wlo_ref[...]) + rb_ref[...]
    lane = lax.broadcasted_iota(i32, (t, LANE), 1)
    neg = -1e30
    g_mask = lane < N_GROUPS
    gl = jnp.where(g_mask, logits, neg)
    gmax = jnp.max(gl, axis=1, keepdims=True)
    g_sum = jnp.sum(jnp.where(g_mask, jnp.exp(gl - gmax), 0.0), axis=1, keepdims=True)
    g_p = 1.0 / g_sum
    g_idx = jnp.min(jnp.where(g_mask & (gl == gmax), lane, LANE), axis=1, keepdims=True)
    base = N_GROUPS + g_idx * EXPERTS_PER_GROUP
    e_mask = (lane >= base) & (lane < base + EXPERTS_PER_GROUP)
    el = jnp.where(e_mask, logits, neg)
    m1 = jnp.max(el, axis=1, keepdims=True)
    i1 = jnp.min(jnp.where(e_mask & (el == m1), lane, LANE), axis=1, keepdims=True)
    el2 = jnp.where(lane == i1, neg, el)
    m2 = jnp.max(el2, axis=1, keepdims=True)
    i2 = jnp.min(jnp.where(e_mask & (lane != i1) & (el2 == m2), lane, LANE), axis=1, keepdims=True)
    r = jnp.exp(m2 - m1)
    ga = g_p / (1.0 + r)
    gb = g_p * r / (1.0 + r)
    la_ = i1 - base
    lb_ = i2 - base
    lo = jnp.minimum(la_, lb_)
    hi = jnp.maximum(la_, lb_)
    g_lo = jnp.where(la_ < lb_, ga, gb)
    g_hi = jnp.where(la_ < lb_, gb, ga)
    pidx = ((lo * (2 * EXPERTS_PER_GROUP - 1 - lo)) >> 1) + (hi - lo - 1)
    bucket = g_idx * N_PAIRS + pidx
    blane = lax.broadcasted_iota(i32, (t, BUCKET_LANES), 1)
    ohf = jnp.where(blane == bucket, 1.0, 0.0)
    oh = ohf.astype(bf16)
    ri = lax.broadcasted_iota(i32, (t, t), 0)
    ci = lax.broadcasted_iota(i32, (t, t), 1)
    tri = jnp.where(ci < ri, 1.0, 0.0).astype(bf16)
    cum = _dot(tri, oh)
    rank = jnp.sum(ohf * cum, axis=1, keepdims=True)
    cnt_ref[...] = jnp.sum(ohf, axis=0, keepdims=True).reshape(1, 1, BUCKET_LANES)
    meta = jnp.where(lane == 0, bucket.astype(f32),
                     jnp.where(lane == 1, rank,
                               jnp.where(lane == 2, g_lo, jnp.where(lane == 3, g_hi, 0.0))))
    meta_ref[...] = meta
    ux_ref[:, D_MODEL:ROW_W] = meta


def _outproj_call(x2d, yg, ym, wo_g, wo_m, gain, w_hi, w_lo, rbias):
    t = x2d.shape[0]
    nt = t // OUT_TILE

    def row(i):
        return (i, 0)

    def const(i):
        return (0, 0)

    return pl.pallas_call(
        _outproj_kernel,
        grid=(nt,),
        in_specs=[
            pl.BlockSpec((OUT_TILE, D_MODEL), row),
            pl.BlockSpec((OUT_TILE, GLA_VW), row),
            pl.BlockSpec((OUT_TILE, MLA_OUT), row),
            pl.BlockSpec((GLA_VW, D_MODEL), const),
            pl.BlockSpec((MLA_OUT, D_MODEL), const),
            pl.BlockSpec((1, D_MODEL), const),
            pl.BlockSpec((D_MODEL, LANE), const),
            pl.BlockSpec((D_MODEL, LANE), const),
            pl.BlockSpec((1, LANE), const),
        ],
        out_specs=[
            pl.BlockSpec((OUT_TILE, D_MODEL), row),
            pl.BlockSpec((OUT_TILE, ROW_W), row),
            pl.BlockSpec((OUT_TILE, META_W), row),
            pl.BlockSpec((1, 1, BUCKET_LANES), lambda i: (i, 0, 0)),
        ],
        out_shape=[
            jax.ShapeDtypeStruct((t, D_MODEL), f32),
            jax.ShapeDtypeStruct((t, ROW_W), f32),
            jax.ShapeDtypeStruct((t, META_W), f32),
            jax.ShapeDtypeStruct((nt, 1, BUCKET_LANES), f32),
        ],
        compiler_params=pltpu.CompilerParams(
            dimension_semantics=("parallel",), vmem_limit_bytes=VMEM_LIMIT),
        name="outproj",
    )(x2d, yg, ym, wo_g, wo_m, gain, w_hi, w_lo, rbias)


def _scatter_kernel(pos_ref, ux_ref, hs_in_ref, hs_ref, sem):
    del hs_in_ref

    def copy(r):
        return pltpu.make_async_copy(ux_ref.at[pl.ds(r, 1)], hs_ref.at[pl.ds(pos_ref[r], 1)], sem)

    def start(r, c):
        copy(r).start()
        return c

    def wait(r, c):
        copy(r).wait()
        return c

    lax.fori_loop(0, SCATTER_TILE, start, 0)
    lax.fori_loop(0, SCATTER_TILE, wait, 0)


def _scatter_call(pos, ux, hs0):
    t = ux.shape[0]
    return pl.pallas_call(
        _scatter_kernel,
        grid=(t // SCATTER_TILE,),
        in_specs=[
            pl.BlockSpec((SCATTER_TILE,), lambda i: (i,), memory_space=pltpu.SMEM),
            pl.BlockSpec((SCATTER_TILE, ROW_W), lambda i: (i, 0)),
            pl.BlockSpec(memory_space=pl.ANY),
        ],
        out_specs=pl.BlockSpec(memory_space=pl.ANY),
        out_shape=jax.ShapeDtypeStruct(hs0.shape, f32),
        scratch_shapes=[pltpu.SemaphoreType.DMA(())],
        input_output_aliases={2: 0},
        compiler_params=pltpu.CompilerParams(
            dimension_semantics=("arbitrary",), vmem_limit_bytes=VMEM_LIMIT),
        name="scatter",
    )(pos, ux, hs0)


def _moe_kernel(ie_ref, ib_ref, ir_ref, iv_ref, if_ref, hs_ref, wg_ref, wu_ref, wd_ref, y_ref, wgu_s, wd_s):
    w = pl.program_id(0)

    @pl.when(if_ref[w] == 1)
    def _():
        wgu_s[:, 0:D_EXPERT] = wg_ref[...].astype(bf16)
        wgu_s[:, D_EXPERT:2 * D_EXPERT] = wu_ref[...].astype(bf16)
        wd_s[...] = wd_ref[...].astype(bf16)

    @pl.when(iv_ref[w] == 1)
    def _():
        u = hs_ref[:, 0:D_MODEL].astype(bf16)
        meta = hs_ref[:, D_MODEL:ROW_W]
        gate = jnp.where(ir_ref[w] == 0, meta[:, 2:3], meta[:, 3:4])
        gu = _dot(u, wgu_s[...])
        g = gu[:, 0:D_EXPERT]
        hdn = (g * jax.nn.sigmoid(g) * gu[:, D_EXPERT:]).astype(bf16)
        y_ref[...] = _dot(hdn, wd_s[...]) * gate

    @pl.when(iv_ref[w] == 0)
    def _():
        y_ref[...] = jnp.zeros_like(y_ref)


def _moe_call(items, hs, w_gate, w_up, w_down):
    ie, ib, ir, iv, ifirst = items
    n_items = ie.shape[0]
    n_slots = hs.shape[0]
    grid_spec = pltpu.PrefetchScalarGridSpec(
        num_scalar_prefetch=5,
        grid=(n_items,),
        in_specs=[
            pl.BlockSpec((MOE_BLOCK, ROW_W), lambda w, ie, ib, ir, iv, f: (ib[w], 0)),
            pl.BlockSpec((None, D_MODEL, D_EXPERT), lambda w, ie, ib, ir, iv, f: (ie[w], 0, 0)),
            pl.BlockSpec((None, D_MODEL, D_EXPERT), lambda w, ie, ib, ir, iv, f: (ie[w], 0, 0)),
            pl.BlockSpec((None, D_EXPERT, D_MODEL), lambda w, ie, ib, ir, iv, f: (ie[w], 0, 0)),
        ],
        out_specs=pl.BlockSpec((MOE_BLOCK, D_MODEL), lambda w, ie, ib, ir, iv, f: (ib[w], ir[w])),
        scratch_shapes=[pltpu.VMEM((D_MODEL, 2 * D_EXPERT), bf16), pltpu.VMEM((D_EXPERT, D_MODEL), bf16)],
    )
    return pl.pallas_call(
        _moe_kernel,
        grid_spec=grid_spec,
        out_shape=jax.ShapeDtypeStruct((n_slots, 2 * D_MODEL), f32),
        compiler_params=pltpu.CompilerParams(
            dimension_semantics=("arbitrary",), vmem_limit_bytes=VMEM_LIMIT),
        name="moe",
    )(ie, ib, ir, iv, ifirst, hs, w_gate, w_up, w_down)


def _final_kernel(pos_ref, h_ref, gain_ref, y_hbm, o_ref, ybuf, sem):
    def copy(r):
        return pltpu.make_async_copy(y_hbm.at[pl.ds(pos_ref[r], 1)], ybuf.at[pl.ds(r, 1)], sem)

    def start(r, c):
        copy(r).start()
        return c

    def wait(r, c):
        copy(r).wait()
        return c

    lax.fori_loop(0, FINAL_TILE, start, 0)
    lax.fori_loop(0, FINAL_TILE, wait, 0)
    h = h_ref[...] + ybuf[:, 0:D_MODEL] + ybuf[:, D_MODEL:2 * D_MODEL]
    o_ref[...] = _rms(h, gain_ref[...])


def _final_call(pos, h1, gain, y):
    t = h1.shape[0]
    return pl.pallas_call(
        _final_kernel,
        grid=(t // FINAL_TILE,),
        in_specs=[
            pl.BlockSpec((FINAL_TILE,), lambda i: (i,), memory_space=pltpu.SMEM),
            pl.BlockSpec((FINAL_TILE, D_MODEL), lambda i: (i, 0)),
            pl.BlockSpec((1, D_MODEL), lambda i: (0, 0)),
            pl.BlockSpec(memory_space=pl.ANY),
        ],
        out_specs=pl.BlockSpec((FINAL_TILE, D_MODEL), lambda i: (i, 0)),
        out_shape=jax.ShapeDtypeStruct((t, D_MODEL), f32),
        scratch_shapes=[pltpu.VMEM((FINAL_TILE, 2 * D_MODEL), f32), pltpu.SemaphoreType.DMA(())],
        compiler_params=pltpu.CompilerParams(
            dimension_semantics=("arbitrary",), vmem_limit_bytes=VMEM_LIMIT),
        name="final",
    )(pos, h1, gain, y)


def _rope_tables(pos):
    inv = ROPE_BASE ** (-jnp.arange(0, MLA_ROPE, 2, dtype=f32) / MLA_ROPE)
    ang = pos.astype(f32)[:, None] * inv[None, :]
    cos, sin = jnp.cos(ang), jnp.sin(ang)
    z = jnp.zeros((pos.shape[0], LANE - MLA_ROPE), f32)
    return jnp.concatenate([cos, cos, z], axis=1), jnp.concatenate([-sin, sin, z], axis=1)


def _relayout_weights(w_in, w_qb, w_kvb):
    half = MLA_ROPE // 2
    perm = (np.arange(MLA_ROPE) + half) % MLA_ROPE
    pts = np.cumsum((GLA_QK, GLA_QK, GLA_VW, GLA_VW, GLA_GATE_RANK, MLA_Q_RANK, MLA_KV_RANK, MLA_ROPE))
    q_g, k_g, v_g, r_g, a_l, q_lat, kv_lat, k_rope = jnp.split(w_in, pts[:-1], axis=1)
    a_seg = jnp.pad(a_l, ((0, 0), (0, LANE - GLA_GATE_RANK)))
    w_in_r = jnp.concatenate(
        [q_g, k_g, v_g, r_g, q_lat, kv_lat, k_rope, k_rope[:, perm], a_seg], axis=1).astype(bf16)
    qcols, kcols, vcols = [], [], []
    for h in range(MLA_HEADS):
        c = h * (MLA_NOPE + MLA_ROPE)
        rope = w_qb[:, c + MLA_NOPE:c + MLA_NOPE + MLA_ROPE]
        qcols += [w_qb[:, c:c + MLA_NOPE], rope, rope[:, perm]]
        c2 = h * (MLA_NOPE + MLA_V)
        kcols.append(w_kvb[:, c2:c2 + MLA_NOPE])
        vcols.append(w_kvb[:, c2 + MLA_NOPE:c2 + MLA_NOPE + MLA_V])
    return w_in_r, jnp.concatenate(qcols, axis=1).astype(bf16), jnp.concatenate(kcols + vcols, axis=1).astype(bf16)


def _route_plan(counts, bucket, rank, n_tok):
    nt = counts.shape[0]
    tot = counts.sum(axis=0)
    padded = (tot + MOE_BLOCK - 1) // MOE_BLOCK * MOE_BLOCK
    bstart = jnp.cumsum(padded) - padded
    tile_base = bstart[None, :] + jnp.cumsum(counts, axis=0) - counts
    pos = jnp.take_along_axis(tile_base, bucket.reshape(nt, -1), axis=1).reshape(-1) + rank
    nb_max = (n_tok + N_BUCKETS * (MOE_BLOCK - 1)) // MOE_BLOCK
    nblk = padded // MOE_BLOCK
    bend = jnp.cumsum(nblk)
    n_blocks = bend[-1]
    blk = jnp.arange(nb_max, dtype=i32)
    bb = jnp.minimum(jnp.searchsorted(bend, blk, side="right"), N_BUCKETS - 1).astype(i32)
    valid = blk < n_blocks
    grp, pidx = bb // N_PAIRS, bb % N_PAIRS
    e_lo = grp * EXPERTS_PER_GROUP + jnp.asarray(_PAIR_LO)[pidx]
    e_hi = grp * EXPERTS_PER_GROUP + jnp.asarray(_PAIR_HI)[pidx]
    expert = jnp.concatenate([e_lo, e_hi])
    block = jnp.concatenate([blk, blk])
    role = jnp.concatenate([jnp.zeros_like(blk), jnp.ones_like(blk)])
    valid2 = jnp.concatenate([valid, valid])
    order = jnp.argsort(jnp.where(valid2, expert, N_EXPERTS), stable=True)
    expert, block, role, valid2 = expert[order], block[order], role[order], valid2[order]
    last_e = expert[jnp.maximum(2 * n_blocks - 1, 0)]
    expert = jnp.where(valid2, expert, last_e)
    first = jnp.concatenate([jnp.ones((1,), bool), expert[1:] != expert[:-1]])
    items = tuple(a.astype(i32) for a in (expert, block, role, valid2, first))
    return pos.astype(i32), items, nb_max


def kernel(x, meta_tokens, mix_norm, w_in, gla_w_a2, gla_b_a, gla_out_norm, mla_q_norm, mla_w_qb, mla_kv_norm,
           mla_w_kvb, w_out, ffn_norm, router_group_w, router_group_b, router_expert_w, router_expert_b,
           expert_w_gate, expert_w_up, expert_w_down, final_norm):
    batch, seq, d = x.shape
    assert d == D_MODEL and seq % max(PREP_TILE, GLA_TILE, ATT_TILE) == 0
    assert (batch * seq) % OUT_TILE == 0
    n_tok = batch * seq
    x2d = x.reshape(n_tok, d)

    w_in_r, w_qb_r, w_kvb_r = _relayout_weights(w_in[0], mla_w_qb[0], mla_w_kvb[0])
    mixg = mix_norm[0].reshape(1, d)
    qn = mla_q_norm[0].reshape(1, MLA_Q_RANK)
    kvn = mla_kv_norm[0].reshape(1, MLA_KV_RANK)
    ct_m, st_m = _rope_tables(jnp.arange(N_META))
    ct_x, st_x = _rope_tables(N_META + jnp.arange(seq))

    _, kg_m, vg_m, _, a_m, _, km_m, vm_m = _prep_call(
        meta_tokens.astype(f32), N_META, N_META, mixg, w_in_r, qn, w_qb_r, kvn, w_kvb_r, ct_m, st_m)
    qg, kg, vg, rg, ag, qm, km, vm = _prep_call(
        x2d, seq, PREP_TILE, mixg, w_in_r, qn, w_qb_r, kvn, w_kvb_r, ct_x, st_x)

    front = ((CHUNK - N_META, 0), (0, 0))
    wa2_p = jnp.pad(gla_w_a2[0], ((0, LANE - GLA_GATE_RANK), (0, 0))).astype(bf16)
    y_gla = _gla_call(qg, kg, vg, rg, ag, jnp.pad(kg_m, front), jnp.pad(vg_m, front), jnp.pad(a_m, front),
                      wa2_p, gla_b_a[0].reshape(1, GLA_QK), gla_out_norm[0].reshape(1, GLA_VW), batch, seq)
    y_mla = _mla_call(qm, km, vm, km_m, vm_m, batch, seq)

    wo = w_out[0].astype(bf16)
    rw = jnp.concatenate([router_group_w[0], router_expert_w[0],
                          jnp.zeros((d, LANE - N_GROUPS - N_EXPERTS), f32)], axis=1)
    rw_hi = rw.astype(bf16)
    rw_lo = (rw - rw_hi.astype(f32)).astype(bf16)
    rb = jnp.concatenate([router_group_b[0], router_expert_b[0],
                          jnp.zeros((LANE - N_GROUPS - N_EXPERTS,), f32)]).reshape(1, LANE)
    h1, ux, meta, cnt = _outproj_call(x2d, y_gla, y_mla, wo[:GLA_VW], wo[GLA_VW:], ffn_norm[0].reshape(1, d),
                                      rw_hi, rw_lo, rb)

    counts = cnt.reshape(-1, BUCKET_LANES)[:, :N_BUCKETS].astype(i32)
    pos, items, nb_max = _route_plan(counts, meta[:, 0].astype(i32), meta[:, 1].astype(i32), n_tok)
    n_slots = nb_max * MOE_BLOCK
    hs = _scatter_call(pos, ux, jnp.zeros((n_slots, ROW_W), f32))
    y = _moe_call(items, hs, expert_w_gate[0], expert_w_up[0], expert_w_down[0])
    out = _final_call(pos, h1, final_norm.reshape(1, d), y)
    return out.reshape(batch, seq, d)
```

```python
import functools

import numpy as np
import jax
import jax.numpy as jnp
from jax import lax
from jax.experimental import pallas as pl
from jax.experimental.pallas import tpu as pltpu

f32 = jnp.float32
bf16 = jnp.bfloat16
i32 = jnp.int32

D_MODEL = 1024
CHUNK = 64
N_META = 16
EPS = 1e-6
GLA_HEADS = 4
GLA_DK = 64
GLA_DV = 128
GLA_GATE_RANK = 16
GLA_TAU = 16.0
GLA_QK = GLA_HEADS * GLA_DK
GLA_VW = GLA_HEADS * GLA_DV
MLA_HEADS = 4
MLA_Q_RANK = 256
MLA_KV_RANK = 128
MLA_NOPE = 128
MLA_ROPE = 64
MLA_V = 128
MLA_OUT = MLA_HEADS * MLA_V
MLA_QK_PAD = 256
ROPE_BASE = 10000.0
N_GROUPS = 8
EXPERTS_PER_GROUP = 8
N_EXPERTS = N_GROUPS * EXPERTS_PER_GROUP
D_EXPERT = 512
N_PAIRS = EXPERTS_PER_GROUP * (EXPERTS_PER_GROUP - 1) // 2
N_BUCKETS = N_GROUPS * N_PAIRS
BUCKET_LANES = 256
LANE = 128
META_W = LANE
ROW_W = D_MODEL + META_W

PREP_TILE = 512
GLA_TILE = 512
ATT_TILE = 512
ATT_HEADS = 2
META_TILE = 128
OUT_TILE = 512
SCATTER_TILE = 256
FINAL_TILE = 256
MOE_BLOCK = 128
ROW_DMA_UNROLL = 8
VMEM_LIMIT = 56 * 1024 * 1024

C_Q, C_K, C_V, C_R = 0, 256, 512, 1024
C_QLAT, C_KVLAT, C_KROPE, C_A, C_END = 1536, 1792, 1920, 2048, 2176

_PAIR_LO = np.array([lo for lo in range(8) for hi in range(lo + 1, 8)], np.int32)
_PAIR_HI = np.array([hi for lo in range(8) for hi in range(lo + 1, 8)], np.int32)


def _dot(a, b):
    return jnp.dot(a, b, preferred_element_type=f32)


def _dot_nt(a, b):
    return lax.dot_general(a, b, (((1,), (1,)), ((), ())), preferred_element_type=f32)


def _dot_tn(a, b):
    return lax.dot_general(a, b, (((0,), (0,)), ((), ())), preferred_element_type=f32)


def _rms(x, gain):
    return x * lax.rsqrt(jnp.mean(x * x, axis=-1, keepdims=True) + EPS) * gain


def _split3(x):
    hi = x.astype(bf16)
    r1 = x - hi.astype(f32)
    mid = r1.astype(bf16)
    lo = (r1 - mid.astype(f32)).astype(bf16)
    return hi, mid, lo


def _prep_kernel(x_ref, g_ref, win_ref, qn_ref, wqb_ref, kvn_ref, wkvb_ref, ct_ref, st_ref,
                 qg_ref, kg_ref, vg_ref, rg_ref, a_ref, qm_ref, km_ref, vmt_ref):
    u = _rms(x_ref[...], g_ref[...]).astype(bf16)

    def proj(lo, hi):
        return _dot(u, win_ref[:, lo:hi])

    qg_ref[...] = proj(C_Q, C_K).astype(bf16)
    kg_ref[...] = proj(C_K, C_V).astype(bf16)
    vg_ref[...] = proj(C_V, C_R).astype(bf16)
    rg_ref[...] = proj(C_R, C_QLAT).astype(bf16)
    z = proj(C_QLAT, C_END)
    a_ref[...] = z[:, C_A - C_QLAT:].astype(bf16)
    ctab = ct_ref[...]
    stab = st_ref[...]

    def rope(seg):
        return seg * ctab + pltpu.roll(seg, 64, axis=1) * stab

    k_rope = rope(z[:, C_KROPE - C_QLAT:C_A - C_QLAT]).astype(bf16)
    qn = _rms(z[:, 0:MLA_Q_RANK], qn_ref[...]).astype(bf16)
    kvn = _rms(z[:, MLA_Q_RANK:MLA_Q_RANK + MLA_KV_RANK], kvn_ref[...]).astype(bf16)
    scale = (MLA_NOPE + MLA_ROPE) ** -0.5
    qf = _dot(qn, wqb_ref[...])
    kvf = _dot(kvn, wkvb_ref[...])
    for h in range(MLA_HEADS):
        c = h * MLA_QK_PAD
        qm_ref[:, c:c + LANE] = (qf[:, c:c + LANE] * scale).astype(bf16)
        qm_ref[:, c + LANE:c + 2 * LANE] = (rope(qf[:, c + LANE:c + 2 * LANE]) * scale).astype(bf16)
        km_ref[:, c:c + LANE] = kvf[:, h * LANE:(h + 1) * LANE].astype(bf16)
        km_ref[:, c + LANE:c + 2 * LANE] = k_rope
    vmt_ref[...] = kvf[:, MLA_HEADS * MLA_NOPE:].T.astype(bf16)


def _prep_call(x2d, rows_per_seq, tile, gain, w_in_r, q_norm, w_qb_r, kv_norm, w_kvb_r, ctab, stab):
    t = x2d.shape[0]
    nj = rows_per_seq // tile
    grid = (t // rows_per_seq, nj)

    def row(b, j):
        return (b * nj + j, 0)

    def const(b, j):
        return (0, 0)

    def tab(b, j):
        return (j, 0)

    widths = (GLA_QK, GLA_QK, GLA_VW, GLA_VW, LANE, MLA_HEADS * MLA_QK_PAD, MLA_HEADS * MLA_QK_PAD)
    return pl.pallas_call(
        _prep_kernel,
        grid=grid,
        in_specs=[
            pl.BlockSpec((tile, D_MODEL), row),
            pl.BlockSpec((1, D_MODEL), const),
            pl.BlockSpec((D_MODEL, C_END), const),
            pl.BlockSpec((1, MLA_Q_RANK), const),
            pl.BlockSpec((MLA_Q_RANK, MLA_HEADS * MLA_QK_PAD), const),
            pl.BlockSpec((1, MLA_KV_RANK), const),
            pl.BlockSpec((MLA_KV_RANK, 2 * MLA_OUT), const),
            pl.BlockSpec((tile, LANE), tab),
            pl.BlockSpec((tile, LANE), tab),
        ],
        out_specs=[pl.BlockSpec((tile, w), row) for w in widths]
        + [pl.BlockSpec((None, MLA_OUT, tile), lambda b, j: (b * nj + j, 0, 0))],
        out_shape=[jax.ShapeDtypeStruct((t, w), bf16) for w in widths]
        + [jax.ShapeDtypeStruct((t // tile, MLA_OUT, tile), bf16)],
        compiler_params=pltpu.CompilerParams(
            dimension_semantics=("parallel", "parallel"), vmem_limit_bytes=VMEM_LIMIT),
        name="prep",
    )(x2d, gain, w_in_r, q_norm, w_qb_r, kv_norm, w_kvb_r, ctab, stab)


def _gla_log_decay(a, wa2_ref, ba_ref):
    s = _dot(a, wa2_ref[...]) + ba_ref[...]
    return (jnp.minimum(s, 0.0) - jnp.log(1.0 + jnp.exp(-jnp.abs(s)))) * (1.0 / GLA_TAU)


def _gla_chunk(q, k, v, la, st_ref, want_out):
    c = CHUNK
    ri = lax.broadcasted_iota(i32, (c, c), 0)
    ci = lax.broadcasted_iota(i32, (c, c), 1)
    tri = jnp.where(ci <= ri, 1.0, 0.0).astype(bf16)
    hi, mid, lo = _split3(la)
    b = _dot(tri, hi) + _dot(tri, mid) + _dot(tri, lo)
    b_last = b[c - 1:c, :]
    kf = k.astype(f32)
    kd = (kf * jnp.exp(b_last - b)).astype(bf16)
    dec = jnp.exp(b_last)
    upd = _dot_tn(v, kd)
    rr = lax.broadcasted_iota(i32, (GLA_VW, GLA_QK), 0) // GLA_DV
    cc = lax.broadcasted_iota(i32, (GLA_VW, GLA_QK), 1) // GLA_DK
    o = None
    if want_out:
        qe = (q.astype(f32) * (GLA_DK ** -0.5) * jnp.exp(b)).astype(bf16)
        ke = kf * jnp.exp(-b)
        lane_h = lax.broadcasted_iota(i32, (c, GLA_QK), 1) // GLA_DK
        kbd = jnp.concatenate(
            [jnp.where(lane_h == h, ke, 0.0) for h in range(GLA_HEADS)], axis=0).astype(bf16)
        att = _dot_nt(qe, kbd)
        a_row = lax.broadcasted_iota(i32, (c, GLA_QK), 0)
        a_col = lax.broadcasted_iota(i32, (c, GLA_QK), 1) % c
        att = jnp.where(a_col <= a_row, att, 0.0).astype(bf16)
        vlane_h = lax.broadcasted_iota(i32, (c, GLA_VW), 1) // GLA_DV
        vf = v.astype(f32)
        vbd = jnp.concatenate(
            [jnp.where(vlane_h == h, vf, 0.0) for h in range(GLA_HEADS)], axis=0).astype(bf16)
        o = _dot(att, vbd) + _dot_nt(qe, st_ref[...].astype(bf16))
    st_ref[...] = st_ref[...] * dec + jnp.where(rr == cc, upd, 0.0)
    return o


def _gla_kernel(q_ref, k_ref, v_ref, r_ref, a_ref, km_ref, vm_ref, am_ref, wa2_ref, ba_ref, gain_ref,
                y_ref, st_ref):
    j = pl.program_id(1)

    @pl.when(j == 0)
    def _():
        st_ref[...] = jnp.zeros_like(st_ref)
        la = _gla_log_decay(am_ref[...], wa2_ref, ba_ref)
        row = lax.broadcasted_iota(i32, la.shape, 0)
        la = jnp.where(row >= CHUNK - N_META, la, 0.0)
        _gla_chunk(None, km_ref[...], vm_ref[...], la, st_ref, False)

    gain = gain_ref[...]

    def body(ci, carry):
        rows = pl.ds(pl.multiple_of(ci * CHUNK, CHUNK), CHUNK)
        la = _gla_log_decay(a_ref[rows, :], wa2_ref, ba_ref)
        o = _gla_chunk(q_ref[rows, :], k_ref[rows, :], v_ref[rows, :], la, st_ref, True)
        r = r_ref[rows, :].astype(f32)
        outs = []
        for h in range(GLA_HEADS):
            oh = o[:, h * GLA_DV:(h + 1) * GLA_DV]
            outs.append(oh * lax.rsqrt(jnp.mean(oh * oh, axis=-1, keepdims=True) + EPS))
        on = jnp.concatenate(outs, axis=1) * gain
        y_ref[rows, :] = (on * (r * jax.nn.sigmoid(r))).astype(bf16)
        return carry

    lax.fori_loop(0, GLA_TILE // CHUNK, body, 0)


def _gla_call(qg, kg, vg, rg, ag, km, vm, am, wa2_p, b_a, gain, batch, seq):
    nj = seq // GLA_TILE

    def row(b, j):
        return (b * nj + j, 0)

    def const(b, j):
        return (0, 0)

    return pl.pallas_call(
        _gla_kernel,
        grid=(batch, nj),
        in_specs=[
            pl.BlockSpec((GLA_TILE, GLA_QK), row),
            pl.BlockSpec((GLA_TILE, GLA_QK), row),
            pl.BlockSpec((GLA_TILE, GLA_VW), row),
            pl.BlockSpec((GLA_TILE, GLA_VW), row),
            pl.BlockSpec((GLA_TILE, LANE), row),
            pl.BlockSpec((CHUNK, GLA_QK), const),
            pl.BlockSpec((CHUNK, GLA_VW), const),
            pl.BlockSpec((CHUNK, LANE), const),
            pl.BlockSpec((LANE, GLA_QK), const),
            pl.BlockSpec((1, GLA_QK), const),
            pl.BlockSpec((1, GLA_VW), const),
        ],
        out_specs=pl.BlockSpec((GLA_TILE, GLA_VW), row),
        out_shape=jax.ShapeDtypeStruct((batch * seq, GLA_VW), bf16),
        scratch_shapes=[pltpu.VMEM((GLA_VW, GLA_QK), f32)],
        compiler_params=pltpu.CompilerParams(
            dimension_semantics=("parallel", "arbitrary"), vmem_limit_bytes=VMEM_LIMIT),
        name="gla",
    )(qg, kg, vg, rg, ag, km, vm, am, wa2_p, b_a, gain)


def _mla_kernel(q_ref, k_ref, vt_ref, km_ref, vmt_ref, o_ref):
    i = pl.program_id(2)
    tq = ATT_TILE
    w = MLA_QK_PAD

    def update(h, kb, vtb, carry, mask=None):
        m, l, acc = carry
        s = _dot_nt(kb, q_ref[:, h * w:(h + 1) * w])
        if mask is not None:
            s = jnp.where(mask, s, -1e30)
        m_new = jnp.maximum(m, jnp.max(s, axis=0, keepdims=True))
        alpha = jnp.exp(m - m_new)
        p = jnp.exp(s - m_new)
        l = alpha * l + jnp.sum(p, axis=0, keepdims=True)
        acc = alpha * acc + _dot(vtb, p.astype(bf16))
        return m_new, l, acc

    def body(j, carries):
        rows = pl.ds(pl.multiple_of(j * tq, tq), tq)
        return tuple(
            update(h, k_ref[rows, h * w:(h + 1) * w], vt_ref[j, h * MLA_V:(h + 1) * MLA_V, :], carries[h])
            for h in range(ATT_HEADS))

    init = tuple((jnp.full((1, tq), -1e30, f32), jnp.zeros((1, tq), f32), jnp.zeros((MLA_V, tq), f32))
                 for _ in range(ATT_HEADS))
    carries = lax.fori_loop(0, i, body, init)
    rows = pl.ds(pl.multiple_of(i * tq, tq), tq)
    kc = lax.broadcasted_iota(i32, (tq, tq), 0) // CHUNK
    qc = lax.broadcasted_iota(i32, (tq, tq), 1) // CHUNK
    mask = kc <= qc
    for h in range(ATT_HEADS):
        hv = slice(h * MLA_V, (h + 1) * MLA_V)
        carry = update(h, k_ref[rows, h * w:(h + 1) * w], vt_ref[i, hv, :], carries[h], mask)
        m, l, acc = update(h, km_ref[:, h * w:(h + 1) * w], vmt_ref[hv, :], carry)
        o_ref[:, hv] = (acc * (1.0 / l)).T.astype(bf16)


def _mla_call(qm, km, vmt, km_meta, vmt_meta, batch, seq):
    nq = seq // ATT_TILE
    nh = ATT_HEADS
    qm3 = qm.reshape(batch, seq, MLA_HEADS * MLA_QK_PAD)
    km3 = km.reshape(batch, seq, MLA_HEADS * MLA_QK_PAD)
    vt4 = vmt.reshape(batch, nq, MLA_OUT, ATT_TILE)
    out = pl.pallas_call(
        _mla_kernel,
        grid=(batch, MLA_HEADS // nh, nq),
        in_specs=[
            pl.BlockSpec((None, ATT_TILE, nh * MLA_QK_PAD), lambda b, h, i: (b, i, h)),
            pl.BlockSpec((None, seq, nh * MLA_QK_PAD), lambda b, h, i: (b, 0, h)),
            pl.BlockSpec((None, nq, nh * MLA_V, ATT_TILE), lambda b, h, i: (b, 0, h, 0)),
            pl.BlockSpec((N_META, nh * MLA_QK_PAD), lambda b, h, i: (0, h)),
            pl.BlockSpec((nh * MLA_V, N_META), lambda b, h, i: (h, 0)),
        ],
        out_specs=pl.BlockSpec((None, ATT_TILE, nh * MLA_V), lambda b, h, i: (b, i, h)),
        out_shape=jax.ShapeDtypeStruct((batch, seq, MLA_OUT), bf16),
        compiler_params=pltpu.CompilerParams(
            dimension_semantics=("parallel", "parallel", "arbitrary"), vmem_limit_bytes=VMEM_LIMIT),
        name="mla",
    )(qm3, km3, vt4, km_meta, vmt_meta)
    return out.reshape(batch * seq, MLA_OUT)


def _outproj_kernel(x_ref, yg_ref, ym_ref, wog_ref, wom_ref, gain_ref, whi_ref, wlo_ref, rb_ref,
                    h_ref, ux_ref, meta_ref, cnt_ref):
    t = OUT_TILE
    h1 = x_ref[...] + _dot(yg_ref[...], wog_ref[...]) + _dot(ym_ref[...], wom_ref[...])
    h_ref[...] = h1
    u2 = _rms(h1, gain_ref[...])
    ux_ref[:, 0:D_MODEL] = u2
    u_hi = u2.astype(bf16)
    u_lo = (u2 - u_hi.astype(f32)).astype(bf16)
    whi = whi_ref[...]
    logits = _dot(u_hi, whi) + _dot(u_lo, whi) + _dot(u_hi, wlo_ref[...]) + rb_ref[...]
    lane = lax.broadcasted_iota(i32, (t, LANE), 1)
    neg = -1e30
    g_mask = lane < N_GROUPS
    gl = jnp.where(g_mask, logits, neg)
    gmax = jnp.max(gl, axis=1, keepdims=True)
    g_sum = jnp.sum(jnp.where(g_mask, jnp.exp(gl - gmax), 0.0), axis=1, keepdims=True)
    g_p = 1.0 / g_sum
    g_idx = jnp.min(jnp.where(g_mask & (gl == gmax), lane, LANE), axis=1, keepdims=True)
    base = N_GROUPS + g_idx * EXPERTS_PER_GROUP
    e_mask = (lane >= base) & (lane < base + EXPERTS_PER_GROUP)
    el = jnp.where(e_mask, logits, neg)
    m1 = jnp.max(el, axis=1, keepdims=True)
    i1 = jnp.min(jnp.where(e_mask & (el == m1), lane, LANE), axis=1, keepdims=True)
    el2 = jnp.where(lane == i1, neg, el)
    m2 = jnp.max(el2, axis=1, keepdims=True)
    i2 = jnp.min(jnp.where(e_mask & (lane != i1) & (el2 == m2), lane, LANE), axis=1, keepdims=True)
    r = jnp.exp(m2 - m1)
    ga = g_p / (1.0 + r)
    gb = g_p * r / (1.0 + r)
    la_ = i1 - base
    lb_ = i2 - base
    lo = jnp.minimum(la_, lb_)
    hi = jnp.maximum(la_, lb_)
    g_lo = jnp.where(la_ < lb_, ga, gb)
    g_hi = jnp.where(la_ < lb_, gb, ga)
    pidx = ((lo * (2 * EXPERTS_PER_GROUP - 1 - lo)) >> 1) + (hi - lo - 1)
    bucket = g_idx * N_PAIRS + pidx
    blane = lax.broadcasted_iota(i32, (t, BUCKET_LANES), 1)
    ohf = jnp.where(blane == bucket, 1.0, 0.0)
    oh = ohf.astype(bf16)
    ri = lax.broadcasted_iota(i32, (t, t), 0)
    ci = lax.broadcasted_iota(i32, (t, t), 1)
    tri = jnp.where(ci < ri, 1.0, 0.0).astype(bf16)
    cum = _dot(tri, oh)
    rank = jnp.sum(ohf * cum, axis=1, keepdims=True)
    cnt_ref[...] = jnp.sum(ohf, axis=0, keepdims=True).reshape(1, 1, BUCKET_LANES)
    meta = jnp.where(lane == 0, bucket.astype(f32),
                     jnp.where(lane == 1, rank,
                               jnp.where(lane == 2, g_lo, jnp.where(lane == 3, g_hi, 0.0))))
    meta_ref[...] = meta
    ux_ref[:, D_MODEL:ROW_W] = meta


def _outproj_call(x2d, yg, ym, wo_g, wo_m, gain, w_hi, w_lo, rbias):
    t = x2d.shape[0]
    nt = t // OUT_TILE

    def row(i):
        return (i, 0)

    def const(i):
        return (0, 0)

    return pl.pallas_call(
        _outproj_kernel,
        grid=(nt,),
        in_specs=[
            pl.BlockSpec((OUT_TILE, D_MODEL), row),
            pl.BlockSpec((OUT_TILE, GLA_VW), row),
            pl.BlockSpec((OUT_TILE, MLA_OUT), row),
            pl.BlockSpec((GLA_VW, D_MODEL), const),
            pl.BlockSpec((MLA_OUT, D_MODEL), const),
            pl.BlockSpec((1, D_MODEL), const),
            pl.BlockSpec((D_MODEL, LANE), const),
            pl.BlockSpec((D_MODEL, LANE), const),
            pl.BlockSpec((1, LANE), const),
        ],
        out_specs=[
            pl.BlockSpec((OUT_TILE, D_MODEL), row),
            pl.BlockSpec((OUT_TILE, ROW_W), row),
            pl.BlockSpec((OUT_TILE, META_W), row),
            pl.BlockSpec((1, 1, BUCKET_LANES), lambda i: (i, 0, 0)),
        ],
        out_shape=[
            jax.ShapeDtypeStruct((t, D_MODEL), f32),
            jax.ShapeDtypeStruct((t, ROW_W), f32),
            jax.ShapeDtypeStruct((t, META_W), f32),
            jax.ShapeDtypeStruct((nt, 1, BUCKET_LANES), f32),
        ],
        compiler_params=pltpu.CompilerParams(
            dimension_semantics=("parallel",), vmem_limit_bytes=VMEM_LIMIT),
        name="outproj",
    )(x2d, yg, ym, wo_g, wo_m, gain, w_hi, w_lo, rbias)


def _scatter_kernel(pos_ref, ux_ref, hs_in_ref, hs_ref, sem):
    del hs_in_ref

    def start(r, c):
        pltpu.make_async_copy(ux_ref.at[pl.ds(r, 1)], hs_ref.at[pl.ds(pos_ref[r], 1)], sem).start()
        return c

    lax.fori_loop(0, SCATTER_TILE, start, 0, unroll=ROW_DMA_UNROLL)
    pltpu.make_async_copy(ux_ref, hs_ref.at[pl.ds(0, SCATTER_TILE)], sem).wait()


def _scatter_call(pos, ux, hs0):
    t = ux.shape[0]
    return pl.pallas_call(
        _scatter_kernel,
        grid=(t // SCATTER_TILE,),
        in_specs=[
            pl.BlockSpec((SCATTER_TILE,), lambda i: (i,), memory_space=pltpu.SMEM),
            pl.BlockSpec((SCATTER_TILE, ROW_W), lambda i: (i, 0)),
            pl.BlockSpec(memory_space=pl.ANY),
        ],
        out_specs=pl.BlockSpec(memory_space=pl.ANY),
        out_shape=jax.ShapeDtypeStruct(hs0.shape, f32),
        scratch_shapes=[pltpu.SemaphoreType.DMA(())],
        input_output_aliases={2: 0},
        compiler_params=pltpu.CompilerParams(
            dimension_semantics=("arbitrary",), vmem_limit_bytes=VMEM_LIMIT),
        name="scatter",
    )(pos, ux, hs0)


def _moe_kernel(ie_ref, ib_ref, ir_ref, iv_ref, if_ref, hs_ref, wg_ref, wu_ref, wd_ref, y_ref, wgu_s, wd_s):
    w = pl.program_id(0)

    @pl.when(if_ref[w] == 1)
    def _():
        wgu_s[:, 0:D_EXPERT] = wg_ref[...].astype(bf16)
        wgu_s[:, D_EXPERT:2 * D_EXPERT] = wu_ref[...].astype(bf16)
        wd_s[...] = wd_ref[...].astype(bf16)

    @pl.when(iv_ref[w] == 1)
    def _():
        u = hs_ref[:, 0:D_MODEL].astype(bf16)
        meta = hs_ref[:, D_MODEL:ROW_W]
        gate = jnp.where(ir_ref[w] == 0, meta[:, 2:3], meta[:, 3:4])
        gu = _dot(u, wgu_s[...])
        g = gu[:, 0:D_EXPERT]
        hdn = (g * jax.nn.sigmoid(g) * gu[:, D_EXPERT:]).astype(bf16)
        y_ref[...] = _dot(hdn, wd_s[...]) * gate

    @pl.when(iv_ref[w] == 0)
    def _():
        y_ref[...] = jnp.zeros_like(y_ref)


def _moe_call(items, hs, w_gate, w_up, w_down):
    ie, ib, ir, iv, ifirst = items
    n_items = ie.shape[0]
    n_slots = hs.shape[0]
    grid_spec = pltpu.PrefetchScalarGridSpec(
        num_scalar_prefetch=5,
        grid=(n_items,),
        in_specs=[
            pl.BlockSpec((MOE_BLOCK, ROW_W), lambda w, ie, ib, ir, iv, f: (ib[w], 0)),
            pl.BlockSpec((None, D_MODEL, D_EXPERT), lambda w, ie, ib, ir, iv, f: (ie[w], 0, 0)),
            pl.BlockSpec((None, D_MODEL, D_EXPERT), lambda w, ie, ib, ir, iv, f: (ie[w], 0, 0)),
            pl.BlockSpec((None, D_EXPERT, D_MODEL), lambda w, ie, ib, ir, iv, f: (ie[w], 0, 0)),
        ],
        out_specs=pl.BlockSpec((MOE_BLOCK, D_MODEL), lambda w, ie, ib, ir, iv, f: (ib[w], ir[w])),
        scratch_shapes=[pltpu.VMEM((D_MODEL, 2 * D_EXPERT), bf16), pltpu.VMEM((D_EXPERT, D_MODEL), bf16)],
    )
    return pl.pallas_call(
        _moe_kernel,
        grid_spec=grid_spec,
        out_shape=jax.ShapeDtypeStruct((n_slots, 2 * D_MODEL), f32),
        compiler_params=pltpu.CompilerParams(
            dimension_semantics=("arbitrary",), vmem_limit_bytes=VMEM_LIMIT),
        name="moe",
    )(ie, ib, ir, iv, ifirst, hs, w_gate, w_up, w_down)


def _final_kernel(pos_ref, h_ref, gain_ref, y_hbm, o_ref, ybuf, sem):
    def start(r, c):
        pltpu.make_async_copy(y_hbm.at[pl.ds(pos_ref[r], 1)], ybuf.at[pl.ds(r, 1)], sem).start()
        return c

    lax.fori_loop(0, FINAL_TILE, start, 0, unroll=ROW_DMA_UNROLL)
    pltpu.make_async_copy(y_hbm.at[pl.ds(0, FINAL_TILE)], ybuf, sem).wait()
    h = h_ref[...] + ybuf[:, 0:D_MODEL] + ybuf[:, D_MODEL:2 * D_MODEL]
    o_ref[...] = _rms(h, gain_ref[...])


def _final_call(pos, h1, gain, y):
    t = h1.shape[0]
    return pl.pallas_call(
        _final_kernel,
        grid=(t // FINAL_TILE,),
        in_specs=[
            pl.BlockSpec((FINAL_TILE,), lambda i: (i,), memory_space=pltpu.SMEM),
            pl.BlockSpec((FINAL_TILE, D_MODEL), lambda i: (i, 0)),
            pl.BlockSpec((1, D_MODEL), lambda i: (0, 0)),
            pl.BlockSpec(memory_space=pl.ANY),
        ],
        out_specs=pl.BlockSpec((FINAL_TILE, D_MODEL), lambda i: (i, 0)),
        out_shape=jax.ShapeDtypeStruct((t, D_MODEL), f32),
        scratch_shapes=[pltpu.VMEM((FINAL_TILE, 2 * D_MODEL), f32), pltpu.SemaphoreType.DMA(())],
        compiler_params=pltpu.CompilerParams(
            dimension_semantics=("arbitrary",), vmem_limit_bytes=VMEM_LIMIT),
        name="final",
    )(pos, h1, gain, y)


def _rope_tables(pos):
    inv = ROPE_BASE ** (-jnp.arange(0, MLA_ROPE, 2, dtype=f32) / MLA_ROPE)
    ang = pos.astype(f32)[:, None] * inv[None, :]
    cos, sin = jnp.cos(ang), jnp.sin(ang)
    z = jnp.zeros((pos.shape[0], LANE - MLA_ROPE), f32)
    return jnp.concatenate([cos, cos, z], axis=1), jnp.concatenate([-sin, sin, z], axis=1)


def _relayout_weights(w_in, w_qb, w_kvb):
    half = MLA_ROPE // 2
    perm = (np.arange(MLA_ROPE) + half) % MLA_ROPE
    pts = np.cumsum((GLA_QK, GLA_QK, GLA_VW, GLA_VW, GLA_GATE_RANK, MLA_Q_RANK, MLA_KV_RANK, MLA_ROPE))
    q_g, k_g, v_g, r_g, a_l, q_lat, kv_lat, k_rope = jnp.split(w_in, pts[:-1], axis=1)
    a_seg = jnp.pad(a_l, ((0, 0), (0, LANE - GLA_GATE_RANK)))
    w_in_r = jnp.concatenate(
        [q_g, k_g, v_g, r_g, q_lat, kv_lat, k_rope, k_rope[:, perm], a_seg], axis=1).astype(bf16)
    qcols, kcols, vcols = [], [], []
    for h in range(MLA_HEADS):
        c = h * (MLA_NOPE + MLA_ROPE)
        rope = w_qb[:, c + MLA_NOPE:c + MLA_NOPE + MLA_ROPE]
        qcols += [w_qb[:, c:c + MLA_NOPE], rope, rope[:, perm]]
        c2 = h * (MLA_NOPE + MLA_V)
        kcols.append(w_kvb[:, c2:c2 + MLA_NOPE])
        vcols.append(w_kvb[:, c2 + MLA_NOPE:c2 + MLA_NOPE + MLA_V])
    return w_in_r, jnp.concatenate(qcols, axis=1).astype(bf16), jnp.concatenate(kcols + vcols, axis=1).astype(bf16)


def _route_plan(counts, bucket, rank, n_tok):
    nt = counts.shape[0]
    tot = counts.sum(axis=0)
    padded = (tot + MOE_BLOCK - 1) // MOE_BLOCK * MOE_BLOCK
    bstart = jnp.cumsum(padded) - padded
    tile_base = bstart[None, :] + jnp.cumsum(counts, axis=0) - counts
    hit = bucket.reshape(nt, -1, 1) == jnp.arange(N_BUCKETS, dtype=i32)
    pos = jnp.sum(jnp.where(hit, tile_base[:, None, :], 0), axis=-1).reshape(-1) + rank
    nb_max = (n_tok + N_BUCKETS * (MOE_BLOCK - 1)) // MOE_BLOCK
    nblk = padded // MOE_BLOCK
    bend = jnp.cumsum(nblk)
    n_blocks = bend[-1]
    blk = jnp.arange(nb_max, dtype=i32)
    bb = jnp.minimum(jnp.searchsorted(bend, blk, side="right"), N_BUCKETS - 1).astype(i32)
    valid = blk < n_blocks
    grp, pidx = bb // N_PAIRS, bb % N_PAIRS
    e_lo = grp * EXPERTS_PER_GROUP + jnp.asarray(_PAIR_LO)[pidx]
    e_hi = grp * EXPERTS_PER_GROUP + jnp.asarray(_PAIR_HI)[pidx]
    expert = jnp.concatenate([e_lo, e_hi])
    block = jnp.concatenate([blk, blk])
    role = jnp.concatenate([jnp.zeros_like(blk), jnp.ones_like(blk)])
    valid2 = jnp.concatenate([valid, valid])
    order = jnp.argsort(jnp.where(valid2, expert, N_EXPERTS), stable=True)
    expert, block, role, valid2 = expert[order], block[order], role[order], valid2[order]
    last_e = expert[jnp.maximum(2 * n_blocks - 1, 0)]
    expert = jnp.where(valid2, expert, last_e)
    first = jnp.concatenate([jnp.ones((1,), bool), expert[1:] != expert[:-1]])
    items = tuple(a.astype(i32) for a in (expert, block, role, valid2, first))
    return pos.astype(i32), items, nb_max


def kernel(x, meta_tokens, mix_norm, w_in, gla_w_a2, gla_b_a, gla_out_norm, mla_q_norm, mla_w_qb, mla_kv_norm,
           mla_w_kvb, w_out, ffn_norm, router_group_w, router_group_b, router_expert_w, router_expert_b,
           expert_w_gate, expert_w_up, expert_w_down, final_norm):
    batch, seq, d = x.shape
    assert d == D_MODEL and seq % max(PREP_TILE, GLA_TILE, ATT_TILE) == 0
    assert (batch * seq) % OUT_TILE == 0
    n_tok = batch * seq
    x2d = x.reshape(n_tok, d)

    w_in_r, w_qb_r, w_kvb_r = _relayout_weights(w_in[0], mla_w_qb[0], mla_w_kvb[0])
    mixg = mix_norm[0].reshape(1, d)
    qn = mla_q_norm[0].reshape(1, MLA_Q_RANK)
    kvn = mla_kv_norm[0].reshape(1, MLA_KV_RANK)
    ct_m, st_m = _rope_tables(jnp.arange(META_TILE))
    ct_x, st_x = _rope_tables(N_META + jnp.arange(seq))

    x_meta = jnp.pad(meta_tokens.astype(f32), ((0, META_TILE - N_META), (0, 0)))
    _, kg_m, vg_m, _, a_m, _, km_m, vmt_m = _prep_call(
        x_meta, META_TILE, META_TILE, mixg, w_in_r, qn, w_qb_r, kvn, w_kvb_r, ct_m, st_m)
    qg, kg, vg, rg, ag, qm, km, vmt = _prep_call(
        x2d, seq, PREP_TILE, mixg, w_in_r, qn, w_qb_r, kvn, w_kvb_r, ct_x, st_x)

    def chunk0(a):
        return jnp.pad(a[:N_META], ((CHUNK - N_META, 0), (0, 0)))

    wa2_p = jnp.pad(gla_w_a2[0], ((0, LANE - GLA_GATE_RANK), (0, 0))).astype(bf16)
    y_gla = _gla_call(qg, kg, vg, rg, ag, chunk0(kg_m), chunk0(vg_m), chunk0(a_m),
                      wa2_p, gla_b_a[0].reshape(1, GLA_QK), gla_out_norm[0].reshape(1, GLA_VW), batch, seq)
    y_mla = _mla_call(qm, km, vmt, km_m[:N_META], vmt_m[0, :, :N_META], batch, seq)

    wo = w_out[0].astype(bf16)
    rw = jnp.concatenate([router_group_w[0], router_expert_w[0],
                          jnp.zeros((d, LANE - N_GROUPS - N_EXPERTS), f32)], axis=1)
    rw_hi = rw.astype(bf16)
    rw_lo = (rw - rw_hi.astype(f32)).astype(bf16)
    rb = jnp.concatenate([router_group_b[0], router_expert_b[0],
                          jnp.zeros((LANE - N_GROUPS - N_EXPERTS,), f32)]).reshape(1, LANE)
    h1, ux, meta, cnt = _outproj_call(x2d, y_gla, y_mla, wo[:GLA_VW], wo[GLA_VW:], ffn_norm[0].reshape(1, d),
                                      rw_hi, rw_lo, rb)

    counts = cnt.reshape(-1, BUCKET_LANES)[:, :N_BUCKETS].astype(i32)
    pos, items, nb_max = _route_plan(counts, meta[:, 0].astype(i32), meta[:, 1].astype(i32), n_tok)
    n_slots = nb_max * MOE_BLOCK
    hs = _scatter_call(pos, ux, jnp.zeros((n_slots, ROW_W), f32))
    y = _moe_call(items, hs, expert_w_gate[0], expert_w_up[0], expert_w_down[0])
    out = _final_call(pos, h1, final_norm.reshape(1, d), y)
    return out.reshape(batch, seq, d)
```

```python
import functools

import numpy as np
import jax
import jax.numpy as jnp
from jax import lax
from jax.experimental import pallas as pl
from jax.experimental.pallas import tpu as pltpu

f32 = jnp.float32
bf16 = jnp.bfloat16
i32 = jnp.int32

D_MODEL = 1024
CHUNK = 64
N_META = 16
EPS = 1e-6
GLA_HEADS = 4
GLA_DK = 64
GLA_DV = 128
GLA_GATE_RANK = 16
GLA_TAU = 16.0
GLA_QK = GLA_HEADS * GLA_DK
GLA_VW = GLA_HEADS * GLA_DV
MLA_HEADS = 4
MLA_Q_RANK = 256
MLA_KV_RANK = 128
MLA_NOPE = 128
MLA_ROPE = 64
MLA_V = 128
MLA_OUT = MLA_HEADS * MLA_V
MLA_QK_PAD = 256
ROPE_BASE = 10000.0
N_GROUPS = 8
EXPERTS_PER_GROUP = 8
N_EXPERTS = N_GROUPS * EXPERTS_PER_GROUP
D_EXPERT = 512
N_PAIRS = EXPERTS_PER_GROUP * (EXPERTS_PER_GROUP - 1) // 2
N_BUCKETS = N_GROUPS * N_PAIRS
BUCKET_LANES = 256
LANE = 128
META_W = LANE
ROW_W = D_MODEL + META_W

PREP_TILE = 512
GLA_TILE = 512
ATT_TILE = 512
ATT_HEADS = 2
META_TILE = 128
OUT_TILE = 512
SCATTER_TILE = 256
FINAL_TILE = 256
MOE_BLOCK = 128
MOE_GROUP = 4
ROW_DMA_UNROLL = 8
VMEM_LIMIT = 56 * 1024 * 1024

C_Q, C_K, C_V, C_R = 0, 256, 512, 1024
C_QLAT, C_KVLAT, C_KROPE, C_A, C_END = 1536, 1792, 1920, 2048, 2176

_PAIR_LO = np.array([lo for lo in range(8) for hi in range(lo + 1, 8)], np.int32)
_PAIR_HI = np.array([hi for lo in range(8) for hi in range(lo + 1, 8)], np.int32)


def _dot(a, b):
    return jnp.dot(a, b, preferred_element_type=f32)


def _dot_nt(a, b):
    return lax.dot_general(a, b, (((1,), (1,)), ((), ())), preferred_element_type=f32)


def _dot_tn(a, b):
    return lax.dot_general(a, b, (((0,), (0,)), ((), ())), preferred_element_type=f32)


def _rms(x, gain):
    return x * lax.rsqrt(jnp.mean(x * x, axis=-1, keepdims=True) + EPS) * gain


def _split3(x):
    hi = x.astype(bf16)
    r1 = x - hi.astype(f32)
    mid = r1.astype(bf16)
    lo = (r1 - mid.astype(f32)).astype(bf16)
    return hi, mid, lo


def _prep_kernel(x_ref, g_ref, win_ref, qn_ref, wqb_ref, kvn_ref, wkvb_ref, ct_ref, st_ref,
                 qg_ref, kg_ref, vg_ref, rg_ref, a_ref, qm_ref, km_ref, vmt_ref):
    u = _rms(x_ref[...], g_ref[...]).astype(bf16)

    def proj(lo, hi):
        return _dot(u, win_ref[:, lo:hi])

    qg_ref[...] = proj(C_Q, C_K).astype(bf16)
    kg_ref[...] = proj(C_K, C_V).astype(bf16)
    vg_ref[...] = proj(C_V, C_R).astype(bf16)
    rg_ref[...] = proj(C_R, C_QLAT).astype(bf16)
    z = proj(C_QLAT, C_END)
    a_ref[...] = z[:, C_A - C_QLAT:].astype(bf16)
    ctab = ct_ref[...]
    stab = st_ref[...]

    def rope(seg):
        return seg * ctab + pltpu.roll(seg, 64, axis=1) * stab

    k_rope = rope(z[:, C_KROPE - C_QLAT:C_A - C_QLAT]).astype(bf16)
    qn = _rms(z[:, 0:MLA_Q_RANK], qn_ref[...]).astype(bf16)
    kvn = _rms(z[:, MLA_Q_RANK:MLA_Q_RANK + MLA_KV_RANK], kvn_ref[...]).astype(bf16)
    scale = (MLA_NOPE + MLA_ROPE) ** -0.5
    qf = _dot(qn, wqb_ref[...])
    kvf = _dot(kvn, wkvb_ref[...])
    for h in range(MLA_HEADS):
        c = h * MLA_QK_PAD
        qm_ref[:, c:c + LANE] = (qf[:, c:c + LANE] * scale).astype(bf16)
        qm_ref[:, c + LANE:c + 2 * LANE] = (rope(qf[:, c + LANE:c + 2 * LANE]) * scale).astype(bf16)
        km_ref[:, c:c + LANE] = kvf[:, h * LANE:(h + 1) * LANE].astype(bf16)
        km_ref[:, c + LANE:c + 2 * LANE] = k_rope
    vmt_ref[...] = kvf[:, MLA_HEADS * MLA_NOPE:].T.astype(bf16)


def _prep_call(x2d, rows_per_seq, tile, gain, w_in_r, q_norm, w_qb_r, kv_norm, w_kvb_r, ctab, stab):
    t = x2d.shape[0]
    nj = rows_per_seq // tile
    grid = (t // rows_per_seq, nj)

    def row(b, j):
        return (b * nj + j, 0)

    def const(b, j):
        return (0, 0)

    def tab(b, j):
        return (j, 0)

    widths = (GLA_QK, GLA_QK, GLA_VW, GLA_VW, LANE, MLA_HEADS * MLA_QK_PAD, MLA_HEADS * MLA_QK_PAD)
    return pl.pallas_call(
        _prep_kernel,
        grid=grid,
        in_specs=[
            pl.BlockSpec((tile, D_MODEL), row),
            pl.BlockSpec((1, D_MODEL), const),
            pl.BlockSpec((D_MODEL, C_END), const),
            pl.BlockSpec((1, MLA_Q_RANK), const),
            pl.BlockSpec((MLA_Q_RANK, MLA_HEADS * MLA_QK_PAD), const),
            pl.BlockSpec((1, MLA_KV_RANK), const),
            pl.BlockSpec((MLA_KV_RANK, 2 * MLA_OUT), const),
            pl.BlockSpec((tile, LANE), tab),
            pl.BlockSpec((tile, LANE), tab),
        ],
        out_specs=[pl.BlockSpec((tile, w), row) for w in widths]
        + [pl.BlockSpec((None, MLA_OUT, tile), lambda b, j: (b * nj + j, 0, 0))],
        out_shape=[jax.ShapeDtypeStruct((t, w), bf16) for w in widths]
        + [jax.ShapeDtypeStruct((t // tile, MLA_OUT, tile), bf16)],
        compiler_params=pltpu.CompilerParams(
            dimension_semantics=("parallel", "parallel"), vmem_limit_bytes=VMEM_LIMIT),
        name="prep",
    )(x2d, gain, w_in_r, q_norm, w_qb_r, kv_norm, w_kvb_r, ctab, stab)


def _gla_log_decay(a, wa2_ref, ba_ref):
    s = _dot(a, wa2_ref[...]) + ba_ref[...]
    return (jnp.minimum(s, 0.0) - jnp.log(1.0 + jnp.exp(-jnp.abs(s)))) * (1.0 / GLA_TAU)


def _gla_tile(q, k, v, la, st_ref, want_out):
    t = la.shape[0]
    nc = t // CHUNK
    ri = lax.broadcasted_iota(i32, (t, t), 0)
    ci = lax.broadcasted_iota(i32, (t, t), 1)
    tri = jnp.where((ri // CHUNK == ci // CHUNK) & (ci <= ri), 1.0, 0.0).astype(bf16)
    hi, mid, lo = _split3(la)
    b = _dot(tri, hi) + _dot(tri, mid) + _dot(tri, lo)
    b_last = [b[(c + 1) * CHUNK - 1:(c + 1) * CHUNK, :] for c in range(nc)]
    b_last_full = jnp.concatenate([jnp.broadcast_to(bl, (CHUNK, GLA_QK)) for bl in b_last], axis=0)
    kf = k.astype(f32)
    kd = (kf * jnp.exp(b_last_full - b)).astype(bf16)
    rr = lax.broadcasted_iota(i32, (GLA_VW, GLA_QK), 0) // GLA_DV
    cc = lax.broadcasted_iota(i32, (GLA_VW, GLA_QK), 1) // GLA_DK
    if want_out:
        qe = (q.astype(f32) * (GLA_DK ** -0.5) * jnp.exp(b)).astype(bf16)
        ke = kf * jnp.exp(-b)
        vf = v.astype(f32)
        lane_h = lax.broadcasted_iota(i32, (CHUNK, GLA_QK), 1) // GLA_DK
        vlane_h = lax.broadcasted_iota(i32, (CHUNK, GLA_VW), 1) // GLA_DV
        a_row = lax.broadcasted_iota(i32, (CHUNK, GLA_QK), 0)
        a_col = lax.broadcasted_iota(i32, (CHUNK, GLA_QK), 1) % CHUNK
    outs = []
    st = st_ref[...]
    for c in range(nc):
        rows = slice(c * CHUNK, (c + 1) * CHUNK)
        upd = jnp.where(rr == cc, _dot_tn(v[rows], kd[rows]), 0.0)
        if want_out:
            kbd = jnp.concatenate(
                [jnp.where(lane_h == h, ke[rows], 0.0) for h in range(GLA_HEADS)], axis=0).astype(bf16)
            att = jnp.where(a_col <= a_row, _dot_nt(qe[rows], kbd), 0.0).astype(bf16)
            vbd = jnp.concatenate(
                [jnp.where(vlane_h == h, vf[rows], 0.0) for h in range(GLA_HEADS)], axis=0).astype(bf16)
            outs.append(_dot(att, vbd) + _dot_nt(qe[rows], st.astype(bf16)))
        st = st * jnp.exp(b_last[c]) + upd
    st_ref[...] = st
    return jnp.concatenate(outs, axis=0) if want_out else None


def _gla_kernel(q_ref, k_ref, v_ref, r_ref, a_ref, km_ref, vm_ref, am_ref, wa2_ref, ba_ref, gain_ref,
                y_ref, st_ref):
    j = pl.program_id(1)

    @pl.when(j == 0)
    def _():
        st_ref[...] = jnp.zeros_like(st_ref)
        la = _gla_log_decay(am_ref[...], wa2_ref, ba_ref)
        row = lax.broadcasted_iota(i32, la.shape, 0)
        la = jnp.where(row >= CHUNK - N_META, la, 0.0)
        _gla_tile(None, km_ref[...], vm_ref[...], la, st_ref, False)

    la = _gla_log_decay(a_ref[...], wa2_ref, ba_ref)
    o = _gla_tile(q_ref[...], k_ref[...], v_ref[...], la, st_ref, True)
    r = r_ref[...].astype(f32)
    outs = []
    for h in range(GLA_HEADS):
        oh = o[:, h * GLA_DV:(h + 1) * GLA_DV]
        outs.append(oh * lax.rsqrt(jnp.mean(oh * oh, axis=-1, keepdims=True) + EPS))
    on = jnp.concatenate(outs, axis=1) * gain_ref[...]
    y_ref[...] = (on * (r * jax.nn.sigmoid(r))).astype(bf16)


def _gla_call(qg, kg, vg, rg, ag, km, vm, am, wa2_p, b_a, gain, batch, seq):
    nj = seq // GLA_TILE

    def row(b, j):
        return (b * nj + j, 0)

    def const(b, j):
        return (0, 0)

    return pl.pallas_call(
        _gla_kernel,
        grid=(batch, nj),
        in_specs=[
            pl.BlockSpec((GLA_TILE, GLA_QK), row),
            pl.BlockSpec((GLA_TILE, GLA_QK), row),
            pl.BlockSpec((GLA_TILE, GLA_VW), row),
            pl.BlockSpec((GLA_TILE, GLA_VW), row),
            pl.BlockSpec((GLA_TILE, LANE), row),
            pl.BlockSpec((CHUNK, GLA_QK), const),
            pl.BlockSpec((CHUNK, GLA_VW), const),
            pl.BlockSpec((CHUNK, LANE), const),
            pl.BlockSpec((LANE, GLA_QK), const),
            pl.BlockSpec((1, GLA_QK), const),
            pl.BlockSpec((1, GLA_VW), const),
        ],
        out_specs=pl.BlockSpec((GLA_TILE, GLA_VW), row),
        out_shape=jax.ShapeDtypeStruct((batch * seq, GLA_VW), bf16),
        scratch_shapes=[pltpu.VMEM((GLA_VW, GLA_QK), f32)],
        compiler_params=pltpu.CompilerParams(
            dimension_semantics=("parallel", "arbitrary"), vmem_limit_bytes=VMEM_LIMIT),
        name="gla",
    )(qg, kg, vg, rg, ag, km, vm, am, wa2_p, b_a, gain)


def _mla_kernel(q_ref, k_ref, vt_ref, km_ref, vmt_ref, o_ref):
    i = pl.program_id(2)
    tq = ATT_TILE
    w = MLA_QK_PAD

    def update(h, kb, vtb, carry, mask=None):
        m, l, acc = carry
        s = _dot_nt(kb, q_ref[:, h * w:(h + 1) * w])
        if mask is not None:
            s = jnp.where(mask, s, -1e30)
        m_new = jnp.maximum(m, jnp.max(s, axis=0, keepdims=True))
        alpha = jnp.exp(m - m_new)
        p = jnp.exp(s - m_new)
        l = alpha * l + jnp.sum(p, axis=0, keepdims=True)
        acc = alpha * acc + _dot(vtb, p.astype(bf16))
        return m_new, l, acc

    def body(j, carries):
        rows = pl.ds(pl.multiple_of(j * tq, tq), tq)
        return tuple(
            update(h, k_ref[rows, h * w:(h + 1) * w], vt_ref[j, h * MLA_V:(h + 1) * MLA_V, :], carries[h])
            for h in range(ATT_HEADS))

    init = tuple((jnp.full((1, tq), -1e30, f32), jnp.zeros((1, tq), f32), jnp.zeros((MLA_V, tq), f32))
                 for _ in range(ATT_HEADS))
    carries = lax.fori_loop(0, i, body, init)
    rows = pl.ds(pl.multiple_of(i * tq, tq), tq)
    kc = lax.broadcasted_iota(i32, (tq, tq), 0) // CHUNK
    qc = lax.broadcasted_iota(i32, (tq, tq), 1) // CHUNK
    mask = kc <= qc
    for h in range(ATT_HEADS):
        hv = slice(h * MLA_V, (h + 1) * MLA_V)
        carry = update(h, k_ref[rows, h * w:(h + 1) * w], vt_ref[i, hv, :], carries[h], mask)
        m, l, acc = update(h, km_ref[:, h * w:(h + 1) * w], vmt_ref[hv, :], carry)
        o_ref[:, hv] = (acc * (1.0 / l)).T.astype(bf16)


def _mla_call(qm, km, vmt, km_meta, vmt_meta, batch, seq):
    nq = seq // ATT_TILE
    nh = ATT_HEADS
    qm3 = qm.reshape(batch, seq, MLA_HEADS * MLA_QK_PAD)
    km3 = km.reshape(batch, seq, MLA_HEADS * MLA_QK_PAD)
    vt4 = vmt.reshape(batch, nq, MLA_OUT, ATT_TILE)
    out = pl.pallas_call(
        _mla_kernel,
        grid=(batch, MLA_HEADS // nh, nq),
        in_specs=[
            pl.BlockSpec((None, ATT_TILE, nh * MLA_QK_PAD), lambda b, h, i: (b, i, h)),
            pl.BlockSpec((None, seq, nh * MLA_QK_PAD), lambda b, h, i: (b, 0, h)),
            pl.BlockSpec((None, nq, nh * MLA_V, ATT_TILE), lambda b, h, i: (b, 0, h, 0)),
            pl.BlockSpec((N_META, nh * MLA_QK_PAD), lambda b, h, i: (0, h)),
            pl.BlockSpec((nh * MLA_V, N_META), lambda b, h, i: (h, 0)),
        ],
        out_specs=pl.BlockSpec((None, ATT_TILE, nh * MLA_V), lambda b, h, i: (b, i, h)),
        out_shape=jax.ShapeDtypeStruct((batch, seq, MLA_OUT), bf16),
        compiler_params=pltpu.CompilerParams(
            dimension_semantics=("parallel", "parallel", "arbitrary"), vmem_limit_bytes=VMEM_LIMIT),
        name="mla",
    )(qm3, km3, vt4, km_meta, vmt_meta)
    return out.reshape(batch * seq, MLA_OUT)


def _outproj_kernel(x_ref, yg_ref, ym_ref, wog_ref, wom_ref, gain_ref, whi_ref, wlo_ref, rb_ref,
                    h_ref, ux_ref, meta_ref, cnt_ref):
    t = OUT_TILE
    h1 = x_ref[...] + _dot(yg_ref[...], wog_ref[...]) + _dot(ym_ref[...], wom_ref[...])
    h_ref[...] = h1
    u2 = _rms(h1, gain_ref[...])
    ux_ref[:, 0:D_MODEL] = u2
    u_hi = u2.astype(bf16)
    u_lo = (u2 - u_hi.astype(f32)).astype(bf16)
    whi = whi_ref[...]
    logits = _dot(u_hi, whi) + _dot(u_lo, whi) + _dot(u_hi, wlo_ref[...]) + rb_ref[...]
    lane = lax.broadcasted_iota(i32, (t, LANE), 1)
    neg = -1e30
    g_mask = lane < N_GROUPS
    gl = jnp.where(g_mask, logits, neg)
    gmax = jnp.max(gl, axis=1, keepdims=True)
    g_sum = jnp.sum(jnp.where(g_mask, jnp.exp(gl - gmax), 0.0), axis=1, keepdims=True)
    g_p = 1.0 / g_sum
    g_idx = jnp.min(jnp.where(g_mask & (gl == gmax), lane, LANE), axis=1, keepdims=True)
    base = N_GROUPS + g_idx * EXPERTS_PER_GROUP
    e_mask = (lane >= base) & (lane < base + EXPERTS_PER_GROUP)
    el = jnp.where(e_mask, logits, neg)
    m1 = jnp.max(el, axis=1, keepdims=True)
    i1 = jnp.min(jnp.where(e_mask & (el == m1), lane, LANE), axis=1, keepdims=True)
    el2 = jnp.where(lane == i1, neg, el)
    m2 = jnp.max(el2, axis=1, keepdims=True)
    i2 = jnp.min(jnp.where(e_mask & (lane != i1) & (el2 == m2), lane, LANE), axis=1, keepdims=True)
    r = jnp.exp(m2 - m1)
    ga = g_p / (1.0 + r)
    gb = g_p * r / (1.0 + r)
    la_ = i1 - base
    lb_ = i2 - base
    lo = jnp.minimum(la_, lb_)
    hi = jnp.maximum(la_, lb_)
    g_lo = jnp.where(la_ < lb_, ga, gb)
    g_hi = jnp.where(la_ < lb_, gb, ga)
    pidx = ((lo * (2 * EXPERTS_PER_GROUP - 1 - lo)) >> 1) + (hi - lo - 1)
    bucket = g_idx * N_PAIRS + pidx
    blane = lax.broadcasted_iota(i32, (t, BUCKET_LANES), 1)
    ohf = jnp.where(blane == bucket, 1.0, 0.0)
    oh = ohf.astype(bf16)
    ri = lax.broadcasted_iota(i32, (t, t), 0)
    ci = lax.broadcasted_iota(i32, (t, t), 1)
    tri = jnp.where(ci < ri, 1.0, 0.0).astype(bf16)
    cum = _dot(tri, oh)
    rank = jnp.sum(ohf * cum, axis=1, keepdims=True)
    cnt_ref[...] = jnp.sum(ohf, axis=0, keepdims=True).reshape(1, 1, BUCKET_LANES)
    meta = jnp.where(lane == 0, bucket.astype(f32),
                     jnp.where(lane == 1, rank,
                               jnp.where(lane == 2, g_lo, jnp.where(lane == 3, g_hi, 0.0))))
    meta_ref[...] = meta
    ux_ref[:, D_MODEL:ROW_W] = meta


def _outproj_call(x2d, yg, ym, wo_g, wo_m, gain, w_hi, w_lo, rbias):
    t = x2d.shape[0]
    nt = t // OUT_TILE

    def row(i):
        return (i, 0)

    def const(i):
        return (0, 0)

    return pl.pallas_call(
        _outproj_kernel,
        grid=(nt,),
        in_specs=[
            pl.BlockSpec((OUT_TILE, D_MODEL), row),
            pl.BlockSpec((OUT_TILE, GLA_VW), row),
            pl.BlockSpec((OUT_TILE, MLA_OUT), row),
            pl.BlockSpec((GLA_VW, D_MODEL), const),
            pl.BlockSpec((MLA_OUT, D_MODEL), const),
            pl.BlockSpec((1, D_MODEL), const),
            pl.BlockSpec((D_MODEL, LANE), const),
            pl.BlockSpec((D_MODEL, LANE), const),
            pl.BlockSpec((1, LANE), const),
        ],
        out_specs=[
            pl.BlockSpec((OUT_TILE, D_MODEL), row),
            pl.BlockSpec((OUT_TILE, ROW_W), row),
            pl.BlockSpec((OUT_TILE, META_W), row),
            pl.BlockSpec((1, 1, BUCKET_LANES), lambda i: (i, 0, 0)),
        ],
        out_shape=[
            jax.ShapeDtypeStruct((t, D_MODEL), f32),
            jax.ShapeDtypeStruct((t, ROW_W), f32),
            jax.ShapeDtypeStruct((t, META_W), f32),
            jax.ShapeDtypeStruct((nt, 1, BUCKET_LANES), f32),
        ],
        compiler_params=pltpu.CompilerParams(
            dimension_semantics=("parallel",), vmem_limit_bytes=VMEM_LIMIT),
        name="outproj",
    )(x2d, yg, ym, wo_g, wo_m, gain, w_hi, w_lo, rbias)


def _scatter_kernel(pos_ref, ux_ref, hs_in_ref, hs_ref, sem):
    del hs_in_ref

    def start(r, c):
        pltpu.make_async_copy(ux_ref.at[pl.ds(r, 1)], hs_ref.at[pl.ds(pos_ref[r], 1)], sem).start()
        return c

    lax.fori_loop(0, SCATTER_TILE, start, 0, unroll=ROW_DMA_UNROLL)
    pltpu.make_async_copy(ux_ref, hs_ref.at[pl.ds(0, SCATTER_TILE)], sem).wait()


def _scatter_call(pos, ux, hs0):
    t = ux.shape[0]
    return pl.pallas_call(
        _scatter_kernel,
        grid=(t // SCATTER_TILE,),
        in_specs=[
            pl.BlockSpec((SCATTER_TILE,), lambda i: (i,), memory_space=pltpu.SMEM),
            pl.BlockSpec((SCATTER_TILE, ROW_W), lambda i: (i, 0)),
            pl.BlockSpec(memory_space=pl.ANY),
        ],
        out_specs=pl.BlockSpec(memory_space=pl.ANY),
        out_shape=jax.ShapeDtypeStruct(hs0.shape, f32),
        scratch_shapes=[pltpu.SemaphoreType.DMA(())],
        input_output_aliases={2: 0},
        compiler_params=pltpu.CompilerParams(
            dimension_semantics=("arbitrary",), vmem_limit_bytes=VMEM_LIMIT),
        name="scatter",
    )(pos, ux, hs0)


def _moe_kernel(se_ref, sf_ref, sk_ref, sb_ref, sr_ref, si_ref, so_ref,
                hs_hbm, wg_ref, wu_ref, wd_ref, y_hbm, xbuf, obuf, wgu_s, wd_s, in_sem, out_sem):
    s = pl.program_id(0)
    ns = pl.num_programs(0)
    cur = s % 2
    g_n = MOE_GROUP

    def in_copy(step, g, buf):
        rows = pl.ds(pl.multiple_of(sb_ref[step * g_n + g] * MOE_BLOCK, MOE_BLOCK), MOE_BLOCK)
        return pltpu.make_async_copy(
            hs_hbm.at[rows], xbuf.at[buf, pl.ds(g * MOE_BLOCK, MOE_BLOCK)], in_sem.at[buf])

    def out_copy(step, g, buf):
        rows = pl.ds(pl.multiple_of(sb_ref[step * g_n + g] * MOE_BLOCK, MOE_BLOCK), MOE_BLOCK)
        cols = pl.ds(pl.multiple_of(sr_ref[step * g_n + g] * D_MODEL, D_MODEL), D_MODEL)
        return pltpu.make_async_copy(
            obuf.at[buf, pl.ds(g * MOE_BLOCK, MOE_BLOCK)], y_hbm.at[rows, cols], out_sem.at[buf])

    def for_slots(step, flags_ref, fn):
        for g in range(g_n):
            @pl.when(flags_ref[step * g_n + g] == 1)
            def _():
                fn(g)

    @pl.when(s == 0)
    def _():
        xbuf[...] = jnp.zeros_like(xbuf)
        for_slots(0, si_ref, lambda g: in_copy(0, g, 0).start())

    @pl.when(s + 1 < ns)
    def _():
        for_slots(s + 1, si_ref, lambda g: in_copy(s + 1, g, 1 - cur).start())

    for_slots(s, si_ref, lambda g: in_copy(s, g, cur).wait())

    @pl.when(s >= 2)
    def _():
        for_slots(s - 2, so_ref, lambda g: out_copy(s - 2, g, cur).wait())

    @pl.when(sf_ref[s] == 1)
    def _():
        wgu_s[:, 0:D_EXPERT] = wg_ref[...].astype(bf16)
        wgu_s[:, D_EXPERT:2 * D_EXPERT] = wu_ref[...].astype(bf16)
        wd_s[...] = wd_ref[...].astype(bf16)

    @pl.when(sk_ref[s] == 1)
    def _():
        u = xbuf[cur, :, 0:D_MODEL].astype(bf16)
        meta = xbuf[cur, :, D_MODEL:ROW_W]
        gate = jnp.concatenate(
            [jnp.where(sr_ref[s * g_n + g] == 0, meta[g * MOE_BLOCK:(g + 1) * MOE_BLOCK, 2:3],
                       meta[g * MOE_BLOCK:(g + 1) * MOE_BLOCK, 3:4]) for g in range(g_n)], axis=0)
        gu = _dot(u, wgu_s[...])
        gt = gu[:, 0:D_EXPERT]
        hdn = (gt * jax.nn.sigmoid(gt) * gu[:, D_EXPERT:]).astype(bf16)
        obuf[cur] = _dot(hdn, wd_s[...]) * gate

    @pl.when(sk_ref[s] == 0)
    def _():
        obuf[cur] = jnp.zeros(obuf.shape[1:], f32)

    for_slots(s, so_ref, lambda g: out_copy(s, g, cur).start())

    @pl.when(s == ns - 1)
    def _():
        for_slots(s, so_ref, lambda g: out_copy(s, g, cur).wait())

        @pl.when(s >= 1)
        def _():
            for_slots(s - 1, so_ref, lambda g: out_copy(s - 1, g, 1 - cur).wait())


def _moe_call(plan, hs, w_gate, w_up, w_down):
    n_steps = plan[0].shape[0]
    n_slots = hs.shape[0]
    rows = MOE_GROUP * MOE_BLOCK

    def wmap(s, se, *_):
        return (se[s], 0, 0)

    grid_spec = pltpu.PrefetchScalarGridSpec(
        num_scalar_prefetch=7,
        grid=(n_steps,),
        in_specs=[
            pl.BlockSpec(memory_space=pl.ANY),
            pl.BlockSpec((None, D_MODEL, D_EXPERT), wmap),
            pl.BlockSpec((None, D_MODEL, D_EXPERT), wmap),
            pl.BlockSpec((None, D_EXPERT, D_MODEL), wmap),
        ],
        out_specs=pl.BlockSpec(memory_space=pl.ANY),
        scratch_shapes=[
            pltpu.VMEM((2, rows, ROW_W), f32),
            pltpu.VMEM((2, rows, D_MODEL), f32),
            pltpu.VMEM((D_MODEL, 2 * D_EXPERT), bf16),
            pltpu.VMEM((D_EXPERT, D_MODEL), bf16),
            pltpu.SemaphoreType.DMA((2,)),
            pltpu.SemaphoreType.DMA((2,)),
        ],
    )
    return pl.pallas_call(
        _moe_kernel,
        grid_spec=grid_spec,
        out_shape=jax.ShapeDtypeStruct((n_slots, 2 * D_MODEL), f32),
        compiler_params=pltpu.CompilerParams(
            dimension_semantics=("arbitrary",), vmem_limit_bytes=VMEM_LIMIT),
        name="moe",
    )(*plan, hs, w_gate, w_up, w_down)


def _final_kernel(pos_ref, h_ref, gain_ref, y_hbm, o_ref, ybuf, sem):
    def start(r, c):
        pltpu.make_async_copy(y_hbm.at[pl.ds(pos_ref[r], 1)], ybuf.at[pl.ds(r, 1)], sem).start()
        return c

    lax.fori_loop(0, FINAL_TILE, start, 0, unroll=ROW_DMA_UNROLL)
    pltpu.make_async_copy(y_hbm.at[pl.ds(0, FINAL_TILE)], ybuf, sem).wait()
    h = h_ref[...] + ybuf[:, 0:D_MODEL] + ybuf[:, D_MODEL:2 * D_MODEL]
    o_ref[...] = _rms(h, gain_ref[...])


def _final_call(pos, h1, gain, y):
    t = h1.shape[0]
    return pl.pallas_call(
        _final_kernel,
        grid=(t // FINAL_TILE,),
        in_specs=[
            pl.BlockSpec((FINAL_TILE,), lambda i: (i,), memory_space=pltpu.SMEM),
            pl.BlockSpec((FINAL_TILE, D_MODEL), lambda i: (i, 0)),
            pl.BlockSpec((1, D_MODEL), lambda i: (0, 0)),
            pl.BlockSpec(memory_space=pl.ANY),
        ],
        out_specs=pl.BlockSpec((FINAL_TILE, D_MODEL), lambda i: (i, 0)),
        out_shape=jax.ShapeDtypeStruct((t, D_MODEL), f32),
        scratch_shapes=[pltpu.VMEM((FINAL_TILE, 2 * D_MODEL), f32), pltpu.SemaphoreType.DMA(())],
        compiler_params=pltpu.CompilerParams(
            dimension_semantics=("arbitrary",), vmem_limit_bytes=VMEM_LIMIT),
        name="final",
    )(pos, h1, gain, y)


def _rope_tables(pos):
    inv = ROPE_BASE ** (-jnp.arange(0, MLA_ROPE, 2, dtype=f32) / MLA_ROPE)
    ang = pos.astype(f32)[:, None] * inv[None, :]
    cos, sin = jnp.cos(ang), jnp.sin(ang)
    z = jnp.zeros((pos.shape[0], LANE - MLA_ROPE), f32)
    return jnp.concatenate([cos, cos, z], axis=1), jnp.concatenate([-sin, sin, z], axis=1)


def _relayout_weights(w_in, w_qb, w_kvb):
    half = MLA_ROPE // 2
    perm = (np.arange(MLA_ROPE) + half) % MLA_ROPE
    pts = np.cumsum((GLA_QK, GLA_QK, GLA_VW, GLA_VW, GLA_GATE_RANK, MLA_Q_RANK, MLA_KV_RANK, MLA_ROPE))
    q_g, k_g, v_g, r_g, a_l, q_lat, kv_lat, k_rope = jnp.split(w_in, pts[:-1], axis=1)
    a_seg = jnp.pad(a_l, ((0, 0), (0, LANE - GLA_GATE_RANK)))
    w_in_r = jnp.concatenate(
        [q_g, k_g, v_g, r_g, q_lat, kv_lat, k_rope, k_rope[:, perm], a_seg], axis=1).astype(bf16)
    qcols, kcols, vcols = [], [], []
    for h in range(MLA_HEADS):
        c = h * (MLA_NOPE + MLA_ROPE)
        rope = w_qb[:, c + MLA_NOPE:c + MLA_NOPE + MLA_ROPE]
        qcols += [w_qb[:, c:c + MLA_NOPE], rope, rope[:, perm]]
        c2 = h * (MLA_NOPE + MLA_V)
        kcols.append(w_kvb[:, c2:c2 + MLA_NOPE])
        vcols.append(w_kvb[:, c2 + MLA_NOPE:c2 + MLA_NOPE + MLA_V])
    return w_in_r, jnp.concatenate(qcols, axis=1).astype(bf16), jnp.concatenate(kcols + vcols, axis=1).astype(bf16)


def _route_plan(counts, bucket, rank, n_tok):
    nt = counts.shape[0]
    tot = counts.sum(axis=0)
    padded = (tot + MOE_BLOCK - 1) // MOE_BLOCK * MOE_BLOCK
    bstart = jnp.cumsum(padded) - padded
    tile_base = bstart[None, :] + jnp.cumsum(counts, axis=0) - counts
    hit = bucket.reshape(nt, -1, 1) == jnp.arange(N_BUCKETS, dtype=i32)
    pos = jnp.sum(jnp.where(hit, tile_base[:, None, :], 0), axis=-1).reshape(-1) + rank
    nb_max = (n_tok + N_BUCKETS * (MOE_BLOCK - 1)) // MOE_BLOCK
    nblk = padded // MOE_BLOCK
    bend = jnp.cumsum(nblk)
    n_blocks = bend[-1]
    blk = jnp.arange(nb_max, dtype=i32)
    bb = jnp.minimum(jnp.searchsorted(bend, blk, side="right"), N_BUCKETS - 1).astype(i32)
    valid = blk < n_blocks
    grp, pidx = bb // N_PAIRS, bb % N_PAIRS
    e_lo = grp * EXPERTS_PER_GROUP + jnp.asarray(_PAIR_LO)[pidx]
    e_hi = grp * EXPERTS_PER_GROUP + jnp.asarray(_PAIR_HI)[pidx]
    expert = jnp.concatenate([e_lo, e_hi])
    block = jnp.concatenate([blk, blk])
    role = jnp.concatenate([jnp.zeros_like(blk), jnp.ones_like(blk)])
    valid2 = jnp.concatenate([valid, valid])
    order = jnp.argsort(jnp.where(valid2, expert, N_EXPERTS), stable=True)
    expert, block, role, valid2 = expert[order], block[order], role[order], valid2[order]

    g_n = MOE_GROUP
    c_e = jnp.sum((expert[:, None] == jnp.arange(N_EXPERTS, dtype=i32)) & valid2[:, None], axis=0).astype(i32)
    start_e = jnp.cumsum(c_e) - c_e
    g_e = (c_e + g_n - 1) // g_n
    gend = jnp.cumsum(g_e)
    gstart = gend - g_e
    n_compute = gend[-1]
    n_steps = (2 * nb_max + N_EXPERTS * (g_n - 1) + g_n - 1) // g_n + 1
    step = jnp.arange(n_steps, dtype=i32)
    e_of_step = jnp.minimum(jnp.searchsorted(gend, step, side="right"), N_EXPERTS - 1).astype(i32)
    is_compute = step < n_compute
    local = step - gstart[e_of_step]
    fs = jnp.repeat(step, g_n)
    fk = jnp.tile(jnp.arange(g_n, dtype=i32), n_steps)
    idx_in_e = local[fs] * g_n + fk
    valid_c = is_compute[fs] & (idx_in_e < c_e[e_of_step[fs]])
    item = jnp.clip(start_e[e_of_step[fs]] + idx_in_e, 0, 2 * nb_max - 1)
    u_idx = (fs - n_compute) * g_n + fk
    valid_f = (~is_compute[fs]) & (u_idx >= 0) & (u_idx < 2 * (nb_max - n_blocks))
    slot_block = jnp.where(valid_c, block[item], jnp.where(valid_f, n_blocks + u_idx // 2, 0))
    slot_role = jnp.where(valid_c, role[item], jnp.where(valid_f, u_idx % 2, 0))
    last_e = e_of_step[jnp.maximum(n_compute - 1, 0)]
    step_expert = jnp.where(is_compute, e_of_step, last_e)
    step_first = jnp.concatenate([jnp.ones((1,), bool), step_expert[1:] != step_expert[:-1]])
    plan = tuple(a.astype(i32) for a in
                 (step_expert, step_first, is_compute, slot_block, slot_role, valid_c, valid_c | valid_f))
    return pos.astype(i32), plan, nb_max


def kernel(x, meta_tokens, mix_norm, w_in, gla_w_a2, gla_b_a, gla_out_norm, mla_q_norm, mla_w_qb, mla_kv_norm,
           mla_w_kvb, w_out, ffn_norm, router_group_w, router_group_b, router_expert_w, router_expert_b,
           expert_w_gate, expert_w_up, expert_w_down, final_norm):
    batch, seq, d = x.shape
    assert d == D_MODEL and seq % max(PREP_TILE, GLA_TILE, ATT_TILE) == 0
    assert (batch * seq) % OUT_TILE == 0
    n_tok = batch * seq
    x2d = x.reshape(n_tok, d)

    w_in_r, w_qb_r, w_kvb_r = _relayout_weights(w_in[0], mla_w_qb[0], mla_w_kvb[0])
    mixg = mix_norm[0].reshape(1, d)
    qn = mla_q_norm[0].reshape(1, MLA_Q_RANK)
    kvn = mla_kv_norm[0].reshape(1, MLA_KV_RANK)
    ct_m, st_m = _rope_tables(jnp.arange(META_TILE))
    ct_x, st_x = _rope_tables(N_META + jnp.arange(seq))

    x_meta = jnp.pad(meta_tokens.astype(f32), ((0, META_TILE - N_META), (0, 0)))
    _, kg_m, vg_m, _, a_m, _, km_m, vmt_m = _prep_call(
        x_meta, META_TILE, META_TILE, mixg, w_in_r, qn, w_qb_r, kvn, w_kvb_r, ct_m, st_m)
    qg, kg, vg, rg, ag, qm, km, vmt = _prep_call(
        x2d, seq, PREP_TILE, mixg, w_in_r, qn, w_qb_r, kvn, w_kvb_r, ct_x, st_x)

    def chunk0(a):
        return jnp.pad(a[:N_META], ((CHUNK - N_META, 0), (0, 0)))

    wa2_p = jnp.pad(gla_w_a2[0], ((0, LANE - GLA_GATE_RANK), (0, 0))).astype(bf16)
    y_gla = _gla_call(qg, kg, vg, rg, ag, chunk0(kg_m), chunk0(vg_m), chunk0(a_m),
                      wa2_p, gla_b_a[0].reshape(1, GLA_QK), gla_out_norm[0].reshape(1, GLA_VW), batch, seq)
    y_mla = _mla_call(qm, km, vmt, km_m[:N_META], vmt_m[0, :, :N_META], batch, seq)

    wo = w_out[0].astype(bf16)
    rw = jnp.concatenate([router_group_w[0], router_expert_w[0],
                          jnp.zeros((d, LANE - N_GROUPS - N_EXPERTS), f32)], axis=1)
    rw_hi = rw.astype(bf16)
    rw_lo = (rw - rw_hi.astype(f32)).astype(bf16)
    rb = jnp.concatenate([router_group_b[0], router_expert_b[0],
                          jnp.zeros((LANE - N_GROUPS - N_EXPERTS,), f32)]).reshape(1, LANE)
    h1, ux, meta, cnt = _outproj_call(x2d, y_gla, y_mla, wo[:GLA_VW], wo[GLA_VW:], ffn_norm[0].reshape(1, d),
                                      rw_hi, rw_lo, rb)

    counts = cnt.reshape(-1, BUCKET_LANES)[:, :N_BUCKETS].astype(i32)
    pos, plan, nb_max = _route_plan(counts, meta[:, 0].astype(i32), meta[:, 1].astype(i32), n_tok)
    n_slots = nb_max * MOE_BLOCK
    hs = _scatter_call(pos, ux, jnp.zeros((n_slots, ROW_W), f32))
    y = _moe_call(plan, hs, expert_w_gate[0], expert_w_up[0], expert_w_down[0])
    out = _final_call(pos, h1, final_norm.reshape(1, d), y)
    return out.reshape(batch, seq, d)
```

```python
import functools

import numpy as np
import jax
import jax.numpy as jnp
from jax import lax
from jax.experimental import pallas as pl
from jax.experimental.pallas import tpu as pltpu

f32 = jnp.float32
bf16 = jnp.bfloat16
i32 = jnp.int32

D_MODEL = 1024
CHUNK = 64
N_META = 16
EPS = 1e-6
GLA_HEADS = 4
GLA_DK = 64
GLA_DV = 128
GLA_GATE_RANK = 16
GLA_TAU = 16.0
GLA_QK = GLA_HEADS * GLA_DK
GLA_VW = GLA_HEADS * GLA_DV
MLA_HEADS = 4
MLA_Q_RANK = 256
MLA_KV_RANK = 128
MLA_NOPE = 128
MLA_ROPE = 64
MLA_V = 128
MLA_OUT = MLA_HEADS * MLA_V
MLA_QK_PAD = 256
MLA_VA = MLA_V + 16
LOG2_E = 1.4426950408889634
ROPE_BASE = 10000.0
N_GROUPS = 8
EXPERTS_PER_GROUP = 8
N_EXPERTS = N_GROUPS * EXPERTS_PER_GROUP
D_EXPERT = 512
N_PAIRS = EXPERTS_PER_GROUP * (EXPERTS_PER_GROUP - 1) // 2
N_BUCKETS = N_GROUPS * N_PAIRS
BUCKET_LANES = 256
LANE = 128
META_W = LANE
ROW_W = D_MODEL + META_W

PREP_TILE = 512
GLA_TILE = 512
ATT_TILE = 512
ATT_HEADS = 2
META_TILE = 128
OUT_TILE = 512
SCATTER_TILE = 256
FINAL_TILE = 256
MOE_BLOCK = 128
MOE_GROUP = 4
ROW_DMA_UNROLL = 8
VMEM_LIMIT = 56 * 1024 * 1024

C_Q, C_K, C_V, C_R = 0, 256, 512, 1024
C_QLAT, C_KVLAT, C_KROPE, C_A, C_END = 1536, 1792, 1920, 2048, 2176

_PAIR_LO = np.array([lo for lo in range(8) for hi in range(lo + 1, 8)], np.int32)
_PAIR_HI = np.array([hi for lo in range(8) for hi in range(lo + 1, 8)], np.int32)


def _dot(a, b):
    return jnp.dot(a, b, preferred_element_type=f32)


def _dot_nt(a, b):
    return lax.dot_general(a, b, (((1,), (1,)), ((), ())), preferred_element_type=f32)


def _dot_tn(a, b):
    return lax.dot_general(a, b, (((0,), (0,)), ((), ())), preferred_element_type=f32)


def _rms(x, gain):
    return x * lax.rsqrt(jnp.mean(x * x, axis=-1, keepdims=True) + EPS) * gain


def _split3(x):
    hi = x.astype(bf16)
    r1 = x - hi.astype(f32)
    mid = r1.astype(bf16)
    lo = (r1 - mid.astype(f32)).astype(bf16)
    return hi, mid, lo


def _prep_kernel(x_ref, g_ref, win_ref, qn_ref, wqb_ref, kvn_ref, wkvb_ref, ct_ref, st_ref,
                 qg_ref, kg_ref, vg_ref, rg_ref, a_ref, qm_ref, km_ref, vmt_ref):
    u = _rms(x_ref[...], g_ref[...]).astype(bf16)

    def proj(lo, hi):
        return _dot(u, win_ref[:, lo:hi])

    qg_ref[...] = proj(C_Q, C_K).astype(bf16)
    kg_ref[...] = proj(C_K, C_V).astype(bf16)
    vg_ref[...] = proj(C_V, C_R).astype(bf16)
    rg_ref[...] = proj(C_R, C_QLAT).astype(bf16)
    z = proj(C_QLAT, C_END)
    a_ref[...] = z[:, C_A - C_QLAT:].astype(bf16)
    ctab = ct_ref[...]
    stab = st_ref[...]

    def rope(seg):
        return seg * ctab + pltpu.roll(seg, 64, axis=1) * stab

    k_rope = rope(z[:, C_KROPE - C_QLAT:C_A - C_QLAT]).astype(bf16)
    qn = _rms(z[:, 0:MLA_Q_RANK], qn_ref[...]).astype(bf16)
    kvn = _rms(z[:, MLA_Q_RANK:MLA_Q_RANK + MLA_KV_RANK], kvn_ref[...]).astype(bf16)
    scale = (MLA_NOPE + MLA_ROPE) ** -0.5 * LOG2_E
    qf = _dot(qn, wqb_ref[...])
    kvf = _dot(kvn, wkvb_ref[...])
    for h in range(MLA_HEADS):
        c = h * MLA_QK_PAD
        qm_ref[:, c:c + LANE] = (qf[:, c:c + LANE] * scale).astype(bf16)
        qm_ref[:, c + LANE:c + 2 * LANE] = (rope(qf[:, c + LANE:c + 2 * LANE]) * scale).astype(bf16)
        km_ref[:, c:c + LANE] = kvf[:, h * LANE:(h + 1) * LANE].astype(bf16)
        km_ref[:, c + LANE:c + 2 * LANE] = k_rope
    vt = kvf[:, MLA_HEADS * MLA_NOPE:].T
    for h in range(MLA_HEADS):
        vmt_ref[h * MLA_VA:h * MLA_VA + MLA_V, :] = vt[h * MLA_V:(h + 1) * MLA_V].astype(bf16)
        vmt_ref[h * MLA_VA + MLA_V:(h + 1) * MLA_VA, :] = jnp.ones((MLA_VA - MLA_V, vt.shape[1]), bf16)


def _prep_call(x2d, rows_per_seq, tile, gain, w_in_r, q_norm, w_qb_r, kv_norm, w_kvb_r, ctab, stab):
    t = x2d.shape[0]
    nj = rows_per_seq // tile
    grid = (t // rows_per_seq, nj)

    def row(b, j):
        return (b * nj + j, 0)

    def const(b, j):
        return (0, 0)

    def tab(b, j):
        return (j, 0)

    widths = (GLA_QK, GLA_QK, GLA_VW, GLA_VW, LANE, MLA_HEADS * MLA_QK_PAD, MLA_HEADS * MLA_QK_PAD)
    return pl.pallas_call(
        _prep_kernel,
        grid=grid,
        in_specs=[
            pl.BlockSpec((tile, D_MODEL), row),
            pl.BlockSpec((1, D_MODEL), const),
            pl.BlockSpec((D_MODEL, C_END), const),
            pl.BlockSpec((1, MLA_Q_RANK), const),
            pl.BlockSpec((MLA_Q_RANK, MLA_HEADS * MLA_QK_PAD), const),
            pl.BlockSpec((1, MLA_KV_RANK), const),
            pl.BlockSpec((MLA_KV_RANK, 2 * MLA_OUT), const),
            pl.BlockSpec((tile, LANE), tab),
            pl.BlockSpec((tile, LANE), tab),
        ],
        out_specs=[pl.BlockSpec((tile, w), row) for w in widths]
        + [pl.BlockSpec((None, MLA_HEADS * MLA_VA, tile), lambda b, j: (b * nj + j, 0, 0))],
        out_shape=[jax.ShapeDtypeStruct((t, w), bf16) for w in widths]
        + [jax.ShapeDtypeStruct((t // tile, MLA_HEADS * MLA_VA, tile), bf16)],
        compiler_params=pltpu.CompilerParams(
            dimension_semantics=("parallel", "parallel"), vmem_limit_bytes=VMEM_LIMIT),
        name="prep",
    )(x2d, gain, w_in_r, q_norm, w_qb_r, kv_norm, w_kvb_r, ctab, stab)


def _gla_log_decay(a, wa2_ref, ba_ref):
    s = _dot(a, wa2_ref[...]) + ba_ref[...]
    return (jnp.minimum(s, 0.0) - jnp.log(1.0 + jnp.exp(-jnp.abs(s)))) * (1.0 / GLA_TAU)


def _gla_tile(q, k, v, la, st_ref, want_out):
    t = la.shape[0]
    nc = t // CHUNK
    ri = lax.broadcasted_iota(i32, (t, t), 0)
    ci = lax.broadcasted_iota(i32, (t, t), 1)
    tri = jnp.where((ri // CHUNK == ci // CHUNK) & (ci <= ri), 1.0, 0.0).astype(bf16)
    hi, mid, lo = _split3(la)
    b = _dot(tri, hi) + _dot(tri, mid) + _dot(tri, lo)
    b_last = [b[(c + 1) * CHUNK - 1:(c + 1) * CHUNK, :] for c in range(nc)]
    b_last_full = jnp.concatenate([jnp.broadcast_to(bl, (CHUNK, GLA_QK)) for bl in b_last], axis=0)
    kf = k.astype(f32)
    kd = (kf * jnp.exp(b_last_full - b)).astype(bf16)
    rr = lax.broadcasted_iota(i32, (GLA_VW, GLA_QK), 0) // GLA_DV
    cc = lax.broadcasted_iota(i32, (GLA_VW, GLA_QK), 1) // GLA_DK
    if want_out:
        qe = (q.astype(f32) * (GLA_DK ** -0.5) * jnp.exp(b)).astype(bf16)
        ke = kf * jnp.exp(-b)
        vf = v.astype(f32)
        lane_h = lax.broadcasted_iota(i32, (CHUNK, GLA_QK), 1) // GLA_DK
        vlane_h = lax.broadcasted_iota(i32, (CHUNK, GLA_VW), 1) // GLA_DV
        a_row = lax.broadcasted_iota(i32, (CHUNK, GLA_QK), 0)
        a_col = lax.broadcasted_iota(i32, (CHUNK, GLA_QK), 1) % CHUNK
    outs = []
    st = st_ref[...]
    for c in range(nc):
        rows = slice(c * CHUNK, (c + 1) * CHUNK)
        upd = jnp.where(rr == cc, _dot_tn(v[rows], kd[rows]), 0.0)
        if want_out:
            kbd = jnp.concatenate(
                [jnp.where(lane_h == h, ke[rows], 0.0) for h in range(GLA_HEADS)], axis=0).astype(bf16)
            att = jnp.where(a_col <= a_row, _dot_nt(qe[rows], kbd), 0.0).astype(bf16)
            vbd = jnp.concatenate(
                [jnp.where(vlane_h == h, vf[rows], 0.0) for h in range(GLA_HEADS)], axis=0).astype(bf16)
            outs.append(_dot(att, vbd) + _dot_nt(qe[rows], st.astype(bf16)))
        st = st * jnp.exp(b_last[c]) + upd
    st_ref[...] = st
    return jnp.concatenate(outs, axis=0) if want_out else None


def _gla_kernel(q_ref, k_ref, v_ref, r_ref, a_ref, km_ref, vm_ref, am_ref, wa2_ref, ba_ref, gain_ref,
                y_ref, st_ref):
    j = pl.program_id(1)

    @pl.when(j == 0)
    def _():
        st_ref[...] = jnp.zeros_like(st_ref)
        la = _gla_log_decay(am_ref[...], wa2_ref, ba_ref)
        row = lax.broadcasted_iota(i32, la.shape, 0)
        la = jnp.where(row >= CHUNK - N_META, la, 0.0)
        _gla_tile(None, km_ref[...], vm_ref[...], la, st_ref, False)

    la = _gla_log_decay(a_ref[...], wa2_ref, ba_ref)
    o = _gla_tile(q_ref[...], k_ref[...], v_ref[...], la, st_ref, True)
    r = r_ref[...].astype(f32)
    outs = []
    for h in range(GLA_HEADS):
        oh = o[:, h * GLA_DV:(h + 1) * GLA_DV]
        outs.append(oh * lax.rsqrt(jnp.mean(oh * oh, axis=-1, keepdims=True) + EPS))
    on = jnp.concatenate(outs, axis=1) * gain_ref[...]
    y_ref[...] = (on * (r * jax.nn.sigmoid(r))).astype(bf16)


def _gla_call(qg, kg, vg, rg, ag, km, vm, am, wa2_p, b_a, gain, batch, seq):
    nj = seq // GLA_TILE

    def row(b, j):
        return (b * nj + j, 0)

    def const(b, j):
        return (0, 0)

    return pl.pallas_call(
        _gla_kernel,
        grid=(batch, nj),
        in_specs=[
            pl.BlockSpec((GLA_TILE, GLA_QK), row),
            pl.BlockSpec((GLA_TILE, GLA_QK), row),
            pl.BlockSpec((GLA_TILE, GLA_VW), row),
            pl.BlockSpec((GLA_TILE, GLA_VW), row),
            pl.BlockSpec((GLA_TILE, LANE), row),
            pl.BlockSpec((CHUNK, GLA_QK), const),
            pl.BlockSpec((CHUNK, GLA_VW), const),
            pl.BlockSpec((CHUNK, LANE), const),
            pl.BlockSpec((LANE, GLA_QK), const),
            pl.BlockSpec((1, GLA_QK), const),
            pl.BlockSpec((1, GLA_VW), const),
        ],
        out_specs=pl.BlockSpec((GLA_TILE, GLA_VW), row),
        out_shape=jax.ShapeDtypeStruct((batch * seq, GLA_VW), bf16),
        scratch_shapes=[pltpu.VMEM((GLA_VW, GLA_QK), f32)],
        compiler_params=pltpu.CompilerParams(
            dimension_semantics=("parallel", "arbitrary"), vmem_limit_bytes=VMEM_LIMIT),
        name="gla",
    )(qg, kg, vg, rg, ag, km, vm, am, wa2_p, b_a, gain)


def _mla_kernel(q_ref, k_ref, vt_ref, km_ref, vmt_ref, o_ref, sa_ref, sb_ref):
    i = pl.program_id(2)
    tq = ATT_TILE
    w = MLA_QK_PAD
    va = MLA_VA
    heads = range(ATT_HEADS)

    def scores(h, blk):
        rows = pl.ds(pl.multiple_of(blk * tq, tq), tq)
        return _dot_nt(k_ref[rows, h * w:(h + 1) * w], q_ref[:, h * w:(h + 1) * w])

    def soft(s, vtb, carry, mask=None):
        m, acc = carry
        if mask is not None:
            s = jnp.where(mask, s, -1e30)
        m_new = jnp.maximum(m, jnp.max(s, axis=0, keepdims=True))
        p = jnp.exp2(s - m_new).astype(bf16)
        return m_new, jnp.exp2(m - m_new) * acc + _dot(vtb, p)

    def vt(h, blk):
        return vt_ref[blk, h * va:(h + 1) * va, :]

    def finish(h, carry):
        s = _dot_nt(km_ref[:, h * w:(h + 1) * w], q_ref[:, h * w:(h + 1) * w])
        m, acc = soft(s, vmt_ref[h * va:(h + 1) * va, :], carry)
        o_ref[:, h * MLA_V:(h + 1) * MLA_V] = (acc[:MLA_V] * (1.0 / acc[MLA_V:MLA_V + 1])).T.astype(bf16)

    kc = lax.broadcasted_iota(i32, (tq, tq), 0) // CHUNK
    qc = lax.broadcasted_iota(i32, (tq, tq), 1) // CHUNK
    mask = kc <= qc

    for h in heads:
        sa_ref[h] = scores(h, 0)

    def pair(p, carries):
        b0 = 2 * p
        for h in heads:
            sb_ref[h] = scores(h, b0 + 1)
        carries = [soft(sa_ref[h], vt(h, b0), carries[h]) for h in heads]
        for h in heads:
            sa_ref[h] = scores(h, b0 + 2)
        return tuple(soft(sb_ref[h], vt(h, b0 + 1), carries[h]) for h in heads)

    init = tuple((jnp.full((1, tq), -1e30, f32), jnp.zeros((va, tq), f32)) for _ in heads)
    carries = lax.fori_loop(0, i // 2, pair, init)

    @pl.when(i % 2 == 1)
    def _():
        for h in heads:
            sb_ref[h] = scores(h, i)
        for h in heads:
            c = soft(sa_ref[h], vt(h, i - 1), carries[h])
            finish(h, soft(sb_ref[h], vt(h, i), c, mask))

    @pl.when(i % 2 == 0)
    def _():
        for h in heads:
            finish(h, soft(sa_ref[h], vt(h, i), carries[h], mask))


def _mla_call(qm, km, vmt, km_meta, vmt_meta, batch, seq):
    nq = seq // ATT_TILE
    nh = ATT_HEADS
    qm3 = qm.reshape(batch, seq, MLA_HEADS * MLA_QK_PAD)
    km3 = km.reshape(batch, seq, MLA_HEADS * MLA_QK_PAD)
    vt4 = vmt.reshape(batch, nq, MLA_HEADS * MLA_VA, ATT_TILE)
    out = pl.pallas_call(
        _mla_kernel,
        grid=(batch, MLA_HEADS // nh, nq),
        in_specs=[
            pl.BlockSpec((None, ATT_TILE, nh * MLA_QK_PAD), lambda b, h, i: (b, i, h)),
            pl.BlockSpec((None, seq, nh * MLA_QK_PAD), lambda b, h, i: (b, 0, h)),
            pl.BlockSpec((None, nq, nh * MLA_VA, ATT_TILE), lambda b, h, i: (b, 0, h, 0)),
            pl.BlockSpec((N_META, nh * MLA_QK_PAD), lambda b, h, i: (0, h)),
            pl.BlockSpec((nh * MLA_VA, N_META), lambda b, h, i: (h, 0)),
        ],
        out_specs=pl.BlockSpec((None, ATT_TILE, nh * MLA_V), lambda b, h, i: (b, i, h)),
        out_shape=jax.ShapeDtypeStruct((batch, seq, MLA_OUT), bf16),
        scratch_shapes=[pltpu.VMEM((nh, ATT_TILE, ATT_TILE), f32), pltpu.VMEM((nh, ATT_TILE, ATT_TILE), f32)],
        compiler_params=pltpu.CompilerParams(
            dimension_semantics=("parallel", "parallel", "arbitrary"), vmem_limit_bytes=VMEM_LIMIT),
        name="mla",
    )(qm3, km3, vt4, km_meta, vmt_meta)
    return out.reshape(batch * seq, MLA_OUT)


def _outproj_kernel(x_ref, yg_ref, ym_ref, wog_ref, wom_ref, gain_ref, whi_ref, wlo_ref, rb_ref,
                    h_ref, ux_ref, meta_ref, cnt_ref):
    t = OUT_TILE
    h1 = x_ref[...] + _dot(yg_ref[...], wog_ref[...]) + _dot(ym_ref[...], wom_ref[...])
    h_ref[...] = h1
    u2 = _rms(h1, gain_ref[...])
    ux_ref[:, 0:D_MODEL] = u2
    u_hi = u2.astype(bf16)
    u_lo = (u2 - u_hi.astype(f32)).astype(bf16)
    whi = whi_ref[...]
    logits = _dot(u_hi, whi) + _dot(u_lo, whi) + _dot(u_hi, wlo_ref[...]) + rb_ref[...]
    lane = lax.broadcasted_iota(i32, (t, LANE), 1)
    neg = -1e30
    g_mask = lane < N_GROUPS
    gl = jnp.where(g_mask, logits, neg)
    gmax = jnp.max(gl, axis=1, keepdims=True)
    g_sum = jnp.sum(jnp.where(g_mask, jnp.exp(gl - gmax), 0.0), axis=1, keepdims=True)
    g_p = 1.0 / g_sum
    g_idx = jnp.min(jnp.where(g_mask & (gl == gmax), lane, LANE), axis=1, keepdims=True)
    base = N_GROUPS + g_idx * EXPERTS_PER_GROUP
    e_mask = (lane >= base) & (lane < base + EXPERTS_PER_GROUP)
    el = jnp.where(e_mask, logits, neg)
    m1 = jnp.max(el, axis=1, keepdims=True)
    i1 = jnp.min(jnp.where(e_mask & (el == m1), lane, LANE), axis=1, keepdims=True)
    el2 = jnp.where(lane == i1, neg, el)
    m2 = jnp.max(el2, axis=1, keepdims=True)
    i2 = jnp.min(jnp.where(e_mask & (lane != i1) & (el2 == m2), lane, LANE), axis=1, keepdims=True)
    r = jnp.exp(m2 - m1)
    ga = g_p / (1.0 + r)
    gb = g_p * r / (1.0 + r)
    la_ = i1 - base
    lb_ = i2 - base
    lo = jnp.minimum(la_, lb_)
    hi = jnp.maximum(la_, lb_)
    g_lo = jnp.where(la_ < lb_, ga, gb)
    g_hi = jnp.where(la_ < lb_, gb, ga)
    pidx = ((lo * (2 * EXPERTS_PER_GROUP - 1 - lo)) >> 1) + (hi - lo - 1)
    bucket = g_idx * N_PAIRS + pidx
    blane = lax.broadcasted_iota(i32, (t, BUCKET_LANES), 1)
    ohf = jnp.where(blane == bucket, 1.0, 0.0)
    oh = ohf.astype(bf16)
    ri = lax.broadcasted_iota(i32, (t, t), 0)
    ci = lax.broadcasted_iota(i32, (t, t), 1)
    tri = jnp.where(ci < ri, 1.0, 0.0).astype(bf16)
    cum = _dot(tri, oh)
    rank = jnp.sum(ohf * cum, axis=1, keepdims=True)
    cnt_ref[...] = jnp.sum(ohf, axis=0, keepdims=True).reshape(1, 1, BUCKET_LANES)
    meta = jnp.where(lane == 0, bucket.astype(f32),
                     jnp.where(lane == 1, rank,
                               jnp.where(lane == 2, g_lo, jnp.where(lane == 3, g_hi, 0.0))))
    meta_ref[...] = meta
    ux_ref[:, D_MODEL:ROW_W] = meta


def _outproj_call(x2d, yg, ym, wo_g, wo_m, gain, w_hi, w_lo, rbias):
    t = x2d.shape[0]
    nt = t // OUT_TILE

    def row(i):
        return (i, 0)

    def const(i):
        return (0, 0)

    return pl.pallas_call(
        _outproj_kernel,
        grid=(nt,),
        in_specs=[
            pl.BlockSpec((OUT_TILE, D_MODEL), row),
            pl.BlockSpec((OUT_TILE, GLA_VW), row),
            pl.BlockSpec((OUT_TILE, MLA_OUT), row),
            pl.BlockSpec((GLA_VW, D_MODEL), const),
            pl.BlockSpec((MLA_OUT, D_MODEL), const),
            pl.BlockSpec((1, D_MODEL), const),
            pl.BlockSpec((D_MODEL, LANE), const),
            pl.BlockSpec((D_MODEL, LANE), const),
            pl.BlockSpec((1, LANE), const),
        ],
        out_specs=[
            pl.BlockSpec((OUT_TILE, D_MODEL), row),
            pl.BlockSpec((OUT_TILE, ROW_W), row),
            pl.BlockSpec((OUT_TILE, META_W), row),
            pl.BlockSpec((1, 1, BUCKET_LANES), lambda i: (i, 0, 0)),
        ],
        out_shape=[
            jax.ShapeDtypeStruct((t, D_MODEL), f32),
            jax.ShapeDtypeStruct((t, ROW_W), f32),
            jax.ShapeDtypeStruct((t, META_W), f32),
            jax.ShapeDtypeStruct((nt, 1, BUCKET_LANES), f32),
        ],
        compiler_params=pltpu.CompilerParams(
            dimension_semantics=("parallel",), vmem_limit_bytes=VMEM_LIMIT),
        name="outproj",
    )(x2d, yg, ym, wo_g, wo_m, gain, w_hi, w_lo, rbias)


def _scatter_kernel(pos_ref, ux_ref, hs_in_ref, hs_ref, sem):
    del hs_in_ref

    def start(r, c):
        pltpu.make_async_copy(ux_ref.at[pl.ds(r, 1)], hs_ref.at[pl.ds(pos_ref[r], 1)], sem).start()
        return c

    lax.fori_loop(0, SCATTER_TILE, start, 0, unroll=ROW_DMA_UNROLL)
    pltpu.make_async_copy(ux_ref, hs_ref.at[pl.ds(0, SCATTER_TILE)], sem).wait()


def _scatter_call(pos, ux, hs0):
    t = ux.shape[0]
    return pl.pallas_call(
        _scatter_kernel,
        grid=(t // SCATTER_TILE,),
        in_specs=[
            pl.BlockSpec((SCATTER_TILE,), lambda i: (i,), memory_space=pltpu.SMEM),
            pl.BlockSpec((SCATTER_TILE, ROW_W), lambda i: (i, 0)),
            pl.BlockSpec(memory_space=pl.ANY),
        ],
        out_specs=pl.BlockSpec(memory_space=pl.ANY),
        out_shape=jax.ShapeDtypeStruct(hs0.shape, f32),
        scratch_shapes=[pltpu.SemaphoreType.DMA(())],
        input_output_aliases={2: 0},
        compiler_params=pltpu.CompilerParams(
            dimension_semantics=("arbitrary",), vmem_limit_bytes=VMEM_LIMIT),
        name="scatter",
    )(pos, ux, hs0)


def _moe_kernel(se_ref, sf_ref, sk_ref, sb_ref, sr_ref, si_ref, so_ref,
                hs_hbm, wg_ref, wu_ref, wd_ref, y_hbm, xbuf, obuf, wgu_s, wd_s, in_sem, out_sem):
    s = pl.program_id(0)
    ns = pl.num_programs(0)
    cur = s % 2
    g_n = MOE_GROUP

    def in_copy(step, g, buf):
        rows = pl.ds(pl.multiple_of(sb_ref[step * g_n + g] * MOE_BLOCK, MOE_BLOCK), MOE_BLOCK)
        return pltpu.make_async_copy(
            hs_hbm.at[rows], xbuf.at[buf, pl.ds(g * MOE_BLOCK, MOE_BLOCK)], in_sem.at[buf])

    def out_copy(step, g, buf):
        rows = pl.ds(pl.multiple_of(sb_ref[step * g_n + g] * MOE_BLOCK, MOE_BLOCK), MOE_BLOCK)
        cols = pl.ds(pl.multiple_of(sr_ref[step * g_n + g] * D_MODEL, D_MODEL), D_MODEL)
        return pltpu.make_async_copy(
            obuf.at[buf, pl.ds(g * MOE_BLOCK, MOE_BLOCK)], y_hbm.at[rows, cols], out_sem.at[buf])

    def for_slots(step, flags_ref, fn):
        for g in range(g_n):
            @pl.when(flags_ref[step * g_n + g] == 1)
            def _():
                fn(g)

    @pl.when(s == 0)
    def _():
        xbuf[...] = jnp.zeros_like(xbuf)
        for_slots(0, si_ref, lambda g: in_copy(0, g, 0).start())

    @pl.when(s + 1 < ns)
    def _():
        for_slots(s + 1, si_ref, lambda g: in_copy(s + 1, g, 1 - cur).start())

    for_slots(s, si_ref, lambda g: in_copy(s, g, cur).wait())

    @pl.when(s >= 2)
    def _():
        for_slots(s - 2, so_ref, lambda g: out_copy(s - 2, g, cur).wait())

    @pl.when(sf_ref[s] == 1)
    def _():
        wgu_s[:, 0:D_EXPERT] = wg_ref[...].astype(bf16)
        wgu_s[:, D_EXPERT:2 * D_EXPERT] = wu_ref[...].astype(bf16)
        wd_s[...] = wd_ref[...].astype(bf16)

    @pl.when(sk_ref[s] == 1)
    def _():
        u = xbuf[cur, :, 0:D_MODEL].astype(bf16)
        meta = xbuf[cur, :, D_MODEL:ROW_W]
        gate = jnp.concatenate(
            [jnp.where(sr_ref[s * g_n + g] == 0, meta[g * MOE_BLOCK:(g + 1) * MOE_BLOCK, 2:3],
                       meta[g * MOE_BLOCK:(g + 1) * MOE_BLOCK, 3:4]) for g in range(g_n)], axis=0)
        gu = _dot(u, wgu_s[...])
        gt = gu[:, 0:D_EXPERT]
        hdn = (gt * jax.nn.sigmoid(gt) * gu[:, D_EXPERT:]).astype(bf16)
        obuf[cur] = _dot(hdn, wd_s[...]) * gate

    @pl.when(sk_ref[s] == 0)
    def _():
        obuf[cur] = jnp.zeros(obuf.shape[1:], f32)

    for_slots(s, so_ref, lambda g: out_copy(s, g, cur).start())

    @pl.when(s == ns - 1)
    def _():
        for_slots(s, so_ref, lambda g: out_copy(s, g, cur).wait())

        @pl.when(s >= 1)
        def _():
            for_slots(s - 1, so_ref, lambda g: out_copy(s - 1, g, 1 - cur).wait())


def _moe_call(plan, hs, w_gate, w_up, w_down):
    n_steps = plan[0].shape[0]
    n_slots = hs.shape[0]
    rows = MOE_GROUP * MOE_BLOCK

    def wmap(s, se, *_):
        return (se[s], 0, 0)

    grid_spec = pltpu.PrefetchScalarGridSpec(
        num_scalar_prefetch=7,
        grid=(n_steps,),
        in_specs=[
            pl.BlockSpec(memory_space=pl.ANY),
            pl.BlockSpec((None, D_MODEL, D_EXPERT), wmap),
            pl.BlockSpec((None, D_MODEL, D_EXPERT), wmap),
            pl.BlockSpec((None, D_EXPERT, D_MODEL), wmap),
        ],
        out_specs=pl.BlockSpec(memory_space=pl.ANY),
        scratch_shapes=[
            pltpu.VMEM((2, rows, ROW_W), f32),
            pltpu.VMEM((2, rows, D_MODEL), f32),
            pltpu.VMEM((D_MODEL, 2 * D_EXPERT), bf16),
            pltpu.VMEM((D_EXPERT, D_MODEL), bf16),
            pltpu.SemaphoreType.DMA((2,)),
            pltpu.SemaphoreType.DMA((2,)),
        ],
    )
    return pl.pallas_call(
        _moe_kernel,
        grid_spec=grid_spec,
        out_shape=jax.ShapeDtypeStruct((n_slots, 2 * D_MODEL), f32),
        compiler_params=pltpu.CompilerParams(
            dimension_semantics=("arbitrary",), vmem_limit_bytes=VMEM_LIMIT),
        name="moe",
    )(*plan, hs, w_gate, w_up, w_down)


def _final_kernel(pos_ref, h_ref, gain_ref, y_hbm, o_ref, ybuf, sem):
    def start(r, c):
        pltpu.make_async_copy(y_hbm.at[pl.ds(pos_ref[r], 1)], ybuf.at[pl.ds(r, 1)], sem).start()
        return c

    lax.fori_loop(0, FINAL_TILE, start, 0, unroll=ROW_DMA_UNROLL)
    pltpu.make_async_copy(y_hbm.at[pl.ds(0, FINAL_TILE)], ybuf, sem).wait()
    h = h_ref[...] + ybuf[:, 0:D_MODEL] + ybuf[:, D_MODEL:2 * D_MODEL]
    o_ref[...] = _rms(h, gain_ref[...])


def _final_call(pos, h1, gain, y):
    t = h1.shape[0]
    return pl.pallas_call(
        _final_kernel,
        grid=(t // FINAL_TILE,),
        in_specs=[
            pl.BlockSpec((FINAL_TILE,), lambda i: (i,), memory_space=pltpu.SMEM),
            pl.BlockSpec((FINAL_TILE, D_MODEL), lambda i: (i, 0)),
            pl.BlockSpec((1, D_MODEL), lambda i: (0, 0)),
            pl.BlockSpec(memory_space=pl.ANY),
        ],
        out_specs=pl.BlockSpec((FINAL_TILE, D_MODEL), lambda i: (i, 0)),
        out_shape=jax.ShapeDtypeStruct((t, D_MODEL), f32),
        scratch_shapes=[pltpu.VMEM((FINAL_TILE, 2 * D_MODEL), f32), pltpu.SemaphoreType.DMA(())],
        compiler_params=pltpu.CompilerParams(
            dimension_semantics=("arbitrary",), vmem_limit_bytes=VMEM_LIMIT),
        name="final",
    )(pos, h1, gain, y)


def _rope_tables(pos):
    inv = ROPE_BASE ** (-jnp.arange(0, MLA_ROPE, 2, dtype=f32) / MLA_ROPE)
    ang = pos.astype(f32)[:, None] * inv[None, :]
    cos, sin = jnp.cos(ang), jnp.sin(ang)
    z = jnp.zeros((pos.shape[0], LANE - MLA_ROPE), f32)
    return jnp.concatenate([cos, cos, z], axis=1), jnp.concatenate([-sin, sin, z], axis=1)


def _relayout_weights(w_in, w_qb, w_kvb):
    half = MLA_ROPE // 2
    perm = (np.arange(MLA_ROPE) + half) % MLA_ROPE
    pts = np.cumsum((GLA_QK, GLA_QK, GLA_VW, GLA_VW, GLA_GATE_RANK, MLA_Q_RANK, MLA_KV_RANK, MLA_ROPE))
    q_g, k_g, v_g, r_g, a_l, q_lat, kv_lat, k_rope = jnp.split(w_in, pts[:-1], axis=1)
    a_seg = jnp.pad(a_l, ((0, 0), (0, LANE - GLA_GATE_RANK)))
    w_in_r = jnp.concatenate(
        [q_g, k_g, v_g, r_g, q_lat, kv_lat, k_rope, k_rope[:, perm], a_seg], axis=1).astype(bf16)
    qcols, kcols, vcols = [], [], []
    for h in range(MLA_HEADS):
        c = h * (MLA_NOPE + MLA_ROPE)
        rope = w_qb[:, c + MLA_NOPE:c + MLA_NOPE + MLA_ROPE]
        qcols += [w_qb[:, c:c + MLA_NOPE], rope, rope[:, perm]]
        c2 = h * (MLA_NOPE + MLA_V)
        kcols.append(w_kvb[:, c2:c2 + MLA_NOPE])
        vcols.append(w_kvb[:, c2 + MLA_NOPE:c2 + MLA_NOPE + MLA_V])
    return w_in_r, jnp.concatenate(qcols, axis=1).astype(bf16), jnp.concatenate(kcols + vcols, axis=1).astype(bf16)


_BUCKET_GROUP = np.arange(N_BUCKETS) // N_PAIRS
_RUN_EXPERT = np.concatenate([_BUCKET_GROUP * EXPERTS_PER_GROUP + _PAIR_LO[np.arange(N_BUCKETS) % N_PAIRS],
                              _BUCKET_GROUP * EXPERTS_PER_GROUP + _PAIR_HI[np.arange(N_BUCKETS) % N_PAIRS]])
_RUN_IS_EXPERT = (_RUN_EXPERT[:, None] == np.arange(N_EXPERTS)[None, :]).astype(np.int32)
_RUN_BEFORE = ((_RUN_EXPERT[:, None] == _RUN_EXPERT[None, :])
               & (np.arange(2 * N_BUCKETS)[None, :] < np.arange(2 * N_BUCKETS)[:, None])).astype(np.int32)


def _route_plan(counts, bucket, rank, n_tok):
    nt = counts.shape[0]
    g_n = MOE_GROUP
    tot = counts.sum(axis=0)
    nblk = (tot + MOE_BLOCK - 1) // MOE_BLOCK
    bstart_blk = jnp.cumsum(nblk) - nblk
    n_blocks = jnp.sum(nblk)
    tile_base = bstart_blk[None, :] * MOE_BLOCK + jnp.cumsum(counts, axis=0) - counts
    hit = bucket.reshape(nt, -1, 1) == jnp.arange(N_BUCKETS, dtype=i32)
    pos = jnp.sum(jnp.where(hit, tile_base[:, None, :], 0), axis=-1).reshape(-1) + rank
    nb_max = (n_tok + N_BUCKETS * (MOE_BLOCK - 1)) // MOE_BLOCK

    n_run = jnp.concatenate([nblk, nblk])
    b0_run = jnp.concatenate([bstart_blk, bstart_blk])
    c_e = jnp.sum(n_run[:, None] * _RUN_IS_EXPERT, axis=0)
    g_e = (c_e + g_n - 1) // g_n
    gend = jnp.cumsum(g_e)
    gstart = gend - g_e
    n_compute = gend[-1]
    off_run = jnp.sum(_RUN_BEFORE * n_run[None, :], axis=1)
    f_run = jnp.sum(_RUN_IS_EXPERT * gstart[None, :], axis=1) * g_n + off_run

    n_steps = (2 * nb_max + N_EXPERTS * (g_n - 1) + g_n - 1) // g_n + 1
    f = jnp.arange(n_steps * g_n, dtype=i32)
    in_run = (f[:, None] >= f_run[None, :]) & (f[:, None] < (f_run + n_run)[None, :])
    valid_c = jnp.any(in_run, axis=1)
    block_c = jnp.sum(jnp.where(in_run, b0_run[None, :] + f[:, None] - f_run[None, :], 0), axis=1)
    role_c = jnp.sum(jnp.where(in_run[:, N_BUCKETS:], 1, 0), axis=1)
    u_idx = f - n_compute * g_n
    valid_f = (u_idx >= 0) & (u_idx < 2 * (nb_max - n_blocks))
    slot_block = jnp.where(valid_c, block_c, jnp.where(valid_f, n_blocks + u_idx // 2, 0))
    slot_role = jnp.where(valid_c, role_c, jnp.where(valid_f, u_idx % 2, 0))

    step = jnp.arange(n_steps, dtype=i32)
    e_of_step = jnp.minimum(jnp.sum(gend[None, :] <= step[:, None], axis=1), N_EXPERTS - 1)
    is_compute = step < n_compute
    last_e = jnp.max(jnp.where(is_compute, e_of_step, 0))
    step_expert = jnp.where(is_compute, e_of_step, last_e)
    step_first = jnp.concatenate([jnp.ones((1,), bool), step_expert[1:] != step_expert[:-1]])
    plan = tuple(a.astype(i32) for a in
                 (step_expert, step_first, is_compute, slot_block, slot_role, valid_c, valid_c | valid_f))
    return pos.astype(i32), plan, nb_max


def kernel(x, meta_tokens, mix_norm, w_in, gla_w_a2, gla_b_a, gla_out_norm, mla_q_norm, mla_w_qb, mla_kv_norm,
           mla_w_kvb, w_out, ffn_norm, router_group_w, router_group_b, router_expert_w, router_expert_b,
           expert_w_gate, expert_w_up, expert_w_down, final_norm):
    batch, seq, d = x.shape
    assert d == D_MODEL and seq % max(PREP_TILE, GLA_TILE, ATT_TILE) == 0
    assert (batch * seq) % OUT_TILE == 0
    n_tok = batch * seq
    x2d = x.reshape(n_tok, d)

    w_in_r, w_qb_r, w_kvb_r = _relayout_weights(w_in[0], mla_w_qb[0], mla_w_kvb[0])
    mixg = mix_norm[0].reshape(1, d)
    qn = mla_q_norm[0].reshape(1, MLA_Q_RANK)
    kvn = mla_kv_norm[0].reshape(1, MLA_KV_RANK)
    ct_m, st_m = _rope_tables(jnp.arange(META_TILE))
    ct_x, st_x = _rope_tables(N_META + jnp.arange(seq))

    x_meta = jnp.pad(meta_tokens.astype(f32), ((0, META_TILE - N_META), (0, 0)))
    _, kg_m, vg_m, _, a_m, _, km_m, vmt_m = _prep_call(
        x_meta, META_TILE, META_TILE, mixg, w_in_r, qn, w_qb_r, kvn, w_kvb_r, ct_m, st_m)
    qg, kg, vg, rg, ag, qm, km, vmt = _prep_call(
        x2d, seq, PREP_TILE, mixg, w_in_r, qn, w_qb_r, kvn, w_kvb_r, ct_x, st_x)

    def chunk0(a):
        return jnp.pad(a[:N_META], ((CHUNK - N_META, 0), (0, 0)))

    wa2_p = jnp.pad(gla_w_a2[0], ((0, LANE - GLA_GATE_RANK), (0, 0))).astype(bf16)
    y_gla = _gla_call(qg, kg, vg, rg, ag, chunk0(kg_m), chunk0(vg_m), chunk0(a_m),
                      wa2_p, gla_b_a[0].reshape(1, GLA_QK), gla_out_norm[0].reshape(1, GLA_VW), batch, seq)
    y_mla = _mla_call(qm, km, vmt, km_m[:N_META], vmt_m[0, :, :N_META], batch, seq)

    wo = w_out[0].astype(bf16)
    rw = jnp.concatenate([router_group_w[0], router_expert_w[0],
                          jnp.zeros((d, LANE - N_GROUPS - N_EXPERTS), f32)], axis=1)
    rw_hi = rw.astype(bf16)
    rw_lo = (rw - rw_hi.astype(f32)).astype(bf16)
    rb = jnp.concatenate([router_group_b[0], router_expert_b[0],
                          jnp.zeros((LANE - N_GROUPS - N_EXPERTS,), f32)]).reshape(1, LANE)
    h1, ux, meta, cnt = _outproj_call(x2d, y_gla, y_mla, wo[:GLA_VW], wo[GLA_VW:], ffn_norm[0].reshape(1, d),
                                      rw_hi, rw_lo, rb)

    counts = cnt.reshape(-1, BUCKET_LANES)[:, :N_BUCKETS].astype(i32)
    pos, plan, nb_max = _route_plan(counts, meta[:, 0].astype(i32), meta[:, 1].astype(i32), n_tok)
    n_slots = nb_max * MOE_BLOCK
    hs = _scatter_call(pos, ux, jnp.zeros((n_slots, ROW_W), f32))
    y = _moe_call(plan, hs, expert_w_gate[0], expert_w_up[0], expert_w_down[0])
    out = _final_call(pos, h1, final_norm.reshape(1, d), y)
    return out.reshape(batch, seq, d)
```

```python
import functools

import numpy as np
import jax
import jax.numpy as jnp
from jax import lax
from jax.experimental import pallas as pl
from jax.experimental.pallas import tpu as pltpu

f32 = jnp.float32
bf16 = jnp.bfloat16
i32 = jnp.int32

D_MODEL = 1024
CHUNK = 64
N_META = 16
EPS = 1e-6
GLA_HEADS = 4
GLA_DK = 64
GLA_DV = 128
GLA_GATE_RANK = 16
GLA_TAU = 16.0
GLA_QK = GLA_HEADS * GLA_DK
GLA_VW = GLA_HEADS * GLA_DV
MLA_HEADS = 4
MLA_Q_RANK = 256
MLA_KV_RANK = 128
MLA_NOPE = 128
MLA_ROPE = 64
MLA_V = 128
MLA_OUT = MLA_HEADS * MLA_V
MLA_QK_PAD = 256
MLA_VA = MLA_V + 16
LOG2_E = 1.4426950408889634
ROPE_BASE = 10000.0
N_GROUPS = 8
EXPERTS_PER_GROUP = 8
N_EXPERTS = N_GROUPS * EXPERTS_PER_GROUP
D_EXPERT = 512
N_PAIRS = EXPERTS_PER_GROUP * (EXPERTS_PER_GROUP - 1) // 2
N_BUCKETS = N_GROUPS * N_PAIRS
BUCKET_LANES = 256
LANE = 128
SUBLANES = 8
META_W = LANE
ROW_W = D_MODEL + META_W

PREP_TILE = 512
GLA_TILE = 512
ATT_TILE = 512
ATT_HEADS = 2
META_TILE = 128
OUT_TILE = 512
SCATTER_TILE = 256
FINAL_TILE = 256
MOE_BLOCK = 128
MOE_GROUP = 4
VMEM_LIMIT = 56 * 1024 * 1024

C_Q, C_K, C_V, C_R = 0, 256, 512, 1024
C_QLAT, C_KVLAT, C_KROPE, C_A, C_END = 1536, 1792, 1920, 2048, 2176

_PAIR_LO = np.array([lo for lo in range(8) for hi in range(lo + 1, 8)], np.int32)
_PAIR_HI = np.array([hi for lo in range(8) for hi in range(lo + 1, 8)], np.int32)


def _dot(a, b):
    return jnp.dot(a, b, preferred_element_type=f32)


def _dot_nt(a, b):
    return lax.dot_general(a, b, (((1,), (1,)), ((), ())), preferred_element_type=f32)


def _dot_tn(a, b):
    return lax.dot_general(a, b, (((0,), (0,)), ((), ())), preferred_element_type=f32)


def _rms(x, gain):
    return x * lax.rsqrt(jnp.mean(x * x, axis=-1, keepdims=True) + EPS) * gain


def _split3(x):
    hi = x.astype(bf16)
    r1 = x - hi.astype(f32)
    mid = r1.astype(bf16)
    lo = (r1 - mid.astype(f32)).astype(bf16)
    return hi, mid, lo


def _prep_kernel(x_ref, g_ref, win_ref, qn_ref, wqb_ref, kvn_ref, wkvb_ref, ct_ref, st_ref,
                 qg_ref, kg_ref, vg_ref, rg_ref, a_ref, qm_ref, km_ref, vmt_ref):
    u = _rms(x_ref[...], g_ref[...]).astype(bf16)

    def proj(lo, hi):
        return _dot(u, win_ref[:, lo:hi])

    qg_ref[...] = proj(C_Q, C_K).astype(bf16)
    kg_ref[...] = proj(C_K, C_V).astype(bf16)
    vg_ref[...] = proj(C_V, C_R).astype(bf16)
    rg_ref[...] = proj(C_R, C_QLAT).astype(bf16)
    z = proj(C_QLAT, C_END)
    a_ref[...] = z[:, C_A - C_QLAT:].astype(bf16)
    ctab = ct_ref[...]
    stab = st_ref[...]

    def rope(seg):
        return seg * ctab + pltpu.roll(seg, 64, axis=1) * stab

    k_rope = rope(z[:, C_KROPE - C_QLAT:C_A - C_QLAT]).astype(bf16)
    qn = _rms(z[:, 0:MLA_Q_RANK], qn_ref[...]).astype(bf16)
    kvn = _rms(z[:, MLA_Q_RANK:MLA_Q_RANK + MLA_KV_RANK], kvn_ref[...]).astype(bf16)
    scale = (MLA_NOPE + MLA_ROPE) ** -0.5 * LOG2_E
    qf = _dot(qn, wqb_ref[...])
    kvf = _dot(kvn, wkvb_ref[...])
    for h in range(MLA_HEADS):
        c = h * MLA_QK_PAD
        qm_ref[:, c:c + LANE] = (qf[:, c:c + LANE] * scale).astype(bf16)
        qm_ref[:, c + LANE:c + 2 * LANE] = (rope(qf[:, c + LANE:c + 2 * LANE]) * scale).astype(bf16)
        km_ref[:, c:c + LANE] = kvf[:, h * LANE:(h + 1) * LANE].astype(bf16)
        km_ref[:, c + LANE:c + 2 * LANE] = k_rope
    vt = kvf[:, MLA_HEADS * MLA_NOPE:].T
    for h in range(MLA_HEADS):
        vmt_ref[h * MLA_VA:h * MLA_VA + MLA_V, :] = vt[h * MLA_V:(h + 1) * MLA_V].astype(bf16)
        vmt_ref[h * MLA_VA + MLA_V:(h + 1) * MLA_VA, :] = jnp.ones((MLA_VA - MLA_V, vt.shape[1]), bf16)


def _prep_call(x2d, rows_per_seq, tile, gain, w_in_r, q_norm, w_qb_r, kv_norm, w_kvb_r, ctab, stab):
    t = x2d.shape[0]
    nj = rows_per_seq // tile
    grid = (t // rows_per_seq, nj)

    def row(b, j):
        return (b * nj + j, 0)

    def const(b, j):
        return (0, 0)

    def tab(b, j):
        return (j, 0)

    widths = (GLA_QK, GLA_QK, GLA_VW, GLA_VW, LANE, MLA_HEADS * MLA_QK_PAD, MLA_HEADS * MLA_QK_PAD)
    return pl.pallas_call(
        _prep_kernel,
        grid=grid,
        in_specs=[
            pl.BlockSpec((tile, D_MODEL), row),
            pl.BlockSpec((1, D_MODEL), const),
            pl.BlockSpec((D_MODEL, C_END), const),
            pl.BlockSpec((1, MLA_Q_RANK), const),
            pl.BlockSpec((MLA_Q_RANK, MLA_HEADS * MLA_QK_PAD), const),
            pl.BlockSpec((1, MLA_KV_RANK), const),
            pl.BlockSpec((MLA_KV_RANK, 2 * MLA_OUT), const),
            pl.BlockSpec((tile, LANE), tab),
            pl.BlockSpec((tile, LANE), tab),
        ],
        out_specs=[pl.BlockSpec((tile, w), row) for w in widths]
        + [pl.BlockSpec((None, MLA_HEADS * MLA_VA, tile), lambda b, j: (b * nj + j, 0, 0))],
        out_shape=[jax.ShapeDtypeStruct((t, w), bf16) for w in widths]
        + [jax.ShapeDtypeStruct((t // tile, MLA_HEADS * MLA_VA, tile), bf16)],
        compiler_params=pltpu.CompilerParams(
            dimension_semantics=("parallel", "parallel"), vmem_limit_bytes=VMEM_LIMIT),
        name="prep",
    )(x2d, gain, w_in_r, q_norm, w_qb_r, kv_norm, w_kvb_r, ctab, stab)


def _gla_log_decay(a, wa2_ref, ba_ref):
    s = _dot(a, wa2_ref[...]) + ba_ref[...]
    return (jnp.minimum(s, 0.0) - jnp.log(1.0 + jnp.exp(-jnp.abs(s)))) * (1.0 / GLA_TAU)


def _gla_tile(q, k, v, la, st_ref, want_out):
    t = la.shape[0]
    nc = t // CHUNK
    ri = lax.broadcasted_iota(i32, (t, t), 0)
    ci = lax.broadcasted_iota(i32, (t, t), 1)
    tri = jnp.where((ri // CHUNK == ci // CHUNK) & (ci <= ri), 1.0, 0.0).astype(bf16)
    hi, mid, lo = _split3(la)
    b = _dot(tri, hi) + _dot(tri, mid) + _dot(tri, lo)
    b_last = [b[(c + 1) * CHUNK - 1:(c + 1) * CHUNK, :] for c in range(nc)]
    b_last_full = jnp.concatenate([jnp.broadcast_to(bl, (CHUNK, GLA_QK)) for bl in b_last], axis=0)
    kf = k.astype(f32)
    kd = (kf * jnp.exp(b_last_full - b)).astype(bf16)
    rr = lax.broadcasted_iota(i32, (GLA_VW, GLA_QK), 0) // GLA_DV
    cc = lax.broadcasted_iota(i32, (GLA_VW, GLA_QK), 1) // GLA_DK
    if want_out:
        qe = (q.astype(f32) * (GLA_DK ** -0.5) * jnp.exp(b)).astype(bf16)
        ke = kf * jnp.exp(-b)
        vf = v.astype(f32)
        lane_h = lax.broadcasted_iota(i32, (CHUNK, GLA_QK), 1) // GLA_DK
        vlane_h = lax.broadcasted_iota(i32, (CHUNK, GLA_VW), 1) // GLA_DV
        a_row = lax.broadcasted_iota(i32, (CHUNK, GLA_QK), 0)
        a_col = lax.broadcasted_iota(i32, (CHUNK, GLA_QK), 1) % CHUNK
    outs = []
    st = st_ref[...]
    for c in range(nc):
        rows = slice(c * CHUNK, (c + 1) * CHUNK)
        upd = jnp.where(rr == cc, _dot_tn(v[rows], kd[rows]), 0.0)
        if want_out:
            kbd = jnp.concatenate(
                [jnp.where(lane_h == h, ke[rows], 0.0) for h in range(GLA_HEADS)], axis=0).astype(bf16)
            att = jnp.where(a_col <= a_row, _dot_nt(qe[rows], kbd), 0.0).astype(bf16)
            vbd = jnp.concatenate(
                [jnp.where(vlane_h == h, vf[rows], 0.0) for h in range(GLA_HEADS)], axis=0).astype(bf16)
            outs.append(_dot(att, vbd) + _dot_nt(qe[rows], st.astype(bf16)))
        st = st * jnp.exp(b_last[c]) + upd
    st_ref[...] = st
    return jnp.concatenate(outs, axis=0) if want_out else None


def _gla_kernel(q_ref, k_ref, v_ref, r_ref, a_ref, km_ref, vm_ref, am_ref, wa2_ref, ba_ref, gain_ref,
                y_ref, st_ref):
    j = pl.program_id(1)

    @pl.when(j == 0)
    def _():
        st_ref[...] = jnp.zeros_like(st_ref)
        la = _gla_log_decay(am_ref[...], wa2_ref, ba_ref)
        row = lax.broadcasted_iota(i32, la.shape, 0)
        la = jnp.where(row >= CHUNK - N_META, la, 0.0)
        _gla_tile(None, km_ref[...], vm_ref[...], la, st_ref, False)

    la = _gla_log_decay(a_ref[...], wa2_ref, ba_ref)
    o = _gla_tile(q_ref[...], k_ref[...], v_ref[...], la, st_ref, True)
    r = r_ref[...].astype(f32)
    outs = []
    for h in range(GLA_HEADS):
        oh = o[:, h * GLA_DV:(h + 1) * GLA_DV]
        outs.append(oh * lax.rsqrt(jnp.mean(oh * oh, axis=-1, keepdims=True) + EPS))
    on = jnp.concatenate(outs, axis=1) * gain_ref[...]
    y_ref[...] = (on * (r * jax.nn.sigmoid(r))).astype(bf16)


def _gla_call(qg, kg, vg, rg, ag, km, vm, am, wa2_p, b_a, gain, batch, seq):
    nj = seq // GLA_TILE

    def row(b, j):
        return (b * nj + j, 0)

    def const(b, j):
        return (0, 0)

    return pl.pallas_call(
        _gla_kernel,
        grid=(batch, nj),
        in_specs=[
            pl.BlockSpec((GLA_TILE, GLA_QK), row),
            pl.BlockSpec((GLA_TILE, GLA_QK), row),
            pl.BlockSpec((GLA_TILE, GLA_VW), row),
            pl.BlockSpec((GLA_TILE, GLA_VW), row),
            pl.BlockSpec((GLA_TILE, LANE), row),
            pl.BlockSpec((CHUNK, GLA_QK), const),
            pl.BlockSpec((CHUNK, GLA_VW), const),
            pl.BlockSpec((CHUNK, LANE), const),
            pl.BlockSpec((LANE, GLA_QK), const),
            pl.BlockSpec((1, GLA_QK), const),
            pl.BlockSpec((1, GLA_VW), const),
        ],
        out_specs=pl.BlockSpec((GLA_TILE, GLA_VW), row),
        out_shape=jax.ShapeDtypeStruct((batch * seq, GLA_VW), bf16),
        scratch_shapes=[pltpu.VMEM((GLA_VW, GLA_QK), f32)],
        compiler_params=pltpu.CompilerParams(
            dimension_semantics=("parallel", "arbitrary"), vmem_limit_bytes=VMEM_LIMIT),
        name="gla",
    )(qg, kg, vg, rg, ag, km, vm, am, wa2_p, b_a, gain)


def _mla_kernel(q_ref, k_ref, vt_ref, km_ref, vmt_ref, o_ref, sa_ref, sb_ref):
    i = pl.program_id(2)
    tq = ATT_TILE
    w = MLA_QK_PAD
    va = MLA_VA
    heads = range(ATT_HEADS)

    def scores(h, blk):
        rows = pl.ds(pl.multiple_of(blk * tq, tq), tq)
        return _dot_nt(k_ref[rows, h * w:(h + 1) * w], q_ref[:, h * w:(h + 1) * w])

    def soft(s, vtb, carry, mask=None):
        m, acc = carry
        if mask is not None:
            s = jnp.where(mask, s, -1e30)
        m_new = jnp.maximum(m, jnp.max(s, axis=0, keepdims=True))
        p = jnp.exp2(s - m_new).astype(bf16)
        return m_new, jnp.exp2(m - m_new) * acc + _dot(vtb, p)

    def vt(h, blk):
        return vt_ref[blk, h * va:(h + 1) * va, :]

    def finish(h, carry):
        s = _dot_nt(km_ref[:, h * w:(h + 1) * w], q_ref[:, h * w:(h + 1) * w])
        m, acc = soft(s, vmt_ref[h * va:(h + 1) * va, :], carry)
        o_ref[:, h * MLA_V:(h + 1) * MLA_V] = (acc[:MLA_V] * (1.0 / acc[MLA_V:MLA_V + 1])).T.astype(bf16)

    kc = lax.broadcasted_iota(i32, (tq, tq), 0) // CHUNK
    qc = lax.broadcasted_iota(i32, (tq, tq), 1) // CHUNK
    mask = kc <= qc

    for h in heads:
        sa_ref[h] = scores(h, 0)

    def pair(p, carries):
        b0 = 2 * p
        for h in heads:
            sb_ref[h] = scores(h, b0 + 1)
        carries = [soft(sa_ref[h], vt(h, b0), carries[h]) for h in heads]
        for h in heads:
            sa_ref[h] = scores(h, b0 + 2)
        return tuple(soft(sb_ref[h], vt(h, b0 + 1), carries[h]) for h in heads)

    init = tuple((jnp.full((1, tq), -1e30, f32), jnp.zeros((va, tq), f32)) for _ in heads)
    carries = lax.fori_loop(0, i // 2, pair, init)

    @pl.when(i % 2 == 1)
    def _():
        for h in heads:
            sb_ref[h] = scores(h, i)
        for h in heads:
            c = soft(sa_ref[h], vt(h, i - 1), carries[h])
            finish(h, soft(sb_ref[h], vt(h, i), c, mask))

    @pl.when(i % 2 == 0)
    def _():
        for h in heads:
            finish(h, soft(sa_ref[h], vt(h, i), carries[h], mask))


def _mla_call(qm, km, vmt, km_meta, vmt_meta, batch, seq):
    nq = seq // ATT_TILE
    nh = ATT_HEADS
    qm3 = qm.reshape(batch, seq, MLA_HEADS * MLA_QK_PAD)
    km3 = km.reshape(batch, seq, MLA_HEADS * MLA_QK_PAD)
    vt4 = vmt.reshape(batch, nq, MLA_HEADS * MLA_VA, ATT_TILE)
    out = pl.pallas_call(
        _mla_kernel,
        grid=(batch, MLA_HEADS // nh, nq),
        in_specs=[
            pl.BlockSpec((None, ATT_TILE, nh * MLA_QK_PAD), lambda b, h, i: (b, i, h)),
            pl.BlockSpec((None, seq, nh * MLA_QK_PAD), lambda b, h, i: (b, 0, h)),
            pl.BlockSpec((None, nq, nh * MLA_VA, ATT_TILE), lambda b, h, i: (b, 0, h, 0)),
            pl.BlockSpec((N_META, nh * MLA_QK_PAD), lambda b, h, i: (0, h)),
            pl.BlockSpec((nh * MLA_VA, N_META), lambda b, h, i: (h, 0)),
        ],
        out_specs=pl.BlockSpec((None, ATT_TILE, nh * MLA_V), lambda b, h, i: (b, i, h)),
        out_shape=jax.ShapeDtypeStruct((batch, seq, MLA_OUT), bf16),
        scratch_shapes=[pltpu.VMEM((nh, ATT_TILE, ATT_TILE), f32), pltpu.VMEM((nh, ATT_TILE, ATT_TILE), f32)],
        compiler_params=pltpu.CompilerParams(
            dimension_semantics=("parallel", "parallel", "arbitrary"), vmem_limit_bytes=VMEM_LIMIT),
        name="mla",
    )(qm3, km3, vt4, km_meta, vmt_meta)
    return out.reshape(batch * seq, MLA_OUT)


def _outproj_kernel(x_ref, yg_ref, ym_ref, wog_ref, wom_ref, gain_ref, whi_ref, wlo_ref, rb_ref,
                    h_ref, ux_ref, meta_ref, cnt_ref):
    t = OUT_TILE
    h1 = x_ref[...] + _dot(yg_ref[...], wog_ref[...]) + _dot(ym_ref[...], wom_ref[...])
    h_ref[...] = h1
    u2 = _rms(h1, gain_ref[...])
    ux_ref[:, 0:D_MODEL] = u2
    u_hi = u2.astype(bf16)
    u_lo = (u2 - u_hi.astype(f32)).astype(bf16)
    whi = whi_ref[...]
    logits = _dot(u_hi, whi) + _dot(u_lo, whi) + _dot(u_hi, wlo_ref[...]) + rb_ref[...]
    lane = lax.broadcasted_iota(i32, (t, LANE), 1)
    neg = -1e30
    g_mask = lane < N_GROUPS
    gl = jnp.where(g_mask, logits, neg)
    gmax = jnp.max(gl, axis=1, keepdims=True)
    g_sum = jnp.sum(jnp.where(g_mask, jnp.exp(gl - gmax), 0.0), axis=1, keepdims=True)
    g_p = 1.0 / g_sum
    g_idx = jnp.min(jnp.where(g_mask & (gl == gmax), lane, LANE), axis=1, keepdims=True)
    base = N_GROUPS + g_idx * EXPERTS_PER_GROUP
    e_mask = (lane >= base) & (lane < base + EXPERTS_PER_GROUP)
    el = jnp.where(e_mask, logits, neg)
    m1 = jnp.max(el, axis=1, keepdims=True)
    i1 = jnp.min(jnp.where(e_mask & (el == m1), lane, LANE), axis=1, keepdims=True)
    el2 = jnp.where(lane == i1, neg, el)
    m2 = jnp.max(el2, axis=1, keepdims=True)
    i2 = jnp.min(jnp.where(e_mask & (lane != i1) & (el2 == m2), lane, LANE), axis=1, keepdims=True)
    r = jnp.exp(m2 - m1)
    ga = g_p / (1.0 + r)
    gb = g_p * r / (1.0 + r)
    la_ = i1 - base
    lb_ = i2 - base
    lo = jnp.minimum(la_, lb_)
    hi = jnp.maximum(la_, lb_)
    g_lo = jnp.where(la_ < lb_, ga, gb)
    g_hi = jnp.where(la_ < lb_, gb, ga)
    pidx = ((lo * (2 * EXPERTS_PER_GROUP - 1 - lo)) >> 1) + (hi - lo - 1)
    bucket = g_idx * N_PAIRS + pidx
    blane = lax.broadcasted_iota(i32, (t, BUCKET_LANES), 1)
    ohf = jnp.where(blane == bucket, 1.0, 0.0)
    oh = ohf.astype(bf16)
    ri = lax.broadcasted_iota(i32, (t, t), 0)
    ci = lax.broadcasted_iota(i32, (t, t), 1)
    tri = jnp.where(ci < ri, 1.0, 0.0).astype(bf16)
    cum = _dot(tri, oh)
    rank = jnp.sum(ohf * cum, axis=1, keepdims=True)
    cnt_ref[...] = jnp.sum(ohf, axis=0, keepdims=True).reshape(1, 1, BUCKET_LANES)
    meta = jnp.where(lane == 0, bucket.astype(f32),
                     jnp.where(lane == 1, rank,
                               jnp.where(lane == 2, g_lo, jnp.where(lane == 3, g_hi, 0.0))))
    meta_ref[...] = meta
    ux_ref[:, D_MODEL:ROW_W] = meta


def _outproj_call(x2d, yg, ym, wo_g, wo_m, gain, w_hi, w_lo, rbias):
    t = x2d.shape[0]
    nt = t // OUT_TILE

    def row(i):
        return (i, 0)

    def const(i):
        return (0, 0)

    return pl.pallas_call(
        _outproj_kernel,
        grid=(nt,),
        in_specs=[
            pl.BlockSpec((OUT_TILE, D_MODEL), row),
            pl.BlockSpec((OUT_TILE, GLA_VW), row),
            pl.BlockSpec((OUT_TILE, MLA_OUT), row),
            pl.BlockSpec((GLA_VW, D_MODEL), const),
            pl.BlockSpec((MLA_OUT, D_MODEL), const),
            pl.BlockSpec((1, D_MODEL), const),
            pl.BlockSpec((D_MODEL, LANE), const),
            pl.BlockSpec((D_MODEL, LANE), const),
            pl.BlockSpec((1, LANE), const),
        ],
        out_specs=[
            pl.BlockSpec((OUT_TILE, D_MODEL), row),
            pl.BlockSpec((OUT_TILE, ROW_W), row),
            pl.BlockSpec((OUT_TILE, META_W), row),
            pl.BlockSpec((1, 1, BUCKET_LANES), lambda i: (i, 0, 0)),
        ],
        out_shape=[
            jax.ShapeDtypeStruct((t, D_MODEL), f32),
            jax.ShapeDtypeStruct((t, ROW_W), f32),
            jax.ShapeDtypeStruct((t, META_W), f32),
            jax.ShapeDtypeStruct((nt, 1, BUCKET_LANES), f32),
        ],
        compiler_params=pltpu.CompilerParams(
            dimension_semantics=("parallel",), vmem_limit_bytes=VMEM_LIMIT),
        name="outproj",
    )(x2d, yg, ym, wo_g, wo_m, gain, w_hi, w_lo, rbias)


def _scatter_kernel(pos_ref, ux_hbm, hs_in_ref, hs_ref, sem):
    del hs_in_ref
    i = pl.program_id(0)
    base = i * (SCATTER_TILE // SUBLANES)

    def start(ii, c):
        for k in range(SUBLANES):
            pltpu.make_async_copy(ux_hbm.at[base + ii, pl.ds(k, 1)],
                                  hs_ref.at[pl.ds(pos_ref[ii * SUBLANES + k], 1)], sem).start()
        return c

    lax.fori_loop(0, SCATTER_TILE // SUBLANES, start, 0)

    def wait_tile():
        pltpu.make_async_copy(hs_ref.at[pl.ds(0, SCATTER_TILE)], hs_ref.at[pl.ds(0, SCATTER_TILE)], sem).wait()

    @pl.when(i >= 1)
    def _():
        wait_tile()

    @pl.when(i == pl.num_programs(0) - 1)
    def _():
        wait_tile()


def _scatter_call(pos, ux, hs0):
    t = ux.shape[0]
    return pl.pallas_call(
        _scatter_kernel,
        grid=(t // SCATTER_TILE,),
        in_specs=[
            pl.BlockSpec((SCATTER_TILE,), lambda i: (i,), memory_space=pltpu.SMEM),
            pl.BlockSpec(memory_space=pl.ANY),
            pl.BlockSpec(memory_space=pl.ANY),
        ],
        out_specs=pl.BlockSpec(memory_space=pl.ANY),
        out_shape=jax.ShapeDtypeStruct(hs0.shape, f32),
        scratch_shapes=[pltpu.SemaphoreType.DMA(())],
        input_output_aliases={2: 0},
        compiler_params=pltpu.CompilerParams(
            dimension_semantics=("arbitrary",), vmem_limit_bytes=VMEM_LIMIT),
        name="scatter",
    )(pos, ux.reshape(t // SUBLANES, SUBLANES, ROW_W), hs0)


def _moe_kernel(se_ref, sf_ref, sk_ref, sb_ref, sr_ref, si_ref, so_ref,
                hs_hbm, wg_ref, wu_ref, wd_ref, y_hbm, xbuf, obuf, wgu_s, wd_s, in_sem, out_sem):
    s = pl.program_id(0)
    ns = pl.num_programs(0)
    cur = s % 2
    g_n = MOE_GROUP

    def in_copy(step, g, buf):
        rows = pl.ds(pl.multiple_of(sb_ref[step * g_n + g] * MOE_BLOCK, MOE_BLOCK), MOE_BLOCK)
        return pltpu.make_async_copy(
            hs_hbm.at[rows], xbuf.at[buf, pl.ds(g * MOE_BLOCK, MOE_BLOCK)], in_sem.at[buf])

    def out_copy(step, g, buf):
        rows = pl.ds(pl.multiple_of(sb_ref[step * g_n + g] * MOE_BLOCK, MOE_BLOCK), MOE_BLOCK)
        cols = pl.ds(pl.multiple_of(sr_ref[step * g_n + g] * D_MODEL, D_MODEL), D_MODEL)
        return pltpu.make_async_copy(
            obuf.at[buf, pl.ds(g * MOE_BLOCK, MOE_BLOCK)], y_hbm.at[rows, cols], out_sem.at[buf])

    def for_slots(step, flags_ref, fn):
        for g in range(g_n):
            @pl.when(flags_ref[step * g_n + g] == 1)
            def _():
                fn(g)

    @pl.when(s == 0)
    def _():
        xbuf[...] = jnp.zeros_like(xbuf)
        for_slots(0, si_ref, lambda g: in_copy(0, g, 0).start())

    @pl.when(s + 1 < ns)
    def _():
        for_slots(s + 1, si_ref, lambda g: in_copy(s + 1, g, 1 - cur).start())

    for_slots(s, si_ref, lambda g: in_copy(s, g, cur).wait())

    @pl.when(s >= 2)
    def _():
        for_slots(s - 2, so_ref, lambda g: out_copy(s - 2, g, cur).wait())

    @pl.when(sf_ref[s] == 1)
    def _():
        wgu_s[:, 0:D_EXPERT] = wg_ref[...].astype(bf16)
        wgu_s[:, D_EXPERT:2 * D_EXPERT] = wu_ref[...].astype(bf16)
        wd_s[...] = wd_ref[...].astype(bf16)

    @pl.when(sk_ref[s] == 1)
    def _():
        u = xbuf[cur, :, 0:D_MODEL].astype(bf16)
        meta = xbuf[cur, :, D_MODEL:ROW_W]
        gate = jnp.concatenate(
            [jnp.where(sr_ref[s * g_n + g] == 0, meta[g * MOE_BLOCK:(g + 1) * MOE_BLOCK, 2:3],
                       meta[g * MOE_BLOCK:(g + 1) * MOE_BLOCK, 3:4]) for g in range(g_n)], axis=0)
        gu = _dot(u, wgu_s[...])
        gt = gu[:, 0:D_EXPERT]
        hdn = (gt * jax.nn.sigmoid(gt) * gu[:, D_EXPERT:]).astype(bf16)
        obuf[cur] = _dot(hdn, wd_s[...]) * gate

    @pl.when(sk_ref[s] == 0)
    def _():
        obuf[cur] = jnp.zeros(obuf.shape[1:], f32)

    for_slots(s, so_ref, lambda g: out_copy(s, g, cur).start())

    @pl.when(s == ns - 1)
    def _():
        for_slots(s, so_ref, lambda g: out_copy(s, g, cur).wait())

        @pl.when(s >= 1)
        def _():
            for_slots(s - 1, so_ref, lambda g: out_copy(s - 1, g, 1 - cur).wait())


def _moe_call(plan, hs, w_gate, w_up, w_down):
    n_steps = plan[0].shape[0]
    n_slots = hs.shape[0]
    rows = MOE_GROUP * MOE_BLOCK

    def wmap(s, se, *_):
        return (se[s], 0, 0)

    grid_spec = pltpu.PrefetchScalarGridSpec(
        num_scalar_prefetch=7,
        grid=(n_steps,),
        in_specs=[
            pl.BlockSpec(memory_space=pl.ANY),
            pl.BlockSpec((None, D_MODEL, D_EXPERT), wmap),
            pl.BlockSpec((None, D_MODEL, D_EXPERT), wmap),
            pl.BlockSpec((None, D_EXPERT, D_MODEL), wmap),
        ],
        out_specs=pl.BlockSpec(memory_space=pl.ANY),
        scratch_shapes=[
            pltpu.VMEM((2, rows, ROW_W), f32),
            pltpu.VMEM((2, rows, D_MODEL), f32),
            pltpu.VMEM((D_MODEL, 2 * D_EXPERT), bf16),
            pltpu.VMEM((D_EXPERT, D_MODEL), bf16),
            pltpu.SemaphoreType.DMA((2,)),
            pltpu.SemaphoreType.DMA((2,)),
        ],
    )
    return pl.pallas_call(
        _moe_kernel,
        grid_spec=grid_spec,
        out_shape=jax.ShapeDtypeStruct((n_slots, 2 * D_MODEL), f32),
        compiler_params=pltpu.CompilerParams(
            dimension_semantics=("arbitrary",), vmem_limit_bytes=VMEM_LIMIT),
        name="moe",
    )(*plan, hs, w_gate, w_up, w_down)


def _final_kernel(posc_ref, posn_ref, h_ref, gain_ref, y_hbm, o_ref, ybuf, sem):
    i = pl.program_id(0)
    cur = i % 2

    def issue(pos_ref, buf):
        def start(ii, c):
            for k in range(SUBLANES):
                pltpu.make_async_copy(y_hbm.at[pl.ds(pos_ref[ii * SUBLANES + k], 1)],
                                      ybuf.at[buf, ii, pl.ds(k, 1)], sem.at[buf]).start()
            return c

        lax.fori_loop(0, FINAL_TILE // SUBLANES, start, 0)

    @pl.when(i == 0)
    def _():
        issue(posc_ref, 0)

    @pl.when(i + 1 < pl.num_programs(0))
    def _():
        issue(posn_ref, 1 - cur)

    pltpu.make_async_copy(ybuf.at[cur], ybuf.at[cur], sem.at[cur]).wait()
    h = h_ref[...] + ybuf[cur, :, :, 0:D_MODEL] + ybuf[cur, :, :, D_MODEL:2 * D_MODEL]
    o_ref[...] = _rms(h, gain_ref[...])


def _final_call(pos, h1, gain, y):
    t = h1.shape[0]
    n = t // FINAL_TILE
    rows = FINAL_TILE // SUBLANES
    out = pl.pallas_call(
        _final_kernel,
        grid=(n,),
        in_specs=[
            pl.BlockSpec((FINAL_TILE,), lambda i: (i,), memory_space=pltpu.SMEM),
            pl.BlockSpec((FINAL_TILE,), lambda i: (jnp.minimum(i + 1, n - 1),), memory_space=pltpu.SMEM),
            pl.BlockSpec((rows, SUBLANES, D_MODEL), lambda i: (i, 0, 0)),
            pl.BlockSpec((1, 1, D_MODEL), lambda i: (0, 0, 0)),
            pl.BlockSpec(memory_space=pl.ANY),
        ],
        out_specs=pl.BlockSpec((rows, SUBLANES, D_MODEL), lambda i: (i, 0, 0)),
        out_shape=jax.ShapeDtypeStruct((t // SUBLANES, SUBLANES, D_MODEL), f32),
        scratch_shapes=[pltpu.VMEM((2, rows, SUBLANES, 2 * D_MODEL), f32), pltpu.SemaphoreType.DMA((2,))],
        compiler_params=pltpu.CompilerParams(
            dimension_semantics=("arbitrary",), vmem_limit_bytes=VMEM_LIMIT),
        name="final",
    )(pos, pos, h1.reshape(t // SUBLANES, SUBLANES, D_MODEL), gain.reshape(1, 1, D_MODEL), y)
    return out.reshape(t, D_MODEL)


def _rope_tables(pos):
    inv = ROPE_BASE ** (-jnp.arange(0, MLA_ROPE, 2, dtype=f32) / MLA_ROPE)
    ang = pos.astype(f32)[:, None] * inv[None, :]
    cos, sin = jnp.cos(ang), jnp.sin(ang)
    z = jnp.zeros((pos.shape[0], LANE - MLA_ROPE), f32)
    return jnp.concatenate([cos, cos, z], axis=1), jnp.concatenate([-sin, sin, z], axis=1)


def _relayout_weights(w_in, w_qb, w_kvb):
    half = MLA_ROPE // 2
    perm = (np.arange(MLA_ROPE) + half) % MLA_ROPE
    pts = np.cumsum((GLA_QK, GLA_QK, GLA_VW, GLA_VW, GLA_GATE_RANK, MLA_Q_RANK, MLA_KV_RANK, MLA_ROPE))
    q_g, k_g, v_g, r_g, a_l, q_lat, kv_lat, k_rope = jnp.split(w_in, pts[:-1], axis=1)
    a_seg = jnp.pad(a_l, ((0, 0), (0, LANE - GLA_GATE_RANK)))
    w_in_r = jnp.concatenate(
        [q_g, k_g, v_g, r_g, q_lat, kv_lat, k_rope, k_rope[:, perm], a_seg], axis=1).astype(bf16)
    qcols, kcols, vcols = [], [], []
    for h in range(MLA_HEADS):
        c = h * (MLA_NOPE + MLA_ROPE)
        rope = w_qb[:, c + MLA_NOPE:c + MLA_NOPE + MLA_ROPE]
        qcols += [w_qb[:, c:c + MLA_NOPE], rope, rope[:, perm]]
        c2 = h * (MLA_NOPE + MLA_V)
        kcols.append(w_kvb[:, c2:c2 + MLA_NOPE])
        vcols.append(w_kvb[:, c2 + MLA_NOPE:c2 + MLA_NOPE + MLA_V])
    return w_in_r, jnp.concatenate(qcols, axis=1).astype(bf16), jnp.concatenate(kcols + vcols, axis=1).astype(bf16)


_BUCKET_GROUP = np.arange(N_BUCKETS) // N_PAIRS
_RUN_EXPERT = np.concatenate([_BUCKET_GROUP * EXPERTS_PER_GROUP + _PAIR_LO[np.arange(N_BUCKETS) % N_PAIRS],
                              _BUCKET_GROUP * EXPERTS_PER_GROUP + _PAIR_HI[np.arange(N_BUCKETS) % N_PAIRS]])
_RUN_IS_EXPERT = (_RUN_EXPERT[:, None] == np.arange(N_EXPERTS)[None, :]).astype(np.int32)
_RUN_BEFORE = ((_RUN_EXPERT[:, None] == _RUN_EXPERT[None, :])
               & (np.arange(2 * N_BUCKETS)[None, :] < np.arange(2 * N_BUCKETS)[:, None])).astype(np.int32)


def _route_plan(counts, bucket, rank, n_tok):
    nt = counts.shape[0]
    g_n = MOE_GROUP
    tot = counts.sum(axis=0)
    nblk = (tot + MOE_BLOCK - 1) // MOE_BLOCK
    bstart_blk = jnp.cumsum(nblk) - nblk
    n_blocks = jnp.sum(nblk)
    tile_base = bstart_blk[None, :] * MOE_BLOCK + jnp.cumsum(counts, axis=0) - counts
    hit = bucket.reshape(nt, -1, 1) == jnp.arange(N_BUCKETS, dtype=i32)
    pos = jnp.sum(jnp.where(hit, tile_base[:, None, :], 0), axis=-1).reshape(-1) + rank
    nb_max = (n_tok + N_BUCKETS * (MOE_BLOCK - 1)) // MOE_BLOCK

    n_run = jnp.concatenate([nblk, nblk])
    b0_run = jnp.concatenate([bstart_blk, bstart_blk])
    c_e = jnp.sum(n_run[:, None] * _RUN_IS_EXPERT, axis=0)
    g_e = (c_e + g_n - 1) // g_n
    gend = jnp.cumsum(g_e)
    gstart = gend - g_e
    n_compute = gend[-1]
    off_run = jnp.sum(_RUN_BEFORE * n_run[None, :], axis=1)
    f_run = jnp.sum(_RUN_IS_EXPERT * gstart[None, :], axis=1) * g_n + off_run

    n_steps = (2 * nb_max + N_EXPERTS * (g_n - 1) + g_n - 1) // g_n + 1
    f = jnp.arange(n_steps * g_n, dtype=i32)
    in_run = (f[:, None] >= f_run[None, :]) & (f[:, None] < (f_run + n_run)[None, :])
    valid_c = jnp.any(in_run, axis=1)
    block_c = jnp.sum(jnp.where(in_run, b0_run[None, :] + f[:, None] - f_run[None, :], 0), axis=1)
    role_c = jnp.sum(jnp.where(in_run[:, N_BUCKETS:], 1, 0), axis=1)
    u_idx = f - n_compute * g_n
    valid_f = (u_idx >= 0) & (u_idx < 2 * (nb_max - n_blocks))
    slot_block = jnp.where(valid_c, block_c, jnp.where(valid_f, n_blocks + u_idx // 2, 0))
    slot_role = jnp.where(valid_c, role_c, jnp.where(valid_f, u_idx % 2, 0))

    step = jnp.arange(n_steps, dtype=i32)
    e_of_step = jnp.minimum(jnp.sum(gend[None, :] <= step[:, None], axis=1), N_EXPERTS - 1)
    is_compute = step < n_compute
    last_e = jnp.max(jnp.where(is_compute, e_of_step, 0))
    step_expert = jnp.where(is_compute, e_of_step, last_e)
    step_first = jnp.concatenate([jnp.ones((1,), bool), step_expert[1:] != step_expert[:-1]])
    plan = tuple(a.astype(i32) for a in
                 (step_expert, step_first, is_compute, slot_block, slot_role, valid_c, valid_c | valid_f))
    return pos.astype(i32), plan, nb_max


def kernel(x, meta_tokens, mix_norm, w_in, gla_w_a2, gla_b_a, gla_out_norm, mla_q_norm, mla_w_qb, mla_kv_norm,
           mla_w_kvb, w_out, ffn_norm, router_group_w, router_group_b, router_expert_w, router_expert_b,
           expert_w_gate, expert_w_up, expert_w_down, final_norm):
    batch, seq, d = x.shape
    assert d == D_MODEL and seq % max(PREP_TILE, GLA_TILE, ATT_TILE) == 0
    assert (batch * seq) % OUT_TILE == 0
    n_tok = batch * seq
    x2d = x.reshape(n_tok, d)

    w_in_r, w_qb_r, w_kvb_r = _relayout_weights(w_in[0], mla_w_qb[0], mla_w_kvb[0])
    mixg = mix_norm[0].reshape(1, d)
    qn = mla_q_norm[0].reshape(1, MLA_Q_RANK)
    kvn = mla_kv_norm[0].reshape(1, MLA_KV_RANK)
    ct_m, st_m = _rope_tables(jnp.arange(META_TILE))
    ct_x, st_x = _rope_tables(N_META + jnp.arange(seq))

    x_meta = jnp.pad(meta_tokens.astype(f32), ((0, META_TILE - N_META), (0, 0)))
    _, kg_m, vg_m, _, a_m, _, km_m, vmt_m = _prep_call(
        x_meta, META_TILE, META_TILE, mixg, w_in_r, qn, w_qb_r, kvn, w_kvb_r, ct_m, st_m)
    qg, kg, vg, rg, ag, qm, km, vmt = _prep_call(
        x2d, seq, PREP_TILE, mixg, w_in_r, qn, w_qb_r, kvn, w_kvb_r, ct_x, st_x)

    def chunk0(a):
        return jnp.pad(a[:N_META], ((CHUNK - N_META, 0), (0, 0)))

    wa2_p = jnp.pad(gla_w_a2[0], ((0, LANE - GLA_GATE_RANK), (0, 0))).astype(bf16)
    y_gla = _gla_call(qg, kg, vg, rg, ag, chunk0(kg_m), chunk0(vg_m), chunk0(a_m),
                      wa2_p, gla_b_a[0].reshape(1, GLA_QK), gla_out_norm[0].reshape(1, GLA_VW), batch, seq)
    y_mla = _mla_call(qm, km, vmt, km_m[:N_META], vmt_m[0, :, :N_META], batch, seq)

    wo = w_out[0].astype(bf16)
    rw = jnp.concatenate([router_group_w[0], router_expert_w[0],
                          jnp.zeros((d, LANE - N_GROUPS - N_EXPERTS), f32)], axis=1)
    rw_hi = rw.astype(bf16)
    rw_lo = (rw - rw_hi.astype(f32)).astype(bf16)
    rb = jnp.concatenate([router_group_b[0], router_expert_b[0],
                          jnp.zeros((LANE - N_GROUPS - N_EXPERTS,), f32)]).reshape(1, LANE)
    h1, ux, meta, cnt = _outproj_call(x2d, y_gla, y_mla, wo[:GLA_VW], wo[GLA_VW:], ffn_norm[0].reshape(1, d),
                                      rw_hi, rw_lo, rb)

    counts = cnt.reshape(-1, BUCKET_LANES)[:, :N_BUCKETS].astype(i32)
    pos, plan, nb_max = _route_plan(counts, meta[:, 0].astype(i32), meta[:, 1].astype(i32), n_tok)
    n_slots = nb_max * MOE_BLOCK
    hs = _scatter_call(pos, ux, jnp.zeros((n_slots, ROW_W), f32))
    y = _moe_call(plan, hs, expert_w_gate[0], expert_w_up[0], expert_w_down[0])
    out = _final_call(pos, h1, final_norm.reshape(1, d), y)
    return out.reshape(batch, seq, d)
```

```python
import functools

import numpy as np
import jax
import jax.numpy as jnp
from jax import lax
from jax.experimental import pallas as pl
from jax.experimental.pallas import tpu as pltpu

f32 = jnp.float32
bf16 = jnp.bfloat16
i32 = jnp.int32

D_MODEL = 1024
CHUNK = 64
N_META = 16
EPS = 1e-6
GLA_HEADS = 4
GLA_DK = 64
GLA_DV = 128
GLA_GATE_RANK = 16
GLA_TAU = 16.0
GLA_QK = GLA_HEADS * GLA_DK
GLA_VW = GLA_HEADS * GLA_DV
MLA_HEADS = 4
MLA_Q_RANK = 256
MLA_KV_RANK = 128
MLA_NOPE = 128
MLA_ROPE = 64
MLA_V = 128
MLA_OUT = MLA_HEADS * MLA_V
MLA_QK_PAD = 256
MLA_VA = MLA_V + 16
LOG2_E = 1.4426950408889634
ROPE_BASE = 10000.0
N_GROUPS = 8
EXPERTS_PER_GROUP = 8
N_EXPERTS = N_GROUPS * EXPERTS_PER_GROUP
D_EXPERT = 512
N_PAIRS = EXPERTS_PER_GROUP * (EXPERTS_PER_GROUP - 1) // 2
N_BUCKETS = N_GROUPS * N_PAIRS
BUCKET_LANES = 256
LANE = 128
SUBLANES = 8
META_W = LANE
ROW_W = D_MODEL + META_W

PREP_TILE = 512
GLA_TILE = 512
ATT_TILE = 512
ATT_HEADS = 4
META_TILE = 128
OUT_TILE = 1024
ROUTE_ROWS = 256
SCATTER_TILE = 1024
FINAL_TILE = 256
MOE_BLOCK = 128
MOE_GROUP = 4
VMEM_LIMIT = 56 * 1024 * 1024

C_Q, C_K, C_V, C_R = 0, 256, 512, 1024
C_QLAT, C_KVLAT, C_KROPE, C_A, C_END = 1536, 1792, 1920, 2048, 2176

_PAIR_LO = np.array([lo for lo in range(8) for hi in range(lo + 1, 8)], np.int32)
_PAIR_HI = np.array([hi for lo in range(8) for hi in range(lo + 1, 8)], np.int32)


def _dot(a, b):
    return jnp.dot(a, b, preferred_element_type=f32)


def _dot_nt(a, b):
    return lax.dot_general(a, b, (((1,), (1,)), ((), ())), preferred_element_type=f32)


def _dot_tn(a, b):
    return lax.dot_general(a, b, (((0,), (0,)), ((), ())), preferred_element_type=f32)


def _rms(x, gain):
    return x * lax.rsqrt(jnp.mean(x * x, axis=-1, keepdims=True) + EPS) * gain


def _split3(x):
    hi = x.astype(bf16)
    r1 = x - hi.astype(f32)
    mid = r1.astype(bf16)
    lo = (r1 - mid.astype(f32)).astype(bf16)
    return hi, mid, lo


def _prep_kernel(x_ref, g_ref, win_ref, qn_ref, wqb_ref, kvn_ref, wkvb_ref, ct_ref, st_ref,
                 qg_ref, kg_ref, vg_ref, rg_ref, a_ref, qm_ref, km_ref, vmt_ref):
    u = _rms(x_ref[...], g_ref[...]).astype(bf16)

    def proj(lo, hi):
        return _dot(u, win_ref[:, lo:hi])

    qg_ref[...] = proj(C_Q, C_K).astype(bf16)
    kg_ref[...] = proj(C_K, C_V).astype(bf16)
    vg_ref[...] = proj(C_V, C_R).astype(bf16)
    rg_ref[...] = proj(C_R, C_QLAT).astype(bf16)
    z = proj(C_QLAT, C_END)
    a_ref[...] = z[:, C_A - C_QLAT:].astype(bf16)
    ctab = ct_ref[...]
    stab = st_ref[...]

    def rope(seg):
        return seg * ctab + pltpu.roll(seg, 64, axis=1) * stab

    k_rope = rope(z[:, C_KROPE - C_QLAT:C_A - C_QLAT]).astype(bf16)
    qn = _rms(z[:, 0:MLA_Q_RANK], qn_ref[...]).astype(bf16)
    kvn = _rms(z[:, MLA_Q_RANK:MLA_Q_RANK + MLA_KV_RANK], kvn_ref[...]).astype(bf16)
    scale = (MLA_NOPE + MLA_ROPE) ** -0.5 * LOG2_E
    qf = _dot(qn, wqb_ref[...])
    kvf = _dot(kvn, wkvb_ref[...])
    for h in range(MLA_HEADS):
        c = h * MLA_QK_PAD
        qm_ref[:, c:c + LANE] = (qf[:, c:c + LANE] * scale).astype(bf16)
        qm_ref[:, c + LANE:c + 2 * LANE] = (rope(qf[:, c + LANE:c + 2 * LANE]) * scale).astype(bf16)
        km_ref[:, c:c + LANE] = kvf[:, h * LANE:(h + 1) * LANE].astype(bf16)
        km_ref[:, c + LANE:c + 2 * LANE] = k_rope
    vt = kvf[:, MLA_HEADS * MLA_NOPE:].T
    for h in range(MLA_HEADS):
        vmt_ref[h * MLA_VA:h * MLA_VA + MLA_V, :] = vt[h * MLA_V:(h + 1) * MLA_V].astype(bf16)
        vmt_ref[h * MLA_VA + MLA_V:(h + 1) * MLA_VA, :] = jnp.ones((MLA_VA - MLA_V, vt.shape[1]), bf16)


def _prep_call(x2d, rows_per_seq, tile, gain, w_in_r, q_norm, w_qb_r, kv_norm, w_kvb_r, ctab, stab):
    t = x2d.shape[0]
    nj = rows_per_seq // tile
    grid = (t // rows_per_seq, nj)

    def row(b, j):
        return (b * nj + j, 0)

    def const(b, j):
        return (0, 0)

    def tab(b, j):
        return (j, 0)

    widths = (GLA_QK, GLA_QK, GLA_VW, GLA_VW, LANE, MLA_HEADS * MLA_QK_PAD, MLA_HEADS * MLA_QK_PAD)
    return pl.pallas_call(
        _prep_kernel,
        grid=grid,
        in_specs=[
            pl.BlockSpec((tile, D_MODEL), row),
            pl.BlockSpec((1, D_MODEL), const),
            pl.BlockSpec((D_MODEL, C_END), const),
            pl.BlockSpec((1, MLA_Q_RANK), const),
            pl.BlockSpec((MLA_Q_RANK, MLA_HEADS * MLA_QK_PAD), const),
            pl.BlockSpec((1, MLA_KV_RANK), const),
            pl.BlockSpec((MLA_KV_RANK, 2 * MLA_OUT), const),
            pl.BlockSpec((tile, LANE), tab),
            pl.BlockSpec((tile, LANE), tab),
        ],
        out_specs=[pl.BlockSpec((tile, w), row) for w in widths]
        + [pl.BlockSpec((None, MLA_HEADS * MLA_VA, tile), lambda b, j: (b * nj + j, 0, 0))],
        out_shape=[jax.ShapeDtypeStruct((t, w), bf16) for w in widths]
        + [jax.ShapeDtypeStruct((t // tile, MLA_HEADS * MLA_VA, tile), bf16)],
        compiler_params=pltpu.CompilerParams(
            dimension_semantics=("parallel", "parallel"), vmem_limit_bytes=VMEM_LIMIT),
        name="prep",
    )(x2d, gain, w_in_r, q_norm, w_qb_r, kv_norm, w_kvb_r, ctab, stab)


def _gla_log_decay(a, wa2_ref, ba_ref):
    s = _dot(a, wa2_ref[...]) + ba_ref[...]
    return (jnp.minimum(s, 0.0) - jnp.log(1.0 + jnp.exp(-jnp.abs(s)))) * (1.0 / GLA_TAU)


def _gla_tile(q, k, v, la, st_ref, want_out):
    t = la.shape[0]
    nc = t // CHUNK
    ri = lax.broadcasted_iota(i32, (t, t), 0)
    ci = lax.broadcasted_iota(i32, (t, t), 1)
    tri = jnp.where((ri // CHUNK == ci // CHUNK) & (ci <= ri), 1.0, 0.0).astype(bf16)
    hi, mid, lo = _split3(la)
    b = _dot(tri, hi) + _dot(tri, mid) + _dot(tri, lo)
    b_last = [b[(c + 1) * CHUNK - 1:(c + 1) * CHUNK, :] for c in range(nc)]
    b_last_full = jnp.concatenate([jnp.broadcast_to(bl, (CHUNK, GLA_QK)) for bl in b_last], axis=0)
    kf = k.astype(f32)
    kd = (kf * jnp.exp(b_last_full - b)).astype(bf16)
    rr = lax.broadcasted_iota(i32, (GLA_VW, GLA_QK), 0) // GLA_DV
    cc = lax.broadcasted_iota(i32, (GLA_VW, GLA_QK), 1) // GLA_DK
    if want_out:
        qe = (q.astype(f32) * (GLA_DK ** -0.5) * jnp.exp(b)).astype(bf16)
        ke = kf * jnp.exp(-b)
        vf = v.astype(f32)
        lane_h = lax.broadcasted_iota(i32, (CHUNK, GLA_QK), 1) // GLA_DK
        vlane_h = lax.broadcasted_iota(i32, (CHUNK, GLA_VW), 1) // GLA_DV
        a_row = lax.broadcasted_iota(i32, (CHUNK, GLA_QK), 0)
        a_col = lax.broadcasted_iota(i32, (CHUNK, GLA_QK), 1) % CHUNK
    outs = []
    st = st_ref[...]
    for c in range(nc):
        rows = slice(c * CHUNK, (c + 1) * CHUNK)
        upd = jnp.where(rr == cc, _dot_tn(v[rows], kd[rows]), 0.0)
        if want_out:
            kbd = jnp.concatenate(
                [jnp.where(lane_h == h, ke[rows], 0.0) for h in range(GLA_HEADS)], axis=0).astype(bf16)
            att = jnp.where(a_col <= a_row, _dot_nt(qe[rows], kbd), 0.0).astype(bf16)
            vbd = jnp.concatenate(
                [jnp.where(vlane_h == h, vf[rows], 0.0) for h in range(GLA_HEADS)], axis=0).astype(bf16)
            outs.append(_dot(att, vbd) + _dot_nt(qe[rows], st.astype(bf16)))
        st = st * jnp.exp(b_last[c]) + upd
    st_ref[...] = st
    return jnp.concatenate(outs, axis=0) if want_out else None


def _gla_kernel(q_ref, k_ref, v_ref, r_ref, a_ref, km_ref, vm_ref, am_ref, wa2_ref, ba_ref, gain_ref,
                y_ref, st_ref):
    j = pl.program_id(1)

    @pl.when(j == 0)
    def _():
        st_ref[...] = jnp.zeros_like(st_ref)
        la = _gla_log_decay(am_ref[...], wa2_ref, ba_ref)
        row = lax.broadcasted_iota(i32, la.shape, 0)
        la = jnp.where(row >= CHUNK - N_META, la, 0.0)
        _gla_tile(None, km_ref[...], vm_ref[...], la, st_ref, False)

    la = _gla_log_decay(a_ref[...], wa2_ref, ba_ref)
    o = _gla_tile(q_ref[...], k_ref[...], v_ref[...], la, st_ref, True)
    r = r_ref[...].astype(f32)
    outs = []
    for h in range(GLA_HEADS):
        oh = o[:, h * GLA_DV:(h + 1) * GLA_DV]
        outs.append(oh * lax.rsqrt(jnp.mean(oh * oh, axis=-1, keepdims=True) + EPS))
    on = jnp.concatenate(outs, axis=1) * gain_ref[...]
    y_ref[...] = (on * (r * jax.nn.sigmoid(r))).astype(bf16)


def _gla_call(qg, kg, vg, rg, ag, km, vm, am, wa2_p, b_a, gain, batch, seq):
    nj = seq // GLA_TILE

    def row(b, j):
        return (b * nj + j, 0)

    def const(b, j):
        return (0, 0)

    return pl.pallas_call(
        _gla_kernel,
        grid=(batch, nj),
        in_specs=[
            pl.BlockSpec((GLA_TILE, GLA_QK), row),
            pl.BlockSpec((GLA_TILE, GLA_QK), row),
            pl.BlockSpec((GLA_TILE, GLA_VW), row),
            pl.BlockSpec((GLA_TILE, GLA_VW), row),
            pl.BlockSpec((GLA_TILE, LANE), row),
            pl.BlockSpec((CHUNK, GLA_QK), const),
            pl.BlockSpec((CHUNK, GLA_VW), const),
            pl.BlockSpec((CHUNK, LANE), const),
            pl.BlockSpec((LANE, GLA_QK), const),
            pl.BlockSpec((1, GLA_QK), const),
            pl.BlockSpec((1, GLA_VW), const),
        ],
        out_specs=pl.BlockSpec((GLA_TILE, GLA_VW), row),
        out_shape=jax.ShapeDtypeStruct((batch * seq, GLA_VW), bf16),
        scratch_shapes=[pltpu.VMEM((GLA_VW, GLA_QK), f32)],
        compiler_params=pltpu.CompilerParams(
            dimension_semantics=("parallel", "arbitrary"), vmem_limit_bytes=VMEM_LIMIT),
        name="gla",
    )(qg, kg, vg, rg, ag, km, vm, am, wa2_p, b_a, gain)


def _mla_kernel(q_ref, k_ref, vt_ref, km_ref, vmt_ref, o_ref, sa_ref, sb_ref):
    i = pl.program_id(2)
    tq = ATT_TILE
    w = MLA_QK_PAD
    va = MLA_VA
    heads = range(ATT_HEADS)

    def scores(h, blk):
        rows = pl.ds(pl.multiple_of(blk * tq, tq), tq)
        return _dot_nt(k_ref[rows, h * w:(h + 1) * w], q_ref[:, h * w:(h + 1) * w])

    def soft(s, vtb, carry, mask=None):
        m, acc = carry
        if mask is not None:
            s = jnp.where(mask, s, -1e30)
        m_new = jnp.maximum(m, jnp.max(s, axis=0, keepdims=True))
        p = jnp.exp2(s - m_new).astype(bf16)
        return m_new, jnp.exp2(m - m_new) * acc + _dot(vtb, p)

    def vt(h, blk):
        return vt_ref[blk, h * va:(h + 1) * va, :]

    def finish(h, carry):
        s = _dot_nt(km_ref[:, h * w:(h + 1) * w], q_ref[:, h * w:(h + 1) * w])
        m, acc = soft(s, vmt_ref[h * va:(h + 1) * va, :], carry)
        o_ref[:, h * MLA_V:(h + 1) * MLA_V] = (acc[:MLA_V] * (1.0 / acc[MLA_V:MLA_V + 1])).T.astype(bf16)

    kc = lax.broadcasted_iota(i32, (tq, tq), 0) // CHUNK
    qc = lax.broadcasted_iota(i32, (tq, tq), 1) // CHUNK
    mask = kc <= qc

    for h in heads:
        sa_ref[h] = scores(h, 0)

    def pair(p, carries):
        b0 = 2 * p
        for h in heads:
            sb_ref[h] = scores(h, b0 + 1)
        carries = [soft(sa_ref[h], vt(h, b0), carries[h]) for h in heads]
        for h in heads:
            sa_ref[h] = scores(h, b0 + 2)
        return tuple(soft(sb_ref[h], vt(h, b0 + 1), carries[h]) for h in heads)

    init = tuple((jnp.full((1, tq), -1e30, f32), jnp.zeros((va, tq), f32)) for _ in heads)
    carries = lax.fori_loop(0, i // 2, pair, init)

    @pl.when(i % 2 == 1)
    def _():
        for h in heads:
            sb_ref[h] = scores(h, i)
        for h in heads:
            c = soft(sa_ref[h], vt(h, i - 1), carries[h])
            finish(h, soft(sb_ref[h], vt(h, i), c, mask))

    @pl.when(i % 2 == 0)
    def _():
        for h in heads:
            finish(h, soft(sa_ref[h], vt(h, i), carries[h], mask))


def _mla_call(qm, km, vmt, km_meta, vmt_meta, batch, seq):
    nq = seq // ATT_TILE
    nh = ATT_HEADS
    qm3 = qm.reshape(batch, seq, MLA_HEADS * MLA_QK_PAD)
    km3 = km.reshape(batch, seq, MLA_HEADS * MLA_QK_PAD)
    vt4 = vmt.reshape(batch, nq, MLA_HEADS * MLA_VA, ATT_TILE)
    out = pl.pallas_call(
        _mla_kernel,
        grid=(batch, MLA_HEADS // nh, nq),
        in_specs=[
            pl.BlockSpec((None, ATT_TILE, nh * MLA_QK_PAD), lambda b, h, i: (b, i, h)),
            pl.BlockSpec((None, seq, nh * MLA_QK_PAD), lambda b, h, i: (b, 0, h)),
            pl.BlockSpec((None, nq, nh * MLA_VA, ATT_TILE), lambda b, h, i: (b, 0, h, 0)),
            pl.BlockSpec((N_META, nh * MLA_QK_PAD), lambda b, h, i: (0, h)),
            pl.BlockSpec((nh * MLA_VA, N_META), lambda b, h, i: (h, 0)),
        ],
        out_specs=pl.BlockSpec((None, ATT_TILE, nh * MLA_V), lambda b, h, i: (b, i, h)),
        out_shape=jax.ShapeDtypeStruct((batch, seq, MLA_OUT), bf16),
        scratch_shapes=[pltpu.VMEM((nh, ATT_TILE, ATT_TILE), f32), pltpu.VMEM((nh, ATT_TILE, ATT_TILE), f32)],
        compiler_params=pltpu.CompilerParams(
            dimension_semantics=("parallel", "parallel", "arbitrary"), vmem_limit_bytes=VMEM_LIMIT),
        name="mla",
    )(qm3, km3, vt4, km_meta, vmt_meta)
    return out.reshape(batch * seq, MLA_OUT)


def _route_cols(lt):
    r = lt.shape[1]
    neg = -1e30
    gl = lt[0:N_GROUPS, :]
    gsub = lax.broadcasted_iota(i32, (N_GROUPS, r), 0)
    gmax = jnp.max(gl, axis=0, keepdims=True)
    g_p = 1.0 / jnp.sum(jnp.exp(gl - gmax), axis=0, keepdims=True)
    g_idx = jnp.min(jnp.where(gl == gmax, gsub, N_GROUPS), axis=0, keepdims=True)
    el_all = lt[N_GROUPS:N_GROUPS + N_EXPERTS, :]
    esub = lax.broadcasted_iota(i32, (N_EXPERTS, r), 0)
    base = g_idx * EXPERTS_PER_GROUP
    e_mask = (esub >= base) & (esub < base + EXPERTS_PER_GROUP)
    el = jnp.where(e_mask, el_all, neg)
    m1 = jnp.max(el, axis=0, keepdims=True)
    i1 = jnp.min(jnp.where(e_mask & (el == m1), esub, N_EXPERTS), axis=0, keepdims=True)
    el2 = jnp.where(esub == i1, neg, el)
    m2 = jnp.max(el2, axis=0, keepdims=True)
    i2 = jnp.min(jnp.where(e_mask & (esub != i1) & (el2 == m2), esub, N_EXPERTS), axis=0, keepdims=True)
    rr = jnp.exp(m2 - m1)
    ga = g_p / (1.0 + rr)
    gb = g_p * rr / (1.0 + rr)
    la_ = i1 - base
    lb_ = i2 - base
    lo = jnp.minimum(la_, lb_)
    hi = jnp.maximum(la_, lb_)
    g_lo = jnp.where(la_ < lb_, ga, gb)
    g_hi = jnp.where(la_ < lb_, gb, ga)
    pidx = ((lo * (2 * EXPERTS_PER_GROUP - 1 - lo)) >> 1) + (hi - lo - 1)
    bucket = g_idx * N_PAIRS + pidx
    bsub = lax.broadcasted_iota(i32, (BUCKET_LANES, r), 0)
    oht = jnp.where(bsub == bucket, 1.0, 0.0)
    ohb = oht.astype(bf16)
    ri = lax.broadcasted_iota(i32, (r, r), 0)
    ci = lax.broadcasted_iota(i32, (r, r), 1)
    before = jnp.where(ri < ci, 1.0, 0.0).astype(bf16)
    cum = _dot(ohb, before)
    rank = jnp.sum(oht * cum, axis=0, keepdims=True)
    counts = _dot_nt(jnp.ones((SUBLANES, r), bf16), ohb)[0:1, :]
    msub = lax.broadcasted_iota(i32, (LANE, r), 0)
    meta_t = jnp.where(msub == 0, bucket.astype(f32),
                       jnp.where(msub == 1, rank,
                                 jnp.where(msub == 2, g_lo, jnp.where(msub == 3, g_hi, 0.0))))
    return meta_t.T, counts


def _outproj_kernel(x_ref, yg_ref, ym_ref, wog_ref, wom_ref, gain_ref, wrt_ref, rb_ref,
                    h_ref, ux_ref, cnt_ref):
    for sub in range(OUT_TILE // ROUTE_ROWS):
        rows = slice(sub * ROUTE_ROWS, (sub + 1) * ROUTE_ROWS)
        h1 = x_ref[rows, :] + _dot(yg_ref[rows, :], wog_ref[...]) + _dot(ym_ref[rows, :], wom_ref[...])
        h_ref[rows, :] = h1
        u2 = _rms(h1, gain_ref[...])
        ux_ref[rows, 0:D_MODEL] = u2
        lt = _dot_nt(wrt_ref[...], u2.astype(bf16)) + rb_ref[...]
        meta, counts = _route_cols(lt)
        ux_ref[rows, D_MODEL:ROW_W] = meta
        cnt_ref[sub] = counts


def _outproj_call(x2d, yg, ym, wo_g, wo_m, gain, w_r, rbias):
    t = x2d.shape[0]
    nt = t // OUT_TILE

    def row(i):
        return (i, 0)

    def const(i):
        return (0, 0)

    return pl.pallas_call(
        _outproj_kernel,
        grid=(nt,),
        in_specs=[
            pl.BlockSpec((OUT_TILE, D_MODEL), row),
            pl.BlockSpec((OUT_TILE, GLA_VW), row),
            pl.BlockSpec((OUT_TILE, MLA_OUT), row),
            pl.BlockSpec((GLA_VW, D_MODEL), const),
            pl.BlockSpec((MLA_OUT, D_MODEL), const),
            pl.BlockSpec((1, D_MODEL), const),
            pl.BlockSpec((LANE, D_MODEL), const),
            pl.BlockSpec((LANE, 1), const),
        ],
        out_specs=[
            pl.BlockSpec((OUT_TILE, D_MODEL), row),
            pl.BlockSpec((OUT_TILE, ROW_W), row),
            pl.BlockSpec((OUT_TILE // ROUTE_ROWS, 1, BUCKET_LANES), lambda i: (i, 0, 0)),
        ],
        out_shape=[
            jax.ShapeDtypeStruct((t, D_MODEL), f32),
            jax.ShapeDtypeStruct((t, ROW_W), f32),
            jax.ShapeDtypeStruct((nt * (OUT_TILE // ROUTE_ROWS), 1, BUCKET_LANES), f32),
        ],
        compiler_params=pltpu.CompilerParams(
            dimension_semantics=("parallel",), vmem_limit_bytes=VMEM_LIMIT),
        name="outproj",
    )(x2d, yg, ym, wo_g, wo_m, gain, w_r, rbias)


def _scatter_kernel(pos_ref, ux_ref, hs_in_ref, hs_ref, sem):
    del hs_in_ref

    def start(ii, c):
        for k in range(SUBLANES):
            pltpu.make_async_copy(ux_ref.at[ii, pl.ds(k, 1)],
                                  hs_ref.at[pl.ds(pos_ref[ii * SUBLANES + k], 1)], sem).start()
        return c

    lax.fori_loop(0, SCATTER_TILE // SUBLANES, start, 0)
    pltpu.make_async_copy(hs_ref.at[pl.ds(0, SCATTER_TILE)], hs_ref.at[pl.ds(0, SCATTER_TILE)], sem).wait()


def _scatter_call(pos, ux, hs0):
    t = ux.shape[0]
    return pl.pallas_call(
        _scatter_kernel,
        grid=(t // SCATTER_TILE,),
        in_specs=[
            pl.BlockSpec((SCATTER_TILE,), lambda i: (i,), memory_space=pltpu.SMEM),
            pl.BlockSpec((SCATTER_TILE // SUBLANES, SUBLANES, ROW_W), lambda i: (i, 0, 0)),
            pl.BlockSpec(memory_space=pl.ANY),
        ],
        out_specs=pl.BlockSpec(memory_space=pl.ANY),
        out_shape=jax.ShapeDtypeStruct(hs0.shape, f32),
        scratch_shapes=[pltpu.SemaphoreType.DMA(())],
        input_output_aliases={2: 0},
        compiler_params=pltpu.CompilerParams(
            dimension_semantics=("arbitrary",), vmem_limit_bytes=VMEM_LIMIT),
        name="scatter",
    )(pos, ux.reshape(t // SUBLANES, SUBLANES, ROW_W), hs0)


def _moe_kernel(se_ref, sf_ref, sk_ref, sb_ref, sr_ref, si_ref, so_ref,
                hs_hbm, wg_ref, wu_ref, wd_ref, y_hbm, xbuf, obuf, wgu_s, wd_s, in_sem, out_sem):
    s = pl.program_id(0)
    ns = pl.num_programs(0)
    cur = s % 2
    g_n = MOE_GROUP

    def in_copy(step, g, buf):
        rows = pl.ds(pl.multiple_of(sb_ref[step * g_n + g] * MOE_BLOCK, MOE_BLOCK), MOE_BLOCK)
        return pltpu.make_async_copy(
            hs_hbm.at[rows], xbuf.at[buf, pl.ds(g * MOE_BLOCK, MOE_BLOCK)], in_sem.at[buf])

    def out_copy(step, g, buf):
        rows = pl.ds(pl.multiple_of(sb_ref[step * g_n + g] * MOE_BLOCK, MOE_BLOCK), MOE_BLOCK)
        cols = pl.ds(pl.multiple_of(sr_ref[step * g_n + g] * D_MODEL, D_MODEL), D_MODEL)
        return pltpu.make_async_copy(
            obuf.at[buf, pl.ds(g * MOE_BLOCK, MOE_BLOCK)], y_hbm.at[rows, cols], out_sem.at[buf])

    def for_slots(step, flags_ref, fn):
        for g in range(g_n):
            @pl.when(flags_ref[step * g_n + g] == 1)
            def _():
                fn(g)

    @pl.when(s == 0)
    def _():
        xbuf[...] = jnp.zeros_like(xbuf)
        for_slots(0, si_ref, lambda g: in_copy(0, g, 0).start())

    @pl.when(s + 1 < ns)
    def _():
        for_slots(s + 1, si_ref, lambda g: in_copy(s + 1, g, 1 - cur).start())

    for_slots(s, si_ref, lambda g: in_copy(s, g, cur).wait())

    @pl.when(s >= 2)
    def _():
        for_slots(s - 2, so_ref, lambda g: out_copy(s - 2, g, cur).wait())

    @pl.when(sf_ref[s] == 1)
    def _():
        wgu_s[:, 0:D_EXPERT] = wg_ref[...].astype(bf16)
        wgu_s[:, D_EXPERT:2 * D_EXPERT] = wu_ref[...].astype(bf16)
        wd_s[...] = wd_ref[...].astype(bf16)

    @pl.when(sk_ref[s] == 1)
    def _():
        u = xbuf[cur, :, 0:D_MODEL].astype(bf16)
        meta = xbuf[cur, :, D_MODEL:ROW_W]
        gate = jnp.concatenate(
            [jnp.where(sr_ref[s * g_n + g] == 0, meta[g * MOE_BLOCK:(g + 1) * MOE_BLOCK, 2:3],
                       meta[g * MOE_BLOCK:(g + 1) * MOE_BLOCK, 3:4]) for g in range(g_n)], axis=0)
        gu = _dot(u, wgu_s[...])
        gt = gu[:, 0:D_EXPERT]
        hdn = (gt * jax.nn.sigmoid(gt) * gu[:, D_EXPERT:]).astype(bf16)
        obuf[cur] = _dot(hdn, wd_s[...]) * gate

    @pl.when(sk_ref[s] == 0)
    def _():
        obuf[cur] = jnp.zeros(obuf.shape[1:], f32)

    for_slots(s, so_ref, lambda g: out_copy(s, g, cur).start())

    @pl.when(s == ns - 1)
    def _():
        for_slots(s, so_ref, lambda g: out_copy(s, g, cur).wait())

        @pl.when(s >= 1)
        def _():
            for_slots(s - 1, so_ref, lambda g: out_copy(s - 1, g, 1 - cur).wait())


def _moe_call(plan, hs, w_gate, w_up, w_down):
    n_steps = plan[0].shape[0]
    n_slots = hs.shape[0]
    rows = MOE_GROUP * MOE_BLOCK

    def wmap(s, se, *_):
        return (se[s], 0, 0)

    grid_spec = pltpu.PrefetchScalarGridSpec(
        num_scalar_prefetch=7,
        grid=(n_steps,),
        in_specs=[
            pl.BlockSpec(memory_space=pl.ANY),
            pl.BlockSpec((None, D_MODEL, D_EXPERT), wmap),
            pl.BlockSpec((None, D_MODEL, D_EXPERT), wmap),
            pl.BlockSpec((None, D_EXPERT, D_MODEL), wmap),
        ],
        out_specs=pl.BlockSpec(memory_space=pl.ANY),
        scratch_shapes=[
            pltpu.VMEM((2, rows, ROW_W), f32),
            pltpu.VMEM((2, rows, D_MODEL), f32),
            pltpu.VMEM((D_MODEL, 2 * D_EXPERT), bf16),
            pltpu.VMEM((D_EXPERT, D_MODEL), bf16),
            pltpu.SemaphoreType.DMA((2,)),
            pltpu.SemaphoreType.DMA((2,)),
        ],
    )
    return pl.pallas_call(
        _moe_kernel,
        grid_spec=grid_spec,
        out_shape=jax.ShapeDtypeStruct((n_slots, 2 * D_MODEL), f32),
        compiler_params=pltpu.CompilerParams(
            dimension_semantics=("arbitrary",), vmem_limit_bytes=VMEM_LIMIT),
        name="moe",
    )(*plan, hs, w_gate, w_up, w_down)


def _final_kernel(posc_ref, posn_ref, h_ref, gain_ref, y_hbm, o_ref, ybuf, sem):
    i = pl.program_id(0)
    cur = i % 2

    def issue(pos_ref, buf):
        def start(ii, c):
            for k in range(SUBLANES):
                pltpu.make_async_copy(y_hbm.at[pl.ds(pos_ref[ii * SUBLANES + k], 1)],
                                      ybuf.at[buf, ii, pl.ds(k, 1)], sem.at[buf]).start()
            return c

        lax.fori_loop(0, FINAL_TILE // SUBLANES, start, 0)

    @pl.when(i == 0)
    def _():
        issue(posc_ref, 0)

    @pl.when(i + 1 < pl.num_programs(0))
    def _():
        issue(posn_ref, 1 - cur)

    pltpu.make_async_copy(ybuf.at[cur], ybuf.at[cur], sem.at[cur]).wait()
    h = h_ref[...] + ybuf[cur, :, :, 0:D_MODEL] + ybuf[cur, :, :, D_MODEL:2 * D_MODEL]
    o_ref[...] = _rms(h, gain_ref[...])


def _final_call(pos, h1, gain, y):
    t = h1.shape[0]
    n = t // FINAL_TILE
    rows = FINAL_TILE // SUBLANES
    out = pl.pallas_call(
        _final_kernel,
        grid=(n,),
        in_specs=[
            pl.BlockSpec((FINAL_TILE,), lambda i: (i,), memory_space=pltpu.SMEM),
            pl.BlockSpec((FINAL_TILE,), lambda i: (jnp.minimum(i + 1, n - 1),), memory_space=pltpu.SMEM),
            pl.BlockSpec((rows, SUBLANES, D_MODEL), lambda i: (i, 0, 0)),
            pl.BlockSpec((1, 1, D_MODEL), lambda i: (0, 0, 0)),
            pl.BlockSpec(memory_space=pl.ANY),
        ],
        out_specs=pl.BlockSpec((rows, SUBLANES, D_MODEL), lambda i: (i, 0, 0)),
        out_shape=jax.ShapeDtypeStruct((t // SUBLANES, SUBLANES, D_MODEL), f32),
        scratch_shapes=[pltpu.VMEM((2, rows, SUBLANES, 2 * D_MODEL), f32), pltpu.SemaphoreType.DMA((2,))],
        compiler_params=pltpu.CompilerParams(
            dimension_semantics=("arbitrary",), vmem_limit_bytes=VMEM_LIMIT),
        name="final",
    )(pos, pos, h1.reshape(t // SUBLANES, SUBLANES, D_MODEL), gain.reshape(1, 1, D_MODEL), y)
    return out.reshape(t, D_MODEL)


def _rope_tables(pos):
    inv = ROPE_BASE ** (-jnp.arange(0, MLA_ROPE, 2, dtype=f32) / MLA_ROPE)
    ang = pos.astype(f32)[:, None] * inv[None, :]
    cos, sin = jnp.cos(ang), jnp.sin(ang)
    z = jnp.zeros((pos.shape[0], LANE - MLA_ROPE), f32)
    return jnp.concatenate([cos, cos, z], axis=1), jnp.concatenate([-sin, sin, z], axis=1)


def _relayout_weights(w_in, w_qb, w_kvb):
    half = MLA_ROPE // 2
    perm = (np.arange(MLA_ROPE) + half) % MLA_ROPE
    pts = np.cumsum((GLA_QK, GLA_QK, GLA_VW, GLA_VW, GLA_GATE_RANK, MLA_Q_RANK, MLA_KV_RANK, MLA_ROPE))
    q_g, k_g, v_g, r_g, a_l, q_lat, kv_lat, k_rope = jnp.split(w_in, pts[:-1], axis=1)
    a_seg = jnp.pad(a_l, ((0, 0), (0, LANE - GLA_GATE_RANK)))
    w_in_r = jnp.concatenate(
        [q_g, k_g, v_g, r_g, q_lat, kv_lat, k_rope, k_rope[:, perm], a_seg], axis=1).astype(bf16)
    qcols, kcols, vcols = [], [], []
    for h in range(MLA_HEADS):
        c = h * (MLA_NOPE + MLA_ROPE)
        rope = w_qb[:, c + MLA_NOPE:c + MLA_NOPE + MLA_ROPE]
        qcols += [w_qb[:, c:c + MLA_NOPE], rope, rope[:, perm]]
        c2 = h * (MLA_NOPE + MLA_V)
        kcols.append(w_kvb[:, c2:c2 + MLA_NOPE])
        vcols.append(w_kvb[:, c2 + MLA_NOPE:c2 + MLA_NOPE + MLA_V])
    return w_in_r, jnp.concatenate(qcols, axis=1).astype(bf16), jnp.concatenate(kcols + vcols, axis=1).astype(bf16)


_BUCKET_GROUP = np.arange(N_BUCKETS) // N_PAIRS
_RUN_EXPERT = np.concatenate([_BUCKET_GROUP * EXPERTS_PER_GROUP + _PAIR_LO[np.arange(N_BUCKETS) % N_PAIRS],
                              _BUCKET_GROUP * EXPERTS_PER_GROUP + _PAIR_HI[np.arange(N_BUCKETS) % N_PAIRS]])
_RUN_IS_EXPERT = (_RUN_EXPERT[:, None] == np.arange(N_EXPERTS)[None, :]).astype(np.int32)
_RUN_BEFORE = ((_RUN_EXPERT[:, None] == _RUN_EXPERT[None, :])
               & (np.arange(2 * N_BUCKETS)[None, :] < np.arange(2 * N_BUCKETS)[:, None])).astype(np.int32)


def _route_plan(counts, bucket, rank, n_tok):
    nt = counts.shape[0]
    g_n = MOE_GROUP
    tot = counts.sum(axis=0)
    nblk = (tot + MOE_BLOCK - 1) // MOE_BLOCK
    bstart_blk = jnp.cumsum(nblk) - nblk
    n_blocks = jnp.sum(nblk)
    tile_base = bstart_blk[None, :] * MOE_BLOCK + jnp.cumsum(counts, axis=0) - counts
    hit = bucket.reshape(nt, -1, 1) == jnp.arange(N_BUCKETS, dtype=i32)
    pos = jnp.sum(jnp.where(hit, tile_base[:, None, :], 0), axis=-1).reshape(-1) + rank
    nb_max = (n_tok + N_BUCKETS * (MOE_BLOCK - 1)) // MOE_BLOCK

    n_run = jnp.concatenate([nblk, nblk])
    b0_run = jnp.concatenate([bstart_blk, bstart_blk])
    c_e = jnp.sum(n_run[:, None] * _RUN_IS_EXPERT, axis=0)
    g_e = (c_e + g_n - 1) // g_n
    gend = jnp.cumsum(g_e)
    gstart = gend - g_e
    n_compute = gend[-1]
    off_run = jnp.sum(_RUN_BEFORE * n_run[None, :], axis=1)
    f_run = jnp.sum(_RUN_IS_EXPERT * gstart[None, :], axis=1) * g_n + off_run

    n_steps = (2 * nb_max + N_EXPERTS * (g_n - 1) + g_n - 1) // g_n + 1
    f = jnp.arange(n_steps * g_n, dtype=i32)
    in_run = (f[:, None] >= f_run[None, :]) & (f[:, None] < (f_run + n_run)[None, :])
    valid_c = jnp.any(in_run, axis=1)
    block_c = jnp.sum(jnp.where(in_run, b0_run[None, :] + f[:, None] - f_run[None, :], 0), axis=1)
    role_c = jnp.sum(jnp.where(in_run[:, N_BUCKETS:], 1, 0), axis=1)
    u_idx = f - n_compute * g_n
    valid_f = (u_idx >= 0) & (u_idx < 2 * (nb_max - n_blocks))
    slot_block = jnp.where(valid_c, block_c, jnp.where(valid_f, n_blocks + u_idx // 2, 0))
    slot_role = jnp.where(valid_c, role_c, jnp.where(valid_f, u_idx % 2, 0))

    step = jnp.arange(n_steps, dtype=i32)
    e_of_step = jnp.minimum(jnp.sum(gend[None, :] <= step[:, None], axis=1), N_EXPERTS - 1)
    is_compute = step < n_compute
    last_e = jnp.max(jnp.where(is_compute, e_of_step, 0))
    step_expert = jnp.where(is_compute, e_of_step, last_e)
    step_first = jnp.concatenate([jnp.ones((1,), bool), step_expert[1:] != step_expert[:-1]])
    plan = tuple(a.astype(i32) for a in
                 (step_expert, step_first, is_compute, slot_block, slot_role, valid_c, valid_c | valid_f))
    return pos.astype(i32), plan, nb_max


def kernel(x, meta_tokens, mix_norm, w_in, gla_w_a2, gla_b_a, gla_out_norm, mla_q_norm, mla_w_qb, mla_kv_norm,
           mla_w_kvb, w_out, ffn_norm, router_group_w, router_group_b, router_expert_w, router_expert_b,
           expert_w_gate, expert_w_up, expert_w_down, final_norm):
    batch, seq, d = x.shape
    assert d == D_MODEL and seq % max(PREP_TILE, GLA_TILE, ATT_TILE) == 0
    assert (batch * seq) % max(OUT_TILE, SCATTER_TILE, FINAL_TILE) == 0
    n_tok = batch * seq
    x2d = x.reshape(n_tok, d)

    w_in_r, w_qb_r, w_kvb_r = _relayout_weights(w_in[0], mla_w_qb[0], mla_w_kvb[0])
    mixg = mix_norm[0].reshape(1, d)
    qn = mla_q_norm[0].reshape(1, MLA_Q_RANK)
    kvn = mla_kv_norm[0].reshape(1, MLA_KV_RANK)
    ct_m, st_m = _rope_tables(jnp.arange(META_TILE))
    ct_x, st_x = _rope_tables(N_META + jnp.arange(seq))

    x_meta = jnp.pad(meta_tokens.astype(f32), ((0, META_TILE - N_META), (0, 0)))
    _, kg_m, vg_m, _, a_m, _, km_m, vmt_m = _prep_call(
        x_meta, META_TILE, META_TILE, mixg, w_in_r, qn, w_qb_r, kvn, w_kvb_r, ct_m, st_m)
    qg, kg, vg, rg, ag, qm, km, vmt = _prep_call(
        x2d, seq, PREP_TILE, mixg, w_in_r, qn, w_qb_r, kvn, w_kvb_r, ct_x, st_x)

    def chunk0(a):
        return jnp.pad(a[:N_META], ((CHUNK - N_META, 0), (0, 0)))

    wa2_p = jnp.pad(gla_w_a2[0], ((0, LANE - GLA_GATE_RANK), (0, 0))).astype(bf16)
    y_gla = _gla_call(qg, kg, vg, rg, ag, chunk0(kg_m), chunk0(vg_m), chunk0(a_m),
                      wa2_p, gla_b_a[0].reshape(1, GLA_QK), gla_out_norm[0].reshape(1, GLA_VW), batch, seq)
    y_mla = _mla_call(qm, km, vmt, km_m[:N_META], vmt_m[0, :, :N_META], batch, seq)

    wo = w_out[0].astype(bf16)
    rw = jnp.concatenate([router_group_w[0], router_expert_w[0],
                          jnp.zeros((d, LANE - N_GROUPS - N_EXPERTS), f32)], axis=1)
    rb = jnp.concatenate([router_group_b[0], router_expert_b[0],
                          jnp.zeros((LANE - N_GROUPS - N_EXPERTS,), f32)]).reshape(1, LANE)
    h1, ux, cnt = _outproj_call(x2d, y_gla, y_mla, wo[:GLA_VW], wo[GLA_VW:], ffn_norm[0].reshape(1, d),
                                      rw.T.astype(bf16), rb.reshape(LANE, 1))

    counts = cnt.reshape(-1, BUCKET_LANES)[:, :N_BUCKETS].astype(i32)
    tok_meta = ux[:, D_MODEL:D_MODEL + 2].astype(i32)
    pos, plan, nb_max = _route_plan(counts, tok_meta[:, 0], tok_meta[:, 1], n_tok)
    n_slots = nb_max * MOE_BLOCK
    hs = _scatter_call(pos, ux, jnp.zeros((n_slots, ROW_W), f32))
    y = _moe_call(plan, hs, expert_w_gate[0], expert_w_up[0], expert_w_down[0])
    out = _final_call(pos, h1, final_norm.reshape(1, d), y)
    return out.reshape(batch, seq, d)
```

```python
import functools

import numpy as np
import jax
import jax.numpy as jnp
from jax import lax
from jax.experimental import pallas as pl
from jax.experimental.pallas import tpu as pltpu

f32 = jnp.float32
bf16 = jnp.bfloat16
i32 = jnp.int32

D_MODEL = 1024
CHUNK = 64
N_META = 16
EPS = 1e-6
GLA_HEADS = 4
GLA_DK = 64
GLA_DV = 128
GLA_GATE_RANK = 16
GLA_TAU = 16.0
GLA_QK = GLA_HEADS * GLA_DK
GLA_VW = GLA_HEADS * GLA_DV
MLA_HEADS = 4
MLA_Q_RANK = 256
MLA_KV_RANK = 128
MLA_NOPE = 128
MLA_ROPE = 64
MLA_V = 128
MLA_OUT = MLA_HEADS * MLA_V
MLA_QK_PAD = 256
MLA_VA = MLA_V + 16
LOG2_E = 1.4426950408889634
ROPE_BASE = 10000.0
N_GROUPS = 8
EXPERTS_PER_GROUP = 8
N_EXPERTS = N_GROUPS * EXPERTS_PER_GROUP
D_EXPERT = 512
N_PAIRS = EXPERTS_PER_GROUP * (EXPERTS_PER_GROUP - 1) // 2
N_BUCKETS = N_GROUPS * N_PAIRS
BUCKET_LANES = 256
LANE = 128
SUBLANES = 8
META_W = LANE
ROW_W = D_MODEL + META_W

PREP_TILE = 512
GLA_TILE = 512
ATT_TILE = 512
ATT_HEADS = 4
META_TILE = 128
OUT_TILE = 1024
ROUTE_ROWS = 256
SCATTER_TILE = 1024
FINAL_TILE = 256
MOE_BLOCK = 64
MOE_GROUP = 8
VMEM_LIMIT = 56 * 1024 * 1024

C_Q, C_K, C_V, C_R = 0, 256, 512, 1024
C_QLAT, C_KVLAT, C_KROPE, C_A, C_END = 1536, 1792, 1920, 2048, 2176

_TILE_POS = np.arange(GLA_TILE)
_CHUNK_PREFIX = ((_TILE_POS[:, None] // CHUNK == _TILE_POS[None, :] // CHUNK)
                 & (_TILE_POS[None, :] <= _TILE_POS[:, None])).astype(np.float32)
_PAIR_LO = np.array([lo for lo in range(8) for hi in range(lo + 1, 8)], np.int32)
_PAIR_HI = np.array([hi for lo in range(8) for hi in range(lo + 1, 8)], np.int32)


def _dot(a, b):
    return jnp.dot(a, b, preferred_element_type=f32)


def _dot_nt(a, b):
    return lax.dot_general(a, b, (((1,), (1,)), ((), ())), preferred_element_type=f32)


def _dot_tn(a, b):
    return lax.dot_general(a, b, (((0,), (0,)), ((), ())), preferred_element_type=f32)


def _rms(x, gain):
    return x * lax.rsqrt(jnp.mean(x * x, axis=-1, keepdims=True) + EPS) * gain


def _split3(x):
    hi = x.astype(bf16)
    r1 = x - hi.astype(f32)
    mid = r1.astype(bf16)
    lo = (r1 - mid.astype(f32)).astype(bf16)
    return hi, mid, lo


def _prep_kernel(x_ref, g_ref, win_ref, qn_ref, wqb_ref, kvn_ref, wkvb_ref, ct_ref, st_ref,
                 qg_ref, kg_ref, vg_ref, rg_ref, a_ref, qm_ref, km_ref, vmt_ref):
    u = _rms(x_ref[...], g_ref[...]).astype(bf16)

    def proj(lo, hi):
        return _dot(u, win_ref[:, lo:hi])

    qg_ref[...] = proj(C_Q, C_K).astype(bf16)
    kg_ref[...] = proj(C_K, C_V).astype(bf16)
    vg_ref[...] = proj(C_V, C_R).astype(bf16)
    rg_ref[...] = proj(C_R, C_QLAT).astype(bf16)
    z = proj(C_QLAT, C_END)
    a_ref[...] = z[:, C_A - C_QLAT:].astype(bf16)
    ctab = ct_ref[...]
    stab = st_ref[...]

    def rope(seg):
        return seg * ctab + pltpu.roll(seg, 64, axis=1) * stab

    k_rope = rope(z[:, C_KROPE - C_QLAT:C_A - C_QLAT]).astype(bf16)
    qn = _rms(z[:, 0:MLA_Q_RANK], qn_ref[...]).astype(bf16)
    kvn = _rms(z[:, MLA_Q_RANK:MLA_Q_RANK + MLA_KV_RANK], kvn_ref[...]).astype(bf16)
    scale = (MLA_NOPE + MLA_ROPE) ** -0.5 * LOG2_E
    qf = _dot(qn, wqb_ref[...])
    kvf = _dot(kvn, wkvb_ref[...])
    for h in range(MLA_HEADS):
        c = h * MLA_QK_PAD
        qm_ref[:, c:c + LANE] = (qf[:, c:c + LANE] * scale).astype(bf16)
        qm_ref[:, c + LANE:c + 2 * LANE] = (rope(qf[:, c + LANE:c + 2 * LANE]) * scale).astype(bf16)
        km_ref[:, c:c + LANE] = kvf[:, h * LANE:(h + 1) * LANE].astype(bf16)
        km_ref[:, c + LANE:c + 2 * LANE] = k_rope
    vt = kvf[:, MLA_HEADS * MLA_NOPE:].T
    for h in range(MLA_HEADS):
        vmt_ref[h * MLA_VA:h * MLA_VA + MLA_V, :] = vt[h * MLA_V:(h + 1) * MLA_V].astype(bf16)
        vmt_ref[h * MLA_VA + MLA_V:(h + 1) * MLA_VA, :] = jnp.ones((MLA_VA - MLA_V, vt.shape[1]), bf16)


def _prep_call(x2d, rows_per_seq, tile, gain, w_in_r, q_norm, w_qb_r, kv_norm, w_kvb_r, ctab, stab):
    t = x2d.shape[0]
    nj = rows_per_seq // tile
    grid = (t // rows_per_seq, nj)

    def row(b, j):
        return (b * nj + j, 0)

    def const(b, j):
        return (0, 0)

    def tab(b, j):
        return (j, 0)

    widths = (GLA_QK, GLA_QK, GLA_VW, GLA_VW, LANE, MLA_HEADS * MLA_QK_PAD, MLA_HEADS * MLA_QK_PAD)
    return pl.pallas_call(
        _prep_kernel,
        grid=grid,
        in_specs=[
            pl.BlockSpec((tile, D_MODEL), row),
            pl.BlockSpec((1, D_MODEL), const),
            pl.BlockSpec((D_MODEL, C_END), const),
            pl.BlockSpec((1, MLA_Q_RANK), const),
            pl.BlockSpec((MLA_Q_RANK, MLA_HEADS * MLA_QK_PAD), const),
            pl.BlockSpec((1, MLA_KV_RANK), const),
            pl.BlockSpec((MLA_KV_RANK, 2 * MLA_OUT), const),
            pl.BlockSpec((tile, LANE), tab),
            pl.BlockSpec((tile, LANE), tab),
        ],
        out_specs=[pl.BlockSpec((tile, w), row) for w in widths]
        + [pl.BlockSpec((None, MLA_HEADS * MLA_VA, tile), lambda b, j: (b * nj + j, 0, 0))],
        out_shape=[jax.ShapeDtypeStruct((t, w), bf16) for w in widths]
        + [jax.ShapeDtypeStruct((t // tile, MLA_HEADS * MLA_VA, tile), bf16)],
        compiler_params=pltpu.CompilerParams(
            dimension_semantics=("parallel", "parallel"), vmem_limit_bytes=VMEM_LIMIT),
        name="prep",
    )(x2d, gain, w_in_r, q_norm, w_qb_r, kv_norm, w_kvb_r, ctab, stab)


def _gla_log_decay(a, wa2_ref, ba_ref):
    s = _dot(a, wa2_ref[...]) + ba_ref[...]
    return (jnp.minimum(s, 0.0) - jnp.log(1.0 + jnp.exp(-jnp.abs(s)))) * (1.0 / GLA_TAU)


def _gla_tile(q, k, v, la, tri, st_ref, want_out):
    t = la.shape[0]
    nc = t // CHUNK
    hi, mid, lo = _split3(la)
    b = _dot(tri, hi) + _dot(tri, mid) + _dot(tri, lo)
    b_last = [b[(c + 1) * CHUNK - 1:(c + 1) * CHUNK, :] for c in range(nc)]
    b_last_full = jnp.concatenate([jnp.broadcast_to(bl, (CHUNK, GLA_QK)) for bl in b_last], axis=0)
    kf = k.astype(f32)
    kd = (kf * jnp.exp(b_last_full - b)).astype(bf16)
    rr = lax.broadcasted_iota(i32, (GLA_VW, GLA_QK), 0) // GLA_DV
    cc = lax.broadcasted_iota(i32, (GLA_VW, GLA_QK), 1) // GLA_DK
    if want_out:
        qe = (q.astype(f32) * (GLA_DK ** -0.5) * jnp.exp(b)).astype(bf16)
        ke = kf * jnp.exp(-b)
        vf = v.astype(f32)
        lane_h = lax.broadcasted_iota(i32, (CHUNK, GLA_QK), 1) // GLA_DK
        vlane_h = lax.broadcasted_iota(i32, (CHUNK, GLA_VW), 1) // GLA_DV
        a_row = lax.broadcasted_iota(i32, (CHUNK, GLA_QK), 0)
        a_col = lax.broadcasted_iota(i32, (CHUNK, GLA_QK), 1) % CHUNK
    outs = []
    st = st_ref[...]
    for c in range(nc):
        rows = slice(c * CHUNK, (c + 1) * CHUNK)
        upd = jnp.where(rr == cc, _dot_tn(v[rows], kd[rows]), 0.0)
        if want_out:
            kbd = jnp.concatenate(
                [jnp.where(lane_h == h, ke[rows], 0.0) for h in range(GLA_HEADS)], axis=0).astype(bf16)
            att = jnp.where(a_col <= a_row, _dot_nt(qe[rows], kbd), 0.0).astype(bf16)
            vbd = jnp.concatenate(
                [jnp.where(vlane_h == h, vf[rows], 0.0) for h in range(GLA_HEADS)], axis=0).astype(bf16)
            outs.append(_dot(att, vbd) + _dot_nt(qe[rows], st.astype(bf16)))
        st = st * jnp.exp(b_last[c]) + upd
    st_ref[...] = st
    return jnp.concatenate(outs, axis=0) if want_out else None


def _gla_kernel(q_ref, k_ref, v_ref, r_ref, a_ref, km_ref, vm_ref, am_ref, wa2_ref, ba_ref, gain_ref, tri_ref,
                y_ref, st_ref):
    j = pl.program_id(1)

    @pl.when(j == 0)
    def _():
        st_ref[...] = jnp.zeros_like(st_ref)
        la = _gla_log_decay(am_ref[...], wa2_ref, ba_ref)
        row = lax.broadcasted_iota(i32, la.shape, 0)
        la = jnp.where(row >= CHUNK - N_META, la, 0.0)
        _gla_tile(None, km_ref[...], vm_ref[...], la, tri_ref[0:CHUNK, 0:CHUNK], st_ref, False)

    la = _gla_log_decay(a_ref[...], wa2_ref, ba_ref)
    o = _gla_tile(q_ref[...], k_ref[...], v_ref[...], la, tri_ref[...], st_ref, True)
    r = r_ref[...].astype(f32)
    outs = []
    for h in range(GLA_HEADS):
        oh = o[:, h * GLA_DV:(h + 1) * GLA_DV]
        outs.append(oh * lax.rsqrt(jnp.mean(oh * oh, axis=-1, keepdims=True) + EPS))
    on = jnp.concatenate(outs, axis=1) * gain_ref[...]
    y_ref[...] = (on * (r * jax.nn.sigmoid(r))).astype(bf16)


def _gla_call(qg, kg, vg, rg, ag, km, vm, am, wa2_p, b_a, gain, batch, seq):
    nj = seq // GLA_TILE

    def row(b, j):
        return (b * nj + j, 0)

    def const(b, j):
        return (0, 0)

    return pl.pallas_call(
        _gla_kernel,
        grid=(batch, nj),
        in_specs=[
            pl.BlockSpec((GLA_TILE, GLA_QK), row),
            pl.BlockSpec((GLA_TILE, GLA_QK), row),
            pl.BlockSpec((GLA_TILE, GLA_VW), row),
            pl.BlockSpec((GLA_TILE, GLA_VW), row),
            pl.BlockSpec((GLA_TILE, LANE), row),
            pl.BlockSpec((CHUNK, GLA_QK), const),
            pl.BlockSpec((CHUNK, GLA_VW), const),
            pl.BlockSpec((CHUNK, LANE), const),
            pl.BlockSpec((LANE, GLA_QK), const),
            pl.BlockSpec((1, GLA_QK), const),
            pl.BlockSpec((1, GLA_VW), const),
            pl.BlockSpec((GLA_TILE, GLA_TILE), const),
        ],
        out_specs=pl.BlockSpec((GLA_TILE, GLA_VW), row),
        out_shape=jax.ShapeDtypeStruct((batch * seq, GLA_VW), bf16),
        scratch_shapes=[pltpu.VMEM((GLA_VW, GLA_QK), f32)],
        compiler_params=pltpu.CompilerParams(
            dimension_semantics=("parallel", "arbitrary"), vmem_limit_bytes=VMEM_LIMIT),
        name="gla",
    )(qg, kg, vg, rg, ag, km, vm, am, wa2_p, b_a, gain, jnp.asarray(_CHUNK_PREFIX, bf16))


def _mla_kernel(q_ref, k_ref, vt_ref, km_ref, vmt_ref, o_ref, sa_ref, sb_ref):
    i = pl.program_id(2)
    tq = ATT_TILE
    w = MLA_QK_PAD
    va = MLA_VA
    heads = range(ATT_HEADS)

    def scores(h, blk):
        rows = pl.ds(pl.multiple_of(blk * tq, tq), tq)
        return _dot_nt(k_ref[rows, h * w:(h + 1) * w], q_ref[:, h * w:(h + 1) * w])

    def soft(s, vtb, carry, mask=None):
        m, acc = carry
        if mask is not None:
            s = jnp.where(mask, s, -1e30)
        m_new = jnp.maximum(m, jnp.max(s, axis=0, keepdims=True))
        p = jnp.exp2(s - m_new).astype(bf16)
        return m_new, jnp.exp2(m - m_new) * acc + _dot(vtb, p)

    def vt(h, blk):
        return vt_ref[blk, h * va:(h + 1) * va, :]

    def finish(h, carry):
        s = _dot_nt(km_ref[:, h * w:(h + 1) * w], q_ref[:, h * w:(h + 1) * w])
        m, acc = soft(s, vmt_ref[h * va:(h + 1) * va, :], carry)
        o_ref[:, h * MLA_V:(h + 1) * MLA_V] = (acc[:MLA_V] * (1.0 / acc[MLA_V:MLA_V + 1])).T.astype(bf16)

    kc = lax.broadcasted_iota(i32, (tq, tq), 0) // CHUNK
    qc = lax.broadcasted_iota(i32, (tq, tq), 1) // CHUNK
    mask = kc <= qc

    for h in heads:
        sa_ref[h] = scores(h, 0)

    def pair(p, carries):
        b0 = 2 * p
        for h in heads:
            sb_ref[h] = scores(h, b0 + 1)
        carries = [soft(sa_ref[h], vt(h, b0), carries[h]) for h in heads]
        for h in heads:
            sa_ref[h] = scores(h, b0 + 2)
        return tuple(soft(sb_ref[h], vt(h, b0 + 1), carries[h]) for h in heads)

    init = tuple((jnp.full((1, tq), -1e30, f32), jnp.zeros((va, tq), f32)) for _ in heads)
    carries = lax.fori_loop(0, i // 2, pair, init)

    @pl.when(i % 2 == 1)
    def _():
        for h in heads:
            sb_ref[h] = scores(h, i)
        for h in heads:
            c = soft(sa_ref[h], vt(h, i - 1), carries[h])
            finish(h, soft(sb_ref[h], vt(h, i), c, mask))

    @pl.when(i % 2 == 0)
    def _():
        for h in heads:
            finish(h, soft(sa_ref[h], vt(h, i), carries[h], mask))


def _mla_call(qm, km, vmt, km_meta, vmt_meta, batch, seq):
    nq = seq // ATT_TILE
    nh = ATT_HEADS
    qm3 = qm.reshape(batch, seq, MLA_HEADS * MLA_QK_PAD)
    km3 = km.reshape(batch, seq, MLA_HEADS * MLA_QK_PAD)
    vt4 = vmt.reshape(batch, nq, MLA_HEADS * MLA_VA, ATT_TILE)
    out = pl.pallas_call(
        _mla_kernel,
        grid=(batch, MLA_HEADS // nh, nq),
        in_specs=[
            pl.BlockSpec((None, ATT_TILE, nh * MLA_QK_PAD), lambda b, h, i: (b, i, h)),
            pl.BlockSpec((None, seq, nh * MLA_QK_PAD), lambda b, h, i: (b, 0, h)),
            pl.BlockSpec((None, nq, nh * MLA_VA, ATT_TILE), lambda b, h, i: (b, 0, h, 0)),
            pl.BlockSpec((N_META, nh * MLA_QK_PAD), lambda b, h, i: (0, h)),
            pl.BlockSpec((nh * MLA_VA, N_META), lambda b, h, i: (h, 0)),
        ],
        out_specs=pl.BlockSpec((None, ATT_TILE, nh * MLA_V), lambda b, h, i: (b, i, h)),
        out_shape=jax.ShapeDtypeStruct((batch, seq, MLA_OUT), bf16),
        scratch_shapes=[pltpu.VMEM((nh, ATT_TILE, ATT_TILE), f32), pltpu.VMEM((nh, ATT_TILE, ATT_TILE), f32)],
        compiler_params=pltpu.CompilerParams(
            dimension_semantics=("parallel", "parallel", "arbitrary"), vmem_limit_bytes=VMEM_LIMIT),
        name="mla",
    )(qm3, km3, vt4, km_meta, vmt_meta)
    return out.reshape(batch * seq, MLA_OUT)


def _route_cols(lt):
    r = lt.shape[1]
    neg = -1e30
    gl = lt[0:N_GROUPS, :]
    gsub = lax.broadcasted_iota(i32, (N_GROUPS, r), 0)
    gmax = jnp.max(gl, axis=0, keepdims=True)
    g_p = 1.0 / jnp.sum(jnp.exp(gl - gmax), axis=0, keepdims=True)
    g_idx = jnp.min(jnp.where(gl == gmax, gsub, N_GROUPS), axis=0, keepdims=True)
    el_all = lt[N_GROUPS:N_GROUPS + N_EXPERTS, :]
    esub = lax.broadcasted_iota(i32, (N_EXPERTS, r), 0)
    base = g_idx * EXPERTS_PER_GROUP
    e_mask = (esub >= base) & (esub < base + EXPERTS_PER_GROUP)
    el = jnp.where(e_mask, el_all, neg)
    m1 = jnp.max(el, axis=0, keepdims=True)
    i1 = jnp.min(jnp.where(e_mask & (el == m1), esub, N_EXPERTS), axis=0, keepdims=True)
    el2 = jnp.where(esub == i1, neg, el)
    m2 = jnp.max(el2, axis=0, keepdims=True)
    i2 = jnp.min(jnp.where(e_mask & (esub != i1) & (el2 == m2), esub, N_EXPERTS), axis=0, keepdims=True)
    rr = jnp.exp(m2 - m1)
    ga = g_p / (1.0 + rr)
    gb = g_p * rr / (1.0 + rr)
    la_ = i1 - base
    lb_ = i2 - base
    lo = jnp.minimum(la_, lb_)
    hi = jnp.maximum(la_, lb_)
    g_lo = jnp.where(la_ < lb_, ga, gb)
    g_hi = jnp.where(la_ < lb_, gb, ga)
    pidx = ((lo * (2 * EXPERTS_PER_GROUP - 1 - lo)) >> 1) + (hi - lo - 1)
    bucket = g_idx * N_PAIRS + pidx
    bsub = lax.broadcasted_iota(i32, (BUCKET_LANES, r), 0)
    oht = jnp.where(bsub == bucket, 1.0, 0.0)
    ohb = oht.astype(bf16)
    ri = lax.broadcasted_iota(i32, (r, r), 0)
    ci = lax.broadcasted_iota(i32, (r, r), 1)
    before = jnp.where(ri < ci, 1.0, 0.0).astype(bf16)
    cum = _dot(ohb, before)
    rank = jnp.sum(oht * cum, axis=0, keepdims=True)
    counts = _dot_nt(jnp.ones((SUBLANES, r), bf16), ohb)[0:1, :]
    msub = lax.broadcasted_iota(i32, (LANE, r), 0)
    meta_t = jnp.where(msub == 0, bucket.astype(f32),
                       jnp.where(msub == 1, rank,
                                 jnp.where(msub == 2, g_lo, jnp.where(msub == 3, g_hi, 0.0))))
    return meta_t.T, counts


def _outproj_kernel(x_ref, yg_ref, ym_ref, wog_ref, wom_ref, gain_ref, wrt_ref, rb_ref,
                    h_ref, ux_ref, cnt_ref):
    for sub in range(OUT_TILE // ROUTE_ROWS):
        rows = slice(sub * ROUTE_ROWS, (sub + 1) * ROUTE_ROWS)
        h1 = x_ref[rows, :] + _dot(yg_ref[rows, :], wog_ref[...]) + _dot(ym_ref[rows, :], wom_ref[...])
        h_ref[rows, :] = h1
        u2 = _rms(h1, gain_ref[...])
        ux_ref[rows, 0:D_MODEL] = u2
        lt = _dot_nt(wrt_ref[...], u2.astype(bf16)) + rb_ref[...]
        meta, counts = _route_cols(lt)
        ux_ref[rows, D_MODEL:ROW_W] = meta
        cnt_ref[sub] = counts


def _outproj_call(x2d, yg, ym, wo_g, wo_m, gain, w_r, rbias):
    t = x2d.shape[0]
    nt = t // OUT_TILE

    def row(i):
        return (i, 0)

    def const(i):
        return (0, 0)

    return pl.pallas_call(
        _outproj_kernel,
        grid=(nt,),
        in_specs=[
            pl.BlockSpec((OUT_TILE, D_MODEL), row),
            pl.BlockSpec((OUT_TILE, GLA_VW), row),
            pl.BlockSpec((OUT_TILE, MLA_OUT), row),
            pl.BlockSpec((GLA_VW, D_MODEL), const),
            pl.BlockSpec((MLA_OUT, D_MODEL), const),
            pl.BlockSpec((1, D_MODEL), const),
            pl.BlockSpec((LANE, D_MODEL), const),
            pl.BlockSpec((LANE, 1), const),
        ],
        out_specs=[
            pl.BlockSpec((OUT_TILE, D_MODEL), row),
            pl.BlockSpec((OUT_TILE, ROW_W), row),
            pl.BlockSpec((OUT_TILE // ROUTE_ROWS, 1, BUCKET_LANES), lambda i: (i, 0, 0)),
        ],
        out_shape=[
            jax.ShapeDtypeStruct((t, D_MODEL), f32),
            jax.ShapeDtypeStruct((t, ROW_W), f32),
            jax.ShapeDtypeStruct((nt * (OUT_TILE // ROUTE_ROWS), 1, BUCKET_LANES), f32),
        ],
        compiler_params=pltpu.CompilerParams(
            dimension_semantics=("parallel",), vmem_limit_bytes=VMEM_LIMIT),
        name="outproj",
    )(x2d, yg, ym, wo_g, wo_m, gain, w_r, rbias)


def _scatter_kernel(pos_ref, zb_ref, ux_ref, hs_ref, zbuf, sem, zsem):
    @pl.when(pl.program_id(0) == 0)
    def _():
        zbuf[...] = jnp.zeros_like(zbuf)

        def zero_copy(j):
            rows = pl.ds(pl.multiple_of(zb_ref[j] * MOE_BLOCK, MOE_BLOCK), MOE_BLOCK)
            return pltpu.make_async_copy(zbuf, hs_ref.at[rows], zsem)

        def zstart(j, c):
            @pl.when(zb_ref[j] >= 0)
            def _():
                zero_copy(j).start()
            return c

        def zwait(j, c):
            @pl.when(zb_ref[j] >= 0)
            def _():
                zero_copy(j).wait()
            return c

        lax.fori_loop(0, zb_ref.shape[0], zstart, 0)
        lax.fori_loop(0, zb_ref.shape[0], zwait, 0)

    def start(ii, c):
        for k in range(SUBLANES):
            pltpu.make_async_copy(ux_ref.at[ii, pl.ds(k, 1)],
                                  hs_ref.at[pl.ds(pos_ref[ii * SUBLANES + k], 1)], sem).start()
        return c

    lax.fori_loop(0, SCATTER_TILE // SUBLANES, start, 0)
    pltpu.make_async_copy(hs_ref.at[pl.ds(0, SCATTER_TILE)], hs_ref.at[pl.ds(0, SCATTER_TILE)], sem).wait()


def _scatter_call(pos, zero_blocks, ux, n_slots):
    t = ux.shape[0]
    nz = zero_blocks.shape[0]
    return pl.pallas_call(
        _scatter_kernel,
        grid=(t // SCATTER_TILE,),
        in_specs=[
            pl.BlockSpec((SCATTER_TILE,), lambda i: (i,), memory_space=pltpu.SMEM),
            pl.BlockSpec((nz,), lambda i: (0,), memory_space=pltpu.SMEM),
            pl.BlockSpec((SCATTER_TILE // SUBLANES, SUBLANES, ROW_W), lambda i: (i, 0, 0)),
        ],
        out_specs=pl.BlockSpec(memory_space=pl.ANY),
        out_shape=jax.ShapeDtypeStruct((n_slots, ROW_W), f32),
        scratch_shapes=[pltpu.VMEM((MOE_BLOCK, ROW_W), f32), pltpu.SemaphoreType.DMA(()),
                        pltpu.SemaphoreType.DMA(())],
        compiler_params=pltpu.CompilerParams(
            dimension_semantics=("arbitrary",), vmem_limit_bytes=VMEM_LIMIT),
        name="scatter",
    )(pos, zero_blocks, ux.reshape(t // SUBLANES, SUBLANES, ROW_W))


def _moe_kernel(se_ref, sf_ref, sk_ref, sb_ref, sr_ref, si_ref, so_ref,
                hs_hbm, wg_ref, wu_ref, wd_ref, y_hbm, xbuf, obuf, wgu_s, wd_s, in_sem, out_sem):
    s = pl.program_id(0)
    ns = pl.num_programs(0)
    cur = s % 2
    g_n = MOE_GROUP

    def in_copy(step, g, buf):
        rows = pl.ds(pl.multiple_of(sb_ref[step * g_n + g] * MOE_BLOCK, MOE_BLOCK), MOE_BLOCK)
        return pltpu.make_async_copy(
            hs_hbm.at[rows], xbuf.at[buf, pl.ds(g * MOE_BLOCK, MOE_BLOCK)], in_sem.at[buf])

    def out_copy(step, g, buf):
        rows = pl.ds(pl.multiple_of(sb_ref[step * g_n + g] * MOE_BLOCK, MOE_BLOCK), MOE_BLOCK)
        cols = pl.ds(pl.multiple_of(sr_ref[step * g_n + g] * D_MODEL, D_MODEL), D_MODEL)
        return pltpu.make_async_copy(
            obuf.at[buf, pl.ds(g * MOE_BLOCK, MOE_BLOCK)], y_hbm.at[rows, cols], out_sem.at[buf])

    def for_slots(step, flags_ref, fn):
        for g in range(g_n):
            @pl.when(flags_ref[step * g_n + g] == 1)
            def _():
                fn(g)

    @pl.when(s == 0)
    def _():
        xbuf[...] = jnp.zeros_like(xbuf)
        for_slots(0, si_ref, lambda g: in_copy(0, g, 0).start())

    @pl.when(s + 1 < ns)
    def _():
        for_slots(s + 1, si_ref, lambda g: in_copy(s + 1, g, 1 - cur).start())

    for_slots(s, si_ref, lambda g: in_copy(s, g, cur).wait())

    @pl.when(s >= 2)
    def _():
        for_slots(s - 2, so_ref, lambda g: out_copy(s - 2, g, cur).wait())

    @pl.when(sf_ref[s] == 1)
    def _():
        wgu_s[:, 0:D_EXPERT] = wg_ref[...].astype(bf16)
        wgu_s[:, D_EXPERT:2 * D_EXPERT] = wu_ref[...].astype(bf16)
        wd_s[...] = wd_ref[...].astype(bf16)

    @pl.when(sk_ref[s] == 1)
    def _():
        u = xbuf[cur, :, 0:D_MODEL].astype(bf16)
        meta = xbuf[cur, :, D_MODEL:ROW_W]
        gate = jnp.concatenate(
            [jnp.where(sr_ref[s * g_n + g] == 0, meta[g * MOE_BLOCK:(g + 1) * MOE_BLOCK, 2:3],
                       meta[g * MOE_BLOCK:(g + 1) * MOE_BLOCK, 3:4]) for g in range(g_n)], axis=0)
        gu = _dot(u, wgu_s[...])
        gt = gu[:, 0:D_EXPERT]
        hdn = (gt * jax.nn.sigmoid(gt) * gu[:, D_EXPERT:]).astype(bf16)
        obuf[cur] = _dot(hdn, wd_s[...]) * gate

    @pl.when(sk_ref[s] == 0)
    def _():
        obuf[cur] = jnp.zeros(obuf.shape[1:], f32)

    for_slots(s, so_ref, lambda g: out_copy(s, g, cur).start())

    @pl.when(s == ns - 1)
    def _():
        for_slots(s, so_ref, lambda g: out_copy(s, g, cur).wait())

        @pl.when(s >= 1)
        def _():
            for_slots(s - 1, so_ref, lambda g: out_copy(s - 1, g, 1 - cur).wait())


def _moe_call(plan, hs, w_gate, w_up, w_down):
    n_steps = plan[0].shape[0]
    n_slots = hs.shape[0]
    rows = MOE_GROUP * MOE_BLOCK

    def wmap(s, se, *_):
        return (se[s], 0, 0)

    grid_spec = pltpu.PrefetchScalarGridSpec(
        num_scalar_prefetch=7,
        grid=(n_steps,),
        in_specs=[
            pl.BlockSpec(memory_space=pl.ANY),
            pl.BlockSpec((None, D_MODEL, D_EXPERT), wmap),
            pl.BlockSpec((None, D_MODEL, D_EXPERT), wmap),
            pl.BlockSpec((None, D_EXPERT, D_MODEL), wmap),
        ],
        out_specs=pl.BlockSpec(memory_space=pl.ANY),
        scratch_shapes=[
            pltpu.VMEM((2, rows, ROW_W), f32),
            pltpu.VMEM((2, rows, D_MODEL), f32),
            pltpu.VMEM((D_MODEL, 2 * D_EXPERT), bf16),
            pltpu.VMEM((D_EXPERT, D_MODEL), bf16),
            pltpu.SemaphoreType.DMA((2,)),
            pltpu.SemaphoreType.DMA((2,)),
        ],
    )
    return pl.pallas_call(
        _moe_kernel,
        grid_spec=grid_spec,
        out_shape=jax.ShapeDtypeStruct((n_slots, 2 * D_MODEL), f32),
        compiler_params=pltpu.CompilerParams(
            dimension_semantics=("arbitrary",), vmem_limit_bytes=VMEM_LIMIT),
        name="moe",
    )(*plan, hs, w_gate, w_up, w_down)


def _final_kernel(posc_ref, posn_ref, h_ref, gain_ref, y_hbm, o_ref, ybuf, sem):
    i = pl.program_id(0)
    cur = i % 2

    def issue(pos_ref, buf):
        def start(ii, c):
            for k in range(SUBLANES):
                pltpu.make_async_copy(y_hbm.at[pl.ds(pos_ref[ii * SUBLANES + k], 1)],
                                      ybuf.at[buf, ii, pl.ds(k, 1)], sem.at[buf]).start()
            return c

        lax.fori_loop(0, FINAL_TILE // SUBLANES, start, 0)

    @pl.when(i == 0)
    def _():
        issue(posc_ref, 0)

    @pl.when(i + 1 < pl.num_programs(0))
    def _():
        issue(posn_ref, 1 - cur)

    pltpu.make_async_copy(ybuf.at[cur], ybuf.at[cur], sem.at[cur]).wait()
    h = h_ref[...] + ybuf[cur, :, :, 0:D_MODEL] + ybuf[cur, :, :, D_MODEL:2 * D_MODEL]
    o_ref[...] = _rms(h, gain_ref[...])


def _final_call(pos, h1, gain, y):
    t = h1.shape[0]
    n = t // FINAL_TILE
    rows = FINAL_TILE // SUBLANES
    out = pl.pallas_call(
        _final_kernel,
        grid=(n,),
        in_specs=[
            pl.BlockSpec((FINAL_TILE,), lambda i: (i,), memory_space=pltpu.SMEM),
            pl.BlockSpec((FINAL_TILE,), lambda i: (jnp.minimum(i + 1, n - 1),), memory_space=pltpu.SMEM),
            pl.BlockSpec((rows, SUBLANES, D_MODEL), lambda i: (i, 0, 0)),
            pl.BlockSpec((1, 1, D_MODEL), lambda i: (0, 0, 0)),
            pl.BlockSpec(memory_space=pl.ANY),
        ],
        out_specs=pl.BlockSpec((rows, SUBLANES, D_MODEL), lambda i: (i, 0, 0)),
        out_shape=jax.ShapeDtypeStruct((t // SUBLANES, SUBLANES, D_MODEL), f32),
        scratch_shapes=[pltpu.VMEM((2, rows, SUBLANES, 2 * D_MODEL), f32), pltpu.SemaphoreType.DMA((2,))],
        compiler_params=pltpu.CompilerParams(
            dimension_semantics=("arbitrary",), vmem_limit_bytes=VMEM_LIMIT),
        name="final",
    )(pos, pos, h1.reshape(t // SUBLANES, SUBLANES, D_MODEL), gain.reshape(1, 1, D_MODEL), y)
    return out.reshape(t, D_MODEL)


def _rope_tables(pos):
    inv = ROPE_BASE ** (-jnp.arange(0, MLA_ROPE, 2, dtype=f32) / MLA_ROPE)
    ang = pos.astype(f32)[:, None] * inv[None, :]
    cos, sin = jnp.cos(ang), jnp.sin(ang)
    z = jnp.zeros((pos.shape[0], LANE - MLA_ROPE), f32)
    return jnp.concatenate([cos, cos, z], axis=1), jnp.concatenate([-sin, sin, z], axis=1)


def _relayout_weights(w_in, w_qb, w_kvb):
    half = MLA_ROPE // 2
    perm = (np.arange(MLA_ROPE) + half) % MLA_ROPE
    pts = np.cumsum((GLA_QK, GLA_QK, GLA_VW, GLA_VW, GLA_GATE_RANK, MLA_Q_RANK, MLA_KV_RANK, MLA_ROPE))
    q_g, k_g, v_g, r_g, a_l, q_lat, kv_lat, k_rope = jnp.split(w_in, pts[:-1], axis=1)
    a_seg = jnp.pad(a_l, ((0, 0), (0, LANE - GLA_GATE_RANK)))
    w_in_r = jnp.concatenate(
        [q_g, k_g, v_g, r_g, q_lat, kv_lat, k_rope, k_rope[:, perm], a_seg], axis=1).astype(bf16)
    qcols, kcols, vcols = [], [], []
    for h in range(MLA_HEADS):
        c = h * (MLA_NOPE + MLA_ROPE)
        rope = w_qb[:, c + MLA_NOPE:c + MLA_NOPE + MLA_ROPE]
        qcols += [w_qb[:, c:c + MLA_NOPE], rope, rope[:, perm]]
        c2 = h * (MLA_NOPE + MLA_V)
        kcols.append(w_kvb[:, c2:c2 + MLA_NOPE])
        vcols.append(w_kvb[:, c2 + MLA_NOPE:c2 + MLA_NOPE + MLA_V])
    return w_in_r, jnp.concatenate(qcols, axis=1).astype(bf16), jnp.concatenate(kcols + vcols, axis=1).astype(bf16)


_BUCKET_GROUP = np.arange(N_BUCKETS) // N_PAIRS
_RUN_EXPERT = np.concatenate([_BUCKET_GROUP * EXPERTS_PER_GROUP + _PAIR_LO[np.arange(N_BUCKETS) % N_PAIRS],
                              _BUCKET_GROUP * EXPERTS_PER_GROUP + _PAIR_HI[np.arange(N_BUCKETS) % N_PAIRS]])
_RUN_IS_EXPERT = (_RUN_EXPERT[:, None] == np.arange(N_EXPERTS)[None, :]).astype(np.int32)
_RUN_BEFORE = ((_RUN_EXPERT[:, None] == _RUN_EXPERT[None, :])
               & (np.arange(2 * N_BUCKETS)[None, :] < np.arange(2 * N_BUCKETS)[:, None])).astype(np.int32)


def _route_plan(counts, bucket, rank, n_tok):
    nt = counts.shape[0]
    g_n = MOE_GROUP
    tot = counts.sum(axis=0)
    nblk = (tot + MOE_BLOCK - 1) // MOE_BLOCK
    bstart_blk = jnp.cumsum(nblk) - nblk
    n_blocks = jnp.sum(nblk)
    tile_base = bstart_blk[None, :] * MOE_BLOCK + jnp.cumsum(counts, axis=0) - counts
    hit = bucket.reshape(nt, -1, 1) == jnp.arange(N_BUCKETS, dtype=i32)
    pos = jnp.sum(jnp.where(hit, tile_base[:, None, :], 0), axis=-1).reshape(-1) + rank
    nb_max = (n_tok + N_BUCKETS * (MOE_BLOCK - 1)) // MOE_BLOCK

    n_run = jnp.concatenate([nblk, nblk])
    b0_run = jnp.concatenate([bstart_blk, bstart_blk])
    c_e = jnp.sum(n_run[:, None] * _RUN_IS_EXPERT, axis=0)
    g_e = (c_e + g_n - 1) // g_n
    gend = jnp.cumsum(g_e)
    gstart = gend - g_e
    n_compute = gend[-1]
    off_run = jnp.sum(_RUN_BEFORE * n_run[None, :], axis=1)
    f_run = jnp.sum(_RUN_IS_EXPERT * gstart[None, :], axis=1) * g_n + off_run

    n_steps = (2 * nb_max + N_EXPERTS * (g_n - 1) + g_n - 1) // g_n + 1
    f = jnp.arange(n_steps * g_n, dtype=i32)
    in_run = (f[:, None] >= f_run[None, :]) & (f[:, None] < (f_run + n_run)[None, :])
    valid_c = jnp.any(in_run, axis=1)
    block_c = jnp.sum(jnp.where(in_run, b0_run[None, :] + f[:, None] - f_run[None, :], 0), axis=1)
    role_c = jnp.sum(jnp.where(in_run[:, N_BUCKETS:], 1, 0), axis=1)
    u_idx = f - n_compute * g_n
    valid_f = (u_idx >= 0) & (u_idx < 2 * (nb_max - n_blocks))
    slot_block = jnp.where(valid_c, block_c, jnp.where(valid_f, n_blocks + u_idx // 2, 0))
    slot_role = jnp.where(valid_c, role_c, jnp.where(valid_f, u_idx % 2, 0))

    step = jnp.arange(n_steps, dtype=i32)
    e_of_step = jnp.minimum(jnp.sum(gend[None, :] <= step[:, None], axis=1), N_EXPERTS - 1)
    is_compute = step < n_compute
    last_e = jnp.max(jnp.where(is_compute, e_of_step, 0))
    step_expert = jnp.where(is_compute, e_of_step, last_e)
    step_first = jnp.concatenate([jnp.ones((1,), bool), step_expert[1:] != step_expert[:-1]])
    plan = tuple(a.astype(i32) for a in
                 (step_expert, step_first, is_compute, slot_block, slot_role, valid_c, valid_c | valid_f))
    last_blk = jnp.where(nblk > 0, bstart_blk + nblk - 1, -1)
    spare = n_blocks + jnp.arange(nb_max - n_tok // MOE_BLOCK, dtype=i32)
    zero_blocks = jnp.concatenate([last_blk, jnp.where(spare < nb_max, spare, -1)]).astype(i32)
    return pos.astype(i32), plan, zero_blocks, nb_max


def kernel(x, meta_tokens, mix_norm, w_in, gla_w_a2, gla_b_a, gla_out_norm, mla_q_norm, mla_w_qb, mla_kv_norm,
           mla_w_kvb, w_out, ffn_norm, router_group_w, router_group_b, router_expert_w, router_expert_b,
           expert_w_gate, expert_w_up, expert_w_down, final_norm):
    batch, seq, d = x.shape
    assert d == D_MODEL and seq % max(PREP_TILE, GLA_TILE, ATT_TILE) == 0
    assert (batch * seq) % max(OUT_TILE, SCATTER_TILE, FINAL_TILE) == 0
    n_tok = batch * seq
    x2d = x.reshape(n_tok, d)

    w_in_r, w_qb_r, w_kvb_r = _relayout_weights(w_in[0], mla_w_qb[0], mla_w_kvb[0])
    mixg = mix_norm[0].reshape(1, d)
    qn = mla_q_norm[0].reshape(1, MLA_Q_RANK)
    kvn = mla_kv_norm[0].reshape(1, MLA_KV_RANK)
    ct_m, st_m = _rope_tables(jnp.arange(META_TILE))
    ct_x, st_x = _rope_tables(N_META + jnp.arange(seq))

    x_meta = jnp.pad(meta_tokens.astype(f32), ((0, META_TILE - N_META), (0, 0)))
    _, kg_m, vg_m, _, a_m, _, km_m, vmt_m = _prep_call(
        x_meta, META_TILE, META_TILE, mixg, w_in_r, qn, w_qb_r, kvn, w_kvb_r, ct_m, st_m)
    qg, kg, vg, rg, ag, qm, km, vmt = _prep_call(
        x2d, seq, PREP_TILE, mixg, w_in_r, qn, w_qb_r, kvn, w_kvb_r, ct_x, st_x)

    def chunk0(a):
        return jnp.pad(a[:N_META], ((CHUNK - N_META, 0), (0, 0)))

    wa2_p = jnp.pad(gla_w_a2[0], ((0, LANE - GLA_GATE_RANK), (0, 0))).astype(bf16)
    y_gla = _gla_call(qg, kg, vg, rg, ag, chunk0(kg_m), chunk0(vg_m), chunk0(a_m),
                      wa2_p, gla_b_a[0].reshape(1, GLA_QK), gla_out_norm[0].reshape(1, GLA_VW), batch, seq)
    y_mla = _mla_call(qm, km, vmt, km_m[:N_META], vmt_m[0, :, :N_META], batch, seq)

    wo = w_out[0].astype(bf16)
    rw = jnp.concatenate([router_group_w[0], router_expert_w[0],
                          jnp.zeros((d, LANE - N_GROUPS - N_EXPERTS), f32)], axis=1)
    rb = jnp.concatenate([router_group_b[0], router_expert_b[0],
                          jnp.zeros((LANE - N_GROUPS - N_EXPERTS,), f32)]).reshape(1, LANE)
    h1, ux, cnt = _outproj_call(x2d, y_gla, y_mla, wo[:GLA_VW], wo[GLA_VW:], ffn_norm[0].reshape(1, d),
                                      rw.T.astype(bf16), rb.reshape(LANE, 1))

    counts = cnt.reshape(-1, BUCKET_LANES)[:, :N_BUCKETS].astype(i32)
    tok_meta = ux[:, D_MODEL:D_MODEL + 2].astype(i32)
    pos, plan, zero_blocks, nb_max = _route_plan(counts, tok_meta[:, 0], tok_meta[:, 1], n_tok)
    n_slots = nb_max * MOE_BLOCK
    hs = _scatter_call(pos, zero_blocks, ux, n_slots)
    y = _moe_call(plan, hs, expert_w_gate[0], expert_w_up[0], expert_w_down[0])
    out = _final_call(pos, h1, final_norm.reshape(1, d), y)
    return out.reshape(batch, seq, d)
```

```python
import functools

import numpy as np
import jax
import jax.numpy as jnp
from jax import lax
from jax.experimental import pallas as pl
from jax.experimental.pallas import tpu as pltpu

f32 = jnp.float32
bf16 = jnp.bfloat16
i32 = jnp.int32

D_MODEL = 1024
CHUNK = 64
N_META = 16
EPS = 1e-6
GLA_HEADS = 4
GLA_DK = 64
GLA_DV = 128
GLA_GATE_RANK = 16
GLA_TAU = 16.0
GLA_QK = GLA_HEADS * GLA_DK
GLA_VW = GLA_HEADS * GLA_DV
MLA_HEADS = 4
MLA_Q_RANK = 256
MLA_KV_RANK = 128
MLA_NOPE = 128
MLA_ROPE = 64
MLA_V = 128
MLA_OUT = MLA_HEADS * MLA_V
MLA_QK_PAD = 256
MLA_VA = MLA_V + 16
LOG2_E = 1.4426950408889634
ROPE_BASE = 10000.0
N_GROUPS = 8
EXPERTS_PER_GROUP = 8
N_EXPERTS = N_GROUPS * EXPERTS_PER_GROUP
D_EXPERT = 512
N_PAIRS = EXPERTS_PER_GROUP * (EXPERTS_PER_GROUP - 1) // 2
N_BUCKETS = N_GROUPS * N_PAIRS
BUCKET_LANES = 256
LANE = 128
SUBLANES = 8
META_W = LANE
ROW_W = D_MODEL + META_W

PREP_TILE = 512
GLA_TILE = 512
ATT_TILE = 512
ATT_HEADS = 4
META_TILE = 128
OUT_TILE = 1024
ROUTE_ROWS = 256
SCATTER_TILE = 2048
FINAL_TILE = 512
MOE_BLOCK = 64
MOE_GROUP = 8
VMEM_LIMIT = 56 * 1024 * 1024

C_Q, C_K, C_V, C_R = 0, 256, 512, 1024
C_QLAT, C_KVLAT, C_KROPE, C_A, C_END = 1536, 1792, 1920, 2048, 2176

_TILE_POS = np.arange(GLA_TILE)
_CHUNK_PREFIX = ((_TILE_POS[:, None] // CHUNK == _TILE_POS[None, :] // CHUNK)
                 & (_TILE_POS[None, :] <= _TILE_POS[:, None])).astype(np.float32)
_PAIR_LO = np.array([lo for lo in range(8) for hi in range(lo + 1, 8)], np.int32)
_PAIR_HI = np.array([hi for lo in range(8) for hi in range(lo + 1, 8)], np.int32)


def _dot(a, b):
    return jnp.dot(a, b, preferred_element_type=f32)


def _dot_nt(a, b):
    return lax.dot_general(a, b, (((1,), (1,)), ((), ())), preferred_element_type=f32)


def _dot_tn(a, b):
    return lax.dot_general(a, b, (((0,), (0,)), ((), ())), preferred_element_type=f32)


def _rms(x, gain):
    return x * lax.rsqrt(jnp.mean(x * x, axis=-1, keepdims=True) + EPS) * gain


def _split3(x):
    hi = x.astype(bf16)
    r1 = x - hi.astype(f32)
    mid = r1.astype(bf16)
    lo = (r1 - mid.astype(f32)).astype(bf16)
    return hi, mid, lo


def _prep_kernel(x_ref, g_ref, win_ref, qn_ref, wqb_ref, kvn_ref, wkvb_ref, ct_ref, st_ref,
                 qg_ref, kg_ref, vg_ref, rg_ref, a_ref, qm_ref, km_ref, vmt_ref):
    u = _rms(x_ref[...], g_ref[...]).astype(bf16)

    def proj(lo, hi):
        return _dot(u, win_ref[:, lo:hi])

    qg_ref[...] = proj(C_Q, C_K).astype(bf16)
    kg_ref[...] = proj(C_K, C_V).astype(bf16)
    vg_ref[...] = proj(C_V, C_R).astype(bf16)
    rg_ref[...] = proj(C_R, C_QLAT).astype(bf16)
    z = proj(C_QLAT, C_END)
    a_ref[...] = z[:, C_A - C_QLAT:].astype(bf16)
    ctab = ct_ref[...]
    stab = st_ref[...]

    def rope(seg):
        return seg * ctab + pltpu.roll(seg, 64, axis=1) * stab

    k_rope = rope(z[:, C_KROPE - C_QLAT:C_A - C_QLAT]).astype(bf16)
    qn = _rms(z[:, 0:MLA_Q_RANK], qn_ref[...]).astype(bf16)
    kvn = _rms(z[:, MLA_Q_RANK:MLA_Q_RANK + MLA_KV_RANK], kvn_ref[...]).astype(bf16)
    scale = (MLA_NOPE + MLA_ROPE) ** -0.5 * LOG2_E
    qf = _dot(qn, wqb_ref[...])
    kvf = _dot(kvn, wkvb_ref[...])
    for h in range(MLA_HEADS):
        c = h * MLA_QK_PAD
        qm_ref[:, c:c + LANE] = (qf[:, c:c + LANE] * scale).astype(bf16)
        qm_ref[:, c + LANE:c + 2 * LANE] = (rope(qf[:, c + LANE:c + 2 * LANE]) * scale).astype(bf16)
        km_ref[:, c:c + LANE] = kvf[:, h * LANE:(h + 1) * LANE].astype(bf16)
        km_ref[:, c + LANE:c + 2 * LANE] = k_rope
    vt = kvf[:, MLA_HEADS * MLA_NOPE:].T
    for h in range(MLA_HEADS):
        vmt_ref[h * MLA_VA:h * MLA_VA + MLA_V, :] = vt[h * MLA_V:(h + 1) * MLA_V].astype(bf16)
        vmt_ref[h * MLA_VA + MLA_V:(h + 1) * MLA_VA, :] = jnp.ones((MLA_VA - MLA_V, vt.shape[1]), bf16)


def _prep_call(x2d, rows_per_seq, tile, gain, w_in_r, q_norm, w_qb_r, kv_norm, w_kvb_r, ctab, stab):
    t = x2d.shape[0]
    nj = rows_per_seq // tile
    grid = (t // rows_per_seq, nj)

    def row(b, j):
        return (b * nj + j, 0)

    def const(b, j):
        return (0, 0)

    def tab(b, j):
        return (j, 0)

    widths = (GLA_QK, GLA_QK, GLA_VW, GLA_VW, LANE, MLA_HEADS * MLA_QK_PAD, MLA_HEADS * MLA_QK_PAD)
    return pl.pallas_call(
        _prep_kernel,
        grid=grid,
        in_specs=[
            pl.BlockSpec((tile, D_MODEL), row),
            pl.BlockSpec((1, D_MODEL), const),
            pl.BlockSpec((D_MODEL, C_END), const),
            pl.BlockSpec((1, MLA_Q_RANK), const),
            pl.BlockSpec((MLA_Q_RANK, MLA_HEADS * MLA_QK_PAD), const),
            pl.BlockSpec((1, MLA_KV_RANK), const),
            pl.BlockSpec((MLA_KV_RANK, 2 * MLA_OUT), const),
            pl.BlockSpec((tile, LANE), tab),
            pl.BlockSpec((tile, LANE), tab),
        ],
        out_specs=[pl.BlockSpec((tile, w), row) for w in widths]
        + [pl.BlockSpec((None, MLA_HEADS * MLA_VA, tile), lambda b, j: (b * nj + j, 0, 0))],
        out_shape=[jax.ShapeDtypeStruct((t, w), bf16) for w in widths]
        + [jax.ShapeDtypeStruct((t // tile, MLA_HEADS * MLA_VA, tile), bf16)],
        compiler_params=pltpu.CompilerParams(
            dimension_semantics=("parallel", "parallel"), vmem_limit_bytes=VMEM_LIMIT),
        name="prep",
    )(x2d, gain, w_in_r, q_norm, w_qb_r, kv_norm, w_kvb_r, ctab, stab)


def _gla_log_decay(a, wa2_ref, ba_ref):
    s = _dot(a, wa2_ref[...]) + ba_ref[...]
    return (jnp.minimum(s, 0.0) - jnp.log(1.0 + jnp.exp(-jnp.abs(s)))) * (1.0 / GLA_TAU)


def _gla_tile(q, k, v, la, tri, st_ref, want_out):
    t = la.shape[0]
    nc = t // CHUNK
    hi, mid, lo = _split3(la)
    b = _dot(tri, hi) + _dot(tri, mid) + _dot(tri, lo)
    b_last = [b[(c + 1) * CHUNK - 1:(c + 1) * CHUNK, :] for c in range(nc)]
    b_last_full = jnp.concatenate([jnp.broadcast_to(bl, (CHUNK, GLA_QK)) for bl in b_last], axis=0)
    kf = k.astype(f32)
    kd = (kf * jnp.exp(b_last_full - b)).astype(bf16)
    rr = lax.broadcasted_iota(i32, (GLA_VW, GLA_QK), 0) // GLA_DV
    cc = lax.broadcasted_iota(i32, (GLA_VW, GLA_QK), 1) // GLA_DK
    if want_out:
        qe = (q.astype(f32) * (GLA_DK ** -0.5) * jnp.exp(b)).astype(bf16)
        ke = kf * jnp.exp(-b)
        vf = v.astype(f32)
        lane_h = lax.broadcasted_iota(i32, (CHUNK, GLA_QK), 1) // GLA_DK
        vlane_h = lax.broadcasted_iota(i32, (CHUNK, GLA_VW), 1) // GLA_DV
        a_row = lax.broadcasted_iota(i32, (CHUNK, GLA_QK), 0)
        a_col = lax.broadcasted_iota(i32, (CHUNK, GLA_QK), 1) % CHUNK
    outs = []
    st = st_ref[...]
    for c in range(nc):
        rows = slice(c * CHUNK, (c + 1) * CHUNK)
        upd = jnp.where(rr == cc, _dot_tn(v[rows], kd[rows]), 0.0)
        if want_out:
            kbd = jnp.concatenate(
                [jnp.where(lane_h == h, ke[rows], 0.0) for h in range(GLA_HEADS)], axis=0).astype(bf16)
            att = jnp.where(a_col <= a_row, _dot_nt(qe[rows], kbd), 0.0).astype(bf16)
            vbd = jnp.concatenate(
                [jnp.where(vlane_h == h, vf[rows], 0.0) for h in range(GLA_HEADS)], axis=0).astype(bf16)
            outs.append(_dot(att, vbd) + _dot_nt(qe[rows], st.astype(bf16)))
        st = st * jnp.exp(b_last[c]) + upd
    st_ref[...] = st
    return jnp.concatenate(outs, axis=0) if want_out else None


def _gla_kernel(q_ref, k_ref, v_ref, r_ref, a_ref, km_ref, vm_ref, am_ref, wa2_ref, ba_ref, gain_ref, tri_ref,
                y_ref, st_ref):
    j = pl.program_id(1)

    @pl.when(j == 0)
    def _():
        st_ref[...] = jnp.zeros_like(st_ref)
        la = _gla_log_decay(am_ref[...], wa2_ref, ba_ref)
        row = lax.broadcasted_iota(i32, la.shape, 0)
        la = jnp.where(row >= CHUNK - N_META, la, 0.0)
        _gla_tile(None, km_ref[...], vm_ref[...], la, tri_ref[0:CHUNK, 0:CHUNK], st_ref, False)

    la = _gla_log_decay(a_ref[...], wa2_ref, ba_ref)
    o = _gla_tile(q_ref[...], k_ref[...], v_ref[...], la, tri_ref[...], st_ref, True)
    r = r_ref[...].astype(f32)
    outs = []
    for h in range(GLA_HEADS):
        oh = o[:, h * GLA_DV:(h + 1) * GLA_DV]
        outs.append(oh * lax.rsqrt(jnp.mean(oh * oh, axis=-1, keepdims=True) + EPS))
    on = jnp.concatenate(outs, axis=1) * gain_ref[...]
    y_ref[...] = (on * (r * jax.nn.sigmoid(r))).astype(bf16)


def _gla_call(qg, kg, vg, rg, ag, km, vm, am, wa2_p, b_a, gain, batch, seq):
    nj = seq // GLA_TILE

    def row(b, j):
        return (b * nj + j, 0)

    def const(b, j):
        return (0, 0)

    return pl.pallas_call(
        _gla_kernel,
        grid=(batch, nj),
        in_specs=[
            pl.BlockSpec((GLA_TILE, GLA_QK), row),
            pl.BlockSpec((GLA_TILE, GLA_QK), row),
            pl.BlockSpec((GLA_TILE, GLA_VW), row),
            pl.BlockSpec((GLA_TILE, GLA_VW), row),
            pl.BlockSpec((GLA_TILE, LANE), row),
            pl.BlockSpec((CHUNK, GLA_QK), const),
            pl.BlockSpec((CHUNK, GLA_VW), const),
            pl.BlockSpec((CHUNK, LANE), const),
            pl.BlockSpec((LANE, GLA_QK), const),
            pl.BlockSpec((1, GLA_QK), const),
            pl.BlockSpec((1, GLA_VW), const),
            pl.BlockSpec((GLA_TILE, GLA_TILE), const),
        ],
        out_specs=pl.BlockSpec((GLA_TILE, GLA_VW), row),
        out_shape=jax.ShapeDtypeStruct((batch * seq, GLA_VW), bf16),
        scratch_shapes=[pltpu.VMEM((GLA_VW, GLA_QK), f32)],
        compiler_params=pltpu.CompilerParams(
            dimension_semantics=("parallel", "arbitrary"), vmem_limit_bytes=VMEM_LIMIT),
        name="gla",
    )(qg, kg, vg, rg, ag, km, vm, am, wa2_p, b_a, gain, jnp.asarray(_CHUNK_PREFIX, bf16))


def _mla_kernel(q_ref, k_ref, vt_ref, km_ref, vmt_ref, o_ref, sa_ref, sb_ref):
    i = pl.program_id(2)
    tq = ATT_TILE
    w = MLA_QK_PAD
    va = MLA_VA
    heads = range(ATT_HEADS)

    def scores(h, blk):
        rows = pl.ds(pl.multiple_of(blk * tq, tq), tq)
        return _dot_nt(k_ref[rows, h * w:(h + 1) * w], q_ref[:, h * w:(h + 1) * w])

    def soft(s, vtb, carry, mask=None):
        m, acc = carry
        if mask is not None:
            s = jnp.where(mask, s, -1e30)
        m_new = jnp.maximum(m, jnp.max(s, axis=0, keepdims=True))
        p = jnp.exp2(s - m_new).astype(bf16)
        return m_new, jnp.exp2(m - m_new) * acc + _dot(vtb, p)

    def vt(h, blk):
        return vt_ref[blk, h * va:(h + 1) * va, :]

    def finish(h, carry):
        s = _dot_nt(km_ref[:, h * w:(h + 1) * w], q_ref[:, h * w:(h + 1) * w])
        m, acc = soft(s, vmt_ref[h * va:(h + 1) * va, :], carry)
        o_ref[:, h * MLA_V:(h + 1) * MLA_V] = (acc[:MLA_V] * (1.0 / acc[MLA_V:MLA_V + 1])).T.astype(bf16)

    kc = lax.broadcasted_iota(i32, (tq, tq), 0) // CHUNK
    qc = lax.broadcasted_iota(i32, (tq, tq), 1) // CHUNK
    mask = kc <= qc

    for h in heads:
        sa_ref[h] = scores(h, 0)

    def pair(p, carries):
        b0 = 2 * p
        for h in heads:
            sb_ref[h] = scores(h, b0 + 1)
        carries = [soft(sa_ref[h], vt(h, b0), carries[h]) for h in heads]
        for h in heads:
            sa_ref[h] = scores(h, b0 + 2)
        return tuple(soft(sb_ref[h], vt(h, b0 + 1), carries[h]) for h in heads)

    init = tuple((jnp.full((1, tq), -1e30, f32), jnp.zeros((va, tq), f32)) for _ in heads)
    carries = lax.fori_loop(0, i // 2, pair, init)

    @pl.when(i % 2 == 1)
    def _():
        for h in heads:
            sb_ref[h] = scores(h, i)
        for h in heads:
            c = soft(sa_ref[h], vt(h, i - 1), carries[h])
            finish(h, soft(sb_ref[h], vt(h, i), c, mask))

    @pl.when(i % 2 == 0)
    def _():
        for h in heads:
            finish(h, soft(sa_ref[h], vt(h, i), carries[h], mask))


def _mla_call(qm, km, vmt, km_meta, vmt_meta, batch, seq):
    nq = seq // ATT_TILE
    nh = ATT_HEADS
    qm3 = qm.reshape(batch, seq, MLA_HEADS * MLA_QK_PAD)
    km3 = km.reshape(batch, seq, MLA_HEADS * MLA_QK_PAD)
    vt4 = vmt.reshape(batch, nq, MLA_HEADS * MLA_VA, ATT_TILE)
    out = pl.pallas_call(
        _mla_kernel,
        grid=(batch, MLA_HEADS // nh, nq),
        in_specs=[
            pl.BlockSpec((None, ATT_TILE, nh * MLA_QK_PAD), lambda b, h, i: (b, i, h)),
            pl.BlockSpec((None, seq, nh * MLA_QK_PAD), lambda b, h, i: (b, 0, h)),
            pl.BlockSpec((None, nq, nh * MLA_VA, ATT_TILE), lambda b, h, i: (b, 0, h, 0)),
            pl.BlockSpec((N_META, nh * MLA_QK_PAD), lambda b, h, i: (0, h)),
            pl.BlockSpec((nh * MLA_VA, N_META), lambda b, h, i: (h, 0)),
        ],
        out_specs=pl.BlockSpec((None, ATT_TILE, nh * MLA_V), lambda b, h, i: (b, i, h)),
        out_shape=jax.ShapeDtypeStruct((batch, seq, MLA_OUT), bf16),
        scratch_shapes=[pltpu.VMEM((nh, ATT_TILE, ATT_TILE), f32), pltpu.VMEM((nh, ATT_TILE, ATT_TILE), f32)],
        compiler_params=pltpu.CompilerParams(
            dimension_semantics=("parallel", "parallel", "arbitrary"), vmem_limit_bytes=VMEM_LIMIT),
        name="mla",
    )(qm3, km3, vt4, km_meta, vmt_meta)
    return out.reshape(batch * seq, MLA_OUT)


def _route_cols(lt):
    r = lt.shape[1]
    neg = -1e30
    gl = lt[0:N_GROUPS, :]
    gsub = lax.broadcasted_iota(i32, (N_GROUPS, r), 0)
    gmax = jnp.max(gl, axis=0, keepdims=True)
    g_p = 1.0 / jnp.sum(jnp.exp(gl - gmax), axis=0, keepdims=True)
    g_idx = jnp.min(jnp.where(gl == gmax, gsub, N_GROUPS), axis=0, keepdims=True)
    el_all = lt[N_GROUPS:N_GROUPS + N_EXPERTS, :]
    esub = lax.broadcasted_iota(i32, (N_EXPERTS, r), 0)
    base = g_idx * EXPERTS_PER_GROUP
    e_mask = (esub >= base) & (esub < base + EXPERTS_PER_GROUP)
    el = jnp.where(e_mask, el_all, neg)
    m1 = jnp.max(el, axis=0, keepdims=True)
    i1 = jnp.min(jnp.where(e_mask & (el == m1), esub, N_EXPERTS), axis=0, keepdims=True)
    el2 = jnp.where(esub == i1, neg, el)
    m2 = jnp.max(el2, axis=0, keepdims=True)
    i2 = jnp.min(jnp.where(e_mask & (esub != i1) & (el2 == m2), esub, N_EXPERTS), axis=0, keepdims=True)
    rr = jnp.exp(m2 - m1)
    ga = g_p / (1.0 + rr)
    gb = g_p * rr / (1.0 + rr)
    la_ = i1 - base
    lb_ = i2 - base
    lo = jnp.minimum(la_, lb_)
    hi = jnp.maximum(la_, lb_)
    g_lo = jnp.where(la_ < lb_, ga, gb)
    g_hi = jnp.where(la_ < lb_, gb, ga)
    pidx = ((lo * (2 * EXPERTS_PER_GROUP - 1 - lo)) >> 1) + (hi - lo - 1)
    bucket = g_idx * N_PAIRS + pidx
    bsub = lax.broadcasted_iota(i32, (BUCKET_LANES, r), 0)
    oht = jnp.where(bsub == bucket, 1.0, 0.0)
    ohb = oht.astype(bf16)
    ri = lax.broadcasted_iota(i32, (r, r), 0)
    ci = lax.broadcasted_iota(i32, (r, r), 1)
    before = jnp.where(ri < ci, 1.0, 0.0).astype(bf16)
    cum = _dot(ohb, before)
    rank = jnp.sum(oht * cum, axis=0, keepdims=True)
    counts = _dot_nt(jnp.ones((SUBLANES, r), bf16), ohb)[0:1, :]
    msub = lax.broadcasted_iota(i32, (LANE, r), 0)
    meta_t = jnp.where(msub == 0, bucket.astype(f32),
                       jnp.where(msub == 1, rank,
                                 jnp.where(msub == 2, g_lo, jnp.where(msub == 3, g_hi, 0.0))))
    return meta_t.T, counts, meta_t[0:SUBLANES, :]


def _outproj_kernel(x_ref, yg_ref, ym_ref, wog_ref, wom_ref, gain_ref, wrt_ref, rb_ref,
                    h_ref, ux_ref, cnt_ref, rt_ref):
    for sub in range(OUT_TILE // ROUTE_ROWS):
        rows = slice(sub * ROUTE_ROWS, (sub + 1) * ROUTE_ROWS)
        h1 = x_ref[rows, :] + _dot(yg_ref[rows, :], wog_ref[...]) + _dot(ym_ref[rows, :], wom_ref[...])
        h_ref[rows, :] = h1
        u2 = _rms(h1, gain_ref[...])
        ux_ref[rows, 0:D_MODEL] = u2
        lt = _dot_nt(wrt_ref[...], u2.astype(bf16)) + rb_ref[...]
        meta, counts, routes = _route_cols(lt)
        ux_ref[rows, D_MODEL:ROW_W] = meta
        cnt_ref[sub] = counts
        rt_ref[sub] = routes


def _outproj_call(x2d, yg, ym, wo_g, wo_m, gain, w_r, rbias):
    t = x2d.shape[0]
    nt = t // OUT_TILE

    def row(i):
        return (i, 0)

    def const(i):
        return (0, 0)

    return pl.pallas_call(
        _outproj_kernel,
        grid=(nt,),
        in_specs=[
            pl.BlockSpec((OUT_TILE, D_MODEL), row),
            pl.BlockSpec((OUT_TILE, GLA_VW), row),
            pl.BlockSpec((OUT_TILE, MLA_OUT), row),
            pl.BlockSpec((GLA_VW, D_MODEL), const),
            pl.BlockSpec((MLA_OUT, D_MODEL), const),
            pl.BlockSpec((1, D_MODEL), const),
            pl.BlockSpec((LANE, D_MODEL), const),
            pl.BlockSpec((LANE, 1), const),
        ],
        out_specs=[
            pl.BlockSpec((OUT_TILE, D_MODEL), row),
            pl.BlockSpec((OUT_TILE, ROW_W), row),
            pl.BlockSpec((OUT_TILE // ROUTE_ROWS, 1, BUCKET_LANES), lambda i: (i, 0, 0)),
            pl.BlockSpec((OUT_TILE // ROUTE_ROWS, SUBLANES, ROUTE_ROWS), lambda i: (i, 0, 0)),
        ],
        out_shape=[
            jax.ShapeDtypeStruct((t, D_MODEL), f32),
            jax.ShapeDtypeStruct((t, ROW_W), f32),
            jax.ShapeDtypeStruct((nt * (OUT_TILE // ROUTE_ROWS), 1, BUCKET_LANES), f32),
            jax.ShapeDtypeStruct((nt * (OUT_TILE // ROUTE_ROWS), SUBLANES, ROUTE_ROWS), f32),
        ],
        compiler_params=pltpu.CompilerParams(
            dimension_semantics=("parallel",), vmem_limit_bytes=VMEM_LIMIT),
        name="outproj",
    )(x2d, yg, ym, wo_g, wo_m, gain, w_r, rbias)


def _scatter_kernel(pos_ref, zb_ref, ux_ref, hs_ref, zbuf, sem, zsem):
    @pl.when(pl.program_id(0) == 0)
    def _():
        zbuf[...] = jnp.zeros_like(zbuf)

        def zero_copy(j):
            rows = pl.ds(pl.multiple_of(zb_ref[j] * MOE_BLOCK, MOE_BLOCK), MOE_BLOCK)
            return pltpu.make_async_copy(zbuf, hs_ref.at[rows], zsem)

        def zstart(j, c):
            @pl.when(zb_ref[j] >= 0)
            def _():
                zero_copy(j).start()
            return c

        def zwait(j, c):
            @pl.when(zb_ref[j] >= 0)
            def _():
                zero_copy(j).wait()
            return c

        lax.fori_loop(0, zb_ref.shape[0], zstart, 0)
        lax.fori_loop(0, zb_ref.shape[0], zwait, 0)

    def start(ii, c):
        for k in range(SUBLANES):
            pltpu.make_async_copy(ux_ref.at[ii, pl.ds(k, 1)],
                                  hs_ref.at[pl.ds(pos_ref[ii * SUBLANES + k], 1)], sem).start()
        return c

    lax.fori_loop(0, SCATTER_TILE // SUBLANES, start, 0)
    pltpu.make_async_copy(hs_ref.at[pl.ds(0, SCATTER_TILE)], hs_ref.at[pl.ds(0, SCATTER_TILE)], sem).wait()


def _scatter_call(pos, zero_blocks, ux, n_slots):
    t = ux.shape[0]
    nz = zero_blocks.shape[0]
    return pl.pallas_call(
        _scatter_kernel,
        grid=(t // SCATTER_TILE,),
        in_specs=[
            pl.BlockSpec((SCATTER_TILE,), lambda i: (i,), memory_space=pltpu.SMEM),
            pl.BlockSpec((nz,), lambda i: (0,), memory_space=pltpu.SMEM),
            pl.BlockSpec((SCATTER_TILE // SUBLANES, SUBLANES, ROW_W), lambda i: (i, 0, 0)),
        ],
        out_specs=pl.BlockSpec(memory_space=pl.ANY),
        out_shape=jax.ShapeDtypeStruct((n_slots, ROW_W), f32),
        scratch_shapes=[pltpu.VMEM((MOE_BLOCK, ROW_W), f32), pltpu.SemaphoreType.DMA(()),
                        pltpu.SemaphoreType.DMA(())],
        compiler_params=pltpu.CompilerParams(
            dimension_semantics=("arbitrary",), vmem_limit_bytes=VMEM_LIMIT),
        name="scatter",
    )(pos, zero_blocks, ux.reshape(t // SUBLANES, SUBLANES, ROW_W))


def _moe_kernel(se_ref, sf_ref, sk_ref, sp_ref, sn_ref, sb_ref, sr_ref, si_ref, so_ref,
                hs_hbm, wg_hbm, wu_hbm, wd_hbm, y_hbm,
                xbuf, obuf, wg_buf, wu_buf, wd_buf, wgu_s, wd_s, in_sem, out_sem, w_sem):
    s = pl.program_id(0)
    ns = pl.num_programs(0)
    cur = s % 2
    g_n = MOE_GROUP

    def in_copy(step, g, buf):
        rows = pl.ds(pl.multiple_of(sb_ref[step * g_n + g] * MOE_BLOCK, MOE_BLOCK), MOE_BLOCK)
        return pltpu.make_async_copy(
            hs_hbm.at[rows], xbuf.at[buf, pl.ds(g * MOE_BLOCK, MOE_BLOCK)], in_sem.at[buf])

    def out_copy(step, g, buf):
        rows = pl.ds(pl.multiple_of(sb_ref[step * g_n + g] * MOE_BLOCK, MOE_BLOCK), MOE_BLOCK)
        cols = pl.ds(pl.multiple_of(sr_ref[step * g_n + g] * D_MODEL, D_MODEL), D_MODEL)
        return pltpu.make_async_copy(
            obuf.at[buf, pl.ds(g * MOE_BLOCK, MOE_BLOCK)], y_hbm.at[rows, cols], out_sem.at[buf])

    def for_slots(step, flags_ref, fn):
        for g in range(g_n):
            @pl.when(flags_ref[step * g_n + g] == 1)
            def _():
                fn(g)

    @pl.when(s == 0)
    def _():
        xbuf[...] = jnp.zeros_like(xbuf)
        for_slots(0, si_ref, lambda g: in_copy(0, g, 0).start())

    @pl.when(s + 1 < ns)
    def _():
        for_slots(s + 1, si_ref, lambda g: in_copy(s + 1, g, 1 - cur).start())

    for_slots(s, si_ref, lambda g: in_copy(s, g, cur).wait())

    @pl.when(s >= 2)
    def _():
        for_slots(s - 2, so_ref, lambda g: out_copy(s - 2, g, cur).wait())

    def w_copies(expert, slot):
        return (pltpu.make_async_copy(wg_hbm.at[expert], wg_buf.at[slot], w_sem.at[slot]),
                pltpu.make_async_copy(wu_hbm.at[expert], wu_buf.at[slot], w_sem.at[slot]),
                pltpu.make_async_copy(wd_hbm.at[expert], wd_buf.at[slot], w_sem.at[slot]))

    @pl.when(s == 0)
    def _():
        for c in w_copies(se_ref[0], sp_ref[0]):
            c.start()

    @pl.when(sf_ref[s] == 1)
    def _():
        slot = sp_ref[s]
        for c in w_copies(se_ref[s], slot):
            c.wait()
        wgu_s[:, 0:D_EXPERT] = wg_buf[slot].astype(bf16)
        wgu_s[:, D_EXPERT:2 * D_EXPERT] = wu_buf[slot].astype(bf16)
        wd_s[...] = wd_buf[slot].astype(bf16)

        @pl.when(sn_ref[s] >= 0)
        def _():
            for c in w_copies(sn_ref[s], 1 - slot):
                c.start()

    @pl.when(sk_ref[s] == 1)
    def _():
        u = xbuf[cur, :, 0:D_MODEL].astype(bf16)
        meta = xbuf[cur, :, D_MODEL:ROW_W]
        gate = jnp.concatenate(
            [jnp.where(sr_ref[s * g_n + g] == 0, meta[g * MOE_BLOCK:(g + 1) * MOE_BLOCK, 2:3],
                       meta[g * MOE_BLOCK:(g + 1) * MOE_BLOCK, 3:4]) for g in range(g_n)], axis=0)
        gu = _dot(u, wgu_s[...])
        gt = gu[:, 0:D_EXPERT]
        hdn = (gt * jax.nn.sigmoid(gt) * gu[:, D_EXPERT:]).astype(bf16)
        obuf[cur] = _dot(hdn, wd_s[...]) * gate

    @pl.when(sk_ref[s] == 0)
    def _():
        obuf[cur] = jnp.zeros(obuf.shape[1:], f32)

    for_slots(s, so_ref, lambda g: out_copy(s, g, cur).start())

    @pl.when(s == ns - 1)
    def _():
        for_slots(s, so_ref, lambda g: out_copy(s, g, cur).wait())

        @pl.when(s >= 1)
        def _():
            for_slots(s - 1, so_ref, lambda g: out_copy(s - 1, g, 1 - cur).wait())


def _moe_call(plan, hs, w_gate, w_up, w_down):
    n_steps = plan[0].shape[0]
    n_slots = hs.shape[0]
    rows = MOE_GROUP * MOE_BLOCK

    grid_spec = pltpu.PrefetchScalarGridSpec(
        num_scalar_prefetch=9,
        grid=(n_steps,),
        in_specs=[
            pl.BlockSpec(memory_space=pl.ANY),
            pl.BlockSpec(memory_space=pl.ANY),
            pl.BlockSpec(memory_space=pl.ANY),
            pl.BlockSpec(memory_space=pl.ANY),
        ],
        out_specs=pl.BlockSpec(memory_space=pl.ANY),
        scratch_shapes=[
            pltpu.VMEM((2, rows, ROW_W), f32),
            pltpu.VMEM((2, rows, D_MODEL), f32),
            pltpu.VMEM((2, D_MODEL, D_EXPERT), f32),
            pltpu.VMEM((2, D_MODEL, D_EXPERT), f32),
            pltpu.VMEM((2, D_EXPERT, D_MODEL), f32),
            pltpu.VMEM((D_MODEL, 2 * D_EXPERT), bf16),
            pltpu.VMEM((D_EXPERT, D_MODEL), bf16),
            pltpu.SemaphoreType.DMA((2,)),
            pltpu.SemaphoreType.DMA((2,)),
            pltpu.SemaphoreType.DMA((2,)),
        ],
    )
    return pl.pallas_call(
        _moe_kernel,
        grid_spec=grid_spec,
        out_shape=jax.ShapeDtypeStruct((n_slots, 2 * D_MODEL), f32),
        compiler_params=pltpu.CompilerParams(
            dimension_semantics=("arbitrary",), vmem_limit_bytes=VMEM_LIMIT),
        name="moe",
    )(*plan, hs, w_gate, w_up, w_down)


def _final_kernel(posc_ref, posn_ref, h_ref, gain_ref, y_hbm, o_ref, ybuf, sem):
    i = pl.program_id(0)
    cur = i % 2

    def issue(pos_ref, buf):
        def start(ii, c):
            for k in range(SUBLANES):
                pltpu.make_async_copy(y_hbm.at[pl.ds(pos_ref[ii * SUBLANES + k], 1)],
                                      ybuf.at[buf, ii, pl.ds(k, 1)], sem.at[buf]).start()
            return c

        lax.fori_loop(0, FINAL_TILE // SUBLANES, start, 0)

    @pl.when(i == 0)
    def _():
        issue(posc_ref, 0)

    @pl.when(i + 1 < pl.num_programs(0))
    def _():
        issue(posn_ref, 1 - cur)

    pltpu.make_async_copy(ybuf.at[cur], ybuf.at[cur], sem.at[cur]).wait()
    h = h_ref[...] + ybuf[cur, :, :, 0:D_MODEL] + ybuf[cur, :, :, D_MODEL:2 * D_MODEL]
    o_ref[...] = _rms(h, gain_ref[...])


def _final_call(pos, h1, gain, y):
    t = h1.shape[0]
    n = t // FINAL_TILE
    rows = FINAL_TILE // SUBLANES
    out = pl.pallas_call(
        _final_kernel,
        grid=(n,),
        in_specs=[
            pl.BlockSpec((FINAL_TILE,), lambda i: (i,), memory_space=pltpu.SMEM),
            pl.BlockSpec((FINAL_TILE,), lambda i: (jnp.minimum(i + 1, n - 1),), memory_space=pltpu.SMEM),
            pl.BlockSpec((rows, SUBLANES, D_MODEL), lambda i: (i, 0, 0)),
            pl.BlockSpec((1, 1, D_MODEL), lambda i: (0, 0, 0)),
            pl.BlockSpec(memory_space=pl.ANY),
        ],
        out_specs=pl.BlockSpec((rows, SUBLANES, D_MODEL), lambda i: (i, 0, 0)),
        out_shape=jax.ShapeDtypeStruct((t // SUBLANES, SUBLANES, D_MODEL), f32),
        scratch_shapes=[pltpu.VMEM((2, rows, SUBLANES, 2 * D_MODEL), f32), pltpu.SemaphoreType.DMA((2,))],
        compiler_params=pltpu.CompilerParams(
            dimension_semantics=("arbitrary",), vmem_limit_bytes=VMEM_LIMIT),
        name="final",
    )(pos, pos, h1.reshape(t // SUBLANES, SUBLANES, D_MODEL), gain.reshape(1, 1, D_MODEL), y)
    return out.reshape(t, D_MODEL)


def _rope_tables(pos):
    inv = ROPE_BASE ** (-jnp.arange(0, MLA_ROPE, 2, dtype=f32) / MLA_ROPE)
    ang = pos.astype(f32)[:, None] * inv[None, :]
    cos, sin = jnp.cos(ang), jnp.sin(ang)
    z = jnp.zeros((pos.shape[0], LANE - MLA_ROPE), f32)
    return jnp.concatenate([cos, cos, z], axis=1), jnp.concatenate([-sin, sin, z], axis=1)


def _relayout_weights(w_in, w_qb, w_kvb):
    half = MLA_ROPE // 2
    perm = (np.arange(MLA_ROPE) + half) % MLA_ROPE
    pts = np.cumsum((GLA_QK, GLA_QK, GLA_VW, GLA_VW, GLA_GATE_RANK, MLA_Q_RANK, MLA_KV_RANK, MLA_ROPE))
    q_g, k_g, v_g, r_g, a_l, q_lat, kv_lat, k_rope = jnp.split(w_in, pts[:-1], axis=1)
    a_seg = jnp.pad(a_l, ((0, 0), (0, LANE - GLA_GATE_RANK)))
    w_in_r = jnp.concatenate(
        [q_g, k_g, v_g, r_g, q_lat, kv_lat, k_rope, k_rope[:, perm], a_seg], axis=1).astype(bf16)
    qcols, kcols, vcols = [], [], []
    for h in range(MLA_HEADS):
        c = h * (MLA_NOPE + MLA_ROPE)
        rope = w_qb[:, c + MLA_NOPE:c + MLA_NOPE + MLA_ROPE]
        qcols += [w_qb[:, c:c + MLA_NOPE], rope, rope[:, perm]]
        c2 = h * (MLA_NOPE + MLA_V)
        kcols.append(w_kvb[:, c2:c2 + MLA_NOPE])
        vcols.append(w_kvb[:, c2 + MLA_NOPE:c2 + MLA_NOPE + MLA_V])
    return w_in_r, jnp.concatenate(qcols, axis=1).astype(bf16), jnp.concatenate(kcols + vcols, axis=1).astype(bf16)


_BUCKET_GROUP = np.arange(N_BUCKETS) // N_PAIRS
_RUN_EXPERT = np.concatenate([_BUCKET_GROUP * EXPERTS_PER_GROUP + _PAIR_LO[np.arange(N_BUCKETS) % N_PAIRS],
                              _BUCKET_GROUP * EXPERTS_PER_GROUP + _PAIR_HI[np.arange(N_BUCKETS) % N_PAIRS]])
_RUN_IS_EXPERT = (_RUN_EXPERT[:, None] == np.arange(N_EXPERTS)[None, :]).astype(np.int32)
_RUN_BEFORE = ((_RUN_EXPERT[:, None] == _RUN_EXPERT[None, :])
               & (np.arange(2 * N_BUCKETS)[None, :] < np.arange(2 * N_BUCKETS)[:, None])).astype(np.int32)


def _route_plan(counts, bucket, rank, n_tok):
    nt = counts.shape[0]
    g_n = MOE_GROUP
    tot = counts.sum(axis=0)
    nblk = (tot + MOE_BLOCK - 1) // MOE_BLOCK
    bstart_blk = jnp.cumsum(nblk) - nblk
    n_blocks = jnp.sum(nblk)
    tile_base = bstart_blk[None, :] * MOE_BLOCK + jnp.cumsum(counts, axis=0) - counts
    hit = bucket.reshape(nt, -1, 1) == jnp.arange(N_BUCKETS, dtype=i32)
    pos = jnp.sum(jnp.where(hit, tile_base[:, None, :], 0), axis=-1).reshape(-1) + rank
    nb_max = (n_tok + N_BUCKETS * (MOE_BLOCK - 1)) // MOE_BLOCK

    n_run = jnp.concatenate([nblk, nblk])
    b0_run = jnp.concatenate([bstart_blk, bstart_blk])
    c_e = jnp.sum(n_run[:, None] * _RUN_IS_EXPERT, axis=0)
    g_e = (c_e + g_n - 1) // g_n
    gend = jnp.cumsum(g_e)
    gstart = gend - g_e
    n_compute = gend[-1]
    off_run = jnp.sum(_RUN_BEFORE * n_run[None, :], axis=1)
    f_run = jnp.sum(_RUN_IS_EXPERT * gstart[None, :], axis=1) * g_n + off_run

    n_steps = (2 * nb_max + N_EXPERTS * (g_n - 1) + g_n - 1) // g_n + 1
    f = jnp.arange(n_steps * g_n, dtype=i32)
    in_run = (f[:, None] >= f_run[None, :]) & (f[:, None] < (f_run + n_run)[None, :])
    valid_c = jnp.any(in_run, axis=1)
    block_c = jnp.sum(jnp.where(in_run, b0_run[None, :] + f[:, None] - f_run[None, :], 0), axis=1)
    role_c = jnp.sum(jnp.where(in_run[:, N_BUCKETS:], 1, 0), axis=1)
    u_idx = f - n_compute * g_n
    valid_f = (u_idx >= 0) & (u_idx < 2 * (nb_max - n_blocks))
    slot_block = jnp.where(valid_c, block_c, jnp.where(valid_f, n_blocks + u_idx // 2, 0))
    slot_role = jnp.where(valid_c, role_c, jnp.where(valid_f, u_idx % 2, 0))

    step = jnp.arange(n_steps, dtype=i32)
    e_of_step = jnp.minimum(jnp.sum(gend[None, :] <= step[:, None], axis=1), N_EXPERTS - 1)
    is_compute = step < n_compute
    last_e = jnp.max(jnp.where(is_compute, e_of_step, 0))
    step_expert = jnp.where(is_compute, e_of_step, last_e)
    step_first = jnp.concatenate([jnp.ones((1,), bool), step_expert[1:] != step_expert[:-1]])
    ordinal = jnp.cumsum(step_first.astype(i32)) - 1
    ords = jnp.arange(N_EXPERTS + 1, dtype=i32)
    expert_of_ord = jnp.sum(jnp.where(step_first[:, None] & (ordinal[:, None] == ords[None, :]),
                                      step_expert[:, None], 0), axis=0)
    has_next = ordinal + 1 <= ordinal[-1]
    next_expert = jnp.sum(jnp.where(ords[None, :] == ordinal[:, None] + 1, expert_of_ord[None, :], 0), axis=1)
    step_next = jnp.where(step_first & has_next, next_expert, -1)
    plan = tuple(a.astype(i32) for a in
                 (step_expert, step_first, is_compute, ordinal % 2, step_next,
                  slot_block, slot_role, valid_c, valid_c | valid_f))
    last_blk = jnp.where(nblk > 0, bstart_blk + nblk - 1, -1)
    spare = n_blocks + jnp.arange(nb_max - n_tok // MOE_BLOCK, dtype=i32)
    zero_blocks = jnp.concatenate([last_blk, jnp.where(spare < nb_max, spare, -1)]).astype(i32)
    return pos.astype(i32), plan, zero_blocks, nb_max


def kernel(x, meta_tokens, mix_norm, w_in, gla_w_a2, gla_b_a, gla_out_norm, mla_q_norm, mla_w_qb, mla_kv_norm,
           mla_w_kvb, w_out, ffn_norm, router_group_w, router_group_b, router_expert_w, router_expert_b,
           expert_w_gate, expert_w_up, expert_w_down, final_norm):
    batch, seq, d = x.shape
    assert d == D_MODEL and seq % max(PREP_TILE, GLA_TILE, ATT_TILE) == 0
    assert (batch * seq) % max(OUT_TILE, SCATTER_TILE, FINAL_TILE) == 0
    n_tok = batch * seq
    x2d = x.reshape(n_tok, d)

    w_in_r, w_qb_r, w_kvb_r = _relayout_weights(w_in[0], mla_w_qb[0], mla_w_kvb[0])
    mixg = mix_norm[0].reshape(1, d)
    qn = mla_q_norm[0].reshape(1, MLA_Q_RANK)
    kvn = mla_kv_norm[0].reshape(1, MLA_KV_RANK)
    ct_m, st_m = _rope_tables(jnp.arange(META_TILE))
    ct_x, st_x = _rope_tables(N_META + jnp.arange(seq))

    x_meta = jnp.pad(meta_tokens.astype(f32), ((0, META_TILE - N_META), (0, 0)))
    _, kg_m, vg_m, _, a_m, _, km_m, vmt_m = _prep_call(
        x_meta, META_TILE, META_TILE, mixg, w_in_r, qn, w_qb_r, kvn, w_kvb_r, ct_m, st_m)
    qg, kg, vg, rg, ag, qm, km, vmt = _prep_call(
        x2d, seq, PREP_TILE, mixg, w_in_r, qn, w_qb_r, kvn, w_kvb_r, ct_x, st_x)

    def chunk0(a):
        return jnp.pad(a[:N_META], ((CHUNK - N_META, 0), (0, 0)))

    wa2_p = jnp.pad(gla_w_a2[0], ((0, LANE - GLA_GATE_RANK), (0, 0))).astype(bf16)
    y_gla = _gla_call(qg, kg, vg, rg, ag, chunk0(kg_m), chunk0(vg_m), chunk0(a_m),
                      wa2_p, gla_b_a[0].reshape(1, GLA_QK), gla_out_norm[0].reshape(1, GLA_VW), batch, seq)
    y_mla = _mla_call(qm, km, vmt, km_m[:N_META], vmt_m[0, :, :N_META], batch, seq)

    wo = w_out[0].astype(bf16)
    rw = jnp.concatenate([router_group_w[0], router_expert_w[0],
                          jnp.zeros((d, LANE - N_GROUPS - N_EXPERTS), f32)], axis=1)
    rb = jnp.concatenate([router_group_b[0], router_expert_b[0],
                          jnp.zeros((LANE - N_GROUPS - N_EXPERTS,), f32)]).reshape(1, LANE)
    h1, ux, cnt, routes = _outproj_call(x2d, y_gla, y_mla, wo[:GLA_VW], wo[GLA_VW:], ffn_norm[0].reshape(1, d),
                                      rw.T.astype(bf16), rb.reshape(LANE, 1))

    counts = cnt.reshape(-1, BUCKET_LANES)[:, :N_BUCKETS].astype(i32)
    tok_bucket = routes[:, 0, :].reshape(-1).astype(i32)
    tok_rank = routes[:, 1, :].reshape(-1).astype(i32)
    pos, plan, zero_blocks, nb_max = _route_plan(counts, tok_bucket, tok_rank, n_tok)
    n_slots = nb_max * MOE_BLOCK
    hs = _scatter_call(pos, zero_blocks, ux, n_slots)
    y = _moe_call(plan, hs, expert_w_gate[0], expert_w_up[0], expert_w_down[0])
    out = _final_call(pos, h1, final_norm.reshape(1, d), y)
    return out.reshape(batch, seq, d)
```

```python
import functools

import numpy as np
import jax
import jax.numpy as jnp
from jax import lax
from jax.experimental import pallas as pl
from jax.experimental.pallas import tpu as pltpu

f32 = jnp.float32
bf16 = jnp.bfloat16
i32 = jnp.int32

D_MODEL = 1024
CHUNK = 64
N_META = 16
EPS = 1e-6
GLA_HEADS = 4
GLA_DK = 64
GLA_DV = 128
GLA_GATE_RANK = 16
GLA_TAU = 16.0
GLA_QK = GLA_HEADS * GLA_DK
GLA_VW = GLA_HEADS * GLA_DV
MLA_HEADS = 4
MLA_Q_RANK = 256
MLA_KV_RANK = 128
MLA_NOPE = 128
MLA_ROPE = 64
MLA_V = 128
MLA_OUT = MLA_HEADS * MLA_V
MLA_QK_PAD = 256
MLA_VA = MLA_V + 16
LOG2_E = 1.4426950408889634
ROPE_BASE = 10000.0
N_GROUPS = 8
EXPERTS_PER_GROUP = 8
N_EXPERTS = N_GROUPS * EXPERTS_PER_GROUP
D_EXPERT = 512
N_PAIRS = EXPERTS_PER_GROUP * (EXPERTS_PER_GROUP - 1) // 2
N_BUCKETS = N_GROUPS * N_PAIRS
BUCKET_LANES = 256
LANE = 128
SUBLANES = 8
META_W = LANE
ROW_W = D_MODEL + META_W

PREP_TILE = 512
GLA_TILE = 512
GLA_BATCH = 4
ATT_TILE = 512
ATT_HEADS = 4
META_TILE = 128
OUT_TILE = 1024
ROUTE_ROWS = 256
ROUTE_GROUP = 512
SCATTER_TILE = 2048
FINAL_TILE = 512
MOE_BLOCK = 64
MOE_GROUP = 8
VMEM_LIMIT = 56 * 1024 * 1024

C_Q, C_K, C_V, C_R = 0, 256, 512, 1024
C_QLAT, C_KVLAT, C_KROPE, C_A, C_END = 1536, 1792, 1920, 2048, 2176

_TILE_POS = np.arange(GLA_TILE)
_CHUNK_PREFIX = ((_TILE_POS[:, None] // CHUNK == _TILE_POS[None, :] // CHUNK)
                 & (_TILE_POS[None, :] <= _TILE_POS[:, None])).astype(np.float32)
_PAIR_LO = np.array([lo for lo in range(8) for hi in range(lo + 1, 8)], np.int32)
_PAIR_HI = np.array([hi for lo in range(8) for hi in range(lo + 1, 8)], np.int32)


def _dot(a, b):
    return jnp.dot(a, b, preferred_element_type=f32)


def _dot_nt(a, b):
    return lax.dot_general(a, b, (((1,), (1,)), ((), ())), preferred_element_type=f32)


def _dot_tn(a, b):
    return lax.dot_general(a, b, (((0,), (0,)), ((), ())), preferred_element_type=f32)


def _rms(x, gain):
    return x * lax.rsqrt(jnp.mean(x * x, axis=-1, keepdims=True) + EPS) * gain


def _split3(x):
    hi = x.astype(bf16)
    r1 = x - hi.astype(f32)
    mid = r1.astype(bf16)
    lo = (r1 - mid.astype(f32)).astype(bf16)
    return hi, mid, lo


def _prep_kernel(x_ref, g_ref, win_ref, qn_ref, wqb_ref, kvn_ref, wkvb_ref, ct_ref, st_ref,
                 qg_ref, kg_ref, vg_ref, rg_ref, a_ref, qm_ref, km_ref, vmt_ref):
    u = _rms(x_ref[...], g_ref[...]).astype(bf16)

    def proj(lo, hi):
        return _dot(u, win_ref[:, lo:hi])

    qg_ref[...] = proj(C_Q, C_K).astype(bf16)
    kg_ref[...] = proj(C_K, C_V).astype(bf16)
    vg_ref[...] = proj(C_V, C_R).astype(bf16)
    rg_ref[...] = proj(C_R, C_QLAT).astype(bf16)
    z = proj(C_QLAT, C_END)
    a_ref[...] = z[:, C_A - C_QLAT:].astype(bf16)
    ctab = ct_ref[...]
    stab = st_ref[...]

    def rope(seg):
        return seg * ctab + pltpu.roll(seg, 64, axis=1) * stab

    k_rope = rope(z[:, C_KROPE - C_QLAT:C_A - C_QLAT]).astype(bf16)
    qn = _rms(z[:, 0:MLA_Q_RANK], qn_ref[...]).astype(bf16)
    kvn = _rms(z[:, MLA_Q_RANK:MLA_Q_RANK + MLA_KV_RANK], kvn_ref[...]).astype(bf16)
    scale = (MLA_NOPE + MLA_ROPE) ** -0.5 * LOG2_E
    qf = _dot(qn, wqb_ref[...])
    kvf = _dot(kvn, wkvb_ref[...])
    for h in range(MLA_HEADS):
        c = h * MLA_QK_PAD
        qm_ref[:, c:c + LANE] = (qf[:, c:c + LANE] * scale).astype(bf16)
        qm_ref[:, c + LANE:c + 2 * LANE] = (rope(qf[:, c + LANE:c + 2 * LANE]) * scale).astype(bf16)
        km_ref[:, c:c + LANE] = kvf[:, h * LANE:(h + 1) * LANE].astype(bf16)
        km_ref[:, c + LANE:c + 2 * LANE] = k_rope
    vt = kvf[:, MLA_HEADS * MLA_NOPE:].T
    for h in range(MLA_HEADS):
        vmt_ref[h * MLA_VA:h * MLA_VA + MLA_V, :] = vt[h * MLA_V:(h + 1) * MLA_V].astype(bf16)
        vmt_ref[h * MLA_VA + MLA_V:(h + 1) * MLA_VA, :] = jnp.ones((MLA_VA - MLA_V, vt.shape[1]), bf16)


def _prep_call(x2d, rows_per_seq, tile, gain, w_in_r, q_norm, w_qb_r, kv_norm, w_kvb_r, ctab, stab):
    t = x2d.shape[0]
    nj = rows_per_seq // tile
    grid = (t // rows_per_seq, nj)

    def row(b, j):
        return (b * nj + j, 0)

    def const(b, j):
        return (0, 0)

    def tab(b, j):
        return (j, 0)

    widths = (GLA_QK, GLA_QK, GLA_VW, GLA_VW, LANE, MLA_HEADS * MLA_QK_PAD, MLA_HEADS * MLA_QK_PAD)
    return pl.pallas_call(
        _prep_kernel,
        grid=grid,
        in_specs=[
            pl.BlockSpec((tile, D_MODEL), row),
            pl.BlockSpec((1, D_MODEL), const),
            pl.BlockSpec((D_MODEL, C_END), const),
            pl.BlockSpec((1, MLA_Q_RANK), const),
            pl.BlockSpec((MLA_Q_RANK, MLA_HEADS * MLA_QK_PAD), const),
            pl.BlockSpec((1, MLA_KV_RANK), const),
            pl.BlockSpec((MLA_KV_RANK, 2 * MLA_OUT), const),
            pl.BlockSpec((tile, LANE), tab),
            pl.BlockSpec((tile, LANE), tab),
        ],
        out_specs=[pl.BlockSpec((tile, w), row) for w in widths]
        + [pl.BlockSpec((None, MLA_HEADS * MLA_VA, tile), lambda b, j: (b * nj + j, 0, 0))],
        out_shape=[jax.ShapeDtypeStruct((t, w), bf16) for w in widths]
        + [jax.ShapeDtypeStruct((t // tile, MLA_HEADS * MLA_VA, tile), bf16)],
        compiler_params=pltpu.CompilerParams(
            dimension_semantics=("parallel", "parallel"), vmem_limit_bytes=VMEM_LIMIT),
        name="prep",
    )(x2d, gain, w_in_r, q_norm, w_qb_r, kv_norm, w_kvb_r, ctab, stab)


def _gla_log_decay(a, wa2_ref, ba_ref):
    s = _dot(a, wa2_ref[...]) + ba_ref[...]
    return (jnp.minimum(s, 0.0) - jnp.log(1.0 + jnp.exp(-jnp.abs(s)))) * (1.0 / GLA_TAU)


def _gla_front(q, k, v, la, tri, want_out):
    nc = la.shape[0] // CHUNK
    hi, mid, lo = _split3(la)
    b = _dot(tri, hi) + _dot(tri, mid) + _dot(tri, lo)
    b_last = [b[(c + 1) * CHUNK - 1:(c + 1) * CHUNK, :] for c in range(nc)]
    b_last_full = jnp.concatenate([jnp.broadcast_to(bl, (CHUNK, GLA_QK)) for bl in b_last], axis=0)
    kf = k.astype(f32)
    front = dict(v=v, b_last=b_last, kd=(kf * jnp.exp(b_last_full - b)).astype(bf16))
    if want_out:
        front.update(qe=(q.astype(f32) * (GLA_DK ** -0.5) * jnp.exp(b)).astype(bf16),
                     ke=kf * jnp.exp(-b), vf=v.astype(f32))
    return front


def _gla_chunks(front, st_ref, want_out):
    v, kd, b_last = front["v"], front["kd"], front["b_last"]
    rr = lax.broadcasted_iota(i32, (GLA_VW, GLA_QK), 0) // GLA_DV
    cc = lax.broadcasted_iota(i32, (GLA_VW, GLA_QK), 1) // GLA_DK
    if want_out:
        qe, ke, vf = front["qe"], front["ke"], front["vf"]
        lane_h = lax.broadcasted_iota(i32, (CHUNK, GLA_QK), 1) // GLA_DK
        vlane_h = lax.broadcasted_iota(i32, (CHUNK, GLA_VW), 1) // GLA_DV
        a_row = lax.broadcasted_iota(i32, (CHUNK, GLA_QK), 0)
        a_col = lax.broadcasted_iota(i32, (CHUNK, GLA_QK), 1) % CHUNK
    outs = []
    st = st_ref[...]
    for c in range(len(b_last)):
        rows = slice(c * CHUNK, (c + 1) * CHUNK)
        upd = jnp.where(rr == cc, _dot_tn(v[rows], kd[rows]), 0.0)
        if want_out:
            kbd = jnp.concatenate(
                [jnp.where(lane_h == h, ke[rows], 0.0) for h in range(GLA_HEADS)], axis=0).astype(bf16)
            att = jnp.where(a_col <= a_row, _dot_nt(qe[rows], kbd), 0.0).astype(bf16)
            vbd = jnp.concatenate(
                [jnp.where(vlane_h == h, vf[rows], 0.0) for h in range(GLA_HEADS)], axis=0).astype(bf16)
            outs.append(_dot(att, vbd) + _dot_nt(qe[rows], st.astype(bf16)))
        st = st * jnp.exp(b_last[c]) + upd
    st_ref[...] = st
    return jnp.concatenate(outs, axis=0) if want_out else None


def _gla_kernel(q_ref, k_ref, v_ref, r_ref, a_ref, km_ref, vm_ref, am_ref, wa2_ref, ba_ref, gain_ref, tri_ref,
                y_ref, st_ref):
    j = pl.program_id(1)

    @pl.when(j == 0)
    def _():
        st_ref[...] = jnp.zeros_like(st_ref)
        la = _gla_log_decay(am_ref[...], wa2_ref, ba_ref)
        row = lax.broadcasted_iota(i32, la.shape, 0)
        la = jnp.where(row >= CHUNK - N_META, la, 0.0)
        front = _gla_front(None, km_ref[...], vm_ref[...], la, tri_ref[0:CHUNK, 0:CHUNK], False)
        _gla_chunks(front, st_ref.at[0], False)
        for bb in range(1, GLA_BATCH):
            st_ref[bb] = st_ref[0]

    fronts = [_gla_front(q_ref[bb], k_ref[bb], v_ref[bb], _gla_log_decay(a_ref[bb], wa2_ref, ba_ref),
                         tri_ref[...], True) for bb in range(GLA_BATCH)]
    for bb in range(GLA_BATCH):
        o = _gla_chunks(fronts[bb], st_ref.at[bb], True)
        r = r_ref[bb].astype(f32)
        outs = []
        for h in range(GLA_HEADS):
            oh = o[:, h * GLA_DV:(h + 1) * GLA_DV]
            outs.append(oh * lax.rsqrt(jnp.mean(oh * oh, axis=-1, keepdims=True) + EPS))
        on = jnp.concatenate(outs, axis=1) * gain_ref[...]
        y_ref[bb] = (on * (r * jax.nn.sigmoid(r))).astype(bf16)


def _gla_call(qg, kg, vg, rg, ag, km, vm, am, wa2_p, b_a, gain, batch, seq):
    nj = seq // GLA_TILE

    def row(b, j):
        return (b, j, 0)

    def const(b, j):
        return (0, 0)

    def seqs(a):
        return a.reshape(batch, seq, a.shape[-1])

    out = pl.pallas_call(
        _gla_kernel,
        grid=(batch // GLA_BATCH, nj),
        in_specs=[
            pl.BlockSpec((GLA_BATCH, GLA_TILE, GLA_QK), row),
            pl.BlockSpec((GLA_BATCH, GLA_TILE, GLA_QK), row),
            pl.BlockSpec((GLA_BATCH, GLA_TILE, GLA_VW), row),
            pl.BlockSpec((GLA_BATCH, GLA_TILE, GLA_VW), row),
            pl.BlockSpec((GLA_BATCH, GLA_TILE, LANE), row),
            pl.BlockSpec((CHUNK, GLA_QK), const),
            pl.BlockSpec((CHUNK, GLA_VW), const),
            pl.BlockSpec((CHUNK, LANE), const),
            pl.BlockSpec((LANE, GLA_QK), const),
            pl.BlockSpec((1, GLA_QK), const),
            pl.BlockSpec((1, GLA_VW), const),
            pl.BlockSpec((GLA_TILE, GLA_TILE), const),
        ],
        out_specs=pl.BlockSpec((GLA_BATCH, GLA_TILE, GLA_VW), row),
        out_shape=jax.ShapeDtypeStruct((batch, seq, GLA_VW), bf16),
        scratch_shapes=[pltpu.VMEM((GLA_BATCH, GLA_VW, GLA_QK), f32)],
        compiler_params=pltpu.CompilerParams(
            dimension_semantics=("parallel", "arbitrary"), vmem_limit_bytes=VMEM_LIMIT),
        name="gla",
    )(seqs(qg), seqs(kg), seqs(vg), seqs(rg), seqs(ag), km, vm, am, wa2_p, b_a, gain,
      jnp.asarray(_CHUNK_PREFIX, bf16))
    return out.reshape(batch * seq, GLA_VW)


def _mla_kernel(q_ref, k_ref, vt_ref, km_ref, vmt_ref, o_ref, sa_ref, sb_ref):
    i = pl.program_id(2)
    tq = ATT_TILE
    w = MLA_QK_PAD
    va = MLA_VA
    heads = range(ATT_HEADS)

    def scores(h, blk):
        rows = pl.ds(pl.multiple_of(blk * tq, tq), tq)
        return _dot_nt(k_ref[rows, h * w:(h + 1) * w], q_ref[:, h * w:(h + 1) * w])

    def soft(s, vtb, carry, mask=None):
        m, acc = carry
        if mask is not None:
            s = jnp.where(mask, s, -1e30)
        m_new = jnp.maximum(m, jnp.max(s, axis=0, keepdims=True))
        p = jnp.exp2(s - m_new).astype(bf16)
        return m_new, jnp.exp2(m - m_new) * acc + _dot(vtb, p)

    def vt(h, blk):
        return vt_ref[blk, h * va:(h + 1) * va, :]

    def finish(h, carry):
        s = _dot_nt(km_ref[:, h * w:(h + 1) * w], q_ref[:, h * w:(h + 1) * w])
        m, acc = soft(s, vmt_ref[h * va:(h + 1) * va, :], carry)
        o_ref[:, h * MLA_V:(h + 1) * MLA_V] = (acc[:MLA_V] * (1.0 / acc[MLA_V:MLA_V + 1])).T.astype(bf16)

    kc = lax.broadcasted_iota(i32, (tq, tq), 0) // CHUNK
    qc = lax.broadcasted_iota(i32, (tq, tq), 1) // CHUNK
    mask = kc <= qc

    for h in heads:
        sa_ref[h] = scores(h, 0)

    def pair(p, carries):
        b0 = 2 * p
        for h in heads:
            sb_ref[h] = scores(h, b0 + 1)
        carries = [soft(sa_ref[h], vt(h, b0), carries[h]) for h in heads]
        for h in heads:
            sa_ref[h] = scores(h, b0 + 2)
        return tuple(soft(sb_ref[h], vt(h, b0 + 1), carries[h]) for h in heads)

    init = tuple((jnp.full((1, tq), -1e30, f32), jnp.zeros((va, tq), f32)) for _ in heads)
    carries = lax.fori_loop(0, i // 2, pair, init)

    @pl.when(i % 2 == 1)
    def _():
        for h in heads:
            sb_ref[h] = scores(h, i)
        for h in heads:
            c = soft(sa_ref[h], vt(h, i - 1), carries[h])
            finish(h, soft(sb_ref[h], vt(h, i), c, mask))

    @pl.when(i % 2 == 0)
    def _():
        for h in heads:
            finish(h, soft(sa_ref[h], vt(h, i), carries[h], mask))


def _mla_call(qm, km, vmt, km_meta, vmt_meta, batch, seq):
    nq = seq // ATT_TILE
    nh = ATT_HEADS
    qm3 = qm.reshape(batch, seq, MLA_HEADS * MLA_QK_PAD)
    km3 = km.reshape(batch, seq, MLA_HEADS * MLA_QK_PAD)
    vt4 = vmt.reshape(batch, nq, MLA_HEADS * MLA_VA, ATT_TILE)
    out = pl.pallas_call(
        _mla_kernel,
        grid=(batch, MLA_HEADS // nh, nq),
        in_specs=[
            pl.BlockSpec((None, ATT_TILE, nh * MLA_QK_PAD), lambda b, h, i: (b, i, h)),
            pl.BlockSpec((None, seq, nh * MLA_QK_PAD), lambda b, h, i: (b, 0, h)),
            pl.BlockSpec((None, nq, nh * MLA_VA, ATT_TILE), lambda b, h, i: (b, 0, h, 0)),
            pl.BlockSpec((N_META, nh * MLA_QK_PAD), lambda b, h, i: (0, h)),
            pl.BlockSpec((nh * MLA_VA, N_META), lambda b, h, i: (h, 0)),
        ],
        out_specs=pl.BlockSpec((None, ATT_TILE, nh * MLA_V), lambda b, h, i: (b, i, h)),
        out_shape=jax.ShapeDtypeStruct((batch, seq, MLA_OUT), bf16),
        scratch_shapes=[pltpu.VMEM((nh, ATT_TILE, ATT_TILE), f32), pltpu.VMEM((nh, ATT_TILE, ATT_TILE), f32)],
        compiler_params=pltpu.CompilerParams(
            dimension_semantics=("parallel", "parallel", "arbitrary"), vmem_limit_bytes=VMEM_LIMIT),
        name="mla",
    )(qm3, km3, vt4, km_meta, vmt_meta)
    return out.reshape(batch * seq, MLA_OUT)


def _route_cols(lt):
    r = lt.shape[1]
    neg = -1e30
    gl = lt[0:N_GROUPS, :]
    gsub = lax.broadcasted_iota(i32, (N_GROUPS, r), 0)
    gmax = jnp.max(gl, axis=0, keepdims=True)
    g_p = 1.0 / jnp.sum(jnp.exp(gl - gmax), axis=0, keepdims=True)
    g_idx = jnp.min(jnp.where(gl == gmax, gsub, N_GROUPS), axis=0, keepdims=True)
    el_all = lt[N_GROUPS:N_GROUPS + N_EXPERTS, :]
    esub = lax.broadcasted_iota(i32, (N_EXPERTS, r), 0)
    base = g_idx * EXPERTS_PER_GROUP
    e_mask = (esub >= base) & (esub < base + EXPERTS_PER_GROUP)
    el = jnp.where(e_mask, el_all, neg)
    m1 = jnp.max(el, axis=0, keepdims=True)
    i1 = jnp.min(jnp.where(e_mask & (el == m1), esub, N_EXPERTS), axis=0, keepdims=True)
    el2 = jnp.where(esub == i1, neg, el)
    m2 = jnp.max(el2, axis=0, keepdims=True)
    i2 = jnp.min(jnp.where(e_mask & (esub != i1) & (el2 == m2), esub, N_EXPERTS), axis=0, keepdims=True)
    rr = jnp.exp(m2 - m1)
    ga = g_p / (1.0 + rr)
    gb = g_p * rr / (1.0 + rr)
    la_ = i1 - base
    lb_ = i2 - base
    lo = jnp.minimum(la_, lb_)
    hi = jnp.maximum(la_, lb_)
    g_lo = jnp.where(la_ < lb_, ga, gb)
    g_hi = jnp.where(la_ < lb_, gb, ga)
    pidx = ((lo * (2 * EXPERTS_PER_GROUP - 1 - lo)) >> 1) + (hi - lo - 1)
    bucket = g_idx * N_PAIRS + pidx
    bsub = lax.broadcasted_iota(i32, (BUCKET_LANES, r), 0)
    oht = jnp.where(bsub == bucket, 1.0, 0.0)
    ohb = oht.astype(bf16)
    ri = lax.broadcasted_iota(i32, (ROUTE_ROWS, ROUTE_ROWS), 0)
    ci = lax.broadcasted_iota(i32, (ROUTE_ROWS, ROUTE_ROWS), 1)
    before = jnp.where(ri < ci, 1.0, 0.0).astype(bf16)
    ones = jnp.ones((SUBLANES, ROUTE_ROWS), bf16)
    subs = [slice(i * ROUTE_ROWS, (i + 1) * ROUTE_ROWS) for i in range(r // ROUTE_ROWS)]
    cum = jnp.concatenate([_dot(ohb[:, sl], before) for sl in subs], axis=1)
    rank = jnp.sum(oht * cum, axis=0, keepdims=True)
    counts = [_dot_nt(ones, ohb[:, sl])[0:1, :] for sl in subs]
    msub = lax.broadcasted_iota(i32, (LANE, r), 0)
    meta_t = jnp.where(msub == 0, bucket.astype(f32),
                       jnp.where(msub == 1, rank,
                                 jnp.where(msub == 2, g_lo, jnp.where(msub == 3, g_hi, 0.0))))
    return meta_t.T, counts, meta_t[0:SUBLANES, :]


def _outproj_kernel(x_ref, yg_ref, ym_ref, wog_ref, wom_ref, gain_ref, wrt_ref, rb_ref,
                    h_ref, ux_ref, cnt_ref, rt_ref):
    per_group = ROUTE_GROUP // ROUTE_ROWS
    for grp in range(OUT_TILE // ROUTE_GROUP):
        rows = slice(grp * ROUTE_GROUP, (grp + 1) * ROUTE_GROUP)
        h1 = x_ref[rows, :] + _dot(yg_ref[rows, :], wog_ref[...]) + _dot(ym_ref[rows, :], wom_ref[...])
        h_ref[rows, :] = h1
        u2 = _rms(h1, gain_ref[...])
        ux_ref[rows, 0:D_MODEL] = u2
        lt = _dot_nt(wrt_ref[...], u2.astype(bf16)) + rb_ref[...]
        meta, counts, routes = _route_cols(lt)
        ux_ref[rows, D_MODEL:ROW_W] = meta
        for i in range(per_group):
            cnt_ref[grp * per_group + i] = counts[i]
            rt_ref[grp * per_group + i] = routes[:, i * ROUTE_ROWS:(i + 1) * ROUTE_ROWS]


def _outproj_call(x2d, yg, ym, wo_g, wo_m, gain, w_r, rbias):
    t = x2d.shape[0]
    nt = t // OUT_TILE

    def row(i):
        return (i, 0)

    def const(i):
        return (0, 0)

    return pl.pallas_call(
        _outproj_kernel,
        grid=(nt,),
        in_specs=[
            pl.BlockSpec((OUT_TILE, D_MODEL), row),
            pl.BlockSpec((OUT_TILE, GLA_VW), row),
            pl.BlockSpec((OUT_TILE, MLA_OUT), row),
            pl.BlockSpec((GLA_VW, D_MODEL), const),
            pl.BlockSpec((MLA_OUT, D_MODEL), const),
            pl.BlockSpec((1, D_MODEL), const),
            pl.BlockSpec((LANE, D_MODEL), const),
            pl.BlockSpec((LANE, 1), const),
        ],
        out_specs=[
            pl.BlockSpec((OUT_TILE, D_MODEL), row),
            pl.BlockSpec((OUT_TILE, ROW_W), row),
            pl.BlockSpec((OUT_TILE // ROUTE_ROWS, 1, BUCKET_LANES), lambda i: (i, 0, 0)),
            pl.BlockSpec((OUT_TILE // ROUTE_ROWS, SUBLANES, ROUTE_ROWS), lambda i: (i, 0, 0)),
        ],
        out_shape=[
            jax.ShapeDtypeStruct((t, D_MODEL), f32),
            jax.ShapeDtypeStruct((t, ROW_W), f32),
            jax.ShapeDtypeStruct((nt * (OUT_TILE // ROUTE_ROWS), 1, BUCKET_LANES), f32),
            jax.ShapeDtypeStruct((nt * (OUT_TILE // ROUTE_ROWS), SUBLANES, ROUTE_ROWS), f32),
        ],
        compiler_params=pltpu.CompilerParams(
            dimension_semantics=("parallel",), vmem_limit_bytes=VMEM_LIMIT),
        name="outproj",
    )(x2d, yg, ym, wo_g, wo_m, gain, w_r, rbias)


def _scatter_kernel(pos_ref, zb_ref, ux_ref, hs_ref, zbuf, sem, zsem):
    @pl.when(pl.program_id(0) == 0)
    def _():
        zbuf[...] = jnp.zeros_like(zbuf)

        def zero_copy(j):
            rows = pl.ds(pl.multiple_of(zb_ref[j] * MOE_BLOCK, MOE_BLOCK), MOE_BLOCK)
            return pltpu.make_async_copy(zbuf, hs_ref.at[rows], zsem)

        def zstart(j, c):
            @pl.when(zb_ref[j] >= 0)
            def _():
                zero_copy(j).start()
            return c

        def zwait(j, c):
            @pl.when(zb_ref[j] >= 0)
            def _():
                zero_copy(j).wait()
            return c

        lax.fori_loop(0, zb_ref.shape[0], zstart, 0)
        lax.fori_loop(0, zb_ref.shape[0], zwait, 0)

    def start(ii, c):
        for k in range(SUBLANES):
            pltpu.make_async_copy(ux_ref.at[ii, pl.ds(k, 1)],
                                  hs_ref.at[pl.ds(pos_ref[ii * SUBLANES + k], 1)], sem).start()
        return c

    lax.fori_loop(0, SCATTER_TILE // SUBLANES, start, 0)
    pltpu.make_async_copy(hs_ref.at[pl.ds(0, SCATTER_TILE)], hs_ref.at[pl.ds(0, SCATTER_TILE)], sem).wait()


def _scatter_call(pos, zero_blocks, ux, n_slots):
    t = ux.shape[0]
    nz = zero_blocks.shape[0]
    return pl.pallas_call(
        _scatter_kernel,
        grid=(t // SCATTER_TILE,),
        in_specs=[
            pl.BlockSpec((SCATTER_TILE,), lambda i: (i,), memory_space=pltpu.SMEM),
            pl.BlockSpec((nz,), lambda i: (0,), memory_space=pltpu.SMEM),
            pl.BlockSpec((SCATTER_TILE // SUBLANES, SUBLANES, ROW_W), lambda i: (i, 0, 0)),
        ],
        out_specs=pl.BlockSpec(memory_space=pl.ANY),
        out_shape=jax.ShapeDtypeStruct((n_slots, ROW_W), f32),
        scratch_shapes=[pltpu.VMEM((MOE_BLOCK, ROW_W), f32), pltpu.SemaphoreType.DMA(()),
                        pltpu.SemaphoreType.DMA(())],
        compiler_params=pltpu.CompilerParams(
            dimension_semantics=("arbitrary",), vmem_limit_bytes=VMEM_LIMIT),
        name="scatter",
    )(pos, zero_blocks, ux.reshape(t // SUBLANES, SUBLANES, ROW_W))


def _moe_kernel(se_ref, sf_ref, sk_ref, sp_ref, sn_ref, sb_ref, sr_ref, si_ref, so_ref,
                hs_hbm, wg_hbm, wu_hbm, wd_hbm, y_hbm,
                xbuf, obuf, wg_buf, wu_buf, wd_buf, wgu_s, wd_s, in_sem, out_sem, w_sem):
    s = pl.program_id(0)
    ns = pl.num_programs(0)
    cur = s % 2
    g_n = MOE_GROUP

    def in_copy(step, g, buf):
        rows = pl.ds(pl.multiple_of(sb_ref[step * g_n + g] * MOE_BLOCK, MOE_BLOCK), MOE_BLOCK)
        return pltpu.make_async_copy(
            hs_hbm.at[rows], xbuf.at[buf, pl.ds(g * MOE_BLOCK, MOE_BLOCK)], in_sem.at[buf])

    def out_copy(step, g, buf):
        rows = pl.ds(pl.multiple_of(sb_ref[step * g_n + g] * MOE_BLOCK, MOE_BLOCK), MOE_BLOCK)
        cols = pl.ds(pl.multiple_of(sr_ref[step * g_n + g] * D_MODEL, D_MODEL), D_MODEL)
        return pltpu.make_async_copy(
            obuf.at[buf, pl.ds(g * MOE_BLOCK, MOE_BLOCK)], y_hbm.at[rows, cols], out_sem.at[buf])

    def for_slots(step, flags_ref, fn):
        for g in range(g_n):
            @pl.when(flags_ref[step * g_n + g] == 1)
            def _():
                fn(g)

    @pl.when(s == 0)
    def _():
        xbuf[...] = jnp.zeros_like(xbuf)
        for_slots(0, si_ref, lambda g: in_copy(0, g, 0).start())

    @pl.when(s + 1 < ns)
    def _():
        for_slots(s + 1, si_ref, lambda g: in_copy(s + 1, g, 1 - cur).start())

    for_slots(s, si_ref, lambda g: in_copy(s, g, cur).wait())

    @pl.when(s >= 2)
    def _():
        for_slots(s - 2, so_ref, lambda g: out_copy(s - 2, g, cur).wait())

    def w_copies(expert, slot):
        return (pltpu.make_async_copy(wg_hbm.at[expert], wg_buf.at[slot], w_sem.at[slot]),
                pltpu.make_async_copy(wu_hbm.at[expert], wu_buf.at[slot], w_sem.at[slot]),
                pltpu.make_async_copy(wd_hbm.at[expert], wd_buf.at[slot], w_sem.at[slot]))

    @pl.when(s == 0)
    def _():
        for c in w_copies(se_ref[0], sp_ref[0]):
            c.start()

    @pl.when(sf_ref[s] == 1)
    def _():
        slot = sp_ref[s]
        for c in w_copies(se_ref[s], slot):
            c.wait()
        wgu_s[:, 0:D_EXPERT] = wg_buf[slot].astype(bf16)
        wgu_s[:, D_EXPERT:2 * D_EXPERT] = wu_buf[slot].astype(bf16)
        wd_s[...] = wd_buf[slot].astype(bf16)

        @pl.when(sn_ref[s] >= 0)
        def _():
            for c in w_copies(sn_ref[s], 1 - slot):
                c.start()

    @pl.when(sk_ref[s] == 1)
    def _():
        u = xbuf[cur, :, 0:D_MODEL].astype(bf16)
        meta = xbuf[cur, :, D_MODEL:ROW_W]
        gate = jnp.concatenate(
            [jnp.where(sr_ref[s * g_n + g] == 0, meta[g * MOE_BLOCK:(g + 1) * MOE_BLOCK, 2:3],
                       meta[g * MOE_BLOCK:(g + 1) * MOE_BLOCK, 3:4]) for g in range(g_n)], axis=0)
        gu = _dot(u, wgu_s[...])
        gt = gu[:, 0:D_EXPERT]
        hdn = (gt * jax.nn.sigmoid(gt) * gu[:, D_EXPERT:]).astype(bf16)
        obuf[cur] = _dot(hdn, wd_s[...]) * gate

    @pl.when(sk_ref[s] == 0)
    def _():
        obuf[cur] = jnp.zeros(obuf.shape[1:], f32)

    for_slots(s, so_ref, lambda g: out_copy(s, g, cur).start())

    @pl.when(s == ns - 1)
    def _():
        for_slots(s, so_ref, lambda g: out_copy(s, g, cur).wait())

        @pl.when(s >= 1)
        def _():
            for_slots(s - 1, so_ref, lambda g: out_copy(s - 1, g, 1 - cur).wait())


def _moe_call(plan, hs, w_gate, w_up, w_down):
    n_steps = plan[0].shape[0]
    n_slots = hs.shape[0]
    rows = MOE_GROUP * MOE_BLOCK

    grid_spec = pltpu.PrefetchScalarGridSpec(
        num_scalar_prefetch=9,
        grid=(n_steps,),
        in_specs=[
            pl.BlockSpec(memory_space=pl.ANY),
            pl.BlockSpec(memory_space=pl.ANY),
            pl.BlockSpec(memory_space=pl.ANY),
            pl.BlockSpec(memory_space=pl.ANY),
        ],
        out_specs=pl.BlockSpec(memory_space=pl.ANY),
        scratch_shapes=[
            pltpu.VMEM((2, rows, ROW_W), f32),
            pltpu.VMEM((2, rows, D_MODEL), f32),
            pltpu.VMEM((2, D_MODEL, D_EXPERT), f32),
            pltpu.VMEM((2, D_MODEL, D_EXPERT), f32),
            pltpu.VMEM((2, D_EXPERT, D_MODEL), f32),
            pltpu.VMEM((D_MODEL, 2 * D_EXPERT), bf16),
            pltpu.VMEM((D_EXPERT, D_MODEL), bf16),
            pltpu.SemaphoreType.DMA((2,)),
            pltpu.SemaphoreType.DMA((2,)),
            pltpu.SemaphoreType.DMA((2,)),
        ],
    )
    return pl.pallas_call(
        _moe_kernel,
        grid_spec=grid_spec,
        out_shape=jax.ShapeDtypeStruct((n_slots, 2 * D_MODEL), f32),
        compiler_params=pltpu.CompilerParams(
            dimension_semantics=("arbitrary",), vmem_limit_bytes=VMEM_LIMIT),
        name="moe",
    )(*plan, hs, w_gate, w_up, w_down)


def _final_kernel(posc_ref, posn_ref, h_ref, gain_ref, y_hbm, o_ref, ybuf, sem):
    i = pl.program_id(0)
    cur = i % 2

    def issue(pos_ref, buf):
        def start(ii, c):
            for k in range(SUBLANES):
                pltpu.make_async_copy(y_hbm.at[pl.ds(pos_ref[ii * SUBLANES + k], 1)],
                                      ybuf.at[buf, ii, pl.ds(k, 1)], sem.at[buf]).start()
            return c

        lax.fori_loop(0, FINAL_TILE // SUBLANES, start, 0)

    @pl.when(i == 0)
    def _():
        issue(posc_ref, 0)

    @pl.when(i + 1 < pl.num_programs(0))
    def _():
        issue(posn_ref, 1 - cur)

    pltpu.make_async_copy(ybuf.at[cur], ybuf.at[cur], sem.at[cur]).wait()
    h = h_ref[...] + ybuf[cur, :, :, 0:D_MODEL] + ybuf[cur, :, :, D_MODEL:2 * D_MODEL]
    o_ref[...] = _rms(h, gain_ref[...])


def _final_call(pos, h1, gain, y):
    t = h1.shape[0]
    n = t // FINAL_TILE
    rows = FINAL_TILE // SUBLANES
    out = pl.pallas_call(
        _final_kernel,
        grid=(n,),
        in_specs=[
            pl.BlockSpec((FINAL_TILE,), lambda i: (i,), memory_space=pltpu.SMEM),
            pl.BlockSpec((FINAL_TILE,), lambda i: (jnp.minimum(i + 1, n - 1),), memory_space=pltpu.SMEM),
            pl.BlockSpec((rows, SUBLANES, D_MODEL), lambda i: (i, 0, 0)),
            pl.BlockSpec((1, 1, D_MODEL), lambda i: (0, 0, 0)),
            pl.BlockSpec(memory_space=pl.ANY),
        ],
        out_specs=pl.BlockSpec((rows, SUBLANES, D_MODEL), lambda i: (i, 0, 0)),
        out_shape=jax.ShapeDtypeStruct((t // SUBLANES, SUBLANES, D_MODEL), f32),
        scratch_shapes=[pltpu.VMEM((2, rows, SUBLANES, 2 * D_MODEL), f32), pltpu.SemaphoreType.DMA((2,))],
        compiler_params=pltpu.CompilerParams(
            dimension_semantics=("arbitrary",), vmem_limit_bytes=VMEM_LIMIT),
        name="final",
    )(pos, pos, h1.reshape(t // SUBLANES, SUBLANES, D_MODEL), gain.reshape(1, 1, D_MODEL), y)
    return out.reshape(t, D_MODEL)


def _rope_tables(pos):
    inv = ROPE_BASE ** (-jnp.arange(0, MLA_ROPE, 2, dtype=f32) / MLA_ROPE)
    ang = pos.astype(f32)[:, None] * inv[None, :]
    cos, sin = jnp.cos(ang), jnp.sin(ang)
    z = jnp.zeros((pos.shape[0], LANE - MLA_ROPE), f32)
    return jnp.concatenate([cos, cos, z], axis=1), jnp.concatenate([-sin, sin, z], axis=1)


def _relayout_weights(w_in, w_qb, w_kvb):
    half = MLA_ROPE // 2
    perm = (np.arange(MLA_ROPE) + half) % MLA_ROPE
    pts = np.cumsum((GLA_QK, GLA_QK, GLA_VW, GLA_VW, GLA_GATE_RANK, MLA_Q_RANK, MLA_KV_RANK, MLA_ROPE))
    q_g, k_g, v_g, r_g, a_l, q_lat, kv_lat, k_rope = jnp.split(w_in, pts[:-1], axis=1)
    a_seg = jnp.pad(a_l, ((0, 0), (0, LANE - GLA_GATE_RANK)))
    w_in_r = jnp.concatenate(
        [q_g, k_g, v_g, r_g, q_lat, kv_lat, k_rope, k_rope[:, perm], a_seg], axis=1).astype(bf16)
    qcols, kcols, vcols = [], [], []
    for h in range(MLA_HEADS):
        c = h * (MLA_NOPE + MLA_ROPE)
        rope = w_qb[:, c + MLA_NOPE:c + MLA_NOPE + MLA_ROPE]
        qcols += [w_qb[:, c:c + MLA_NOPE], rope, rope[:, perm]]
        c2 = h * (MLA_NOPE + MLA_V)
        kcols.append(w_kvb[:, c2:c2 + MLA_NOPE])
        vcols.append(w_kvb[:, c2 + MLA_NOPE:c2 + MLA_NOPE + MLA_V])
    return w_in_r, jnp.concatenate(qcols, axis=1).astype(bf16), jnp.concatenate(kcols + vcols, axis=1).astype(bf16)


_BUCKET_GROUP = np.arange(N_BUCKETS) // N_PAIRS
_RUN_EXPERT = np.concatenate([_BUCKET_GROUP * EXPERTS_PER_GROUP + _PAIR_LO[np.arange(N_BUCKETS) % N_PAIRS],
                              _BUCKET_GROUP * EXPERTS_PER_GROUP + _PAIR_HI[np.arange(N_BUCKETS) % N_PAIRS]])
_RUN_IS_EXPERT = (_RUN_EXPERT[:, None] == np.arange(N_EXPERTS)[None, :]).astype(np.int32)
_RUN_BEFORE = ((_RUN_EXPERT[:, None] == _RUN_EXPERT[None, :])
               & (np.arange(2 * N_BUCKETS)[None, :] < np.arange(2 * N_BUCKETS)[:, None])).astype(np.int32)


def _route_plan(counts, bucket, rank, n_tok):
    nt = counts.shape[0]
    g_n = MOE_GROUP
    tot = counts.sum(axis=0)
    nblk = (tot + MOE_BLOCK - 1) // MOE_BLOCK
    bstart_blk = jnp.cumsum(nblk) - nblk
    n_blocks = jnp.sum(nblk)
    tile_base = bstart_blk[None, :] * MOE_BLOCK + jnp.cumsum(counts, axis=0) - counts
    hit = bucket.reshape(nt, -1, 1) == jnp.arange(N_BUCKETS, dtype=i32)
    pos = jnp.sum(jnp.where(hit, tile_base[:, None, :], 0), axis=-1).reshape(-1) + rank
    nb_max = (n_tok + N_BUCKETS * (MOE_BLOCK - 1)) // MOE_BLOCK

    n_run = jnp.concatenate([nblk, nblk])
    b0_run = jnp.concatenate([bstart_blk, bstart_blk])
    c_e = jnp.sum(n_run[:, None] * _RUN_IS_EXPERT, axis=0)
    g_e = (c_e + g_n - 1) // g_n
    gend = jnp.cumsum(g_e)
    gstart = gend - g_e
    n_compute = gend[-1]
    off_run = jnp.sum(_RUN_BEFORE * n_run[None, :], axis=1)
    f_run = jnp.sum(_RUN_IS_EXPERT * gstart[None, :], axis=1) * g_n + off_run

    n_steps = (2 * nb_max + N_EXPERTS * (g_n - 1) + g_n - 1) // g_n + 1
    f = jnp.arange(n_steps * g_n, dtype=i32)
    in_run = (f[:, None] >= f_run[None, :]) & (f[:, None] < (f_run + n_run)[None, :])
    valid_c = jnp.any(in_run, axis=1)
    block_c = jnp.sum(jnp.where(in_run, b0_run[None, :] + f[:, None] - f_run[None, :], 0), axis=1)
    role_c = jnp.sum(jnp.where(in_run[:, N_BUCKETS:], 1, 0), axis=1)
    u_idx = f - n_compute * g_n
    valid_f = (u_idx >= 0) & (u_idx < 2 * (nb_max - n_blocks))
    slot_block = jnp.where(valid_c, block_c, jnp.where(valid_f, n_blocks + u_idx // 2, 0))
    slot_role = jnp.where(valid_c, role_c, jnp.where(valid_f, u_idx % 2, 0))

    step = jnp.arange(n_steps, dtype=i32)
    e_of_step = jnp.minimum(jnp.sum(gend[None, :] <= step[:, None], axis=1), N_EXPERTS - 1)
    is_compute = step < n_compute
    last_e = jnp.max(jnp.where(is_compute, e_of_step, 0))
    step_expert = jnp.where(is_compute, e_of_step, last_e)
    step_first = jnp.concatenate([jnp.ones((1,), bool), step_expert[1:] != step_expert[:-1]])
    ordinal = jnp.cumsum(step_first.astype(i32)) - 1
    ords = jnp.arange(N_EXPERTS + 1, dtype=i32)
    expert_of_ord = jnp.sum(jnp.where(step_first[:, None] & (ordinal[:, None] == ords[None, :]),
                                      step_expert[:, None], 0), axis=0)
    has_next = ordinal + 1 <= ordinal[-1]
    next_expert = jnp.sum(jnp.where(ords[None, :] == ordinal[:, None] + 1, expert_of_ord[None, :], 0), axis=1)
    step_next = jnp.where(step_first & has_next, next_expert, -1)
    plan = tuple(a.astype(i32) for a in
                 (step_expert, step_first, is_compute, ordinal % 2, step_next,
                  slot_block, slot_role, valid_c, valid_c | valid_f))
    last_blk = jnp.where(nblk > 0, bstart_blk + nblk - 1, -1)
    spare = n_blocks + jnp.arange(nb_max - n_tok // MOE_BLOCK, dtype=i32)
    zero_blocks = jnp.concatenate([last_blk, jnp.where(spare < nb_max, spare, -1)]).astype(i32)
    return pos.astype(i32), plan, zero_blocks, nb_max


def kernel(x, meta_tokens, mix_norm, w_in, gla_w_a2, gla_b_a, gla_out_norm, mla_q_norm, mla_w_qb, mla_kv_norm,
           mla_w_kvb, w_out, ffn_norm, router_group_w, router_group_b, router_expert_w, router_expert_b,
           expert_w_gate, expert_w_up, expert_w_down, final_norm):
    batch, seq, d = x.shape
    assert d == D_MODEL and seq % max(PREP_TILE, GLA_TILE, ATT_TILE) == 0
    assert (batch * seq) % max(OUT_TILE, SCATTER_TILE, FINAL_TILE) == 0 and batch % GLA_BATCH == 0
    n_tok = batch * seq
    x2d = x.reshape(n_tok, d)

    w_in_r, w_qb_r, w_kvb_r = _relayout_weights(w_in[0], mla_w_qb[0], mla_w_kvb[0])
    mixg = mix_norm[0].reshape(1, d)
    qn = mla_q_norm[0].reshape(1, MLA_Q_RANK)
    kvn = mla_kv_norm[0].reshape(1, MLA_KV_RANK)
    ct_m, st_m = _rope_tables(jnp.arange(META_TILE))
    ct_x, st_x = _rope_tables(N_META + jnp.arange(seq))

    x_meta = jnp.pad(meta_tokens.astype(f32), ((0, META_TILE - N_META), (0, 0)))
    _, kg_m, vg_m, _, a_m, _, km_m, vmt_m = _prep_call(
        x_meta, META_TILE, META_TILE, mixg, w_in_r, qn, w_qb_r, kvn, w_kvb_r, ct_m, st_m)
    qg, kg, vg, rg, ag, qm, km, vmt = _prep_call(
        x2d, seq, PREP_TILE, mixg, w_in_r, qn, w_qb_r, kvn, w_kvb_r, ct_x, st_x)

    def chunk0(a):
        return jnp.pad(a[:N_META], ((CHUNK - N_META, 0), (0, 0)))

    wa2_p = jnp.pad(gla_w_a2[0], ((0, LANE - GLA_GATE_RANK), (0, 0))).astype(bf16)
    y_gla = _gla_call(qg, kg, vg, rg, ag, chunk0(kg_m), chunk0(vg_m), chunk0(a_m),
                      wa2_p, gla_b_a[0].reshape(1, GLA_QK), gla_out_norm[0].reshape(1, GLA_VW), batch, seq)
    y_mla = _mla_call(qm, km, vmt, km_m[:N_META], vmt_m[0, :, :N_META], batch, seq)

    wo = w_out[0].astype(bf16)
    rw = jnp.concatenate([router_group_w[0], router_expert_w[0],
                          jnp.zeros((d, LANE - N_GROUPS - N_EXPERTS), f32)], axis=1)
    rb = jnp.concatenate([router_group_b[0], router_expert_b[0],
                          jnp.zeros((LANE - N_GROUPS - N_EXPERTS,), f32)]).reshape(1, LANE)
    h1, ux, cnt, routes = _outproj_call(x2d, y_gla, y_mla, wo[:GLA_VW], wo[GLA_VW:], ffn_norm[0].reshape(1, d),
                                      rw.T.astype(bf16), rb.reshape(LANE, 1))

    counts = cnt.reshape(-1, BUCKET_LANES)[:, :N_BUCKETS].astype(i32)
    tok_bucket = routes[:, 0, :].reshape(-1).astype(i32)
    tok_rank = routes[:, 1, :].reshape(-1).astype(i32)
    pos, plan, zero_blocks, nb_max = _route_plan(counts, tok_bucket, tok_rank, n_tok)
    n_slots = nb_max * MOE_BLOCK
    hs = _scatter_call(pos, zero_blocks, ux, n_slots)
    y = _moe_call(plan, hs, expert_w_gate[0], expert_w_up[0], expert_w_down[0])
    out = _final_call(pos, h1, final_norm.reshape(1, d), y)
    return out.reshape(batch, seq, d)
```

```python
import functools

import numpy as np
import jax
import jax.numpy as jnp
from jax import lax
from jax.experimental import pallas as pl
from jax.experimental.pallas import tpu as pltpu

f32 = jnp.float32
bf16 = jnp.bfloat16
i32 = jnp.int32

D_MODEL = 1024
CHUNK = 64
N_META = 16
EPS = 1e-6
GLA_HEADS = 4
GLA_DK = 64
GLA_DV = 128
GLA_GATE_RANK = 16
GLA_TAU = 16.0
GLA_QK = GLA_HEADS * GLA_DK
GLA_VW = GLA_HEADS * GLA_DV
MLA_HEADS = 4
MLA_Q_RANK = 256
MLA_KV_RANK = 128
MLA_NOPE = 128
MLA_ROPE = 64
MLA_V = 128
MLA_OUT = MLA_HEADS * MLA_V
MLA_QK_PAD = 256
MLA_VA = MLA_V + 16
LOG2_E = 1.4426950408889634
ROPE_BASE = 10000.0
N_GROUPS = 8
EXPERTS_PER_GROUP = 8
N_EXPERTS = N_GROUPS * EXPERTS_PER_GROUP
D_EXPERT = 512
N_PAIRS = EXPERTS_PER_GROUP * (EXPERTS_PER_GROUP - 1) // 2
N_BUCKETS = N_GROUPS * N_PAIRS
BUCKET_LANES = 256
LANE = 128
SUBLANES = 8
META_W = LANE
ROW_W = D_MODEL + META_W

PREP_TILE = 512
GLA_TILE = 512
GLA_BATCH = 4
ATT_TILE = 512
ATT_HEADS = 4
META_TILE = 128
OUT_TILE = 1024
ROUTE_ROWS = 256
ROUTE_GROUP = 512
SCATTER_TILE = 2048
FINAL_TILE = 512
ISSUE_UNROLL = 64
MOE_BLOCK = 32
MOE_GROUP = 16
VMEM_LIMIT = 56 * 1024 * 1024

C_Q, C_K, C_V, C_R = 0, 256, 512, 1024
C_QLAT, C_KVLAT, C_KROPE, C_A, C_END = 1536, 1792, 1920, 2048, 2176

_TILE_POS = np.arange(GLA_TILE)
_CHUNK_PREFIX = ((_TILE_POS[:, None] // CHUNK == _TILE_POS[None, :] // CHUNK)
                 & (_TILE_POS[None, :] <= _TILE_POS[:, None])).astype(np.float32)
_PAIR_LO = np.array([lo for lo in range(8) for hi in range(lo + 1, 8)], np.int32)
_PAIR_HI = np.array([hi for lo in range(8) for hi in range(lo + 1, 8)], np.int32)


def _dot(a, b):
    return jnp.dot(a, b, preferred_element_type=f32)


def _dot_nt(a, b):
    return lax.dot_general(a, b, (((1,), (1,)), ((), ())), preferred_element_type=f32)


def _dot_tn(a, b):
    return lax.dot_general(a, b, (((0,), (0,)), ((), ())), preferred_element_type=f32)


def _rms(x, gain):
    return x * lax.rsqrt(jnp.mean(x * x, axis=-1, keepdims=True) + EPS) * gain


def _split3(x):
    hi = x.astype(bf16)
    r1 = x - hi.astype(f32)
    mid = r1.astype(bf16)
    lo = (r1 - mid.astype(f32)).astype(bf16)
    return hi, mid, lo


def _prep_kernel(x_ref, g_ref, win_ref, qn_ref, wqb_ref, kvn_ref, wkvb_ref, ct_ref, st_ref,
                 qg_ref, kg_ref, vg_ref, rg_ref, a_ref, qm_ref, km_ref, vmt_ref):
    u = _rms(x_ref[...], g_ref[...]).astype(bf16)

    def proj(lo, hi):
        return _dot(u, win_ref[:, lo:hi])

    qg_ref[...] = proj(C_Q, C_K).astype(bf16)
    kg_ref[...] = proj(C_K, C_V).astype(bf16)
    vg_ref[...] = proj(C_V, C_R).astype(bf16)
    rg_ref[...] = proj(C_R, C_QLAT).astype(bf16)
    z = proj(C_QLAT, C_END)
    a_ref[...] = z[:, C_A - C_QLAT:].astype(bf16)
    ctab = ct_ref[...]
    stab = st_ref[...]

    def rope(seg):
        return seg * ctab + pltpu.roll(seg, 64, axis=1) * stab

    k_rope = rope(z[:, C_KROPE - C_QLAT:C_A - C_QLAT]).astype(bf16)
    qn = _rms(z[:, 0:MLA_Q_RANK], qn_ref[...]).astype(bf16)
    kvn = _rms(z[:, MLA_Q_RANK:MLA_Q_RANK + MLA_KV_RANK], kvn_ref[...]).astype(bf16)
    scale = (MLA_NOPE + MLA_ROPE) ** -0.5 * LOG2_E
    qf = _dot(qn, wqb_ref[...])
    kvf = _dot(kvn, wkvb_ref[...])
    for h in range(MLA_HEADS):
        c = h * MLA_QK_PAD
        qm_ref[:, c:c + LANE] = (qf[:, c:c + LANE] * scale).astype(bf16)
        qm_ref[:, c + LANE:c + 2 * LANE] = (rope(qf[:, c + LANE:c + 2 * LANE]) * scale).astype(bf16)
        km_ref[:, c:c + LANE] = kvf[:, h * LANE:(h + 1) * LANE].astype(bf16)
        km_ref[:, c + LANE:c + 2 * LANE] = k_rope
    vt = kvf[:, MLA_HEADS * MLA_NOPE:].T
    for h in range(MLA_HEADS):
        vmt_ref[h * MLA_VA:h * MLA_VA + MLA_V, :] = vt[h * MLA_V:(h + 1) * MLA_V].astype(bf16)
        vmt_ref[h * MLA_VA + MLA_V:(h + 1) * MLA_VA, :] = jnp.ones((MLA_VA - MLA_V, vt.shape[1]), bf16)


def _prep_call(x2d, rows_per_seq, tile, gain, w_in_r, q_norm, w_qb_r, kv_norm, w_kvb_r, ctab, stab):
    t = x2d.shape[0]
    nj = rows_per_seq // tile
    grid = (t // rows_per_seq, nj)

    def row(b, j):
        return (b * nj + j, 0)

    def const(b, j):
        return (0, 0)

    def tab(b, j):
        return (j, 0)

    widths = (GLA_QK, GLA_QK, GLA_VW, GLA_VW, LANE, MLA_HEADS * MLA_QK_PAD, MLA_HEADS * MLA_QK_PAD)
    return pl.pallas_call(
        _prep_kernel,
        grid=grid,
        in_specs=[
            pl.BlockSpec((tile, D_MODEL), row),
            pl.BlockSpec((1, D_MODEL), const),
            pl.BlockSpec((D_MODEL, C_END), const),
            pl.BlockSpec((1, MLA_Q_RANK), const),
            pl.BlockSpec((MLA_Q_RANK, MLA_HEADS * MLA_QK_PAD), const),
            pl.BlockSpec((1, MLA_KV_RANK), const),
            pl.BlockSpec((MLA_KV_RANK, 2 * MLA_OUT), const),
            pl.BlockSpec((tile, LANE), tab),
            pl.BlockSpec((tile, LANE), tab),
        ],
        out_specs=[pl.BlockSpec((tile, w), row) for w in widths]
        + [pl.BlockSpec((None, MLA_HEADS * MLA_VA, tile), lambda b, j: (b * nj + j, 0, 0))],
        out_shape=[jax.ShapeDtypeStruct((t, w), bf16) for w in widths]
        + [jax.ShapeDtypeStruct((t // tile, MLA_HEADS * MLA_VA, tile), bf16)],
        compiler_params=pltpu.CompilerParams(
            dimension_semantics=("parallel", "parallel"), vmem_limit_bytes=VMEM_LIMIT),
        name="prep",
    )(x2d, gain, w_in_r, q_norm, w_qb_r, kv_norm, w_kvb_r, ctab, stab)


def _gla_log_decay(a, wa2_ref, ba_ref):
    s = _dot(a, wa2_ref[...]) + ba_ref[...]
    return (jnp.minimum(s, 0.0) - jnp.log(1.0 + jnp.exp(-jnp.abs(s)))) * (1.0 / GLA_TAU)


def _gla_front(q, k, v, la, tri, want_out):
    nc = la.shape[0] // CHUNK
    hi, mid, lo = _split3(la)
    b = _dot(tri, hi) + _dot(tri, mid) + _dot(tri, lo)
    b_last = [b[(c + 1) * CHUNK - 1:(c + 1) * CHUNK, :] for c in range(nc)]
    b_last_full = jnp.concatenate([jnp.broadcast_to(bl, (CHUNK, GLA_QK)) for bl in b_last], axis=0)
    kf = k.astype(f32)
    front = dict(v=v, b_last=b_last, kd=(kf * jnp.exp(b_last_full - b)).astype(bf16))
    if want_out:
        front.update(qe=(q.astype(f32) * (GLA_DK ** -0.5) * jnp.exp(b)).astype(bf16),
                     ke=kf * jnp.exp(-b), vf=v.astype(f32))
    return front


def _gla_chunks(front, st_ref, want_out):
    v, kd, b_last = front["v"], front["kd"], front["b_last"]
    rr = lax.broadcasted_iota(i32, (GLA_VW, GLA_QK), 0) // GLA_DV
    cc = lax.broadcasted_iota(i32, (GLA_VW, GLA_QK), 1) // GLA_DK
    if want_out:
        qe, ke, vf = front["qe"], front["ke"], front["vf"]
        lane_h = lax.broadcasted_iota(i32, (CHUNK, GLA_QK), 1) // GLA_DK
        vlane_h = lax.broadcasted_iota(i32, (CHUNK, GLA_VW), 1) // GLA_DV
        a_row = lax.broadcasted_iota(i32, (CHUNK, GLA_QK), 0)
        a_col = lax.broadcasted_iota(i32, (CHUNK, GLA_QK), 1) % CHUNK
    outs = []
    st = st_ref[...]
    for c in range(len(b_last)):
        rows = slice(c * CHUNK, (c + 1) * CHUNK)
        upd = jnp.where(rr == cc, _dot_tn(v[rows], kd[rows]), 0.0)
        if want_out:
            kbd = jnp.concatenate(
                [jnp.where(lane_h == h, ke[rows], 0.0) for h in range(GLA_HEADS)], axis=0).astype(bf16)
            att = jnp.where(a_col <= a_row, _dot_nt(qe[rows], kbd), 0.0).astype(bf16)
            vbd = jnp.concatenate(
                [jnp.where(vlane_h == h, vf[rows], 0.0) for h in range(GLA_HEADS)], axis=0).astype(bf16)
            outs.append(_dot(att, vbd) + _dot_nt(qe[rows], st.astype(bf16)))
        st = st * jnp.exp(b_last[c]) + upd
    st_ref[...] = st
    return jnp.concatenate(outs, axis=0) if want_out else None


def _gla_kernel(q_ref, k_ref, v_ref, r_ref, a_ref, km_ref, vm_ref, am_ref, wa2_ref, ba_ref, gain_ref, tri_ref,
                y_ref, st_ref):
    j = pl.program_id(1)

    @pl.when(j == 0)
    def _():
        st_ref[...] = jnp.zeros_like(st_ref)
        la = _gla_log_decay(am_ref[...], wa2_ref, ba_ref)
        row = lax.broadcasted_iota(i32, la.shape, 0)
        la = jnp.where(row >= CHUNK - N_META, la, 0.0)
        front = _gla_front(None, km_ref[...], vm_ref[...], la, tri_ref[0:CHUNK, 0:CHUNK], False)
        _gla_chunks(front, st_ref.at[0], False)
        for bb in range(1, GLA_BATCH):
            st_ref[bb] = st_ref[0]

    fronts = [_gla_front(q_ref[bb], k_ref[bb], v_ref[bb], _gla_log_decay(a_ref[bb], wa2_ref, ba_ref),
                         tri_ref[...], True) for bb in range(GLA_BATCH)]
    for bb in range(GLA_BATCH):
        o = _gla_chunks(fronts[bb], st_ref.at[bb], True)
        r = r_ref[bb].astype(f32)
        outs = []
        for h in range(GLA_HEADS):
            oh = o[:, h * GLA_DV:(h + 1) * GLA_DV]
            outs.append(oh * lax.rsqrt(jnp.mean(oh * oh, axis=-1, keepdims=True) + EPS))
        on = jnp.concatenate(outs, axis=1) * gain_ref[...]
        y_ref[bb] = (on * (r * jax.nn.sigmoid(r))).astype(bf16)


def _gla_call(qg, kg, vg, rg, ag, km, vm, am, wa2_p, b_a, gain, batch, seq):
    nj = seq // GLA_TILE

    def row(b, j):
        return (b, j, 0)

    def const(b, j):
        return (0, 0)

    def seqs(a):
        return a.reshape(batch, seq, a.shape[-1])

    out = pl.pallas_call(
        _gla_kernel,
        grid=(batch // GLA_BATCH, nj),
        in_specs=[
            pl.BlockSpec((GLA_BATCH, GLA_TILE, GLA_QK), row),
            pl.BlockSpec((GLA_BATCH, GLA_TILE, GLA_QK), row),
            pl.BlockSpec((GLA_BATCH, GLA_TILE, GLA_VW), row),
            pl.BlockSpec((GLA_BATCH, GLA_TILE, GLA_VW), row),
            pl.BlockSpec((GLA_BATCH, GLA_TILE, LANE), row),
            pl.BlockSpec((CHUNK, GLA_QK), const),
            pl.BlockSpec((CHUNK, GLA_VW), const),
            pl.BlockSpec((CHUNK, LANE), const),
            pl.BlockSpec((LANE, GLA_QK), const),
            pl.BlockSpec((1, GLA_QK), const),
            pl.BlockSpec((1, GLA_VW), const),
            pl.BlockSpec((GLA_TILE, GLA_TILE), const),
        ],
        out_specs=pl.BlockSpec((GLA_BATCH, GLA_TILE, GLA_VW), row),
        out_shape=jax.ShapeDtypeStruct((batch, seq, GLA_VW), bf16),
        scratch_shapes=[pltpu.VMEM((GLA_BATCH, GLA_VW, GLA_QK), f32)],
        compiler_params=pltpu.CompilerParams(
            dimension_semantics=("parallel", "arbitrary"), vmem_limit_bytes=VMEM_LIMIT),
        name="gla",
    )(seqs(qg), seqs(kg), seqs(vg), seqs(rg), seqs(ag), km, vm, am, wa2_p, b_a, gain,
      jnp.asarray(_CHUNK_PREFIX, bf16))
    return out.reshape(batch * seq, GLA_VW)


def _mla_kernel(q_ref, k_ref, vt_ref, km_ref, vmt_ref, o_ref, sa_ref, sb_ref):
    i = pl.program_id(2)
    tq = ATT_TILE
    w = MLA_QK_PAD
    va = MLA_VA
    heads = range(ATT_HEADS)

    def scores(h, blk):
        rows = pl.ds(pl.multiple_of(blk * tq, tq), tq)
        return _dot_nt(k_ref[rows, h * w:(h + 1) * w], q_ref[:, h * w:(h + 1) * w])

    def soft(s, vtb, carry, mask=None):
        m, acc = carry
        if mask is not None:
            s = jnp.where(mask, s, -1e30)
        m_new = jnp.maximum(m, jnp.max(s, axis=0, keepdims=True))
        p = jnp.exp2(s - m_new).astype(bf16)
        return m_new, jnp.exp2(m - m_new) * acc + _dot(vtb, p)

    def vt(h, blk):
        return vt_ref[blk, h * va:(h + 1) * va, :]

    def finish(carries):
        ss = [_dot_nt(km_ref[:, h * w:(h + 1) * w], q_ref[:, h * w:(h + 1) * w]) for h in heads]
        accs = [soft(ss[h], vmt_ref[h * va:(h + 1) * va, :], carries[h])[1] for h in heads]
        for h in heads:
            acc = accs[h]
            o_ref[:, h * MLA_V:(h + 1) * MLA_V] = (acc[:MLA_V] * (1.0 / acc[MLA_V:MLA_V + 1])).T.astype(bf16)

    kc = lax.broadcasted_iota(i32, (tq, tq), 0) // CHUNK
    qc = lax.broadcasted_iota(i32, (tq, tq), 1) // CHUNK
    mask = kc <= qc

    for h in heads:
        sa_ref[h] = scores(h, 0)

    def pair(p, carries):
        b0 = 2 * p
        for h in heads:
            sb_ref[h] = scores(h, b0 + 1)
        carries = [soft(sa_ref[h], vt(h, b0), carries[h]) for h in heads]
        for h in heads:
            sa_ref[h] = scores(h, b0 + 2)
        return tuple(soft(sb_ref[h], vt(h, b0 + 1), carries[h]) for h in heads)

    init = tuple((jnp.full((1, tq), -1e30, f32), jnp.zeros((va, tq), f32)) for _ in heads)
    carries = lax.fori_loop(0, i // 2, pair, init)

    @pl.when(i % 2 == 1)
    def _():
        for h in heads:
            sb_ref[h] = scores(h, i)
        c1 = [soft(sa_ref[h], vt(h, i - 1), carries[h]) for h in heads]
        finish([soft(sb_ref[h], vt(h, i), c1[h], mask) for h in heads])

    @pl.when(i % 2 == 0)
    def _():
        finish([soft(sa_ref[h], vt(h, i), carries[h], mask) for h in heads])


def _mla_call(qm, km, vmt, km_meta, vmt_meta, batch, seq):
    nq = seq // ATT_TILE
    nh = ATT_HEADS
    qm3 = qm.reshape(batch, seq, MLA_HEADS * MLA_QK_PAD)
    km3 = km.reshape(batch, seq, MLA_HEADS * MLA_QK_PAD)
    vt4 = vmt.reshape(batch, nq, MLA_HEADS * MLA_VA, ATT_TILE)
    out = pl.pallas_call(
        _mla_kernel,
        grid=(batch, MLA_HEADS // nh, nq),
        in_specs=[
            pl.BlockSpec((None, ATT_TILE, nh * MLA_QK_PAD), lambda b, h, i: (b, i, h)),
            pl.BlockSpec((None, seq, nh * MLA_QK_PAD), lambda b, h, i: (b, 0, h)),
            pl.BlockSpec((None, nq, nh * MLA_VA, ATT_TILE), lambda b, h, i: (b, 0, h, 0)),
            pl.BlockSpec((N_META, nh * MLA_QK_PAD), lambda b, h, i: (0, h)),
            pl.BlockSpec((nh * MLA_VA, N_META), lambda b, h, i: (h, 0)),
        ],
        out_specs=pl.BlockSpec((None, ATT_TILE, nh * MLA_V), lambda b, h, i: (b, i, h)),
        out_shape=jax.ShapeDtypeStruct((batch, seq, MLA_OUT), bf16),
        scratch_shapes=[pltpu.VMEM((nh, ATT_TILE, ATT_TILE), f32), pltpu.VMEM((nh, ATT_TILE, ATT_TILE), f32)],
        compiler_params=pltpu.CompilerParams(
            dimension_semantics=("parallel", "parallel", "arbitrary"), vmem_limit_bytes=VMEM_LIMIT),
        name="mla",
    )(qm3, km3, vt4, km_meta, vmt_meta)
    return out.reshape(batch * seq, MLA_OUT)


def _route_cols(lt):
    r = lt.shape[1]
    neg = -1e30
    gl = lt[0:N_GROUPS, :]
    gsub = lax.broadcasted_iota(i32, (N_GROUPS, r), 0)
    gmax = jnp.max(gl, axis=0, keepdims=True)
    g_p = 1.0 / jnp.sum(jnp.exp(gl - gmax), axis=0, keepdims=True)
    g_idx = jnp.min(jnp.where(gl == gmax, gsub, N_GROUPS), axis=0, keepdims=True)
    el_all = lt[N_GROUPS:N_GROUPS + N_EXPERTS, :]
    esub = lax.broadcasted_iota(i32, (N_EXPERTS, r), 0)
    base = g_idx * EXPERTS_PER_GROUP
    e_mask = (esub >= base) & (esub < base + EXPERTS_PER_GROUP)
    el = jnp.where(e_mask, el_all, neg)
    m1 = jnp.max(el, axis=0, keepdims=True)
    i1 = jnp.min(jnp.where(e_mask & (el == m1), esub, N_EXPERTS), axis=0, keepdims=True)
    el2 = jnp.where(esub == i1, neg, el)
    m2 = jnp.max(el2, axis=0, keepdims=True)
    i2 = jnp.min(jnp.where(e_mask & (esub != i1) & (el2 == m2), esub, N_EXPERTS), axis=0, keepdims=True)
    rr = jnp.exp(m2 - m1)
    ga = g_p / (1.0 + rr)
    gb = g_p * rr / (1.0 + rr)
    la_ = i1 - base
    lb_ = i2 - base
    lo = jnp.minimum(la_, lb_)
    hi = jnp.maximum(la_, lb_)
    g_lo = jnp.where(la_ < lb_, ga, gb)
    g_hi = jnp.where(la_ < lb_, gb, ga)
    pidx = ((lo * (2 * EXPERTS_PER_GROUP - 1 - lo)) >> 1) + (hi - lo - 1)
    bucket = g_idx * N_PAIRS + pidx
    bsub = lax.broadcasted_iota(i32, (BUCKET_LANES, r), 0)
    oht = jnp.where(bsub == bucket, 1.0, 0.0)
    ohb = oht.astype(bf16)
    ri = lax.broadcasted_iota(i32, (ROUTE_ROWS, ROUTE_ROWS), 0)
    ci = lax.broadcasted_iota(i32, (ROUTE_ROWS, ROUTE_ROWS), 1)
    before = jnp.where(ri < ci, 1.0, 0.0).astype(bf16)
    ones = jnp.ones((SUBLANES, ROUTE_ROWS), bf16)
    subs = [slice(i * ROUTE_ROWS, (i + 1) * ROUTE_ROWS) for i in range(r // ROUTE_ROWS)]
    cum = jnp.concatenate([_dot(ohb[:, sl], before) for sl in subs], axis=1)
    rank = jnp.sum(oht * cum, axis=0, keepdims=True)
    counts = [_dot_nt(ones, ohb[:, sl])[0:1, :] for sl in subs]
    msub = lax.broadcasted_iota(i32, (LANE, r), 0)
    meta_t = jnp.where(msub == 0, bucket.astype(f32),
                       jnp.where(msub == 1, rank,
                                 jnp.where(msub == 2, g_lo, jnp.where(msub == 3, g_hi, 0.0))))
    return meta_t.T, counts, meta_t[0:SUBLANES, :]


def _outproj_kernel(x_ref, yg_ref, ym_ref, wog_ref, wom_ref, gain_ref, wrt_ref, rb_ref,
                    h_ref, ux_ref, cnt_ref, rt_ref):
    per_group = ROUTE_GROUP // ROUTE_ROWS
    for grp in range(OUT_TILE // ROUTE_GROUP):
        rows = slice(grp * ROUTE_GROUP, (grp + 1) * ROUTE_GROUP)
        h1 = x_ref[rows, :] + _dot(yg_ref[rows, :], wog_ref[...]) + _dot(ym_ref[rows, :], wom_ref[...])
        h_ref[rows, :] = h1
        u2 = _rms(h1, gain_ref[...])
        ux_ref[rows, 0:D_MODEL] = u2
        lt = _dot_nt(wrt_ref[...], u2.astype(bf16)) + rb_ref[...]
        meta, counts, routes = _route_cols(lt)
        ux_ref[rows, D_MODEL:ROW_W] = meta
        for i in range(per_group):
            cnt_ref[grp * per_group + i] = counts[i]
            rt_ref[grp * per_group + i] = routes[:, i * ROUTE_ROWS:(i + 1) * ROUTE_ROWS]


def _outproj_call(x2d, yg, ym, wo_g, wo_m, gain, w_r, rbias):
    t = x2d.shape[0]
    nt = t // OUT_TILE

    def row(i):
        return (i, 0)

    def const(i):
        return (0, 0)

    return pl.pallas_call(
        _outproj_kernel,
        grid=(nt,),
        in_specs=[
            pl.BlockSpec((OUT_TILE, D_MODEL), row),
            pl.BlockSpec((OUT_TILE, GLA_VW), row),
            pl.BlockSpec((OUT_TILE, MLA_OUT), row),
            pl.BlockSpec((GLA_VW, D_MODEL), const),
            pl.BlockSpec((MLA_OUT, D_MODEL), const),
            pl.BlockSpec((1, D_MODEL), const),
            pl.BlockSpec((LANE, D_MODEL), const),
            pl.BlockSpec((LANE, 1), const),
        ],
        out_specs=[
            pl.BlockSpec((OUT_TILE, D_MODEL), row),
            pl.BlockSpec((OUT_TILE, ROW_W), row),
            pl.BlockSpec((OUT_TILE // ROUTE_ROWS, 1, BUCKET_LANES), lambda i: (i, 0, 0)),
            pl.BlockSpec((OUT_TILE // ROUTE_ROWS, SUBLANES, ROUTE_ROWS), lambda i: (i, 0, 0)),
        ],
        out_shape=[
            jax.ShapeDtypeStruct((t, D_MODEL), f32),
            jax.ShapeDtypeStruct((t, ROW_W), f32),
            jax.ShapeDtypeStruct((nt * (OUT_TILE // ROUTE_ROWS), 1, BUCKET_LANES), f32),
            jax.ShapeDtypeStruct((nt * (OUT_TILE // ROUTE_ROWS), SUBLANES, ROUTE_ROWS), f32),
        ],
        compiler_params=pltpu.CompilerParams(
            dimension_semantics=("parallel",), vmem_limit_bytes=VMEM_LIMIT),
        name="outproj",
    )(x2d, yg, ym, wo_g, wo_m, gain, w_r, rbias)


def _scatter_kernel(pos_ref, zb_ref, ux_ref, hs_ref, zbuf, sem, zsem):
    @pl.when(pl.program_id(0) == 0)
    def _():
        zbuf[...] = jnp.zeros_like(zbuf)

        def zero_copy(j):
            rows = pl.ds(pl.multiple_of(zb_ref[j] * MOE_BLOCK, MOE_BLOCK), MOE_BLOCK)
            return pltpu.make_async_copy(zbuf, hs_ref.at[rows], zsem)

        def zstart(j, c):
            @pl.when(zb_ref[j] >= 0)
            def _():
                zero_copy(j).start()
            return c

        def zwait(j, c):
            @pl.when(zb_ref[j] >= 0)
            def _():
                zero_copy(j).wait()
            return c

        lax.fori_loop(0, zb_ref.shape[0], zstart, 0)
        lax.fori_loop(0, zb_ref.shape[0], zwait, 0)

    def start(ii, c):
        for k in range(SUBLANES):
            pltpu.make_async_copy(ux_ref.at[ii, pl.ds(k, 1)],
                                  hs_ref.at[pl.ds(pos_ref[ii * SUBLANES + k], 1)], sem).start()
        return c

    lax.fori_loop(0, SCATTER_TILE // SUBLANES, start, 0)
    pltpu.make_async_copy(hs_ref.at[pl.ds(0, SCATTER_TILE)], hs_ref.at[pl.ds(0, SCATTER_TILE)], sem).wait()


def _scatter_call(pos, zero_blocks, ux, n_slots):
    t = ux.shape[0]
    nz = zero_blocks.shape[0]
    return pl.pallas_call(
        _scatter_kernel,
        grid=(t // SCATTER_TILE,),
        in_specs=[
            pl.BlockSpec((SCATTER_TILE,), lambda i: (i,), memory_space=pltpu.SMEM),
            pl.BlockSpec((nz,), lambda i: (0,), memory_space=pltpu.SMEM),
            pl.BlockSpec((SCATTER_TILE // SUBLANES, SUBLANES, ROW_W), lambda i: (i, 0, 0)),
        ],
        out_specs=pl.BlockSpec(memory_space=pl.ANY),
        out_shape=jax.ShapeDtypeStruct((n_slots, ROW_W), f32),
        scratch_shapes=[pltpu.VMEM((MOE_BLOCK, ROW_W), f32), pltpu.SemaphoreType.DMA(()),
                        pltpu.SemaphoreType.DMA(())],
        compiler_params=pltpu.CompilerParams(
            dimension_semantics=("arbitrary",), vmem_limit_bytes=VMEM_LIMIT),
        name="scatter",
    )(pos, zero_blocks, ux.reshape(t // SUBLANES, SUBLANES, ROW_W))


def _moe_kernel(se_ref, sf_ref, sk_ref, sp_ref, sn_ref, sb_ref, sr_ref, si_ref, so_ref,
                hs_hbm, wg_hbm, wu_hbm, wd_hbm, y_hbm,
                xbuf, obuf, wg_buf, wu_buf, wd_buf, wgu_s, wd_s, in_sem, out_sem, w_sem):
    s = pl.program_id(0)
    ns = pl.num_programs(0)
    cur = s % 2
    g_n = MOE_GROUP

    def in_copy(step, g, buf):
        rows = pl.ds(pl.multiple_of(sb_ref[step * g_n + g] * MOE_BLOCK, MOE_BLOCK), MOE_BLOCK)
        return pltpu.make_async_copy(
            hs_hbm.at[rows], xbuf.at[buf, pl.ds(g * MOE_BLOCK, MOE_BLOCK)], in_sem.at[buf])

    def out_copy(step, g, buf):
        rows = pl.ds(pl.multiple_of(sb_ref[step * g_n + g] * MOE_BLOCK, MOE_BLOCK), MOE_BLOCK)
        cols = pl.ds(pl.multiple_of(sr_ref[step * g_n + g] * D_MODEL, D_MODEL), D_MODEL)
        return pltpu.make_async_copy(
            obuf.at[buf, pl.ds(g * MOE_BLOCK, MOE_BLOCK)], y_hbm.at[rows, cols], out_sem.at[buf])

    def for_slots(step, flags_ref, fn):
        for g in range(g_n):
            @pl.when(flags_ref[step * g_n + g] == 1)
            def _():
                fn(g)

    @pl.when(s == 0)
    def _():
        xbuf[...] = jnp.zeros_like(xbuf)
        for_slots(0, si_ref, lambda g: in_copy(0, g, 0).start())

    @pl.when(s + 1 < ns)
    def _():
        for_slots(s + 1, si_ref, lambda g: in_copy(s + 1, g, 1 - cur).start())

    for_slots(s, si_ref, lambda g: in_copy(s, g, cur).wait())

    @pl.when(s >= 2)
    def _():
        for_slots(s - 2, so_ref, lambda g: out_copy(s - 2, g, cur).wait())

    def w_copies(expert, slot):
        return (pltpu.make_async_copy(wg_hbm.at[expert], wg_buf.at[slot], w_sem.at[slot]),
                pltpu.make_async_copy(wu_hbm.at[expert], wu_buf.at[slot], w_sem.at[slot]),
                pltpu.make_async_copy(wd_hbm.at[expert], wd_buf.at[slot], w_sem.at[slot]))

    @pl.when(s == 0)
    def _():
        for c in w_copies(se_ref[0], sp_ref[0]):
            c.start()

    @pl.when(sf_ref[s] == 1)
    def _():
        slot = sp_ref[s]
        for c in w_copies(se_ref[s], slot):
            c.wait()
        wgu_s[:, 0:D_EXPERT] = wg_buf[slot].astype(bf16)
        wgu_s[:, D_EXPERT:2 * D_EXPERT] = wu_buf[slot].astype(bf16)
        wd_s[...] = wd_buf[slot].astype(bf16)

        @pl.when(sn_ref[s] >= 0)
        def _():
            for c in w_copies(sn_ref[s], 1 - slot):
                c.start()

    @pl.when(sk_ref[s] == 1)
    def _():
        u = xbuf[cur, :, 0:D_MODEL].astype(bf16)
        meta = xbuf[cur, :, D_MODEL:ROW_W]
        gate = jnp.concatenate(
            [jnp.where(sr_ref[s * g_n + g] == 0, meta[g * MOE_BLOCK:(g + 1) * MOE_BLOCK, 2:3],
                       meta[g * MOE_BLOCK:(g + 1) * MOE_BLOCK, 3:4]) for g in range(g_n)], axis=0)
        gu = _dot(u, wgu_s[...])
        gt = gu[:, 0:D_EXPERT]
        hdn = (gt * jax.nn.sigmoid(gt) * gu[:, D_EXPERT:]).astype(bf16)
        obuf[cur] = _dot(hdn, wd_s[...]) * gate

    @pl.when(sk_ref[s] == 0)
    def _():
        obuf[cur] = jnp.zeros(obuf.shape[1:], f32)

    for_slots(s, so_ref, lambda g: out_copy(s, g, cur).start())

    @pl.when(s == ns - 1)
    def _():
        for_slots(s, so_ref, lambda g: out_copy(s, g, cur).wait())

        @pl.when(s >= 1)
        def _():
            for_slots(s - 1, so_ref, lambda g: out_copy(s - 1, g, 1 - cur).wait())


def _moe_call(plan, hs, w_gate, w_up, w_down):
    n_steps = plan[0].shape[0]
    n_slots = hs.shape[0]
    rows = MOE_GROUP * MOE_BLOCK

    grid_spec = pltpu.PrefetchScalarGridSpec(
        num_scalar_prefetch=9,
        grid=(n_steps,),
        in_specs=[
            pl.BlockSpec(memory_space=pl.ANY),
            pl.BlockSpec(memory_space=pl.ANY),
            pl.BlockSpec(memory_space=pl.ANY),
            pl.BlockSpec(memory_space=pl.ANY),
        ],
        out_specs=pl.BlockSpec(memory_space=pl.ANY),
        scratch_shapes=[
            pltpu.VMEM((2, rows, ROW_W), f32),
            pltpu.VMEM((2, rows, D_MODEL), f32),
            pltpu.VMEM((2, D_MODEL, D_EXPERT), f32),
            pltpu.VMEM((2, D_MODEL, D_EXPERT), f32),
            pltpu.VMEM((2, D_EXPERT, D_MODEL), f32),
            pltpu.VMEM((D_MODEL, 2 * D_EXPERT), bf16),
            pltpu.VMEM((D_EXPERT, D_MODEL), bf16),
            pltpu.SemaphoreType.DMA((2,)),
            pltpu.SemaphoreType.DMA((2,)),
            pltpu.SemaphoreType.DMA((2,)),
        ],
    )
    return pl.pallas_call(
        _moe_kernel,
        grid_spec=grid_spec,
        out_shape=jax.ShapeDtypeStruct((n_slots, 2 * D_MODEL), f32),
        compiler_params=pltpu.CompilerParams(
            dimension_semantics=("arbitrary",), vmem_limit_bytes=VMEM_LIMIT),
        name="moe",
    )(*plan, hs, w_gate, w_up, w_down)


def _final_kernel(posc_ref, posn_ref, h_ref, gain_ref, y_hbm, o_ref, ybuf, sem):
    i = pl.program_id(0)
    cur = i % 2

    def issue(pos_ref, buf):
        def start(io, c):
            for r in range(ISSUE_UNROLL):
                ii = io * (ISSUE_UNROLL // SUBLANES) + r // SUBLANES
                pltpu.make_async_copy(y_hbm.at[pl.ds(pos_ref[io * ISSUE_UNROLL + r], 1)],
                                      ybuf.at[buf, ii, pl.ds(r % SUBLANES, 1)], sem.at[buf]).start()
            return c

        lax.fori_loop(0, FINAL_TILE // ISSUE_UNROLL, start, 0)

    @pl.when(i == 0)
    def _():
        issue(posc_ref, 0)

    @pl.when(i + 1 < pl.num_programs(0))
    def _():
        issue(posn_ref, 1 - cur)

    pltpu.make_async_copy(ybuf.at[cur], ybuf.at[cur], sem.at[cur]).wait()
    h = h_ref[...] + ybuf[cur, :, :, 0:D_MODEL] + ybuf[cur, :, :, D_MODEL:2 * D_MODEL]
    o_ref[...] = _rms(h, gain_ref[...])


def _final_call(pos, h1, gain, y):
    t = h1.shape[0]
    n = t // FINAL_TILE
    rows = FINAL_TILE // SUBLANES
    out = pl.pallas_call(
        _final_kernel,
        grid=(n,),
        in_specs=[
            pl.BlockSpec((FINAL_TILE,), lambda i: (i,), memory_space=pltpu.SMEM),
            pl.BlockSpec((FINAL_TILE,), lambda i: (jnp.minimum(i + 1, n - 1),), memory_space=pltpu.SMEM),
            pl.BlockSpec((rows, SUBLANES, D_MODEL), lambda i: (i, 0, 0)),
            pl.BlockSpec((1, 1, D_MODEL), lambda i: (0, 0, 0)),
            pl.BlockSpec(memory_space=pl.ANY),
        ],
        out_specs=pl.BlockSpec((rows, SUBLANES, D_MODEL), lambda i: (i, 0, 0)),
        out_shape=jax.ShapeDtypeStruct((t // SUBLANES, SUBLANES, D_MODEL), f32),
        scratch_shapes=[pltpu.VMEM((2, rows, SUBLANES, 2 * D_MODEL), f32), pltpu.SemaphoreType.DMA((2,))],
        compiler_params=pltpu.CompilerParams(
            dimension_semantics=("arbitrary",), vmem_limit_bytes=VMEM_LIMIT),
        name="final",
    )(pos, pos, h1.reshape(t // SUBLANES, SUBLANES, D_MODEL), gain.reshape(1, 1, D_MODEL), y)
    return out.reshape(t, D_MODEL)


def _rope_tables(pos):
    inv = ROPE_BASE ** (-jnp.arange(0, MLA_ROPE, 2, dtype=f32) / MLA_ROPE)
    ang = pos.astype(f32)[:, None] * inv[None, :]
    cos, sin = jnp.cos(ang), jnp.sin(ang)
    z = jnp.zeros((pos.shape[0], LANE - MLA_ROPE), f32)
    return jnp.concatenate([cos, cos, z], axis=1), jnp.concatenate([-sin, sin, z], axis=1)


def _relayout_weights(w_in, w_qb, w_kvb):
    half = MLA_ROPE // 2
    perm = (np.arange(MLA_ROPE) + half) % MLA_ROPE
    pts = np.cumsum((GLA_QK, GLA_QK, GLA_VW, GLA_VW, GLA_GATE_RANK, MLA_Q_RANK, MLA_KV_RANK, MLA_ROPE))
    q_g, k_g, v_g, r_g, a_l, q_lat, kv_lat, k_rope = jnp.split(w_in, pts[:-1], axis=1)
    a_seg = jnp.pad(a_l, ((0, 0), (0, LANE - GLA_GATE_RANK)))
    w_in_r = jnp.concatenate(
        [q_g, k_g, v_g, r_g, q_lat, kv_lat, k_rope, k_rope[:, perm], a_seg], axis=1).astype(bf16)
    qcols, kcols, vcols = [], [], []
    for h in range(MLA_HEADS):
        c = h * (MLA_NOPE + MLA_ROPE)
        rope = w_qb[:, c + MLA_NOPE:c + MLA_NOPE + MLA_ROPE]
        qcols += [w_qb[:, c:c + MLA_NOPE], rope, rope[:, perm]]
        c2 = h * (MLA_NOPE + MLA_V)
        kcols.append(w_kvb[:, c2:c2 + MLA_NOPE])
        vcols.append(w_kvb[:, c2 + MLA_NOPE:c2 + MLA_NOPE + MLA_V])
    return w_in_r, jnp.concatenate(qcols, axis=1).astype(bf16), jnp.concatenate(kcols + vcols, axis=1).astype(bf16)


_BUCKET_GROUP = np.arange(N_BUCKETS) // N_PAIRS
_RUN_EXPERT = np.concatenate([_BUCKET_GROUP * EXPERTS_PER_GROUP + _PAIR_LO[np.arange(N_BUCKETS) % N_PAIRS],
                              _BUCKET_GROUP * EXPERTS_PER_GROUP + _PAIR_HI[np.arange(N_BUCKETS) % N_PAIRS]])
_RUN_IS_EXPERT = (_RUN_EXPERT[:, None] == np.arange(N_EXPERTS)[None, :]).astype(np.int32)
_RUN_BEFORE = ((_RUN_EXPERT[:, None] == _RUN_EXPERT[None, :])
               & (np.arange(2 * N_BUCKETS)[None, :] < np.arange(2 * N_BUCKETS)[:, None])).astype(np.int32)


def _route_plan(counts, bucket, rank, n_tok):
    nt = counts.shape[0]
    g_n = MOE_GROUP
    tot = counts.sum(axis=0)
    nblk = (tot + MOE_BLOCK - 1) // MOE_BLOCK
    bstart_blk = jnp.cumsum(nblk) - nblk
    n_blocks = jnp.sum(nblk)
    tile_base = bstart_blk[None, :] * MOE_BLOCK + jnp.cumsum(counts, axis=0) - counts
    hit = bucket.reshape(nt, -1, 1) == jnp.arange(N_BUCKETS, dtype=i32)
    pos = jnp.sum(jnp.where(hit, tile_base[:, None, :], 0), axis=-1).reshape(-1) + rank
    nb_max = (n_tok + N_BUCKETS * (MOE_BLOCK - 1)) // MOE_BLOCK

    n_run = jnp.concatenate([nblk, nblk])
    b0_run = jnp.concatenate([bstart_blk, bstart_blk])
    c_e = jnp.sum(n_run[:, None] * _RUN_IS_EXPERT, axis=0)
    g_e = (c_e + g_n - 1) // g_n
    gend = jnp.cumsum(g_e)
    gstart = gend - g_e
    n_compute = gend[-1]
    off_run = jnp.sum(_RUN_BEFORE * n_run[None, :], axis=1)
    f_run = jnp.sum(_RUN_IS_EXPERT * gstart[None, :], axis=1) * g_n + off_run

    n_steps = (2 * nb_max + N_EXPERTS * (g_n - 1) + g_n - 1) // g_n + 1
    f = jnp.arange(n_steps * g_n, dtype=i32)
    in_run = (f[:, None] >= f_run[None, :]) & (f[:, None] < (f_run + n_run)[None, :])
    valid_c = jnp.any(in_run, axis=1)
    block_c = jnp.sum(jnp.where(in_run, b0_run[None, :] + f[:, None] - f_run[None, :], 0), axis=1)
    role_c = jnp.sum(jnp.where(in_run[:, N_BUCKETS:], 1, 0), axis=1)
    u_idx = f - n_compute * g_n
    valid_f = (u_idx >= 0) & (u_idx < 2 * (nb_max - n_blocks))
    slot_block = jnp.where(valid_c, block_c, jnp.where(valid_f, n_blocks + u_idx // 2, 0))
    slot_role = jnp.where(valid_c, role_c, jnp.where(valid_f, u_idx % 2, 0))

    step = jnp.arange(n_steps, dtype=i32)
    e_of_step = jnp.minimum(jnp.sum(gend[None, :] <= step[:, None], axis=1), N_EXPERTS - 1)
    is_compute = step < n_compute
    last_e = jnp.max(jnp.where(is_compute, e_of_step, 0))
    step_expert = jnp.where(is_compute, e_of_step, last_e)
    step_first = jnp.concatenate([jnp.ones((1,), bool), step_expert[1:] != step_expert[:-1]])
    ordinal = jnp.cumsum(step_first.astype(i32)) - 1
    ords = jnp.arange(N_EXPERTS + 1, dtype=i32)
    expert_of_ord = jnp.sum(jnp.where(step_first[:, None] & (ordinal[:, None] == ords[None, :]),
                                      step_expert[:, None], 0), axis=0)
    has_next = ordinal + 1 <= ordinal[-1]
    next_expert = jnp.sum(jnp.where(ords[None, :] == ordinal[:, None] + 1, expert_of_ord[None, :], 0), axis=1)
    step_next = jnp.where(step_first & has_next, next_expert, -1)
    plan = tuple(a.astype(i32) for a in
                 (step_expert, step_first, is_compute, ordinal % 2, step_next,
                  slot_block, slot_role, valid_c, valid_c | valid_f))
    last_blk = jnp.where(nblk > 0, bstart_blk + nblk - 1, -1)
    spare = n_blocks + jnp.arange(nb_max - n_tok // MOE_BLOCK, dtype=i32)
    zero_blocks = jnp.concatenate([last_blk, jnp.where(spare < nb_max, spare, -1)]).astype(i32)
    return pos.astype(i32), plan, zero_blocks, nb_max


def kernel(x, meta_tokens, mix_norm, w_in, gla_w_a2, gla_b_a, gla_out_norm, mla_q_norm, mla_w_qb, mla_kv_norm,
           mla_w_kvb, w_out, ffn_norm, router_group_w, router_group_b, router_expert_w, router_expert_b,
           expert_w_gate, expert_w_up, expert_w_down, final_norm):
    batch, seq, d = x.shape
    assert d == D_MODEL and seq % max(PREP_TILE, GLA_TILE, ATT_TILE) == 0
    assert (batch * seq) % max(OUT_TILE, SCATTER_TILE, FINAL_TILE) == 0 and batch % GLA_BATCH == 0
    n_tok = batch * seq
    x2d = x.reshape(n_tok, d)

    w_in_r, w_qb_r, w_kvb_r = _relayout_weights(w_in[0], mla_w_qb[0], mla_w_kvb[0])
    mixg = mix_norm[0].reshape(1, d)
    qn = mla_q_norm[0].reshape(1, MLA_Q_RANK)
    kvn = mla_kv_norm[0].reshape(1, MLA_KV_RANK)
    ct_m, st_m = _rope_tables(jnp.arange(META_TILE))
    ct_x, st_x = _rope_tables(N_META + jnp.arange(seq))

    x_meta = jnp.pad(meta_tokens.astype(f32), ((0, META_TILE - N_META), (0, 0)))
    _, kg_m, vg_m, _, a_m, _, km_m, vmt_m = _prep_call(
        x_meta, META_TILE, META_TILE, mixg, w_in_r, qn, w_qb_r, kvn, w_kvb_r, ct_m, st_m)
    qg, kg, vg, rg, ag, qm, km, vmt = _prep_call(
        x2d, seq, PREP_TILE, mixg, w_in_r, qn, w_qb_r, kvn, w_kvb_r, ct_x, st_x)

    def chunk0(a):
        return jnp.pad(a[:N_META], ((CHUNK - N_META, 0), (0, 0)))

    wa2_p = jnp.pad(gla_w_a2[0], ((0, LANE - GLA_GATE_RANK), (0, 0))).astype(bf16)
    y_gla = _gla_call(qg, kg, vg, rg, ag, chunk0(kg_m), chunk0(vg_m), chunk0(a_m),
                      wa2_p, gla_b_a[0].reshape(1, GLA_QK), gla_out_norm[0].reshape(1, GLA_VW), batch, seq)
    y_mla = _mla_call(qm, km, vmt, km_m[:N_META], vmt_m[0, :, :N_META], batch, seq)

    wo = w_out[0].astype(bf16)
    rw = jnp.concatenate([router_group_w[0], router_expert_w[0],
                          jnp.zeros((d, LANE - N_GROUPS - N_EXPERTS), f32)], axis=1)
    rb = jnp.concatenate([router_group_b[0], router_expert_b[0],
                          jnp.zeros((LANE - N_GROUPS - N_EXPERTS,), f32)]).reshape(1, LANE)
    h1, ux, cnt, routes = _outproj_call(x2d, y_gla, y_mla, wo[:GLA_VW], wo[GLA_VW:], ffn_norm[0].reshape(1, d),
                                      rw.T.astype(bf16), rb.reshape(LANE, 1))

    counts = cnt.reshape(-1, BUCKET_LANES)[:, :N_BUCKETS].astype(i32)
    tok_bucket = routes[:, 0, :].reshape(-1).astype(i32)
    tok_rank = routes[:, 1, :].reshape(-1).astype(i32)
    pos, plan, zero_blocks, nb_max = _route_plan(counts, tok_bucket, tok_rank, n_tok)
    n_slots = nb_max * MOE_BLOCK
    hs = _scatter_call(pos, zero_blocks, ux, n_slots)
    y = _moe_call(plan, hs, expert_w_gate[0], expert_w_up[0], expert_w_down[0])
    out = _final_call(pos, h1, final_norm.reshape(1, d), y)
    return out.reshape(batch, seq, d)
```

```python
import functools

import numpy as np
import jax
import jax.numpy as jnp
from jax import lax
from jax.experimental import pallas as pl
from jax.experimental.pallas import tpu as pltpu

f32 = jnp.float32
bf16 = jnp.bfloat16
i32 = jnp.int32

D_MODEL = 1024
CHUNK = 64
N_META = 16
EPS = 1e-6
GLA_HEADS = 4
GLA_DK = 64
GLA_DV = 128
GLA_GATE_RANK = 16
GLA_TAU = 16.0
GLA_QK = GLA_HEADS * GLA_DK
GLA_VW = GLA_HEADS * GLA_DV
MLA_HEADS = 4
MLA_Q_RANK = 256
MLA_KV_RANK = 128
MLA_NOPE = 128
MLA_ROPE = 64
MLA_V = 128
MLA_OUT = MLA_HEADS * MLA_V
MLA_QK_PAD = 256
MLA_VA = MLA_V + 16
LOG2_E = 1.4426950408889634
ROPE_BASE = 10000.0
N_GROUPS = 8
EXPERTS_PER_GROUP = 8
N_EXPERTS = N_GROUPS * EXPERTS_PER_GROUP
D_EXPERT = 512
N_PAIRS = EXPERTS_PER_GROUP * (EXPERTS_PER_GROUP - 1) // 2
N_BUCKETS = N_GROUPS * N_PAIRS
BUCKET_LANES = 256
LANE = 128
SUBLANES = 8
META_W = LANE
ROW_W = D_MODEL + META_W

PREP_TILE = 512
GLA_TILE = 512
GLA_BATCH = 4
ATT_TILE = 512
ATT_HEADS = 4
META_TILE = 128
OUT_TILE = 1024
ROUTE_ROWS = 256
ROUTE_GROUP = 512
SCATTER_TILE = 2048
FINAL_TILE = 512
ISSUE_UNROLL = 64
MOE_BLOCK = 32
MOE_GROUP = 16
VMEM_LIMIT = 56 * 1024 * 1024

C_Q, C_K, C_V, C_R = 0, 256, 512, 1024
C_QLAT, C_KVLAT, C_KROPE, C_A, C_END = 1536, 1792, 1920, 2048, 2176

_TILE_POS = np.arange(GLA_TILE)
_CHUNK_PREFIX = ((_TILE_POS[:, None] // CHUNK == _TILE_POS[None, :] // CHUNK)
                 & (_TILE_POS[None, :] <= _TILE_POS[:, None])).astype(np.float32)
_PAIR_LO = np.array([lo for lo in range(8) for hi in range(lo + 1, 8)], np.int32)
_PAIR_HI = np.array([hi for lo in range(8) for hi in range(lo + 1, 8)], np.int32)


def _dot(a, b):
    return jnp.dot(a, b, preferred_element_type=f32)


def _dot_nt(a, b):
    return lax.dot_general(a, b, (((1,), (1,)), ((), ())), preferred_element_type=f32)


def _dot_tn(a, b):
    return lax.dot_general(a, b, (((0,), (0,)), ((), ())), preferred_element_type=f32)


def _rms(x, gain):
    return x * lax.rsqrt(jnp.mean(x * x, axis=-1, keepdims=True) + EPS) * gain


def _split3(x):
    hi = x.astype(bf16)
    r1 = x - hi.astype(f32)
    mid = r1.astype(bf16)
    lo = (r1 - mid.astype(f32)).astype(bf16)
    return hi, mid, lo


def _prep_kernel(x_ref, g_ref, win_ref, qn_ref, wqb_ref, kvn_ref, wkvb_ref, ct_ref, st_ref,
                 qg_ref, kg_ref, vg_ref, rg_ref, a_ref, qm_ref, km_ref, vmt_ref):
    u = _rms(x_ref[...], g_ref[...]).astype(bf16)

    def proj(lo, hi):
        return _dot(u, win_ref[:, lo:hi])

    qg_ref[...] = proj(C_Q, C_K).astype(bf16)
    kg_ref[...] = proj(C_K, C_V).astype(bf16)
    vg_ref[...] = proj(C_V, C_R).astype(bf16)
    rg_ref[...] = proj(C_R, C_QLAT).astype(bf16)
    z = proj(C_QLAT, C_END)
    a_ref[...] = z[:, C_A - C_QLAT:].astype(bf16)
    ctab = ct_ref[...]
    stab = st_ref[...]

    def rope(seg):
        return seg * ctab + pltpu.roll(seg, 64, axis=1) * stab

    k_rope = rope(z[:, C_KROPE - C_QLAT:C_A - C_QLAT]).astype(bf16)
    qn = _rms(z[:, 0:MLA_Q_RANK], qn_ref[...]).astype(bf16)
    kvn = _rms(z[:, MLA_Q_RANK:MLA_Q_RANK + MLA_KV_RANK], kvn_ref[...]).astype(bf16)
    scale = (MLA_NOPE + MLA_ROPE) ** -0.5 * LOG2_E
    qf = _dot(qn, wqb_ref[...])
    kvf = _dot(kvn, wkvb_ref[...])
    for h in range(MLA_HEADS):
        c = h * MLA_QK_PAD
        qm_ref[:, c:c + LANE] = (qf[:, c:c + LANE] * scale).astype(bf16)
        qm_ref[:, c + LANE:c + 2 * LANE] = (rope(qf[:, c + LANE:c + 2 * LANE]) * scale).astype(bf16)
        km_ref[:, c:c + LANE] = kvf[:, h * LANE:(h + 1) * LANE].astype(bf16)
        km_ref[:, c + LANE:c + 2 * LANE] = k_rope
    vt = kvf[:, MLA_HEADS * MLA_NOPE:].T
    for h in range(MLA_HEADS):
        vmt_ref[h * MLA_VA:h * MLA_VA + MLA_V, :] = vt[h * MLA_V:(h + 1) * MLA_V].astype(bf16)
        vmt_ref[h * MLA_VA + MLA_V:(h + 1) * MLA_VA, :] = jnp.ones((MLA_VA - MLA_V, vt.shape[1]), bf16)


def _prep_call(x2d, rows_per_seq, tile, gain, w_in_r, q_norm, w_qb_r, kv_norm, w_kvb_r, ctab, stab):
    t = x2d.shape[0]
    nj = rows_per_seq // tile
    grid = (t // rows_per_seq, nj)

    def row(b, j):
        return (b * nj + j, 0)

    def const(b, j):
        return (0, 0)

    def tab(b, j):
        return (j, 0)

    widths = (GLA_QK, GLA_QK, GLA_VW, GLA_VW, LANE, MLA_HEADS * MLA_QK_PAD, MLA_HEADS * MLA_QK_PAD)
    return pl.pallas_call(
        _prep_kernel,
        grid=grid,
        in_specs=[
            pl.BlockSpec((tile, D_MODEL), row),
            pl.BlockSpec((1, D_MODEL), const),
            pl.BlockSpec((D_MODEL, C_END), const),
            pl.BlockSpec((1, MLA_Q_RANK), const),
            pl.BlockSpec((MLA_Q_RANK, MLA_HEADS * MLA_QK_PAD), const),
            pl.BlockSpec((1, MLA_KV_RANK), const),
            pl.BlockSpec((MLA_KV_RANK, 2 * MLA_OUT), const),
            pl.BlockSpec((tile, LANE), tab),
            pl.BlockSpec((tile, LANE), tab),
        ],
        out_specs=[pl.BlockSpec((tile, w), row) for w in widths]
        + [pl.BlockSpec((None, MLA_HEADS * MLA_VA, tile), lambda b, j: (b * nj + j, 0, 0))],
        out_shape=[jax.ShapeDtypeStruct((t, w), bf16) for w in widths]
        + [jax.ShapeDtypeStruct((t // tile, MLA_HEADS * MLA_VA, tile), bf16)],
        compiler_params=pltpu.CompilerParams(
            dimension_semantics=("parallel", "parallel"), vmem_limit_bytes=VMEM_LIMIT),
        name="prep",
    )(x2d, gain, w_in_r, q_norm, w_qb_r, kv_norm, w_kvb_r, ctab, stab)


def _gla_log_decay(a, wa2_ref, ba_ref):
    s = _dot(a, wa2_ref[...]) + ba_ref[...]
    return (jnp.minimum(s, 0.0) - jnp.log(1.0 + jnp.exp(-jnp.abs(s)))) * (1.0 / GLA_TAU)


def _gla_front(q, k, v, la, tri, want_out):
    nc = la.shape[0] // CHUNK
    hi, mid, lo = _split3(la)
    b = _dot(tri, hi) + _dot(tri, mid) + _dot(tri, lo)
    b_last = [b[(c + 1) * CHUNK - 1:(c + 1) * CHUNK, :] for c in range(nc)]
    b_last_full = jnp.concatenate([jnp.broadcast_to(bl, (CHUNK, GLA_QK)) for bl in b_last], axis=0)
    kf = k.astype(f32)
    front = dict(v=v, b_last=b_last, kd=(kf * jnp.exp(b_last_full - b)).astype(bf16))
    if want_out:
        front.update(qe=(q.astype(f32) * (GLA_DK ** -0.5) * jnp.exp(b)).astype(bf16),
                     ke=kf * jnp.exp(-b), vf=v.astype(f32))
    return front


def _gla_chunks(front, st_ref, want_out):
    v, kd, b_last = front["v"], front["kd"], front["b_last"]
    rr = lax.broadcasted_iota(i32, (GLA_VW, GLA_QK), 0) // GLA_DV
    cc = lax.broadcasted_iota(i32, (GLA_VW, GLA_QK), 1) // GLA_DK
    if want_out:
        qe, ke, vf = front["qe"], front["ke"], front["vf"]
        lane_h = lax.broadcasted_iota(i32, (CHUNK, GLA_QK), 1) // GLA_DK
        vlane_h = lax.broadcasted_iota(i32, (CHUNK, GLA_VW), 1) // GLA_DV
        a_row = lax.broadcasted_iota(i32, (CHUNK, GLA_QK), 0)
        a_col = lax.broadcasted_iota(i32, (CHUNK, GLA_QK), 1) % CHUNK
    outs = []
    st = st_ref[...]
    for c in range(len(b_last)):
        rows = slice(c * CHUNK, (c + 1) * CHUNK)
        upd = jnp.where(rr == cc, _dot_tn(v[rows], kd[rows]), 0.0)
        if want_out:
            kbd = jnp.concatenate(
                [jnp.where(lane_h == h, ke[rows], 0.0) for h in range(GLA_HEADS)], axis=0).astype(bf16)
            att = jnp.where(a_col <= a_row, _dot_nt(qe[rows], kbd), 0.0).astype(bf16)
            vbd = jnp.concatenate(
                [jnp.where(vlane_h == h, vf[rows], 0.0) for h in range(GLA_HEADS)], axis=0).astype(bf16)
            outs.append(_dot(att, vbd) + _dot_nt(qe[rows], st.astype(bf16)))
        st = st * jnp.exp(b_last[c]) + upd
    st_ref[...] = st
    return jnp.concatenate(outs, axis=0) if want_out else None


def _gla_kernel(q_ref, k_ref, v_ref, r_ref, a_ref, km_ref, vm_ref, am_ref, wa2_ref, ba_ref, gain_ref, tri_ref,
                y_ref, st_ref):
    j = pl.program_id(1)

    @pl.when(j == 0)
    def _():
        st_ref[...] = jnp.zeros_like(st_ref)
        la = _gla_log_decay(am_ref[...], wa2_ref, ba_ref)
        row = lax.broadcasted_iota(i32, la.shape, 0)
        la = jnp.where(row >= CHUNK - N_META, la, 0.0)
        front = _gla_front(None, km_ref[...], vm_ref[...], la, tri_ref[0:CHUNK, 0:CHUNK], False)
        _gla_chunks(front, st_ref.at[0], False)
        for bb in range(1, GLA_BATCH):
            st_ref[bb] = st_ref[0]

    fronts = [_gla_front(q_ref[bb], k_ref[bb], v_ref[bb], _gla_log_decay(a_ref[bb], wa2_ref, ba_ref),
                         tri_ref[...], True) for bb in range(GLA_BATCH)]
    for bb in range(GLA_BATCH):
        o = _gla_chunks(fronts[bb], st_ref.at[bb], True)
        r = r_ref[bb].astype(f32)
        outs = []
        for h in range(GLA_HEADS):
            oh = o[:, h * GLA_DV:(h + 1) * GLA_DV]
            outs.append(oh * lax.rsqrt(jnp.mean(oh * oh, axis=-1, keepdims=True) + EPS))
        on = jnp.concatenate(outs, axis=1) * gain_ref[...]
        y_ref[bb] = (on * (r * jax.nn.sigmoid(r))).astype(bf16)


def _gla_call(qg, kg, vg, rg, ag, km, vm, am, wa2_p, b_a, gain, batch, seq):
    nj = seq // GLA_TILE

    def row(b, j):
        return (b, j, 0)

    def const(b, j):
        return (0, 0)

    def seqs(a):
        return a.reshape(batch, seq, a.shape[-1])

    out = pl.pallas_call(
        _gla_kernel,
        grid=(batch // GLA_BATCH, nj),
        in_specs=[
            pl.BlockSpec((GLA_BATCH, GLA_TILE, GLA_QK), row),
            pl.BlockSpec((GLA_BATCH, GLA_TILE, GLA_QK), row),
            pl.BlockSpec((GLA_BATCH, GLA_TILE, GLA_VW), row),
            pl.BlockSpec((GLA_BATCH, GLA_TILE, GLA_VW), row),
            pl.BlockSpec((GLA_BATCH, GLA_TILE, LANE), row),
            pl.BlockSpec((CHUNK, GLA_QK), const),
            pl.BlockSpec((CHUNK, GLA_VW), const),
            pl.BlockSpec((CHUNK, LANE), const),
            pl.BlockSpec((LANE, GLA_QK), const),
            pl.BlockSpec((1, GLA_QK), const),
            pl.BlockSpec((1, GLA_VW), const),
            pl.BlockSpec((GLA_TILE, GLA_TILE), const),
        ],
        out_specs=pl.BlockSpec((GLA_BATCH, GLA_TILE, GLA_VW), row),
        out_shape=jax.ShapeDtypeStruct((batch, seq, GLA_VW), bf16),
        scratch_shapes=[pltpu.VMEM((GLA_BATCH, GLA_VW, GLA_QK), f32)],
        compiler_params=pltpu.CompilerParams(
            dimension_semantics=("parallel", "arbitrary"), vmem_limit_bytes=VMEM_LIMIT),
        name="gla",
    )(seqs(qg), seqs(kg), seqs(vg), seqs(rg), seqs(ag), km, vm, am, wa2_p, b_a, gain,
      jnp.asarray(_CHUNK_PREFIX, bf16))
    return out.reshape(batch * seq, GLA_VW)


def _mla_kernel(q_ref, k_ref, vt_ref, km_ref, vmt_ref, o_ref, sa_ref, sb_ref):
    i = pl.program_id(2)
    tq = ATT_TILE
    w = MLA_QK_PAD
    va = MLA_VA
    heads = range(ATT_HEADS)

    def scores(h, blk):
        rows = pl.ds(pl.multiple_of(blk * tq, tq), tq)
        return _dot_nt(k_ref[rows, h * w:(h + 1) * w], q_ref[:, h * w:(h + 1) * w])

    def soft(s, vtb, carry, mask=None):
        m, acc = carry
        if mask is not None:
            s = jnp.where(mask, s, -1e30)
        m_new = jnp.maximum(m, jnp.max(s, axis=0, keepdims=True))
        p = jnp.exp2(s - m_new).astype(bf16)
        return m_new, jnp.exp2(m - m_new) * acc + _dot(vtb, p)

    def vt(h, blk):
        return vt_ref[blk, h * va:(h + 1) * va, :]

    def finish(carries):
        ss = [_dot_nt(km_ref[:, h * w:(h + 1) * w], q_ref[:, h * w:(h + 1) * w]) for h in heads]
        accs = [soft(ss[h], vmt_ref[h * va:(h + 1) * va, :], carries[h])[1] for h in heads]
        for h in heads:
            acc = accs[h]
            o_ref[:, h * MLA_V:(h + 1) * MLA_V] = (acc[:MLA_V] * (1.0 / acc[MLA_V:MLA_V + 1])).T.astype(bf16)

    kc = lax.broadcasted_iota(i32, (tq, tq), 0) // CHUNK
    qc = lax.broadcasted_iota(i32, (tq, tq), 1) // CHUNK
    mask = kc <= qc

    for h in heads:
        sa_ref[h] = scores(h, 0)

    def pair(p, carries):
        b0 = 2 * p
        for h in heads:
            sb_ref[h] = scores(h, b0 + 1)
        carries = [soft(sa_ref[h], vt(h, b0), carries[h]) for h in heads]
        for h in heads:
            sa_ref[h] = scores(h, b0 + 2)
        return tuple(soft(sb_ref[h], vt(h, b0 + 1), carries[h]) for h in heads)

    init = tuple((jnp.full((1, tq), -1e30, f32), jnp.zeros((va, tq), f32)) for _ in heads)
    carries = lax.fori_loop(0, i // 2, pair, init)

    @pl.when(i % 2 == 1)
    def _():
        for h in heads:
            sb_ref[h] = scores(h, i)
        c1 = [soft(sa_ref[h], vt(h, i - 1), carries[h]) for h in heads]
        finish([soft(sb_ref[h], vt(h, i), c1[h], mask) for h in heads])

    @pl.when(i % 2 == 0)
    def _():
        finish([soft(sa_ref[h], vt(h, i), carries[h], mask) for h in heads])


def _mla_call(qm, km, vmt, km_meta, vmt_meta, batch, seq):
    nq = seq // ATT_TILE
    nh = ATT_HEADS
    qm3 = qm.reshape(batch, seq, MLA_HEADS * MLA_QK_PAD)
    km3 = km.reshape(batch, seq, MLA_HEADS * MLA_QK_PAD)
    vt4 = vmt.reshape(batch, nq, MLA_HEADS * MLA_VA, ATT_TILE)
    out = pl.pallas_call(
        _mla_kernel,
        grid=(batch, MLA_HEADS // nh, nq),
        in_specs=[
            pl.BlockSpec((None, ATT_TILE, nh * MLA_QK_PAD), lambda b, h, i: (b, i, h)),
            pl.BlockSpec((None, seq, nh * MLA_QK_PAD), lambda b, h, i: (b, 0, h)),
            pl.BlockSpec((None, nq, nh * MLA_VA, ATT_TILE), lambda b, h, i: (b, 0, h, 0)),
            pl.BlockSpec((N_META, nh * MLA_QK_PAD), lambda b, h, i: (0, h)),
            pl.BlockSpec((nh * MLA_VA, N_META), lambda b, h, i: (h, 0)),
        ],
        out_specs=pl.BlockSpec((None, ATT_TILE, nh * MLA_V), lambda b, h, i: (b, i, h)),
        out_shape=jax.ShapeDtypeStruct((batch, seq, MLA_OUT), bf16),
        scratch_shapes=[pltpu.VMEM((nh, ATT_TILE, ATT_TILE), f32), pltpu.VMEM((nh, ATT_TILE, ATT_TILE), f32)],
        compiler_params=pltpu.CompilerParams(
            dimension_semantics=("parallel", "parallel", "arbitrary"), vmem_limit_bytes=VMEM_LIMIT),
        name="mla",
    )(qm3, km3, vt4, km_meta, vmt_meta)
    return out.reshape(batch * seq, MLA_OUT)


def _route_cols(lt):
    r = lt.shape[1]
    neg = -1e30
    gl = lt[0:N_GROUPS, :]
    gsub = lax.broadcasted_iota(i32, (N_GROUPS, r), 0)
    gmax = jnp.max(gl, axis=0, keepdims=True)
    g_p = 1.0 / jnp.sum(jnp.exp(gl - gmax), axis=0, keepdims=True)
    g_idx = jnp.min(jnp.where(gl == gmax, gsub, N_GROUPS), axis=0, keepdims=True)
    el_all = lt[N_GROUPS:N_GROUPS + N_EXPERTS, :]
    esub = lax.broadcasted_iota(i32, (N_EXPERTS, r), 0)
    base = g_idx * EXPERTS_PER_GROUP
    e_mask = (esub >= base) & (esub < base + EXPERTS_PER_GROUP)
    el = jnp.where(e_mask, el_all, neg)
    m1 = jnp.max(el, axis=0, keepdims=True)
    i1 = jnp.min(jnp.where(e_mask & (el == m1), esub, N_EXPERTS), axis=0, keepdims=True)
    el2 = jnp.where(esub == i1, neg, el)
    m2 = jnp.max(el2, axis=0, keepdims=True)
    i2 = jnp.min(jnp.where(e_mask & (esub != i1) & (el2 == m2), esub, N_EXPERTS), axis=0, keepdims=True)
    rr = jnp.exp(m2 - m1)
    ga = g_p / (1.0 + rr)
    gb = g_p * rr / (1.0 + rr)
    la_ = i1 - base
    lb_ = i2 - base
    lo = jnp.minimum(la_, lb_)
    hi = jnp.maximum(la_, lb_)
    g_lo = jnp.where(la_ < lb_, ga, gb)
    g_hi = jnp.where(la_ < lb_, gb, ga)
    pidx = ((lo * (2 * EXPERTS_PER_GROUP - 1 - lo)) >> 1) + (hi - lo - 1)
    bucket = g_idx * N_PAIRS + pidx
    bsub = lax.broadcasted_iota(i32, (BUCKET_LANES, r), 0)
    oht = jnp.where(bsub == bucket, 1.0, 0.0)
    ohb = oht.astype(bf16)
    ri = lax.broadcasted_iota(i32, (ROUTE_ROWS, ROUTE_ROWS), 0)
    ci = lax.broadcasted_iota(i32, (ROUTE_ROWS, ROUTE_ROWS), 1)
    before = jnp.where(ri < ci, 1.0, 0.0).astype(bf16)
    ones = jnp.ones((SUBLANES, ROUTE_ROWS), bf16)
    subs = [slice(i * ROUTE_ROWS, (i + 1) * ROUTE_ROWS) for i in range(r // ROUTE_ROWS)]
    cum = jnp.concatenate([_dot(ohb[:, sl], before) for sl in subs], axis=1)
    rank = jnp.sum(oht * cum, axis=0, keepdims=True)
    counts = [_dot_nt(ones, ohb[:, sl])[0:1, :] for sl in subs]
    msub = lax.broadcasted_iota(i32, (LANE, r), 0)
    meta_t = jnp.where(msub == 0, bucket.astype(f32),
                       jnp.where(msub == 1, rank,
                                 jnp.where(msub == 2, g_lo, jnp.where(msub == 3, g_hi, 0.0))))
    return meta_t.T, counts, meta_t[0:SUBLANES, :]


def _outproj_kernel(x_ref, yg_ref, ym_ref, wog_ref, wom_ref, gain_ref, wrt_ref, rb_ref,
                    h_ref, ux_ref, cnt_ref, rt_ref):
    per_group = ROUTE_GROUP // ROUTE_ROWS
    for grp in range(OUT_TILE // ROUTE_GROUP):
        rows = slice(grp * ROUTE_GROUP, (grp + 1) * ROUTE_GROUP)
        h1 = x_ref[rows, :] + _dot(yg_ref[rows, :], wog_ref[...]) + _dot(ym_ref[rows, :], wom_ref[...])
        h_ref[rows, :] = h1
        u2 = _rms(h1, gain_ref[...])
        ux_ref[rows, 0:D_MODEL] = u2
        lt = _dot_nt(wrt_ref[...], u2.astype(bf16)) + rb_ref[...]
        meta, counts, routes = _route_cols(lt)
        ux_ref[rows, D_MODEL:ROW_W] = meta
        for i in range(per_group):
            cnt_ref[grp * per_group + i] = counts[i]
            rt_ref[grp * per_group + i] = routes[:, i * ROUTE_ROWS:(i + 1) * ROUTE_ROWS]


def _outproj_call(x2d, yg, ym, wo_g, wo_m, gain, w_r, rbias):
    t = x2d.shape[0]
    nt = t // OUT_TILE

    def row(i):
        return (i, 0)

    def const(i):
        return (0, 0)

    return pl.pallas_call(
        _outproj_kernel,
        grid=(nt,),
        in_specs=[
            pl.BlockSpec((OUT_TILE, D_MODEL), row),
            pl.BlockSpec((OUT_TILE, GLA_VW), row),
            pl.BlockSpec((OUT_TILE, MLA_OUT), row),
            pl.BlockSpec((GLA_VW, D_MODEL), const),
            pl.BlockSpec((MLA_OUT, D_MODEL), const),
            pl.BlockSpec((1, D_MODEL), const),
            pl.BlockSpec((LANE, D_MODEL), const),
            pl.BlockSpec((LANE, 1), const),
        ],
        out_specs=[
            pl.BlockSpec((OUT_TILE, D_MODEL), row),
            pl.BlockSpec((OUT_TILE, ROW_W), row),
            pl.BlockSpec((OUT_TILE // ROUTE_ROWS, 1, BUCKET_LANES), lambda i: (i, 0, 0)),
            pl.BlockSpec((OUT_TILE // ROUTE_ROWS, SUBLANES, ROUTE_ROWS), lambda i: (i, 0, 0)),
        ],
        out_shape=[
            jax.ShapeDtypeStruct((t, D_MODEL), f32),
            jax.ShapeDtypeStruct((t, ROW_W), f32),
            jax.ShapeDtypeStruct((nt * (OUT_TILE // ROUTE_ROWS), 1, BUCKET_LANES), f32),
            jax.ShapeDtypeStruct((nt * (OUT_TILE // ROUTE_ROWS), SUBLANES, ROUTE_ROWS), f32),
        ],
        compiler_params=pltpu.CompilerParams(
            dimension_semantics=("parallel",), vmem_limit_bytes=VMEM_LIMIT),
        name="outproj",
    )(x2d, yg, ym, wo_g, wo_m, gain, w_r, rbias)


def _scatter_kernel(pos_ref, zb_ref, ux_ref, hs_ref, zbuf, sem, zsem):
    @pl.when(pl.program_id(0) == 0)
    def _():
        zbuf[...] = jnp.zeros_like(zbuf)

        def zero_copy(j):
            rows = pl.ds(pl.multiple_of(zb_ref[j] * MOE_BLOCK, MOE_BLOCK), MOE_BLOCK)
            return pltpu.make_async_copy(zbuf, hs_ref.at[rows], zsem)

        def zstart(j, c):
            @pl.when(zb_ref[j] >= 0)
            def _():
                zero_copy(j).start()
            return c

        def zwait(j, c):
            @pl.when(zb_ref[j] >= 0)
            def _():
                zero_copy(j).wait()
            return c

        lax.fori_loop(0, zb_ref.shape[0], zstart, 0)
        lax.fori_loop(0, zb_ref.shape[0], zwait, 0)

    def start(io, c):
        for r in range(ISSUE_UNROLL):
            ii = io * (ISSUE_UNROLL // SUBLANES) + r // SUBLANES
            pltpu.make_async_copy(ux_ref.at[ii, pl.ds(r % SUBLANES, 1)],
                                  hs_ref.at[pl.ds(pos_ref[io * ISSUE_UNROLL + r], 1)], sem).start()
        return c

    lax.fori_loop(0, SCATTER_TILE // ISSUE_UNROLL, start, 0)
    pltpu.make_async_copy(hs_ref.at[pl.ds(0, SCATTER_TILE)], hs_ref.at[pl.ds(0, SCATTER_TILE)], sem).wait()


def _scatter_call(pos, zero_blocks, ux, n_slots):
    t = ux.shape[0]
    nz = zero_blocks.shape[0]
    return pl.pallas_call(
        _scatter_kernel,
        grid=(t // SCATTER_TILE,),
        in_specs=[
            pl.BlockSpec((SCATTER_TILE,), lambda i: (i,), memory_space=pltpu.SMEM),
            pl.BlockSpec((nz,), lambda i: (0,), memory_space=pltpu.SMEM),
            pl.BlockSpec((SCATTER_TILE // SUBLANES, SUBLANES, ROW_W), lambda i: (i, 0, 0)),
        ],
        out_specs=pl.BlockSpec(memory_space=pl.ANY),
        out_shape=jax.ShapeDtypeStruct((n_slots, ROW_W), f32),
        scratch_shapes=[pltpu.VMEM((MOE_BLOCK, ROW_W), f32), pltpu.SemaphoreType.DMA(()),
                        pltpu.SemaphoreType.DMA(())],
        compiler_params=pltpu.CompilerParams(
            dimension_semantics=("arbitrary",), vmem_limit_bytes=VMEM_LIMIT),
        name="scatter",
    )(pos, zero_blocks, ux.reshape(t // SUBLANES, SUBLANES, ROW_W))


def _moe_kernel(se_ref, sf_ref, sk_ref, sp_ref, sn_ref, sh_ref, sb_ref, sr_ref, si_ref, so_ref,
                hs_hbm, wg_hbm, wu_hbm, wd_hbm, y_hbm,
                xbuf, obuf, wg_buf, wu_buf, wd_buf, wgu_s, wd_s, in_sem, out_sem, w_sem):
    s = pl.program_id(0)
    ns = pl.num_programs(0)
    cur = s % 2
    g_n = MOE_GROUP

    def in_copy(step, g, buf):
        rows = pl.ds(pl.multiple_of(sb_ref[step * g_n + g] * MOE_BLOCK, MOE_BLOCK), MOE_BLOCK)
        return pltpu.make_async_copy(
            hs_hbm.at[rows], xbuf.at[buf, pl.ds(g * MOE_BLOCK, MOE_BLOCK)], in_sem.at[buf])

    def out_copy(step, g, buf):
        rows = pl.ds(pl.multiple_of(sb_ref[step * g_n + g] * MOE_BLOCK, MOE_BLOCK), MOE_BLOCK)
        cols = pl.ds(pl.multiple_of(sr_ref[step * g_n + g] * D_MODEL, D_MODEL), D_MODEL)
        return pltpu.make_async_copy(
            obuf.at[buf, pl.ds(g * MOE_BLOCK, MOE_BLOCK)], y_hbm.at[rows, cols], out_sem.at[buf])

    def for_slots(step, flags_ref, fn):
        for g in range(g_n):
            @pl.when(flags_ref[step * g_n + g] == 1)
            def _():
                fn(g)

    @pl.when(s == 0)
    def _():
        xbuf[...] = jnp.zeros_like(xbuf)
        for_slots(0, si_ref, lambda g: in_copy(0, g, 0).start())

    @pl.when(s + 1 < ns)
    def _():
        for_slots(s + 1, si_ref, lambda g: in_copy(s + 1, g, 1 - cur).start())

    for_slots(s, si_ref, lambda g: in_copy(s, g, cur).wait())

    @pl.when(s >= 2)
    def _():
        for_slots(s - 2, so_ref, lambda g: out_copy(s - 2, g, cur).wait())

    def w_copies(expert, slot):
        return (pltpu.make_async_copy(wg_hbm.at[expert], wg_buf.at[slot], w_sem.at[slot]),
                pltpu.make_async_copy(wu_hbm.at[expert], wu_buf.at[slot], w_sem.at[slot]),
                pltpu.make_async_copy(wd_hbm.at[expert], wd_buf.at[slot], w_sem.at[slot]))

    @pl.when(s == 0)
    def _():
        for c in w_copies(se_ref[0], sp_ref[0]):
            c.start()

    @pl.when(sf_ref[s] == 1)
    def _():
        slot = sp_ref[s]
        for c in w_copies(se_ref[s], slot):
            c.wait()
        wgu_s[:, 0:D_EXPERT] = wg_buf[slot].astype(bf16)
        wgu_s[:, D_EXPERT:2 * D_EXPERT] = wu_buf[slot].astype(bf16)
        wd_s[...] = wd_buf[slot].astype(bf16)

        @pl.when(sn_ref[s] >= 0)
        def _():
            for c in w_copies(sn_ref[s], 1 - slot):
                c.start()

    def experts(n_blk):
        rows = n_blk * MOE_BLOCK
        u = xbuf[cur, 0:rows, 0:D_MODEL].astype(bf16)
        meta = xbuf[cur, 0:rows, D_MODEL:ROW_W]
        gate = jnp.concatenate(
            [jnp.where(sr_ref[s * g_n + g] == 0, meta[g * MOE_BLOCK:(g + 1) * MOE_BLOCK, 2:3],
                       meta[g * MOE_BLOCK:(g + 1) * MOE_BLOCK, 3:4]) for g in range(n_blk)], axis=0)
        gu = _dot(u, wgu_s[...])
        gt = gu[:, 0:D_EXPERT]
        hdn = (gt * jax.nn.sigmoid(gt) * gu[:, D_EXPERT:]).astype(bf16)
        obuf[cur, 0:rows, :] = _dot(hdn, wd_s[...]) * gate

    @pl.when((sk_ref[s] == 1) & (sh_ref[s] == 0))
    def _():
        experts(g_n)

    @pl.when((sk_ref[s] == 1) & (sh_ref[s] == 1))
    def _():
        experts(g_n // 2)

    @pl.when(sk_ref[s] == 0)
    def _():
        obuf[cur] = jnp.zeros(obuf.shape[1:], f32)

    for_slots(s, so_ref, lambda g: out_copy(s, g, cur).start())

    @pl.when(s == ns - 1)
    def _():
        for_slots(s, so_ref, lambda g: out_copy(s, g, cur).wait())

        @pl.when(s >= 1)
        def _():
            for_slots(s - 1, so_ref, lambda g: out_copy(s - 1, g, 1 - cur).wait())


def _moe_call(plan, hs, w_gate, w_up, w_down):
    n_steps = plan[0].shape[0]
    n_slots = hs.shape[0]
    rows = MOE_GROUP * MOE_BLOCK

    grid_spec = pltpu.PrefetchScalarGridSpec(
        num_scalar_prefetch=10,
        grid=(n_steps,),
        in_specs=[
            pl.BlockSpec(memory_space=pl.ANY),
            pl.BlockSpec(memory_space=pl.ANY),
            pl.BlockSpec(memory_space=pl.ANY),
            pl.BlockSpec(memory_space=pl.ANY),
        ],
        out_specs=pl.BlockSpec(memory_space=pl.ANY),
        scratch_shapes=[
            pltpu.VMEM((2, rows, ROW_W), f32),
            pltpu.VMEM((2, rows, D_MODEL), f32),
            pltpu.VMEM((2, D_MODEL, D_EXPERT), f32),
            pltpu.VMEM((2, D_MODEL, D_EXPERT), f32),
            pltpu.VMEM((2, D_EXPERT, D_MODEL), f32),
            pltpu.VMEM((D_MODEL, 2 * D_EXPERT), bf16),
            pltpu.VMEM((D_EXPERT, D_MODEL), bf16),
            pltpu.SemaphoreType.DMA((2,)),
            pltpu.SemaphoreType.DMA((2,)),
            pltpu.SemaphoreType.DMA((2,)),
        ],
    )
    return pl.pallas_call(
        _moe_kernel,
        grid_spec=grid_spec,
        out_shape=jax.ShapeDtypeStruct((n_slots, 2 * D_MODEL), f32),
        compiler_params=pltpu.CompilerParams(
            dimension_semantics=("arbitrary",), vmem_limit_bytes=VMEM_LIMIT),
        name="moe",
    )(*plan, hs, w_gate, w_up, w_down)


def _final_kernel(posc_ref, posn_ref, h_ref, gain_ref, y_hbm, o_ref, ybuf, sem):
    i = pl.program_id(0)
    cur = i % 2

    def issue(pos_ref, buf):
        def start(io, c):
            for r in range(ISSUE_UNROLL):
                ii = io * (ISSUE_UNROLL // SUBLANES) + r // SUBLANES
                pltpu.make_async_copy(y_hbm.at[pl.ds(pos_ref[io * ISSUE_UNROLL + r], 1)],
                                      ybuf.at[buf, ii, pl.ds(r % SUBLANES, 1)], sem.at[buf]).start()
            return c

        lax.fori_loop(0, FINAL_TILE // ISSUE_UNROLL, start, 0)

    @pl.when(i == 0)
    def _():
        issue(posc_ref, 0)

    @pl.when(i + 1 < pl.num_programs(0))
    def _():
        issue(posn_ref, 1 - cur)

    pltpu.make_async_copy(ybuf.at[cur], ybuf.at[cur], sem.at[cur]).wait()
    h = h_ref[...] + ybuf[cur, :, :, 0:D_MODEL] + ybuf[cur, :, :, D_MODEL:2 * D_MODEL]
    o_ref[...] = _rms(h, gain_ref[...])


def _final_call(pos, h1, gain, y):
    t = h1.shape[0]
    n = t // FINAL_TILE
    rows = FINAL_TILE // SUBLANES
    out = pl.pallas_call(
        _final_kernel,
        grid=(n,),
        in_specs=[
            pl.BlockSpec((FINAL_TILE,), lambda i: (i,), memory_space=pltpu.SMEM),
            pl.BlockSpec((FINAL_TILE,), lambda i: (jnp.minimum(i + 1, n - 1),), memory_space=pltpu.SMEM),
            pl.BlockSpec((rows, SUBLANES, D_MODEL), lambda i: (i, 0, 0)),
            pl.BlockSpec((1, 1, D_MODEL), lambda i: (0, 0, 0)),
            pl.BlockSpec(memory_space=pl.ANY),
        ],
        out_specs=pl.BlockSpec((rows, SUBLANES, D_MODEL), lambda i: (i, 0, 0)),
        out_shape=jax.ShapeDtypeStruct((t // SUBLANES, SUBLANES, D_MODEL), f32),
        scratch_shapes=[pltpu.VMEM((2, rows, SUBLANES, 2 * D_MODEL), f32), pltpu.SemaphoreType.DMA((2,))],
        compiler_params=pltpu.CompilerParams(
            dimension_semantics=("arbitrary",), vmem_limit_bytes=VMEM_LIMIT),
        name="final",
    )(pos, pos, h1.reshape(t // SUBLANES, SUBLANES, D_MODEL), gain.reshape(1, 1, D_MODEL), y)
    return out.reshape(t, D_MODEL)


def _rope_tables(pos):
    inv = ROPE_BASE ** (-jnp.arange(0, MLA_ROPE, 2, dtype=f32) / MLA_ROPE)
    ang = pos.astype(f32)[:, None] * inv[None, :]
    cos, sin = jnp.cos(ang), jnp.sin(ang)
    z = jnp.zeros((pos.shape[0], LANE - MLA_ROPE), f32)
    return jnp.concatenate([cos, cos, z], axis=1), jnp.concatenate([-sin, sin, z], axis=1)


def _relayout_weights(w_in, w_qb, w_kvb):
    half = MLA_ROPE // 2
    perm = (np.arange(MLA_ROPE) + half) % MLA_ROPE
    pts = np.cumsum((GLA_QK, GLA_QK, GLA_VW, GLA_VW, GLA_GATE_RANK, MLA_Q_RANK, MLA_KV_RANK, MLA_ROPE))
    q_g, k_g, v_g, r_g, a_l, q_lat, kv_lat, k_rope = jnp.split(w_in, pts[:-1], axis=1)
    a_seg = jnp.pad(a_l, ((0, 0), (0, LANE - GLA_GATE_RANK)))
    w_in_r = jnp.concatenate(
        [q_g, k_g, v_g, r_g, q_lat, kv_lat, k_rope, k_rope[:, perm], a_seg], axis=1).astype(bf16)
    qcols, kcols, vcols = [], [], []
    for h in range(MLA_HEADS):
        c = h * (MLA_NOPE + MLA_ROPE)
        rope = w_qb[:, c + MLA_NOPE:c + MLA_NOPE + MLA_ROPE]
        qcols += [w_qb[:, c:c + MLA_NOPE], rope, rope[:, perm]]
        c2 = h * (MLA_NOPE + MLA_V)
        kcols.append(w_kvb[:, c2:c2 + MLA_NOPE])
        vcols.append(w_kvb[:, c2 + MLA_NOPE:c2 + MLA_NOPE + MLA_V])
    return w_in_r, jnp.concatenate(qcols, axis=1).astype(bf16), jnp.concatenate(kcols + vcols, axis=1).astype(bf16)


_BUCKET_GROUP = np.arange(N_BUCKETS) // N_PAIRS
_RUN_EXPERT = np.concatenate([_BUCKET_GROUP * EXPERTS_PER_GROUP + _PAIR_LO[np.arange(N_BUCKETS) % N_PAIRS],
                              _BUCKET_GROUP * EXPERTS_PER_GROUP + _PAIR_HI[np.arange(N_BUCKETS) % N_PAIRS]])
_RUN_IS_EXPERT = (_RUN_EXPERT[:, None] == np.arange(N_EXPERTS)[None, :]).astype(np.int32)
_RUN_BEFORE = ((_RUN_EXPERT[:, None] == _RUN_EXPERT[None, :])
               & (np.arange(2 * N_BUCKETS)[None, :] < np.arange(2 * N_BUCKETS)[:, None])).astype(np.int32)


def _route_plan(counts, bucket, rank, n_tok):
    nt = counts.shape[0]
    g_n = MOE_GROUP
    tot = counts.sum(axis=0)
    nblk = (tot + MOE_BLOCK - 1) // MOE_BLOCK
    bstart_blk = jnp.cumsum(nblk) - nblk
    n_blocks = jnp.sum(nblk)
    tile_base = bstart_blk[None, :] * MOE_BLOCK + jnp.cumsum(counts, axis=0) - counts
    hit = bucket.reshape(nt, -1, 1) == jnp.arange(N_BUCKETS, dtype=i32)
    pos = jnp.sum(jnp.where(hit, tile_base[:, None, :], 0), axis=-1).reshape(-1) + rank
    nb_max = (n_tok + N_BUCKETS * (MOE_BLOCK - 1)) // MOE_BLOCK

    n_run = jnp.concatenate([nblk, nblk])
    b0_run = jnp.concatenate([bstart_blk, bstart_blk])
    c_e = jnp.sum(n_run[:, None] * _RUN_IS_EXPERT, axis=0)
    g_e = (c_e + g_n - 1) // g_n
    gend = jnp.cumsum(g_e)
    gstart = gend - g_e
    n_compute = gend[-1]
    off_run = jnp.sum(_RUN_BEFORE * n_run[None, :], axis=1)
    f_run = jnp.sum(_RUN_IS_EXPERT * gstart[None, :], axis=1) * g_n + off_run

    n_steps = (2 * nb_max + N_EXPERTS * (g_n - 1) + g_n - 1) // g_n + 1
    f = jnp.arange(n_steps * g_n, dtype=i32)
    in_run = (f[:, None] >= f_run[None, :]) & (f[:, None] < (f_run + n_run)[None, :])
    valid_c = jnp.any(in_run, axis=1)
    block_c = jnp.sum(jnp.where(in_run, b0_run[None, :] + f[:, None] - f_run[None, :], 0), axis=1)
    role_c = jnp.sum(jnp.where(in_run[:, N_BUCKETS:], 1, 0), axis=1)
    u_idx = f - n_compute * g_n
    valid_f = (u_idx >= 0) & (u_idx < 2 * (nb_max - n_blocks))
    slot_block = jnp.where(valid_c, block_c, jnp.where(valid_f, n_blocks + u_idx // 2, 0))
    slot_role = jnp.where(valid_c, role_c, jnp.where(valid_f, u_idx % 2, 0))

    step = jnp.arange(n_steps, dtype=i32)
    e_of_step = jnp.minimum(jnp.sum(gend[None, :] <= step[:, None], axis=1), N_EXPERTS - 1)
    is_compute = step < n_compute
    last_e = jnp.max(jnp.where(is_compute, e_of_step, 0))
    step_expert = jnp.where(is_compute, e_of_step, last_e)
    step_first = jnp.concatenate([jnp.ones((1,), bool), step_expert[1:] != step_expert[:-1]])
    ordinal = jnp.cumsum(step_first.astype(i32)) - 1
    ords = jnp.arange(N_EXPERTS + 1, dtype=i32)
    expert_of_ord = jnp.sum(jnp.where(step_first[:, None] & (ordinal[:, None] == ords[None, :]),
                                      step_expert[:, None], 0), axis=0)
    has_next = ordinal + 1 <= ordinal[-1]
    next_expert = jnp.sum(jnp.where(ords[None, :] == ordinal[:, None] + 1, expert_of_ord[None, :], 0), axis=1)
    step_next = jnp.where(step_first & has_next, next_expert, -1)
    used = jnp.sum(valid_c.reshape(n_steps, g_n).astype(i32), axis=1)
    step_half = is_compute & (used <= g_n // 2)
    plan = tuple(a.astype(i32) for a in
                 (step_expert, step_first, is_compute, ordinal % 2, step_next, step_half,
                  slot_block, slot_role, valid_c, valid_c | valid_f))
    last_blk = jnp.where(nblk > 0, bstart_blk + nblk - 1, -1)
    spare = n_blocks + jnp.arange(nb_max - n_tok // MOE_BLOCK, dtype=i32)
    zero_blocks = jnp.concatenate([last_blk, jnp.where(spare < nb_max, spare, -1)]).astype(i32)
    return pos.astype(i32), plan, zero_blocks, nb_max


def kernel(x, meta_tokens, mix_norm, w_in, gla_w_a2, gla_b_a, gla_out_norm, mla_q_norm, mla_w_qb, mla_kv_norm,
           mla_w_kvb, w_out, ffn_norm, router_group_w, router_group_b, router_expert_w, router_expert_b,
           expert_w_gate, expert_w_up, expert_w_down, final_norm):
    batch, seq, d = x.shape
    assert PREP_TILE == ATT_TILE
    assert d == D_MODEL and seq % max(PREP_TILE, GLA_TILE, ATT_TILE) == 0
    assert (batch * seq) % max(OUT_TILE, SCATTER_TILE, FINAL_TILE) == 0 and batch % GLA_BATCH == 0
    n_tok = batch * seq
    x2d = x.reshape(n_tok, d)

    w_in_r, w_qb_r, w_kvb_r = _relayout_weights(w_in[0], mla_w_qb[0], mla_w_kvb[0])
    mixg = mix_norm[0].reshape(1, d)
    qn = mla_q_norm[0].reshape(1, MLA_Q_RANK)
    kvn = mla_kv_norm[0].reshape(1, MLA_KV_RANK)
    ct_m, st_m = _rope_tables(jnp.arange(META_TILE))
    ct_x, st_x = _rope_tables(N_META + jnp.arange(seq))

    x_meta = jnp.pad(meta_tokens.astype(f32), ((0, META_TILE - N_META), (0, 0)))
    _, kg_m, vg_m, _, a_m, _, km_m, vmt_m = _prep_call(
        x_meta, META_TILE, META_TILE, mixg, w_in_r, qn, w_qb_r, kvn, w_kvb_r, ct_m, st_m)
    qg, kg, vg, rg, ag, qm, km, vmt = _prep_call(
        x2d, seq, PREP_TILE, mixg, w_in_r, qn, w_qb_r, kvn, w_kvb_r, ct_x, st_x)

    def chunk0(a):
        return jnp.pad(a[:N_META], ((CHUNK - N_META, 0), (0, 0)))

    wa2_p = jnp.pad(gla_w_a2[0], ((0, LANE - GLA_GATE_RANK), (0, 0))).astype(bf16)
    y_gla = _gla_call(qg, kg, vg, rg, ag, chunk0(kg_m), chunk0(vg_m), chunk0(a_m),
                      wa2_p, gla_b_a[0].reshape(1, GLA_QK), gla_out_norm[0].reshape(1, GLA_VW), batch, seq)
    y_mla = _mla_call(qm, km, vmt, km_m[:N_META], vmt_m[0, :, :N_META], batch, seq)

    wo = w_out[0].astype(bf16)
    rw = jnp.concatenate([router_group_w[0], router_expert_w[0],
                          jnp.zeros((d, LANE - N_GROUPS - N_EXPERTS), f32)], axis=1)
    rb = jnp.concatenate([router_group_b[0], router_expert_b[0],
                          jnp.zeros((LANE - N_GROUPS - N_EXPERTS,), f32)]).reshape(1, LANE)
    h1, ux, cnt, routes = _outproj_call(x2d, y_gla, y_mla, wo[:GLA_VW], wo[GLA_VW:], ffn_norm[0].reshape(1, d),
                                      rw.T.astype(bf16), rb.reshape(LANE, 1))

    counts = cnt.reshape(-1, BUCKET_LANES)[:, :N_BUCKETS].astype(i32)
    tok_bucket = routes[:, 0, :].reshape(-1).astype(i32)
    tok_rank = routes[:, 1, :].reshape(-1).astype(i32)
    pos, plan, zero_blocks, nb_max = _route_plan(counts, tok_bucket, tok_rank, n_tok)
    n_slots = nb_max * MOE_BLOCK
    hs = _scatter_call(pos, zero_blocks, ux, n_slots)
    y = _moe_call(plan, hs, expert_w_gate[0], expert_w_up[0], expert_w_down[0])
    out = _final_call(pos, h1, final_norm.reshape(1, d), y)
    return out.reshape(batch, seq, d)
```

```python
import functools

import numpy as np
import jax
import jax.numpy as jnp
from jax import lax
from jax.experimental import pallas as pl
from jax.experimental.pallas import tpu as pltpu

f32 = jnp.float32
bf16 = jnp.bfloat16
i32 = jnp.int32

D_MODEL = 1024
CHUNK = 64
N_META = 16
EPS = 1e-6
GLA_HEADS = 4
GLA_DK = 64
GLA_DV = 128
GLA_GATE_RANK = 16
GLA_TAU = 16.0
GLA_QK = GLA_HEADS * GLA_DK
GLA_VW = GLA_HEADS * GLA_DV
MLA_HEADS = 4
MLA_Q_RANK = 256
MLA_KV_RANK = 128
MLA_NOPE = 128
MLA_ROPE = 64
MLA_V = 128
MLA_OUT = MLA_HEADS * MLA_V
MLA_QK_PAD = 256
MLA_VA = MLA_V + 16
LOG2_E = 1.4426950408889634
ROPE_BASE = 10000.0
N_GROUPS = 8
EXPERTS_PER_GROUP = 8
N_EXPERTS = N_GROUPS * EXPERTS_PER_GROUP
D_EXPERT = 512
N_PAIRS = EXPERTS_PER_GROUP * (EXPERTS_PER_GROUP - 1) // 2
N_BUCKETS = N_GROUPS * N_PAIRS
BUCKET_LANES = 256
LANE = 128
SUBLANES = 8
META_W = LANE
ROW_W = D_MODEL + META_W

PREP_TILE = 512
GLA_TILE = 512
GLA_BATCH = 4
ATT_TILE = 512
ATT_HEADS = 4
META_TILE = 128
OUT_TILE = 1024
ROUTE_ROWS = 256
ROUTE_GROUP = 512
SCATTER_TILE = 2048
FINAL_TILE = 512
ISSUE_UNROLL = 64
MOE_BLOCK = 32
MOE_GROUP = 16
VMEM_LIMIT = 56 * 1024 * 1024

C_Q, C_K, C_V, C_R = 0, 256, 512, 1024
C_QLAT, C_KVLAT, C_KROPE, C_A, C_END = 1536, 1792, 1920, 2048, 2176

_TILE_POS = np.arange(GLA_TILE)
_CHUNK_PREFIX = ((_TILE_POS[:, None] // CHUNK == _TILE_POS[None, :] // CHUNK)
                 & (_TILE_POS[None, :] <= _TILE_POS[:, None])).astype(np.float32)
_PAIR_LO = np.array([lo for lo in range(8) for hi in range(lo + 1, 8)], np.int32)
_PAIR_HI = np.array([hi for lo in range(8) for hi in range(lo + 1, 8)], np.int32)


def _dot(a, b):
    return jnp.dot(a, b, preferred_element_type=f32)


def _dot_nt(a, b):
    return lax.dot_general(a, b, (((1,), (1,)), ((), ())), preferred_element_type=f32)


def _dot_tn(a, b):
    return lax.dot_general(a, b, (((0,), (0,)), ((), ())), preferred_element_type=f32)


def _rms(x, gain):
    return x * lax.rsqrt(jnp.mean(x * x, axis=-1, keepdims=True) + EPS) * gain


def _split3(x):
    hi = x.astype(bf16)
    r1 = x - hi.astype(f32)
    mid = r1.astype(bf16)
    lo = (r1 - mid.astype(f32)).astype(bf16)
    return hi, mid, lo


def _prep_kernel(x_ref, g_ref, win_ref, qn_ref, wqb_ref, kvn_ref, wkvb_ref, ct_ref, st_ref,
                 qg_ref, kg_ref, vg_ref, rg_ref, a_ref, qm_ref, km_ref, vmt_ref):
    u = _rms(x_ref[...], g_ref[...]).astype(bf16)

    def proj(lo, hi):
        return _dot(u, win_ref[:, lo:hi])

    qg_ref[...] = proj(C_Q, C_K).astype(bf16)
    kg_ref[...] = proj(C_K, C_V).astype(bf16)
    vg_ref[...] = proj(C_V, C_R).astype(bf16)
    rg_ref[...] = proj(C_R, C_QLAT).astype(bf16)
    z = proj(C_QLAT, C_END)
    a_ref[...] = z[:, C_A - C_QLAT:].astype(bf16)
    ctab = ct_ref[...]
    stab = st_ref[...]

    def rope(seg):
        return seg * ctab + pltpu.roll(seg, 64, axis=1) * stab

    k_rope = rope(z[:, C_KROPE - C_QLAT:C_A - C_QLAT]).astype(bf16)
    qn = _rms(z[:, 0:MLA_Q_RANK], qn_ref[...]).astype(bf16)
    kvn = _rms(z[:, MLA_Q_RANK:MLA_Q_RANK + MLA_KV_RANK], kvn_ref[...]).astype(bf16)
    scale = (MLA_NOPE + MLA_ROPE) ** -0.5 * LOG2_E
    qf = _dot(qn, wqb_ref[...])
    kvf = _dot(kvn, wkvb_ref[...])
    for h in range(MLA_HEADS):
        c = h * MLA_QK_PAD
        qm_ref[:, c:c + LANE] = (qf[:, c:c + LANE] * scale).astype(bf16)
        qm_ref[:, c + LANE:c + 2 * LANE] = (rope(qf[:, c + LANE:c + 2 * LANE]) * scale).astype(bf16)
        km_ref[:, c:c + LANE] = kvf[:, h * LANE:(h + 1) * LANE].astype(bf16)
        km_ref[:, c + LANE:c + 2 * LANE] = k_rope
    vt = kvf[:, MLA_HEADS * MLA_NOPE:].T
    for h in range(MLA_HEADS):
        vmt_ref[h * MLA_VA:h * MLA_VA + MLA_V, :] = vt[h * MLA_V:(h + 1) * MLA_V].astype(bf16)
        vmt_ref[h * MLA_VA + MLA_V:(h + 1) * MLA_VA, :] = jnp.ones((MLA_VA - MLA_V, vt.shape[1]), bf16)


def _prep_call(x2d, rows_per_seq, tile, gain, w_in_r, q_norm, w_qb_r, kv_norm, w_kvb_r, ctab, stab):
    t = x2d.shape[0]
    nj = rows_per_seq // tile
    grid = (t // rows_per_seq, nj)

    def row(b, j):
        return (b * nj + j, 0)

    def const(b, j):
        return (0, 0)

    def tab(b, j):
        return (j, 0)

    widths = (GLA_QK, GLA_QK, GLA_VW, GLA_VW, LANE, MLA_HEADS * MLA_QK_PAD, MLA_HEADS * MLA_QK_PAD)
    return pl.pallas_call(
        _prep_kernel,
        grid=grid,
        in_specs=[
            pl.BlockSpec((tile, D_MODEL), row),
            pl.BlockSpec((1, D_MODEL), const),
            pl.BlockSpec((D_MODEL, C_END), const),
            pl.BlockSpec((1, MLA_Q_RANK), const),
            pl.BlockSpec((MLA_Q_RANK, MLA_HEADS * MLA_QK_PAD), const),
            pl.BlockSpec((1, MLA_KV_RANK), const),
            pl.BlockSpec((MLA_KV_RANK, 2 * MLA_OUT), const),
            pl.BlockSpec((tile, LANE), tab),
            pl.BlockSpec((tile, LANE), tab),
        ],
        out_specs=[pl.BlockSpec((tile, w), row) for w in widths]
        + [pl.BlockSpec((None, MLA_HEADS * MLA_VA, tile), lambda b, j: (b * nj + j, 0, 0))],
        out_shape=[jax.ShapeDtypeStruct((t, w), bf16) for w in widths]
        + [jax.ShapeDtypeStruct((t // tile, MLA_HEADS * MLA_VA, tile), bf16)],
        compiler_params=pltpu.CompilerParams(
            dimension_semantics=("parallel", "parallel"), vmem_limit_bytes=VMEM_LIMIT),
        name="prep",
    )(x2d, gain, w_in_r, q_norm, w_qb_r, kv_norm, w_kvb_r, ctab, stab)


def _gla_log_decay(a, wa2_ref, ba_ref):
    s = _dot(a, wa2_ref[...]) + ba_ref[...]
    return (jnp.minimum(s, 0.0) - jnp.log(1.0 + jnp.exp(-jnp.abs(s)))) * (1.0 / GLA_TAU)


def _gla_front(q, k, v, la, tri, want_out):
    nc = la.shape[0] // CHUNK
    hi, mid, lo = _split3(la)
    b = _dot(tri, hi) + _dot(tri, mid) + _dot(tri, lo)
    b_last = [b[(c + 1) * CHUNK - 1:(c + 1) * CHUNK, :] for c in range(nc)]
    b_last_full = jnp.concatenate([jnp.broadcast_to(bl, (CHUNK, GLA_QK)) for bl in b_last], axis=0)
    kf = k.astype(f32)
    front = dict(v=v, b_last=b_last, kd=(kf * jnp.exp(b_last_full - b)).astype(bf16))
    if want_out:
        front.update(qe=(q.astype(f32) * (GLA_DK ** -0.5) * jnp.exp(b)).astype(bf16),
                     ke=kf * jnp.exp(-b), vf=v.astype(f32))
    return front


def _gla_chunks(front, st_ref, want_out):
    v, kd, b_last = front["v"], front["kd"], front["b_last"]
    rr = lax.broadcasted_iota(i32, (GLA_VW, GLA_QK), 0) // GLA_DV
    cc = lax.broadcasted_iota(i32, (GLA_VW, GLA_QK), 1) // GLA_DK
    if want_out:
        qe, ke, vf = front["qe"], front["ke"], front["vf"]
        lane_h = lax.broadcasted_iota(i32, (CHUNK, GLA_QK), 1) // GLA_DK
        vlane_h = lax.broadcasted_iota(i32, (CHUNK, GLA_VW), 1) // GLA_DV
        a_row = lax.broadcasted_iota(i32, (CHUNK, GLA_QK), 0)
        a_col = lax.broadcasted_iota(i32, (CHUNK, GLA_QK), 1) % CHUNK
    outs = []
    st = st_ref[...]
    for c in range(len(b_last)):
        rows = slice(c * CHUNK, (c + 1) * CHUNK)
        upd = jnp.where(rr == cc, _dot_tn(v[rows], kd[rows]), 0.0)
        if want_out:
            kbd = jnp.concatenate(
                [jnp.where(lane_h == h, ke[rows], 0.0) for h in range(GLA_HEADS)], axis=0).astype(bf16)
            att = jnp.where(a_col <= a_row, _dot_nt(qe[rows], kbd), 0.0).astype(bf16)
            vbd = jnp.concatenate(
                [jnp.where(vlane_h == h, vf[rows], 0.0) for h in range(GLA_HEADS)], axis=0).astype(bf16)
            outs.append(_dot(att, vbd) + _dot_nt(qe[rows], st.astype(bf16)))
        st = st * jnp.exp(b_last[c]) + upd
    st_ref[...] = st
    return jnp.concatenate(outs, axis=0) if want_out else None


def _gla_kernel(q_ref, k_ref, v_ref, r_ref, a_ref, km_ref, vm_ref, am_ref, wa2_ref, ba_ref, gain_ref, tri_ref,
                y_ref, st_ref):
    j = pl.program_id(1)

    @pl.when(j == 0)
    def _():
        st_ref[...] = jnp.zeros_like(st_ref)
        la = _gla_log_decay(am_ref[...], wa2_ref, ba_ref)
        row = lax.broadcasted_iota(i32, la.shape, 0)
        la = jnp.where(row >= CHUNK - N_META, la, 0.0)
        front = _gla_front(None, km_ref[...], vm_ref[...], la, tri_ref[0:CHUNK, 0:CHUNK], False)
        _gla_chunks(front, st_ref.at[0], False)
        for bb in range(1, GLA_BATCH):
            st_ref[bb] = st_ref[0]

    fronts = [_gla_front(q_ref[bb], k_ref[bb], v_ref[bb], _gla_log_decay(a_ref[bb], wa2_ref, ba_ref),
                         tri_ref[...], True) for bb in range(GLA_BATCH)]
    for bb in range(GLA_BATCH):
        o = _gla_chunks(fronts[bb], st_ref.at[bb], True)
        r = r_ref[bb].astype(f32)
        outs = []
        for h in range(GLA_HEADS):
            oh = o[:, h * GLA_DV:(h + 1) * GLA_DV]
            outs.append(oh * lax.rsqrt(jnp.mean(oh * oh, axis=-1, keepdims=True) + EPS))
        on = jnp.concatenate(outs, axis=1) * gain_ref[...]
        y_ref[bb] = (on * (r * jax.nn.sigmoid(r))).astype(bf16)


def _gla_call(qg, kg, vg, rg, ag, km, vm, am, wa2_p, b_a, gain, batch, seq):
    nj = seq // GLA_TILE

    def row(b, j):
        return (b, j, 0)

    def const(b, j):
        return (0, 0)

    def seqs(a):
        return a.reshape(batch, seq, a.shape[-1])

    out = pl.pallas_call(
        _gla_kernel,
        grid=(batch // GLA_BATCH, nj),
        in_specs=[
            pl.BlockSpec((GLA_BATCH, GLA_TILE, GLA_QK), row),
            pl.BlockSpec((GLA_BATCH, GLA_TILE, GLA_QK), row),
            pl.BlockSpec((GLA_BATCH, GLA_TILE, GLA_VW), row),
            pl.BlockSpec((GLA_BATCH, GLA_TILE, GLA_VW), row),
            pl.BlockSpec((GLA_BATCH, GLA_TILE, LANE), row),
            pl.BlockSpec((CHUNK, GLA_QK), const),
            pl.BlockSpec((CHUNK, GLA_VW), const),
            pl.BlockSpec((CHUNK, LANE), const),
            pl.BlockSpec((LANE, GLA_QK), const),
            pl.BlockSpec((1, GLA_QK), const),
            pl.BlockSpec((1, GLA_VW), const),
            pl.BlockSpec((GLA_TILE, GLA_TILE), const),
        ],
        out_specs=pl.BlockSpec((GLA_BATCH, GLA_TILE, GLA_VW), row),
        out_shape=jax.ShapeDtypeStruct((batch, seq, GLA_VW), bf16),
        scratch_shapes=[pltpu.VMEM((GLA_BATCH, GLA_VW, GLA_QK), f32)],
        compiler_params=pltpu.CompilerParams(
            dimension_semantics=("parallel", "arbitrary"), vmem_limit_bytes=VMEM_LIMIT),
        name="gla",
    )(seqs(qg), seqs(kg), seqs(vg), seqs(rg), seqs(ag), km, vm, am, wa2_p, b_a, gain,
      jnp.asarray(_CHUNK_PREFIX, bf16))
    return out.reshape(batch * seq, GLA_VW)


def _mla_kernel(q_ref, k_ref, vt_ref, km_ref, vmt_ref, o_ref, sa_ref, sb_ref):
    i = pl.program_id(2)
    tq = ATT_TILE
    w = MLA_QK_PAD
    va = MLA_VA
    heads = range(ATT_HEADS)

    def scores(h, blk):
        rows = pl.ds(pl.multiple_of(blk * tq, tq), tq)
        return _dot_nt(k_ref[rows, h * w:(h + 1) * w], q_ref[:, h * w:(h + 1) * w])

    def soft(s, vtb, carry, mask=None):
        m, acc = carry
        if mask is not None:
            s = jnp.where(mask, s, -1e30)
        m_new = jnp.maximum(m, jnp.max(s, axis=0, keepdims=True))
        p = jnp.exp2(s - m_new).astype(bf16)
        return m_new, jnp.exp2(m - m_new) * acc + _dot(vtb, p)

    def vt(h, blk):
        return vt_ref[blk, h * va:(h + 1) * va, :]

    def finish(carries):
        ss = [_dot_nt(km_ref[:, h * w:(h + 1) * w], q_ref[:, h * w:(h + 1) * w]) for h in heads]
        accs = [soft(ss[h], vmt_ref[h * va:(h + 1) * va, :], carries[h])[1] for h in heads]
        for h in heads:
            acc = accs[h]
            o_ref[:, h * MLA_V:(h + 1) * MLA_V] = (acc[:MLA_V] * (1.0 / acc[MLA_V:MLA_V + 1])).T.astype(bf16)

    kc = lax.broadcasted_iota(i32, (tq, tq), 0) // CHUNK
    qc = lax.broadcasted_iota(i32, (tq, tq), 1) // CHUNK
    mask = kc <= qc

    for h in heads:
        sa_ref[h] = scores(h, 0)

    def pair(p, carries):
        b0 = 2 * p
        for h in heads:
            sb_ref[h] = scores(h, b0 + 1)
        carries = [soft(sa_ref[h], vt(h, b0), carries[h]) for h in heads]
        for h in heads:
            sa_ref[h] = scores(h, b0 + 2)
        return tuple(soft(sb_ref[h], vt(h, b0 + 1), carries[h]) for h in heads)

    init = tuple((jnp.full((1, tq), -1e30, f32), jnp.zeros((va, tq), f32)) for _ in heads)
    carries = lax.fori_loop(0, i // 2, pair, init)

    @pl.when(i % 2 == 1)
    def _():
        for h in heads:
            sb_ref[h] = scores(h, i)
        c1 = [soft(sa_ref[h], vt(h, i - 1), carries[h]) for h in heads]
        finish([soft(sb_ref[h], vt(h, i), c1[h], mask) for h in heads])

    @pl.when(i % 2 == 0)
    def _():
        finish([soft(sa_ref[h], vt(h, i), carries[h], mask) for h in heads])


def _mla_call(qm, km, vmt, km_meta, vmt_meta, batch, seq):
    nq = seq // ATT_TILE
    nh = ATT_HEADS
    qm3 = qm.reshape(batch, seq, MLA_HEADS * MLA_QK_PAD)
    km3 = km.reshape(batch, seq, MLA_HEADS * MLA_QK_PAD)
    vt4 = vmt.reshape(batch, nq, MLA_HEADS * MLA_VA, ATT_TILE)
    out = pl.pallas_call(
        _mla_kernel,
        grid=(batch, MLA_HEADS // nh, nq),
        in_specs=[
            pl.BlockSpec((None, ATT_TILE, nh * MLA_QK_PAD), lambda b, h, i: (b, i, h)),
            pl.BlockSpec((None, seq, nh * MLA_QK_PAD), lambda b, h, i: (b, 0, h)),
            pl.BlockSpec((None, nq, nh * MLA_VA, ATT_TILE), lambda b, h, i: (b, 0, h, 0)),
            pl.BlockSpec((N_META, nh * MLA_QK_PAD), lambda b, h, i: (0, h)),
            pl.BlockSpec((nh * MLA_VA, N_META), lambda b, h, i: (h, 0)),
        ],
        out_specs=pl.BlockSpec((None, ATT_TILE, nh * MLA_V), lambda b, h, i: (b, i, h)),
        out_shape=jax.ShapeDtypeStruct((batch, seq, MLA_OUT), bf16),
        scratch_shapes=[pltpu.VMEM((nh, ATT_TILE, ATT_TILE), f32), pltpu.VMEM((nh, ATT_TILE, ATT_TILE), f32)],
        compiler_params=pltpu.CompilerParams(
            dimension_semantics=("parallel", "parallel", "arbitrary"), vmem_limit_bytes=VMEM_LIMIT),
        name="mla",
    )(qm3, km3, vt4, km_meta, vmt_meta)
    return out.reshape(batch * seq, MLA_OUT)


def _route_cols(lt):
    r = lt.shape[1]
    neg = -1e30
    gl = lt[0:N_GROUPS, :]
    gsub = lax.broadcasted_iota(i32, (N_GROUPS, r), 0)
    gmax = jnp.max(gl, axis=0, keepdims=True)
    g_p = 1.0 / jnp.sum(jnp.exp(gl - gmax), axis=0, keepdims=True)
    g_idx = jnp.min(jnp.where(gl == gmax, gsub, N_GROUPS), axis=0, keepdims=True)
    el_all = lt[N_GROUPS:N_GROUPS + N_EXPERTS, :]
    esub = lax.broadcasted_iota(i32, (N_EXPERTS, r), 0)
    base = g_idx * EXPERTS_PER_GROUP
    e_mask = (esub >= base) & (esub < base + EXPERTS_PER_GROUP)
    el = jnp.where(e_mask, el_all, neg)
    m1 = jnp.max(el, axis=0, keepdims=True)
    i1 = jnp.min(jnp.where(e_mask & (el == m1), esub, N_EXPERTS), axis=0, keepdims=True)
    el2 = jnp.where(esub == i1, neg, el)
    m2 = jnp.max(el2, axis=0, keepdims=True)
    i2 = jnp.min(jnp.where(e_mask & (esub != i1) & (el2 == m2), esub, N_EXPERTS), axis=0, keepdims=True)
    rr = jnp.exp(m2 - m1)
    ga = g_p / (1.0 + rr)
    gb = g_p * rr / (1.0 + rr)
    la_ = i1 - base
    lb_ = i2 - base
    lo = jnp.minimum(la_, lb_)
    hi = jnp.maximum(la_, lb_)
    g_lo = jnp.where(la_ < lb_, ga, gb)
    g_hi = jnp.where(la_ < lb_, gb, ga)
    pidx = ((lo * (2 * EXPERTS_PER_GROUP - 1 - lo)) >> 1) + (hi - lo - 1)
    bucket = g_idx * N_PAIRS + pidx
    bsub = lax.broadcasted_iota(i32, (BUCKET_LANES, r), 0)
    oht = jnp.where(bsub == bucket, 1.0, 0.0)
    ohb = oht.astype(bf16)
    ri = lax.broadcasted_iota(i32, (ROUTE_ROWS, ROUTE_ROWS), 0)
    ci = lax.broadcasted_iota(i32, (ROUTE_ROWS, ROUTE_ROWS), 1)
    before = jnp.where(ri < ci, 1.0, 0.0).astype(bf16)
    ones = jnp.ones((SUBLANES, ROUTE_ROWS), bf16)
    subs = [slice(i * ROUTE_ROWS, (i + 1) * ROUTE_ROWS) for i in range(r // ROUTE_ROWS)]
    cum = jnp.concatenate([_dot(ohb[:, sl], before) for sl in subs], axis=1)
    rank = jnp.sum(oht * cum, axis=0, keepdims=True)
    counts = [_dot_nt(ones, ohb[:, sl])[0:1, :] for sl in subs]
    msub = lax.broadcasted_iota(i32, (LANE, r), 0)
    meta_t = jnp.where(msub == 0, bucket.astype(f32),
                       jnp.where(msub == 1, rank,
                                 jnp.where(msub == 2, g_lo, jnp.where(msub == 3, g_hi, 0.0))))
    return meta_t.T, counts, meta_t[0:SUBLANES, :]


def _outproj_kernel(x_ref, yg_ref, ym_ref, wog_ref, wom_ref, gain_ref, wrt_ref, rb_ref,
                    ux_ref, cnt_ref, rt_ref):
    per_group = ROUTE_GROUP // ROUTE_ROWS
    for grp in range(OUT_TILE // ROUTE_GROUP):
        rows = slice(grp * ROUTE_GROUP, (grp + 1) * ROUTE_GROUP)
        h1 = x_ref[rows, :] + _dot(yg_ref[rows, :], wog_ref[...]) + _dot(ym_ref[rows, :], wom_ref[...])
        ux_ref[rows, 0:D_MODEL] = h1
        u2 = _rms(h1, gain_ref[...])
        lt = _dot_nt(wrt_ref[...], u2.astype(bf16)) + rb_ref[...]
        meta, counts, routes = _route_cols(lt)
        ux_ref[rows, D_MODEL:ROW_W] = meta
        for i in range(per_group):
            cnt_ref[grp * per_group + i] = counts[i]
            rt_ref[grp * per_group + i] = routes[:, i * ROUTE_ROWS:(i + 1) * ROUTE_ROWS]


def _outproj_call(x2d, yg, ym, wo_g, wo_m, gain, w_r, rbias):
    t = x2d.shape[0]
    nt = t // OUT_TILE

    def row(i):
        return (i, 0)

    def const(i):
        return (0, 0)

    return pl.pallas_call(
        _outproj_kernel,
        grid=(nt,),
        in_specs=[
            pl.BlockSpec((OUT_TILE, D_MODEL), row),
            pl.BlockSpec((OUT_TILE, GLA_VW), row),
            pl.BlockSpec((OUT_TILE, MLA_OUT), row),
            pl.BlockSpec((GLA_VW, D_MODEL), const),
            pl.BlockSpec((MLA_OUT, D_MODEL), const),
            pl.BlockSpec((1, D_MODEL), const),
            pl.BlockSpec((LANE, D_MODEL), const),
            pl.BlockSpec((LANE, 1), const),
        ],
        out_specs=[
            pl.BlockSpec((OUT_TILE, ROW_W), row),
            pl.BlockSpec((OUT_TILE // ROUTE_ROWS, 1, BUCKET_LANES), lambda i: (i, 0, 0)),
            pl.BlockSpec((OUT_TILE // ROUTE_ROWS, SUBLANES, ROUTE_ROWS), lambda i: (i, 0, 0)),
        ],
        out_shape=[
            jax.ShapeDtypeStruct((t, ROW_W), f32),
            jax.ShapeDtypeStruct((nt * (OUT_TILE // ROUTE_ROWS), 1, BUCKET_LANES), f32),
            jax.ShapeDtypeStruct((nt * (OUT_TILE // ROUTE_ROWS), SUBLANES, ROUTE_ROWS), f32),
        ],
        compiler_params=pltpu.CompilerParams(
            dimension_semantics=("parallel",), vmem_limit_bytes=VMEM_LIMIT),
        name="outproj",
    )(x2d, yg, ym, wo_g, wo_m, gain, w_r, rbias)


def _scatter_kernel(pos_ref, zb_ref, ux_ref, hs_ref, zbuf, sem, zsem):
    @pl.when(pl.program_id(0) == 0)
    def _():
        zbuf[...] = jnp.zeros_like(zbuf)

        def zero_copy(j):
            rows = pl.ds(pl.multiple_of(zb_ref[j] * MOE_BLOCK, MOE_BLOCK), MOE_BLOCK)
            return pltpu.make_async_copy(zbuf, hs_ref.at[rows], zsem)

        def zstart(j, c):
            @pl.when(zb_ref[j] >= 0)
            def _():
                zero_copy(j).start()
            return c

        def zwait(j, c):
            @pl.when(zb_ref[j] >= 0)
            def _():
                zero_copy(j).wait()
            return c

        lax.fori_loop(0, zb_ref.shape[0], zstart, 0)
        lax.fori_loop(0, zb_ref.shape[0], zwait, 0)

    def start(io, c):
        for r in range(ISSUE_UNROLL):
            ii = io * (ISSUE_UNROLL // SUBLANES) + r // SUBLANES
            pltpu.make_async_copy(ux_ref.at[ii, pl.ds(r % SUBLANES, 1)],
                                  hs_ref.at[pl.ds(pos_ref[io * ISSUE_UNROLL + r], 1)], sem).start()
        return c

    lax.fori_loop(0, SCATTER_TILE // ISSUE_UNROLL, start, 0)
    pltpu.make_async_copy(hs_ref.at[pl.ds(0, SCATTER_TILE)], hs_ref.at[pl.ds(0, SCATTER_TILE)], sem).wait()


def _scatter_call(pos, zero_blocks, ux, n_slots):
    t = ux.shape[0]
    nz = zero_blocks.shape[0]
    return pl.pallas_call(
        _scatter_kernel,
        grid=(t // SCATTER_TILE,),
        in_specs=[
            pl.BlockSpec((SCATTER_TILE,), lambda i: (i,), memory_space=pltpu.SMEM),
            pl.BlockSpec((nz,), lambda i: (0,), memory_space=pltpu.SMEM),
            pl.BlockSpec((SCATTER_TILE // SUBLANES, SUBLANES, ROW_W), lambda i: (i, 0, 0)),
        ],
        out_specs=pl.BlockSpec(memory_space=pl.ANY),
        out_shape=jax.ShapeDtypeStruct((n_slots, ROW_W), f32),
        scratch_shapes=[pltpu.VMEM((MOE_BLOCK, ROW_W), f32), pltpu.SemaphoreType.DMA(()),
                        pltpu.SemaphoreType.DMA(())],
        compiler_params=pltpu.CompilerParams(
            dimension_semantics=("arbitrary",), vmem_limit_bytes=VMEM_LIMIT),
        name="scatter",
    )(pos, zero_blocks, ux.reshape(t // SUBLANES, SUBLANES, ROW_W))


def _moe_kernel(se_ref, sf_ref, sk_ref, sp_ref, sn_ref, sb_ref, sr_ref, si_ref, so_ref,
                hs_hbm, wg_hbm, wu_hbm, wd_hbm, fg_ref, y_hbm,
                xbuf, obuf, wg_buf, wu_buf, wd_buf, wgu_s, wd_s, in_sem, out_sem, w_sem):
    s = pl.program_id(0)
    ns = pl.num_programs(0)
    cur = s % 2
    g_n = MOE_GROUP

    def in_copy(step, g, buf):
        rows = pl.ds(pl.multiple_of(sb_ref[step * g_n + g] * MOE_BLOCK, MOE_BLOCK), MOE_BLOCK)
        return pltpu.make_async_copy(
            hs_hbm.at[rows], xbuf.at[buf, pl.ds(g * MOE_BLOCK, MOE_BLOCK)], in_sem.at[buf])

    def out_copy(step, g, buf):
        rows = pl.ds(pl.multiple_of(sb_ref[step * g_n + g] * MOE_BLOCK, MOE_BLOCK), MOE_BLOCK)
        cols = pl.ds(pl.multiple_of(sr_ref[step * g_n + g] * D_MODEL, D_MODEL), D_MODEL)
        return pltpu.make_async_copy(
            obuf.at[buf, pl.ds(g * MOE_BLOCK, MOE_BLOCK)], y_hbm.at[rows, cols], out_sem.at[buf])

    def for_slots(step, flags_ref, fn):
        for g in range(g_n):
            @pl.when(flags_ref[step * g_n + g] == 1)
            def _():
                fn(g)

    @pl.when(s == 0)
    def _():
        xbuf[...] = jnp.zeros_like(xbuf)
        for_slots(0, si_ref, lambda g: in_copy(0, g, 0).start())

    @pl.when(s + 1 < ns)
    def _():
        for_slots(s + 1, si_ref, lambda g: in_copy(s + 1, g, 1 - cur).start())

    for_slots(s, si_ref, lambda g: in_copy(s, g, cur).wait())

    @pl.when(s >= 2)
    def _():
        for_slots(s - 2, so_ref, lambda g: out_copy(s - 2, g, cur).wait())

    def w_copies(expert, slot):
        return (pltpu.make_async_copy(wg_hbm.at[expert], wg_buf.at[slot], w_sem.at[slot]),
                pltpu.make_async_copy(wu_hbm.at[expert], wu_buf.at[slot], w_sem.at[slot]),
                pltpu.make_async_copy(wd_hbm.at[expert], wd_buf.at[slot], w_sem.at[slot]))

    @pl.when(s == 0)
    def _():
        for c in w_copies(se_ref[0], sp_ref[0]):
            c.start()

    @pl.when(sf_ref[s] == 1)
    def _():
        slot = sp_ref[s]
        for c in w_copies(se_ref[s], slot):
            c.wait()
        wgu_s[:, 0:D_EXPERT] = wg_buf[slot].astype(bf16)
        wgu_s[:, D_EXPERT:2 * D_EXPERT] = wu_buf[slot].astype(bf16)
        wd_s[...] = wd_buf[slot].astype(bf16)

        @pl.when(sn_ref[s] >= 0)
        def _():
            for c in w_copies(sn_ref[s], 1 - slot):
                c.start()

    @pl.when(sk_ref[s] == 1)
    def _():
        h1 = xbuf[cur, :, 0:D_MODEL]
        meta = xbuf[cur, :, D_MODEL:ROW_W]
        u = _rms(h1, fg_ref[...]).astype(bf16)
        ones = jnp.ones((MOE_BLOCK, 1), f32)
        role0 = [sr_ref[s * g_n + g] == 0 for g in range(g_n)]
        gate = jnp.concatenate(
            [jnp.where(role0[g], meta[g * MOE_BLOCK:(g + 1) * MOE_BLOCK, 2:3],
                       meta[g * MOE_BLOCK:(g + 1) * MOE_BLOCK, 3:4]) for g in range(g_n)], axis=0)
        keep = jnp.concatenate([jnp.where(role0[g], ones, 0.0) for g in range(g_n)], axis=0)
        gu = _dot(u, wgu_s[...])
        gt = gu[:, 0:D_EXPERT]
        hdn = (gt * jax.nn.sigmoid(gt) * gu[:, D_EXPERT:]).astype(bf16)
        obuf[cur] = _dot(hdn, wd_s[...]) * gate + h1 * keep

    @pl.when(sk_ref[s] == 0)
    def _():
        obuf[cur] = jnp.zeros(obuf.shape[1:], f32)

    for_slots(s, so_ref, lambda g: out_copy(s, g, cur).start())

    @pl.when(s == ns - 1)
    def _():
        for_slots(s, so_ref, lambda g: out_copy(s, g, cur).wait())

        @pl.when(s >= 1)
        def _():
            for_slots(s - 1, so_ref, lambda g: out_copy(s - 1, g, 1 - cur).wait())


def _moe_call(plan, hs, w_gate, w_up, w_down, ffn_gain):
    n_steps = plan[0].shape[0]
    n_slots = hs.shape[0]
    rows = MOE_GROUP * MOE_BLOCK

    grid_spec = pltpu.PrefetchScalarGridSpec(
        num_scalar_prefetch=9,
        grid=(n_steps,),
        in_specs=[
            pl.BlockSpec(memory_space=pl.ANY),
            pl.BlockSpec(memory_space=pl.ANY),
            pl.BlockSpec(memory_space=pl.ANY),
            pl.BlockSpec(memory_space=pl.ANY),
            pl.BlockSpec((1, D_MODEL), lambda s, *_: (0, 0)),
        ],
        out_specs=pl.BlockSpec(memory_space=pl.ANY),
        scratch_shapes=[
            pltpu.VMEM((2, rows, ROW_W), f32),
            pltpu.VMEM((2, rows, D_MODEL), f32),
            pltpu.VMEM((2, D_MODEL, D_EXPERT), f32),
            pltpu.VMEM((2, D_MODEL, D_EXPERT), f32),
            pltpu.VMEM((2, D_EXPERT, D_MODEL), f32),
            pltpu.VMEM((D_MODEL, 2 * D_EXPERT), bf16),
            pltpu.VMEM((D_EXPERT, D_MODEL), bf16),
            pltpu.SemaphoreType.DMA((2,)),
            pltpu.SemaphoreType.DMA((2,)),
            pltpu.SemaphoreType.DMA((2,)),
        ],
    )
    return pl.pallas_call(
        _moe_kernel,
        grid_spec=grid_spec,
        out_shape=jax.ShapeDtypeStruct((n_slots, 2 * D_MODEL), f32),
        compiler_params=pltpu.CompilerParams(
            dimension_semantics=("arbitrary",), vmem_limit_bytes=VMEM_LIMIT),
        name="moe",
    )(*plan, hs, w_gate, w_up, w_down, ffn_gain)


def _final_kernel(posc_ref, posn_ref, gain_ref, y_hbm, o_ref, ybuf, sem):
    i = pl.program_id(0)
    cur = i % 2

    def issue(pos_ref, buf):
        def start(io, c):
            for r in range(ISSUE_UNROLL):
                ii = io * (ISSUE_UNROLL // SUBLANES) + r // SUBLANES
                pltpu.make_async_copy(y_hbm.at[pl.ds(pos_ref[io * ISSUE_UNROLL + r], 1)],
                                      ybuf.at[buf, ii, pl.ds(r % SUBLANES, 1)], sem.at[buf]).start()
            return c

        lax.fori_loop(0, FINAL_TILE // ISSUE_UNROLL, start, 0)

    @pl.when(i == 0)
    def _():
        issue(posc_ref, 0)

    @pl.when(i + 1 < pl.num_programs(0))
    def _():
        issue(posn_ref, 1 - cur)

    pltpu.make_async_copy(ybuf.at[cur], ybuf.at[cur], sem.at[cur]).wait()
    h = ybuf[cur, :, :, 0:D_MODEL] + ybuf[cur, :, :, D_MODEL:2 * D_MODEL]
    o_ref[...] = _rms(h, gain_ref[...])


def _final_call(pos, gain, y):
    t = pos.shape[0]
    n = t // FINAL_TILE
    rows = FINAL_TILE // SUBLANES
    out = pl.pallas_call(
        _final_kernel,
        grid=(n,),
        in_specs=[
            pl.BlockSpec((FINAL_TILE,), lambda i: (i,), memory_space=pltpu.SMEM),
            pl.BlockSpec((FINAL_TILE,), lambda i: (jnp.minimum(i + 1, n - 1),), memory_space=pltpu.SMEM),
            pl.BlockSpec((1, 1, D_MODEL), lambda i: (0, 0, 0)),
            pl.BlockSpec(memory_space=pl.ANY),
        ],
        out_specs=pl.BlockSpec((rows, SUBLANES, D_MODEL), lambda i: (i, 0, 0)),
        out_shape=jax.ShapeDtypeStruct((t // SUBLANES, SUBLANES, D_MODEL), f32),
        scratch_shapes=[pltpu.VMEM((2, rows, SUBLANES, 2 * D_MODEL), f32), pltpu.SemaphoreType.DMA((2,))],
        compiler_params=pltpu.CompilerParams(
            dimension_semantics=("arbitrary",), vmem_limit_bytes=VMEM_LIMIT),
        name="final",
    )(pos, pos, gain.reshape(1, 1, D_MODEL), y)
    return out.reshape(t, D_MODEL)


def _rope_tables(pos):
    inv = ROPE_BASE ** (-jnp.arange(0, MLA_ROPE, 2, dtype=f32) / MLA_ROPE)
    ang = pos.astype(f32)[:, None] * inv[None, :]
    cos, sin = jnp.cos(ang), jnp.sin(ang)
    z = jnp.zeros((pos.shape[0], LANE - MLA_ROPE), f32)
    return jnp.concatenate([cos, cos, z], axis=1), jnp.concatenate([-sin, sin, z], axis=1)


def _relayout_weights(w_in, w_qb, w_kvb):
    half = MLA_ROPE // 2
    perm = (np.arange(MLA_ROPE) + half) % MLA_ROPE
    pts = np.cumsum((GLA_QK, GLA_QK, GLA_VW, GLA_VW, GLA_GATE_RANK, MLA_Q_RANK, MLA_KV_RANK, MLA_ROPE))
    q_g, k_g, v_g, r_g, a_l, q_lat, kv_lat, k_rope = jnp.split(w_in, pts[:-1], axis=1)
    a_seg = jnp.pad(a_l, ((0, 0), (0, LANE - GLA_GATE_RANK)))
    w_in_r = jnp.concatenate(
        [q_g, k_g, v_g, r_g, q_lat, kv_lat, k_rope, k_rope[:, perm], a_seg], axis=1).astype(bf16)
    qcols, kcols, vcols = [], [], []
    for h in range(MLA_HEADS):
        c = h * (MLA_NOPE + MLA_ROPE)
        rope = w_qb[:, c + MLA_NOPE:c + MLA_NOPE + MLA_ROPE]
        qcols += [w_qb[:, c:c + MLA_NOPE], rope, rope[:, perm]]
        c2 = h * (MLA_NOPE + MLA_V)
        kcols.append(w_kvb[:, c2:c2 + MLA_NOPE])
        vcols.append(w_kvb[:, c2 + MLA_NOPE:c2 + MLA_NOPE + MLA_V])
    return w_in_r, jnp.concatenate(qcols, axis=1).astype(bf16), jnp.concatenate(kcols + vcols, axis=1).astype(bf16)


_BUCKET_GROUP = np.arange(N_BUCKETS) // N_PAIRS
_RUN_EXPERT = np.concatenate([_BUCKET_GROUP * EXPERTS_PER_GROUP + _PAIR_LO[np.arange(N_BUCKETS) % N_PAIRS],
                              _BUCKET_GROUP * EXPERTS_PER_GROUP + _PAIR_HI[np.arange(N_BUCKETS) % N_PAIRS]])
_RUN_IS_EXPERT = (_RUN_EXPERT[:, None] == np.arange(N_EXPERTS)[None, :]).astype(np.int32)
_RUN_BEFORE = ((_RUN_EXPERT[:, None] == _RUN_EXPERT[None, :])
               & (np.arange(2 * N_BUCKETS)[None, :] < np.arange(2 * N_BUCKETS)[:, None])).astype(np.int32)


def _route_plan(counts, bucket, rank, n_tok):
    nt = counts.shape[0]
    g_n = MOE_GROUP
    tot = counts.sum(axis=0)
    nblk = (tot + MOE_BLOCK - 1) // MOE_BLOCK
    bstart_blk = jnp.cumsum(nblk) - nblk
    n_blocks = jnp.sum(nblk)
    tile_base = bstart_blk[None, :] * MOE_BLOCK + jnp.cumsum(counts, axis=0) - counts
    hit = bucket.reshape(nt, -1, 1) == jnp.arange(N_BUCKETS, dtype=i32)
    pos = jnp.sum(jnp.where(hit, tile_base[:, None, :], 0), axis=-1).reshape(-1) + rank
    nb_max = (n_tok + N_BUCKETS * (MOE_BLOCK - 1)) // MOE_BLOCK

    n_run = jnp.concatenate([nblk, nblk])
    b0_run = jnp.concatenate([bstart_blk, bstart_blk])
    c_e = jnp.sum(n_run[:, None] * _RUN_IS_EXPERT, axis=0)
    g_e = (c_e + g_n - 1) // g_n
    gend = jnp.cumsum(g_e)
    gstart = gend - g_e
    n_compute = gend[-1]
    off_run = jnp.sum(_RUN_BEFORE * n_run[None, :], axis=1)
    f_run = jnp.sum(_RUN_IS_EXPERT * gstart[None, :], axis=1) * g_n + off_run

    n_steps = (2 * nb_max + N_EXPERTS * (g_n - 1) + g_n - 1) // g_n + 1
    f = jnp.arange(n_steps * g_n, dtype=i32)
    in_run = (f[:, None] >= f_run[None, :]) & (f[:, None] < (f_run + n_run)[None, :])
    valid_c = jnp.any(in_run, axis=1)
    block_c = jnp.sum(jnp.where(in_run, b0_run[None, :] + f[:, None] - f_run[None, :], 0), axis=1)
    role_c = jnp.sum(jnp.where(in_run[:, N_BUCKETS:], 1, 0), axis=1)
    u_idx = f - n_compute * g_n
    valid_f = (u_idx >= 0) & (u_idx < 2 * (nb_max - n_blocks))
    slot_block = jnp.where(valid_c, block_c, jnp.where(valid_f, n_blocks + u_idx // 2, 0))
    slot_role = jnp.where(valid_c, role_c, jnp.where(valid_f, u_idx % 2, 0))

    step = jnp.arange(n_steps, dtype=i32)
    e_of_step = jnp.minimum(jnp.sum(gend[None, :] <= step[:, None], axis=1), N_EXPERTS - 1)
    is_compute = step < n_compute
    last_e = jnp.max(jnp.where(is_compute, e_of_step, 0))
    step_expert = jnp.where(is_compute, e_of_step, last_e)
    step_first = jnp.concatenate([jnp.ones((1,), bool), step_expert[1:] != step_expert[:-1]])
    ordinal = jnp.cumsum(step_first.astype(i32)) - 1
    ords = jnp.arange(N_EXPERTS + 1, dtype=i32)
    expert_of_ord = jnp.sum(jnp.where(step_first[:, None] & (ordinal[:, None] == ords[None, :]),
                                      step_expert[:, None], 0), axis=0)
    has_next = ordinal + 1 <= ordinal[-1]
    next_expert = jnp.sum(jnp.where(ords[None, :] == ordinal[:, None] + 1, expert_of_ord[None, :], 0), axis=1)
    step_next = jnp.where(step_first & has_next, next_expert, -1)
    plan = tuple(a.astype(i32) for a in
                 (step_expert, step_first, is_compute, ordinal % 2, step_next,
                  slot_block, slot_role, valid_c, valid_c | valid_f))
    last_blk = jnp.where(nblk > 0, bstart_blk + nblk - 1, -1)
    spare = n_blocks + jnp.arange(nb_max - n_tok // MOE_BLOCK, dtype=i32)
    zero_blocks = jnp.concatenate([last_blk, jnp.where(spare < nb_max, spare, -1)]).astype(i32)
    return pos.astype(i32), plan, zero_blocks, nb_max


def kernel(x, meta_tokens, mix_norm, w_in, gla_w_a2, gla_b_a, gla_out_norm, mla_q_norm, mla_w_qb, mla_kv_norm,
           mla_w_kvb, w_out, ffn_norm, router_group_w, router_group_b, router_expert_w, router_expert_b,
           expert_w_gate, expert_w_up, expert_w_down, final_norm):
    batch, seq, d = x.shape
    assert PREP_TILE == ATT_TILE
    assert d == D_MODEL and seq % max(PREP_TILE, GLA_TILE, ATT_TILE) == 0
    assert (batch * seq) % max(OUT_TILE, SCATTER_TILE, FINAL_TILE) == 0 and batch % GLA_BATCH == 0
    n_tok = batch * seq
    x2d = x.reshape(n_tok, d)

    w_in_r, w_qb_r, w_kvb_r = _relayout_weights(w_in[0], mla_w_qb[0], mla_w_kvb[0])
    mixg = mix_norm[0].reshape(1, d)
    qn = mla_q_norm[0].reshape(1, MLA_Q_RANK)
    kvn = mla_kv_norm[0].reshape(1, MLA_KV_RANK)
    ct_m, st_m = _rope_tables(jnp.arange(META_TILE))
    ct_x, st_x = _rope_tables(N_META + jnp.arange(seq))

    x_meta = jnp.pad(meta_tokens.astype(f32), ((0, META_TILE - N_META), (0, 0)))
    _, kg_m, vg_m, _, a_m, _, km_m, vmt_m = _prep_call(
        x_meta, META_TILE, META_TILE, mixg, w_in_r, qn, w_qb_r, kvn, w_kvb_r, ct_m, st_m)
    qg, kg, vg, rg, ag, qm, km, vmt = _prep_call(
        x2d, seq, PREP_TILE, mixg, w_in_r, qn, w_qb_r, kvn, w_kvb_r, ct_x, st_x)

    def chunk0(a):
        return jnp.pad(a[:N_META], ((CHUNK - N_META, 0), (0, 0)))

    wa2_p = jnp.pad(gla_w_a2[0], ((0, LANE - GLA_GATE_RANK), (0, 0))).astype(bf16)
    y_gla = _gla_call(qg, kg, vg, rg, ag, chunk0(kg_m), chunk0(vg_m), chunk0(a_m),
                      wa2_p, gla_b_a[0].reshape(1, GLA_QK), gla_out_norm[0].reshape(1, GLA_VW), batch, seq)
    y_mla = _mla_call(qm, km, vmt, km_m[:N_META], vmt_m[0, :, :N_META], batch, seq)

    wo = w_out[0].astype(bf16)
    rw = jnp.concatenate([router_group_w[0], router_expert_w[0],
                          jnp.zeros((d, LANE - N_GROUPS - N_EXPERTS), f32)], axis=1)
    rb = jnp.concatenate([router_group_b[0], router_expert_b[0],
                          jnp.zeros((LANE - N_GROUPS - N_EXPERTS,), f32)]).reshape(1, LANE)
    ffn_gain = ffn_norm[0].reshape(1, d)
    ux, cnt, routes = _outproj_call(x2d, y_gla, y_mla, wo[:GLA_VW], wo[GLA_VW:], ffn_gain,
                                      rw.T.astype(bf16), rb.reshape(LANE, 1))

    counts = cnt.reshape(-1, BUCKET_LANES)[:, :N_BUCKETS].astype(i32)
    tok_bucket = routes[:, 0, :].reshape(-1).astype(i32)
    tok_rank = routes[:, 1, :].reshape(-1).astype(i32)
    pos, plan, zero_blocks, nb_max = _route_plan(counts, tok_bucket, tok_rank, n_tok)
    n_slots = nb_max * MOE_BLOCK
    hs = _scatter_call(pos, zero_blocks, ux, n_slots)
    y = _moe_call(plan, hs, expert_w_gate[0], expert_w_up[0], expert_w_down[0], ffn_gain)
    out = _final_call(pos, final_norm.reshape(1, d), y)
    return out.reshape(batch, seq, d)
```

```python
import functools

import numpy as np
import jax
import jax.numpy as jnp
from jax import lax
from jax.experimental import pallas as pl
from jax.experimental.pallas import tpu as pltpu

f32 = jnp.float32
bf16 = jnp.bfloat16
i32 = jnp.int32

D_MODEL = 1024
CHUNK = 64
N_META = 16
EPS = 1e-6
GLA_HEADS = 4
GLA_DK = 64
GLA_DV = 128
GLA_GATE_RANK = 16
GLA_TAU = 16.0
GLA_QK = GLA_HEADS * GLA_DK
GLA_VW = GLA_HEADS * GLA_DV
MLA_HEADS = 4
MLA_Q_RANK = 256
MLA_KV_RANK = 128
MLA_NOPE = 128
MLA_ROPE = 64
MLA_V = 128
MLA_OUT = MLA_HEADS * MLA_V
MLA_QK_PAD = 256
MLA_VA = MLA_V + 16
LOG2_E = 1.4426950408889634
ROPE_BASE = 10000.0
N_GROUPS = 8
EXPERTS_PER_GROUP = 8
N_EXPERTS = N_GROUPS * EXPERTS_PER_GROUP
D_EXPERT = 512
N_PAIRS = EXPERTS_PER_GROUP * (EXPERTS_PER_GROUP - 1) // 2
N_BUCKETS = N_GROUPS * N_PAIRS
BUCKET_LANES = 256
LANE = 128
SUBLANES = 8
META_W = LANE
ROW_W = D_MODEL + META_W

PREP_TILE = 512
GLA_TILE = 512
GLA_BATCH = 4
ATT_TILE = 512
ATT_HEADS = 4
META_TILE = 128
OUT_TILE = 1024
ROUTE_ROWS = 256
ROUTE_GROUP = 512
SCATTER_TILE = 2048
FINAL_TILE = 512
ISSUE_UNROLL = 64
MOE_BLOCK = 32
MOE_GROUP = 16
VMEM_LIMIT = 56 * 1024 * 1024

C_Q, C_K, C_V, C_R = 0, 256, 512, 1024
C_QLAT, C_KVLAT, C_KROPE, C_A, C_END = 1536, 1792, 1920, 2048, 2176

_TILE_POS = np.arange(GLA_TILE)
_CHUNK_PREFIX = ((_TILE_POS[:, None] // CHUNK == _TILE_POS[None, :] // CHUNK)
                 & (_TILE_POS[None, :] <= _TILE_POS[:, None])).astype(np.float32)
_PAIR_LO = np.array([lo for lo in range(8) for hi in range(lo + 1, 8)], np.int32)
_PAIR_HI = np.array([hi for lo in range(8) for hi in range(lo + 1, 8)], np.int32)


def _dot(a, b):
    return jnp.dot(a, b, preferred_element_type=f32)


def _dot_nt(a, b):
    return lax.dot_general(a, b, (((1,), (1,)), ((), ())), preferred_element_type=f32)


def _dot_tn(a, b):
    return lax.dot_general(a, b, (((0,), (0,)), ((), ())), preferred_element_type=f32)


def _rms(x, gain):
    return x * lax.rsqrt(jnp.mean(x * x, axis=-1, keepdims=True) + EPS) * gain


def _split3(x):
    hi = x.astype(bf16)
    r1 = x - hi.astype(f32)
    mid = r1.astype(bf16)
    lo = (r1 - mid.astype(f32)).astype(bf16)
    return hi, mid, lo


def _prep_kernel(x_ref, g_ref, win_ref, qn_ref, wqb_ref, kvn_ref, wkvb_ref, ct_ref, st_ref,
                 qg_ref, kg_ref, vg_ref, rg_ref, a_ref, qm_ref, km_ref, vmt_ref):
    u = _rms(x_ref[...], g_ref[...]).astype(bf16)

    def proj(lo, hi):
        return _dot(u, win_ref[:, lo:hi])

    qg_ref[...] = proj(C_Q, C_K).astype(bf16)
    kg_ref[...] = proj(C_K, C_V).astype(bf16)
    vg_ref[...] = proj(C_V, C_R).astype(bf16)
    rg_ref[...] = proj(C_R, C_QLAT).astype(bf16)
    z = proj(C_QLAT, C_END)
    a_ref[...] = z[:, C_A - C_QLAT:].astype(bf16)
    ctab = ct_ref[...]
    stab = st_ref[...]

    def rope(seg):
        return seg * ctab + pltpu.roll(seg, 64, axis=1) * stab

    k_rope = rope(z[:, C_KROPE - C_QLAT:C_A - C_QLAT]).astype(bf16)
    qn = _rms(z[:, 0:MLA_Q_RANK], qn_ref[...]).astype(bf16)
    kvn = _rms(z[:, MLA_Q_RANK:MLA_Q_RANK + MLA_KV_RANK], kvn_ref[...]).astype(bf16)
    scale = (MLA_NOPE + MLA_ROPE) ** -0.5 * LOG2_E
    qf = _dot(qn, wqb_ref[...])
    kvf = _dot(kvn, wkvb_ref[...])
    for h in range(MLA_HEADS):
        c = h * MLA_QK_PAD
        qm_ref[:, c:c + LANE] = (qf[:, c:c + LANE] * scale).astype(bf16)
        qm_ref[:, c + LANE:c + 2 * LANE] = (rope(qf[:, c + LANE:c + 2 * LANE]) * scale).astype(bf16)
        km_ref[:, c:c + LANE] = kvf[:, h * LANE:(h + 1) * LANE].astype(bf16)
        km_ref[:, c + LANE:c + 2 * LANE] = k_rope
    vt = kvf[:, MLA_HEADS * MLA_NOPE:].T
    for h in range(MLA_HEADS):
        vmt_ref[h * MLA_VA:h * MLA_VA + MLA_V, :] = vt[h * MLA_V:(h + 1) * MLA_V].astype(bf16)
        vmt_ref[h * MLA_VA + MLA_V:(h + 1) * MLA_VA, :] = jnp.ones((MLA_VA - MLA_V, vt.shape[1]), bf16)


def _prep_call(x2d, rows_per_seq, tile, gain, w_in_r, q_norm, w_qb_r, kv_norm, w_kvb_r, ctab, stab):
    t = x2d.shape[0]
    nj = rows_per_seq // tile
    grid = (t // rows_per_seq, nj)

    def row(b, j):
        return (b * nj + j, 0)

    def const(b, j):
        return (0, 0)

    def tab(b, j):
        return (j, 0)

    widths = (GLA_QK, GLA_QK, GLA_VW, GLA_VW, LANE, MLA_HEADS * MLA_QK_PAD, MLA_HEADS * MLA_QK_PAD)
    return pl.pallas_call(
        _prep_kernel,
        grid=grid,
        in_specs=[
            pl.BlockSpec((tile, D_MODEL), row),
            pl.BlockSpec((1, D_MODEL), const),
            pl.BlockSpec((D_MODEL, C_END), const),
            pl.BlockSpec((1, MLA_Q_RANK), const),
            pl.BlockSpec((MLA_Q_RANK, MLA_HEADS * MLA_QK_PAD), const),
            pl.BlockSpec((1, MLA_KV_RANK), const),
            pl.BlockSpec((MLA_KV_RANK, 2 * MLA_OUT), const),
            pl.BlockSpec((tile, LANE), tab),
            pl.BlockSpec((tile, LANE), tab),
        ],
        out_specs=[pl.BlockSpec((tile, w), row) for w in widths]
        + [pl.BlockSpec((None, MLA_HEADS * MLA_VA, tile), lambda b, j: (b * nj + j, 0, 0))],
        out_shape=[jax.ShapeDtypeStruct((t, w), bf16) for w in widths]
        + [jax.ShapeDtypeStruct((t // tile, MLA_HEADS * MLA_VA, tile), bf16)],
        compiler_params=pltpu.CompilerParams(
            dimension_semantics=("parallel", "parallel"), vmem_limit_bytes=VMEM_LIMIT),
        name="prep",
    )(x2d, gain, w_in_r, q_norm, w_qb_r, kv_norm, w_kvb_r, ctab, stab)


def _gla_log_decay(a, wa2_ref, ba_ref):
    s = _dot(a, wa2_ref[...]) + ba_ref[...]
    return (jnp.minimum(s, 0.0) - jnp.log(1.0 + jnp.exp(-jnp.abs(s)))) * (1.0 / GLA_TAU)


def _gla_front(q, k, v, la, tri, want_out):
    nc = la.shape[0] // CHUNK
    hi, mid, lo = _split3(la)
    b = _dot(tri, hi) + _dot(tri, mid) + _dot(tri, lo)
    b_last = [b[(c + 1) * CHUNK - 1:(c + 1) * CHUNK, :] for c in range(nc)]
    b_last_full = jnp.concatenate([jnp.broadcast_to(bl, (CHUNK, GLA_QK)) for bl in b_last], axis=0)
    kf = k.astype(f32)
    front = dict(v=v, b_last=b_last, kd=(kf * jnp.exp(b_last_full - b)).astype(bf16))
    if want_out:
        front.update(qe=(q.astype(f32) * (GLA_DK ** -0.5) * jnp.exp(b)).astype(bf16),
                     ke=kf * jnp.exp(-b), vf=v.astype(f32))
    return front


def _gla_chunks(front, st_ref, want_out):
    v, kd, b_last = front["v"], front["kd"], front["b_last"]
    rr = lax.broadcasted_iota(i32, (GLA_VW, GLA_QK), 0) // GLA_DV
    cc = lax.broadcasted_iota(i32, (GLA_VW, GLA_QK), 1) // GLA_DK
    if want_out:
        qe, ke, vf = front["qe"], front["ke"], front["vf"]
        lane_h = lax.broadcasted_iota(i32, (CHUNK, GLA_QK), 1) // GLA_DK
        vlane_h = lax.broadcasted_iota(i32, (CHUNK, GLA_VW), 1) // GLA_DV
        a_row = lax.broadcasted_iota(i32, (CHUNK, GLA_QK), 0)
        a_col = lax.broadcasted_iota(i32, (CHUNK, GLA_QK), 1) % CHUNK
    outs = []
    st = st_ref[...]
    for c in range(len(b_last)):
        rows = slice(c * CHUNK, (c + 1) * CHUNK)
        upd = jnp.where(rr == cc, _dot_tn(v[rows], kd[rows]), 0.0)
        if want_out:
            kbd = jnp.concatenate(
                [jnp.where(lane_h == h, ke[rows], 0.0) for h in range(GLA_HEADS)], axis=0).astype(bf16)
            att = jnp.where(a_col <= a_row, _dot_nt(qe[rows], kbd), 0.0).astype(bf16)
            vbd = jnp.concatenate(
                [jnp.where(vlane_h == h, vf[rows], 0.0) for h in range(GLA_HEADS)], axis=0).astype(bf16)
            outs.append(_dot(att, vbd) + _dot_nt(qe[rows], st.astype(bf16)))
        st = st * jnp.exp(b_last[c]) + upd
    st_ref[...] = st
    return jnp.concatenate(outs, axis=0) if want_out else None


def _gla_kernel(q_ref, k_ref, v_ref, r_ref, a_ref, km_ref, vm_ref, am_ref, wa2_ref, ba_ref, gain_ref, tri_ref,
                y_ref, st_ref):
    j = pl.program_id(1)

    @pl.when(j == 0)
    def _():
        st_ref[...] = jnp.zeros_like(st_ref)
        la = _gla_log_decay(am_ref[...], wa2_ref, ba_ref)
        row = lax.broadcasted_iota(i32, la.shape, 0)
        la = jnp.where(row >= CHUNK - N_META, la, 0.0)
        front = _gla_front(None, km_ref[...], vm_ref[...], la, tri_ref[0:CHUNK, 0:CHUNK], False)
        _gla_chunks(front, st_ref.at[0], False)
        for bb in range(1, GLA_BATCH):
            st_ref[bb] = st_ref[0]

    fronts = [_gla_front(q_ref[bb], k_ref[bb], v_ref[bb], _gla_log_decay(a_ref[bb], wa2_ref, ba_ref),
                         tri_ref[...], True) for bb in range(GLA_BATCH)]
    for bb in range(GLA_BATCH):
        o = _gla_chunks(fronts[bb], st_ref.at[bb], True)
        r = r_ref[bb].astype(f32)
        outs = []
        for h in range(GLA_HEADS):
            oh = o[:, h * GLA_DV:(h + 1) * GLA_DV]
            outs.append(oh * lax.rsqrt(jnp.mean(oh * oh, axis=-1, keepdims=True) + EPS))
        on = jnp.concatenate(outs, axis=1) * gain_ref[...]
        y_ref[bb] = (on * (r * jax.nn.sigmoid(r))).astype(bf16)


def _gla_call(qg, kg, vg, rg, ag, km, vm, am, wa2_p, b_a, gain, batch, seq):
    nj = seq // GLA_TILE

    def row(b, j):
        return (b, j, 0)

    def const(b, j):
        return (0, 0)

    def seqs(a):
        return a.reshape(batch, seq, a.shape[-1])

    out = pl.pallas_call(
        _gla_kernel,
        grid=(batch // GLA_BATCH, nj),
        in_specs=[
            pl.BlockSpec((GLA_BATCH, GLA_TILE, GLA_QK), row),
            pl.BlockSpec((GLA_BATCH, GLA_TILE, GLA_QK), row),
            pl.BlockSpec((GLA_BATCH, GLA_TILE, GLA_VW), row),
            pl.BlockSpec((GLA_BATCH, GLA_TILE, GLA_VW), row),
            pl.BlockSpec((GLA_BATCH, GLA_TILE, LANE), row),
            pl.BlockSpec((CHUNK, GLA_QK), const),
            pl.BlockSpec((CHUNK, GLA_VW), const),
            pl.BlockSpec((CHUNK, LANE), const),
            pl.BlockSpec((LANE, GLA_QK), const),
            pl.BlockSpec((1, GLA_QK), const),
            pl.BlockSpec((1, GLA_VW), const),
            pl.BlockSpec((GLA_TILE, GLA_TILE), const),
        ],
        out_specs=pl.BlockSpec((GLA_BATCH, GLA_TILE, GLA_VW), row),
        out_shape=jax.ShapeDtypeStruct((batch, seq, GLA_VW), bf16),
        scratch_shapes=[pltpu.VMEM((GLA_BATCH, GLA_VW, GLA_QK), f32)],
        compiler_params=pltpu.CompilerParams(
            dimension_semantics=("parallel", "arbitrary"), vmem_limit_bytes=VMEM_LIMIT),
        name="gla",
    )(seqs(qg), seqs(kg), seqs(vg), seqs(rg), seqs(ag), km, vm, am, wa2_p, b_a, gain,
      jnp.asarray(_CHUNK_PREFIX, bf16))
    return out.reshape(batch * seq, GLA_VW)


def _mla_kernel(q_ref, k_ref, vt_ref, km_ref, vmt_ref, o_ref, sa_ref, sb_ref):
    i = pl.program_id(2)
    tq = ATT_TILE
    w = MLA_QK_PAD
    va = MLA_VA
    heads = range(ATT_HEADS)

    def scores(h, blk):
        rows = pl.ds(pl.multiple_of(blk * tq, tq), tq)
        return _dot_nt(k_ref[rows, h * w:(h + 1) * w], q_ref[:, h * w:(h + 1) * w])

    def soft(s, vtb, carry, mask=None):
        m, acc = carry
        if mask is not None:
            s = jnp.where(mask, s, -1e30)
        m_new = jnp.maximum(m, jnp.max(s, axis=0, keepdims=True))
        p = jnp.exp2(s - m_new).astype(bf16)
        return m_new, jnp.exp2(m - m_new) * acc + _dot(vtb, p)

    def vt(h, blk):
        return vt_ref[blk, h * va:(h + 1) * va, :]

    def finish(carries):
        ss = [_dot_nt(km_ref[:, h * w:(h + 1) * w], q_ref[:, h * w:(h + 1) * w]) for h in heads]
        accs = [soft(ss[h], vmt_ref[h * va:(h + 1) * va, :], carries[h])[1] for h in heads]
        for h in heads:
            acc = accs[h]
            o_ref[:, h * MLA_V:(h + 1) * MLA_V] = (acc[:MLA_V] * (1.0 / acc[MLA_V:MLA_V + 1])).T.astype(bf16)

    kc = lax.broadcasted_iota(i32, (tq, tq), 0) // CHUNK
    qc = lax.broadcasted_iota(i32, (tq, tq), 1) // CHUNK
    mask = kc <= qc

    for h in heads:
        sa_ref[h] = scores(h, 0)

    def pair(p, carries):
        b0 = 2 * p
        for h in heads:
            sb_ref[h] = scores(h, b0 + 1)
        carries = [soft(sa_ref[h], vt(h, b0), carries[h]) for h in heads]
        for h in heads:
            sa_ref[h] = scores(h, b0 + 2)
        return tuple(soft(sb_ref[h], vt(h, b0 + 1), carries[h]) for h in heads)

    init = tuple((jnp.full((1, tq), -1e30, f32), jnp.zeros((va, tq), f32)) for _ in heads)
    carries = lax.fori_loop(0, i // 2, pair, init)

    @pl.when(i % 2 == 1)
    def _():
        for h in heads:
            sb_ref[h] = scores(h, i)
        c1 = [soft(sa_ref[h], vt(h, i - 1), carries[h]) for h in heads]
        finish([soft(sb_ref[h], vt(h, i), c1[h], mask) for h in heads])

    @pl.when(i % 2 == 0)
    def _():
        finish([soft(sa_ref[h], vt(h, i), carries[h], mask) for h in heads])


def _mla_call(qm, km, vmt, km_meta, vmt_meta, batch, seq):
    nq = seq // ATT_TILE
    nh = ATT_HEADS
    qm3 = qm.reshape(batch, seq, MLA_HEADS * MLA_QK_PAD)
    km3 = km.reshape(batch, seq, MLA_HEADS * MLA_QK_PAD)
    vt4 = vmt.reshape(batch, nq, MLA_HEADS * MLA_VA, ATT_TILE)
    out = pl.pallas_call(
        _mla_kernel,
        grid=(batch, MLA_HEADS // nh, nq),
        in_specs=[
            pl.BlockSpec((None, ATT_TILE, nh * MLA_QK_PAD), lambda b, h, i: (b, i, h)),
            pl.BlockSpec((None, seq, nh * MLA_QK_PAD), lambda b, h, i: (b, 0, h)),
            pl.BlockSpec((None, nq, nh * MLA_VA, ATT_TILE), lambda b, h, i: (b, 0, h, 0)),
            pl.BlockSpec((N_META, nh * MLA_QK_PAD), lambda b, h, i: (0, h)),
            pl.BlockSpec((nh * MLA_VA, N_META), lambda b, h, i: (h, 0)),
        ],
        out_specs=pl.BlockSpec((None, ATT_TILE, nh * MLA_V), lambda b, h, i: (b, i, h)),
        out_shape=jax.ShapeDtypeStruct((batch, seq, MLA_OUT), bf16),
        scratch_shapes=[pltpu.VMEM((nh, ATT_TILE, ATT_TILE), f32), pltpu.VMEM((nh, ATT_TILE, ATT_TILE), f32)],
        compiler_params=pltpu.CompilerParams(
            dimension_semantics=("parallel", "parallel", "arbitrary"), vmem_limit_bytes=VMEM_LIMIT),
        name="mla",
    )(qm3, km3, vt4, km_meta, vmt_meta)
    return out.reshape(batch * seq, MLA_OUT)


def _route_cols(lt):
    r = lt.shape[1]
    neg = -1e30
    gl = lt[0:N_GROUPS, :]
    gsub = lax.broadcasted_iota(i32, (N_GROUPS, r), 0)
    gmax = jnp.max(gl, axis=0, keepdims=True)
    g_p = 1.0 / jnp.sum(jnp.exp(gl - gmax), axis=0, keepdims=True)
    g_idx = jnp.min(jnp.where(gl == gmax, gsub, N_GROUPS), axis=0, keepdims=True)
    el_all = lt[N_GROUPS:N_GROUPS + N_EXPERTS, :]
    esub = lax.broadcasted_iota(i32, (N_EXPERTS, r), 0)
    base = g_idx * EXPERTS_PER_GROUP
    e_mask = (esub >= base) & (esub < base + EXPERTS_PER_GROUP)
    el = jnp.where(e_mask, el_all, neg)
    m1 = jnp.max(el, axis=0, keepdims=True)
    i1 = jnp.min(jnp.where(e_mask & (el == m1), esub, N_EXPERTS), axis=0, keepdims=True)
    el2 = jnp.where(esub == i1, neg, el)
    m2 = jnp.max(el2, axis=0, keepdims=True)
    i2 = jnp.min(jnp.where(e_mask & (esub != i1) & (el2 == m2), esub, N_EXPERTS), axis=0, keepdims=True)
    rr = jnp.exp(m2 - m1)
    ga = g_p / (1.0 + rr)
    gb = g_p * rr / (1.0 + rr)
    la_ = i1 - base
    lb_ = i2 - base
    lo = jnp.minimum(la_, lb_)
    hi = jnp.maximum(la_, lb_)
    g_lo = jnp.where(la_ < lb_, ga, gb)
    g_hi = jnp.where(la_ < lb_, gb, ga)
    pidx = ((lo * (2 * EXPERTS_PER_GROUP - 1 - lo)) >> 1) + (hi - lo - 1)
    bucket = g_idx * N_PAIRS + pidx
    bsub = lax.broadcasted_iota(i32, (BUCKET_LANES, r), 0)
    oht = jnp.where(bsub == bucket, 1.0, 0.0)
    ohb = oht.astype(bf16)
    ri = lax.broadcasted_iota(i32, (ROUTE_ROWS, ROUTE_ROWS), 0)
    ci = lax.broadcasted_iota(i32, (ROUTE_ROWS, ROUTE_ROWS), 1)
    before = jnp.where(ri < ci, 1.0, 0.0).astype(bf16)
    ones = jnp.ones((SUBLANES, ROUTE_ROWS), bf16)
    subs = [slice(i * ROUTE_ROWS, (i + 1) * ROUTE_ROWS) for i in range(r // ROUTE_ROWS)]
    cum = jnp.concatenate([_dot(ohb[:, sl], before) for sl in subs], axis=1)
    rank = jnp.sum(oht * cum, axis=0, keepdims=True)
    counts = [_dot_nt(ones, ohb[:, sl])[0:1, :] for sl in subs]
    msub = lax.broadcasted_iota(i32, (LANE, r), 0)
    meta_t = jnp.where(msub == 0, bucket.astype(f32),
                       jnp.where(msub == 1, rank,
                                 jnp.where(msub == 2, g_lo, jnp.where(msub == 3, g_hi, 0.0))))
    return meta_t.T, counts, meta_t[0:SUBLANES, :]


def _outproj_kernel(x_ref, yg_ref, ym_ref, wog_ref, wom_ref, gain_ref, wrt_ref, rb_ref,
                    ux_ref, cnt_ref, rt_ref):
    per_group = ROUTE_GROUP // ROUTE_ROWS
    for grp in range(OUT_TILE // ROUTE_GROUP):
        rows = slice(grp * ROUTE_GROUP, (grp + 1) * ROUTE_GROUP)
        h1 = x_ref[rows, :] + _dot(yg_ref[rows, :], wog_ref[...]) + _dot(ym_ref[rows, :], wom_ref[...])
        ux_ref[rows, 0:D_MODEL] = h1
        u2 = _rms(h1, gain_ref[...])
        lt = _dot_nt(wrt_ref[...], u2.astype(bf16)) + rb_ref[...]
        meta, counts, routes = _route_cols(lt)
        ux_ref[rows, D_MODEL:ROW_W] = meta
        for i in range(per_group):
            cnt_ref[grp * per_group + i] = counts[i]
            rt_ref[grp * per_group + i] = routes[:, i * ROUTE_ROWS:(i + 1) * ROUTE_ROWS]


def _outproj_call(x2d, yg, ym, wo_g, wo_m, gain, w_r, rbias):
    t = x2d.shape[0]
    nt = t // OUT_TILE

    def row(i):
        return (i, 0)

    def const(i):
        return (0, 0)

    return pl.pallas_call(
        _outproj_kernel,
        grid=(nt,),
        in_specs=[
            pl.BlockSpec((OUT_TILE, D_MODEL), row),
            pl.BlockSpec((OUT_TILE, GLA_VW), row),
            pl.BlockSpec((OUT_TILE, MLA_OUT), row),
            pl.BlockSpec((GLA_VW, D_MODEL), const),
            pl.BlockSpec((MLA_OUT, D_MODEL), const),
            pl.BlockSpec((1, D_MODEL), const),
            pl.BlockSpec((LANE, D_MODEL), const),
            pl.BlockSpec((LANE, 1), const),
        ],
        out_specs=[
            pl.BlockSpec((OUT_TILE, ROW_W), row),
            pl.BlockSpec((OUT_TILE // ROUTE_ROWS, 1, BUCKET_LANES), lambda i: (i, 0, 0)),
            pl.BlockSpec((OUT_TILE // ROUTE_ROWS, SUBLANES, ROUTE_ROWS), lambda i: (i, 0, 0)),
        ],
        out_shape=[
            jax.ShapeDtypeStruct((t, ROW_W), f32),
            jax.ShapeDtypeStruct((nt * (OUT_TILE // ROUTE_ROWS), 1, BUCKET_LANES), f32),
            jax.ShapeDtypeStruct((nt * (OUT_TILE // ROUTE_ROWS), SUBLANES, ROUTE_ROWS), f32),
        ],
        compiler_params=pltpu.CompilerParams(
            dimension_semantics=("parallel",), vmem_limit_bytes=VMEM_LIMIT),
        name="outproj",
    )(x2d, yg, ym, wo_g, wo_m, gain, w_r, rbias)


def _scatter_kernel(pos_ref, zb_ref, ux_ref, hs_ref, zbuf, sem, zsem):
    @pl.when(pl.program_id(0) == 0)
    def _():
        zbuf[...] = jnp.zeros_like(zbuf)

        def zero_copy(j):
            rows = pl.ds(pl.multiple_of(zb_ref[j] * MOE_BLOCK, MOE_BLOCK), MOE_BLOCK)
            return pltpu.make_async_copy(zbuf, hs_ref.at[rows], zsem)

        def zstart(j, c):
            @pl.when(zb_ref[j] >= 0)
            def _():
                zero_copy(j).start()
            return c

        def zwait(j, c):
            @pl.when(zb_ref[j] >= 0)
            def _():
                zero_copy(j).wait()
            return c

        lax.fori_loop(0, zb_ref.shape[0], zstart, 0)
        lax.fori_loop(0, zb_ref.shape[0], zwait, 0)

    def start(io, c):
        for r in range(ISSUE_UNROLL):
            ii = io * (ISSUE_UNROLL // SUBLANES) + r // SUBLANES
            pltpu.make_async_copy(ux_ref.at[ii, pl.ds(r % SUBLANES, 1)],
                                  hs_ref.at[pl.ds(pos_ref[io * ISSUE_UNROLL + r], 1)], sem).start()
        return c

    lax.fori_loop(0, SCATTER_TILE // ISSUE_UNROLL, start, 0)
    pltpu.make_async_copy(hs_ref.at[pl.ds(0, SCATTER_TILE)], hs_ref.at[pl.ds(0, SCATTER_TILE)], sem).wait()


def _scatter_call(pos, zero_blocks, ux, n_slots):
    t = ux.shape[0]
    nz = zero_blocks.shape[0]
    return pl.pallas_call(
        _scatter_kernel,
        grid=(t // SCATTER_TILE,),
        in_specs=[
            pl.BlockSpec((SCATTER_TILE,), lambda i: (i,), memory_space=pltpu.SMEM),
            pl.BlockSpec((nz,), lambda i: (0,), memory_space=pltpu.SMEM),
            pl.BlockSpec((SCATTER_TILE // SUBLANES, SUBLANES, ROW_W), lambda i: (i, 0, 0)),
        ],
        out_specs=pl.BlockSpec(memory_space=pl.ANY),
        out_shape=jax.ShapeDtypeStruct((n_slots, ROW_W), f32),
        scratch_shapes=[pltpu.VMEM((MOE_BLOCK, ROW_W), f32), pltpu.SemaphoreType.DMA(()),
                        pltpu.SemaphoreType.DMA(())],
        compiler_params=pltpu.CompilerParams(
            dimension_semantics=("arbitrary",), vmem_limit_bytes=VMEM_LIMIT),
        name="scatter",
    )(pos, zero_blocks, ux.reshape(t // SUBLANES, SUBLANES, ROW_W))


def _moe_kernel(se_ref, sf_ref, sk_ref, sp_ref, sn_ref, sbi_ref, sbo_ref, sr_ref,
                hs_hbm, wg_hbm, wu_hbm, wd_hbm, fg_ref, y_hbm,
                xbuf, obuf, wg_buf, wu_buf, wd_buf, wgu_s, wd_s, in_sem, out_sem, w_sem):
    s = pl.program_id(0)
    ns = pl.num_programs(0)
    cur = s % 2
    g_n = MOE_GROUP

    def start_in(step, buf):
        @pl.when(sk_ref[step] == 1)
        def _():
            for g in range(g_n):
                rows = pl.ds(pl.multiple_of(sbi_ref[step * g_n + g] * MOE_BLOCK, MOE_BLOCK), MOE_BLOCK)
                pltpu.make_async_copy(
                    hs_hbm.at[rows], xbuf.at[buf, pl.ds(g * MOE_BLOCK, MOE_BLOCK)], in_sem.at[buf]).start()

    def start_out(step, buf):
        for g in range(g_n):
            rows = pl.ds(pl.multiple_of(sbo_ref[step * g_n + g] * MOE_BLOCK, MOE_BLOCK), MOE_BLOCK)
            cols = pl.ds(pl.multiple_of(sr_ref[step * g_n + g] * D_MODEL, D_MODEL), D_MODEL)
            pltpu.make_async_copy(
                obuf.at[buf, pl.ds(g * MOE_BLOCK, MOE_BLOCK)], y_hbm.at[rows, cols], out_sem.at[buf]).start()

    def wait_in(buf):
        pltpu.make_async_copy(xbuf.at[buf], xbuf.at[buf], in_sem.at[buf]).wait()

    def wait_out(buf):
        pltpu.make_async_copy(obuf.at[buf], obuf.at[buf], out_sem.at[buf]).wait()

    @pl.when(s == 0)
    def _():
        start_in(0, 0)
        obuf[0] = jnp.zeros(obuf.shape[1:], f32)
        rows_per = g_n * MOE_BLOCK
        spare = [pltpu.make_async_copy(
            obuf.at[0], y_hbm.at[pl.ds(y_hbm.shape[0] - (2 - p) * rows_per, rows_per), pl.ds(r * D_MODEL, D_MODEL)],
            out_sem.at[0]) for p in range(2) for r in range(2)]
        for c in spare:
            c.start()
        for c in spare:
            c.wait()

    @pl.when(s + 1 < ns)
    def _():
        start_in(s + 1, 1 - cur)

    @pl.when(sk_ref[s] == 1)
    def _():
        wait_in(cur)

    @pl.when(s >= 2)
    def _():
        wait_out(cur)

    def w_copies(expert, slot):
        return (pltpu.make_async_copy(wg_hbm.at[expert], wg_buf.at[slot], w_sem.at[slot]),
                pltpu.make_async_copy(wu_hbm.at[expert], wu_buf.at[slot], w_sem.at[slot]),
                pltpu.make_async_copy(wd_hbm.at[expert], wd_buf.at[slot], w_sem.at[slot]))

    @pl.when(s == 0)
    def _():
        for c in w_copies(se_ref[0], sp_ref[0]):
            c.start()

    @pl.when(sf_ref[s] == 1)
    def _():
        slot = sp_ref[s]
        for c in w_copies(se_ref[s], slot):
            c.wait()
        wgu_s[:, 0:D_EXPERT] = wg_buf[slot].astype(bf16)
        wgu_s[:, D_EXPERT:2 * D_EXPERT] = wu_buf[slot].astype(bf16)
        wd_s[...] = wd_buf[slot].astype(bf16)

        @pl.when(sn_ref[s] >= 0)
        def _():
            for c in w_copies(sn_ref[s], 1 - slot):
                c.start()

    @pl.when(sk_ref[s] == 1)
    def _():
        h1 = xbuf[cur, :, 0:D_MODEL]
        meta = xbuf[cur, :, D_MODEL:ROW_W]
        u = _rms(h1, fg_ref[...]).astype(bf16)
        ones = jnp.ones((MOE_BLOCK, 1), f32)
        role0 = [sr_ref[s * g_n + g] == 0 for g in range(g_n)]
        gate = jnp.concatenate(
            [jnp.where(role0[g], meta[g * MOE_BLOCK:(g + 1) * MOE_BLOCK, 2:3],
                       meta[g * MOE_BLOCK:(g + 1) * MOE_BLOCK, 3:4]) for g in range(g_n)], axis=0)
        keep = jnp.concatenate([jnp.where(role0[g], ones, 0.0) for g in range(g_n)], axis=0)
        gu = _dot(u, wgu_s[...])
        gt = gu[:, 0:D_EXPERT]
        hdn = (gt * jax.nn.sigmoid(gt) * gu[:, D_EXPERT:]).astype(bf16)
        obuf[cur] = _dot(hdn, wd_s[...]) * gate + h1 * keep

    @pl.when(sk_ref[s] == 0)
    def _():
        obuf[cur] = jnp.zeros(obuf.shape[1:], f32)

    start_out(s, cur)

    @pl.when(s == ns - 1)
    def _():
        wait_out(cur)

        @pl.when(s >= 1)
        def _():
            wait_out(1 - cur)


def _moe_call(plan, hs, w_gate, w_up, w_down, ffn_gain):
    n_steps = plan[0].shape[0]
    n_slots = hs.shape[0] + 2 * MOE_GROUP * MOE_BLOCK
    rows = MOE_GROUP * MOE_BLOCK

    grid_spec = pltpu.PrefetchScalarGridSpec(
        num_scalar_prefetch=8,
        grid=(n_steps,),
        in_specs=[
            pl.BlockSpec(memory_space=pl.ANY),
            pl.BlockSpec(memory_space=pl.ANY),
            pl.BlockSpec(memory_space=pl.ANY),
            pl.BlockSpec(memory_space=pl.ANY),
            pl.BlockSpec((1, D_MODEL), lambda s, *_: (0, 0)),
        ],
        out_specs=pl.BlockSpec(memory_space=pl.ANY),
        scratch_shapes=[
            pltpu.VMEM((2, rows, ROW_W), f32),
            pltpu.VMEM((2, rows, D_MODEL), f32),
            pltpu.VMEM((2, D_MODEL, D_EXPERT), f32),
            pltpu.VMEM((2, D_MODEL, D_EXPERT), f32),
            pltpu.VMEM((2, D_EXPERT, D_MODEL), f32),
            pltpu.VMEM((D_MODEL, 2 * D_EXPERT), bf16),
            pltpu.VMEM((D_EXPERT, D_MODEL), bf16),
            pltpu.SemaphoreType.DMA((2,)),
            pltpu.SemaphoreType.DMA((2,)),
            pltpu.SemaphoreType.DMA((2,)),
        ],
    )
    return pl.pallas_call(
        _moe_kernel,
        grid_spec=grid_spec,
        out_shape=jax.ShapeDtypeStruct((n_slots, 2 * D_MODEL), f32),
        compiler_params=pltpu.CompilerParams(
            dimension_semantics=("arbitrary",), vmem_limit_bytes=VMEM_LIMIT),
        name="moe",
    )(*plan, hs, w_gate, w_up, w_down, ffn_gain)


def _final_kernel(posc_ref, posn_ref, gain_ref, y_hbm, o_ref, ybuf, sem):
    i = pl.program_id(0)
    cur = i % 2

    def issue(pos_ref, buf):
        def start(io, c):
            for r in range(ISSUE_UNROLL):
                ii = io * (ISSUE_UNROLL // SUBLANES) + r // SUBLANES
                pltpu.make_async_copy(y_hbm.at[pl.ds(pos_ref[io * ISSUE_UNROLL + r], 1)],
                                      ybuf.at[buf, ii, pl.ds(r % SUBLANES, 1)], sem.at[buf]).start()
            return c

        lax.fori_loop(0, FINAL_TILE // ISSUE_UNROLL, start, 0)

    @pl.when(i == 0)
    def _():
        issue(posc_ref, 0)

    @pl.when(i + 1 < pl.num_programs(0))
    def _():
        issue(posn_ref, 1 - cur)

    pltpu.make_async_copy(ybuf.at[cur], ybuf.at[cur], sem.at[cur]).wait()
    h = ybuf[cur, :, :, 0:D_MODEL] + ybuf[cur, :, :, D_MODEL:2 * D_MODEL]
    o_ref[...] = _rms(h, gain_ref[...])


def _final_call(pos, gain, y):
    t = pos.shape[0]
    n = t // FINAL_TILE
    rows = FINAL_TILE // SUBLANES
    out = pl.pallas_call(
        _final_kernel,
        grid=(n,),
        in_specs=[
            pl.BlockSpec((FINAL_TILE,), lambda i: (i,), memory_space=pltpu.SMEM),
            pl.BlockSpec((FINAL_TILE,), lambda i: (jnp.minimum(i + 1, n - 1),), memory_space=pltpu.SMEM),
            pl.BlockSpec((1, 1, D_MODEL), lambda i: (0, 0, 0)),
            pl.BlockSpec(memory_space=pl.ANY),
        ],
        out_specs=pl.BlockSpec((rows, SUBLANES, D_MODEL), lambda i: (i, 0, 0)),
        out_shape=jax.ShapeDtypeStruct((t // SUBLANES, SUBLANES, D_MODEL), f32),
        scratch_shapes=[pltpu.VMEM((2, rows, SUBLANES, 2 * D_MODEL), f32), pltpu.SemaphoreType.DMA((2,))],
        compiler_params=pltpu.CompilerParams(
            dimension_semantics=("arbitrary",), vmem_limit_bytes=VMEM_LIMIT),
        name="final",
    )(pos, pos, gain.reshape(1, 1, D_MODEL), y)
    return out.reshape(t, D_MODEL)


def _rope_tables(pos):
    inv = ROPE_BASE ** (-jnp.arange(0, MLA_ROPE, 2, dtype=f32) / MLA_ROPE)
    ang = pos.astype(f32)[:, None] * inv[None, :]
    cos, sin = jnp.cos(ang), jnp.sin(ang)
    z = jnp.zeros((pos.shape[0], LANE - MLA_ROPE), f32)
    return jnp.concatenate([cos, cos, z], axis=1), jnp.concatenate([-sin, sin, z], axis=1)


def _relayout_weights(w_in, w_qb, w_kvb):
    half = MLA_ROPE // 2
    perm = (np.arange(MLA_ROPE) + half) % MLA_ROPE
    pts = np.cumsum((GLA_QK, GLA_QK, GLA_VW, GLA_VW, GLA_GATE_RANK, MLA_Q_RANK, MLA_KV_RANK, MLA_ROPE))
    q_g, k_g, v_g, r_g, a_l, q_lat, kv_lat, k_rope = jnp.split(w_in, pts[:-1], axis=1)
    a_seg = jnp.pad(a_l, ((0, 0), (0, LANE - GLA_GATE_RANK)))
    w_in_r = jnp.concatenate(
        [q_g, k_g, v_g, r_g, q_lat, kv_lat, k_rope, k_rope[:, perm], a_seg], axis=1).astype(bf16)
    qcols, kcols, vcols = [], [], []
    for h in range(MLA_HEADS):
        c = h * (MLA_NOPE + MLA_ROPE)
        rope = w_qb[:, c + MLA_NOPE:c + MLA_NOPE + MLA_ROPE]
        qcols += [w_qb[:, c:c + MLA_NOPE], rope, rope[:, perm]]
        c2 = h * (MLA_NOPE + MLA_V)
        kcols.append(w_kvb[:, c2:c2 + MLA_NOPE])
        vcols.append(w_kvb[:, c2 + MLA_NOPE:c2 + MLA_NOPE + MLA_V])
    return w_in_r, jnp.concatenate(qcols, axis=1).astype(bf16), jnp.concatenate(kcols + vcols, axis=1).astype(bf16)


_BUCKET_GROUP = np.arange(N_BUCKETS) // N_PAIRS
_RUN_EXPERT = np.concatenate([_BUCKET_GROUP * EXPERTS_PER_GROUP + _PAIR_LO[np.arange(N_BUCKETS) % N_PAIRS],
                              _BUCKET_GROUP * EXPERTS_PER_GROUP + _PAIR_HI[np.arange(N_BUCKETS) % N_PAIRS]])
_RUN_IS_EXPERT = (_RUN_EXPERT[:, None] == np.arange(N_EXPERTS)[None, :]).astype(np.int32)
_RUN_BEFORE = ((_RUN_EXPERT[:, None] == _RUN_EXPERT[None, :])
               & (np.arange(2 * N_BUCKETS)[None, :] < np.arange(2 * N_BUCKETS)[:, None])).astype(np.int32)


def _route_plan(counts, bucket, rank, n_tok):
    nt = counts.shape[0]
    g_n = MOE_GROUP
    tot = counts.sum(axis=0)
    nblk = (tot + MOE_BLOCK - 1) // MOE_BLOCK
    bstart_blk = jnp.cumsum(nblk) - nblk
    n_blocks = jnp.sum(nblk)
    tile_base = bstart_blk[None, :] * MOE_BLOCK + jnp.cumsum(counts, axis=0) - counts
    hit = bucket.reshape(nt, -1, 1) == jnp.arange(N_BUCKETS, dtype=i32)
    pos = jnp.sum(jnp.where(hit, tile_base[:, None, :], 0), axis=-1).reshape(-1) + rank
    nb_max = (n_tok + N_BUCKETS * (MOE_BLOCK - 1)) // MOE_BLOCK

    n_run = jnp.concatenate([nblk, nblk])
    b0_run = jnp.concatenate([bstart_blk, bstart_blk])
    c_e = jnp.sum(n_run[:, None] * _RUN_IS_EXPERT, axis=0)
    g_e = (c_e + g_n - 1) // g_n
    gend = jnp.cumsum(g_e)
    gstart = gend - g_e
    n_compute = gend[-1]
    off_run = jnp.sum(_RUN_BEFORE * n_run[None, :], axis=1)
    f_run = jnp.sum(_RUN_IS_EXPERT * gstart[None, :], axis=1) * g_n + off_run

    n_steps = (2 * nb_max + N_EXPERTS * (g_n - 1) + g_n - 1) // g_n + 1
    f = jnp.arange(n_steps * g_n, dtype=i32)
    in_run = (f[:, None] >= f_run[None, :]) & (f[:, None] < (f_run + n_run)[None, :])
    valid_c = jnp.any(in_run, axis=1)
    block_c = jnp.sum(jnp.where(in_run, b0_run[None, :] + f[:, None] - f_run[None, :], 0), axis=1)
    role_c = jnp.sum(jnp.where(in_run[:, N_BUCKETS:], 1, 0), axis=1)
    u_idx = f - n_compute * g_n
    valid_f = (u_idx >= 0) & (u_idx < 2 * (nb_max - n_blocks))
    spare = nb_max + (f // g_n) % 2 * g_n + f % g_n
    slot_in = jnp.where(valid_c, block_c, 0)
    slot_out = jnp.where(valid_c, block_c, jnp.where(valid_f, n_blocks + u_idx // 2, spare))
    slot_role = jnp.where(valid_c, role_c, jnp.where(valid_f, u_idx % 2, 0))

    step = jnp.arange(n_steps, dtype=i32)
    e_of_step = jnp.minimum(jnp.sum(gend[None, :] <= step[:, None], axis=1), N_EXPERTS - 1)
    is_compute = step < n_compute
    last_e = jnp.max(jnp.where(is_compute, e_of_step, 0))
    step_expert = jnp.where(is_compute, e_of_step, last_e)
    step_first = jnp.concatenate([jnp.ones((1,), bool), step_expert[1:] != step_expert[:-1]])
    ordinal = jnp.cumsum(step_first.astype(i32)) - 1
    ords = jnp.arange(N_EXPERTS + 1, dtype=i32)
    expert_of_ord = jnp.sum(jnp.where(step_first[:, None] & (ordinal[:, None] == ords[None, :]),
                                      step_expert[:, None], 0), axis=0)
    has_next = ordinal + 1 <= ordinal[-1]
    next_expert = jnp.sum(jnp.where(ords[None, :] == ordinal[:, None] + 1, expert_of_ord[None, :], 0), axis=1)
    step_next = jnp.where(step_first & has_next, next_expert, -1)
    plan = tuple(a.astype(i32) for a in
                 (step_expert, step_first, is_compute, ordinal % 2, step_next, slot_in, slot_out, slot_role))
    last_blk = jnp.where(nblk > 0, bstart_blk + nblk - 1, -1)
    spare = n_blocks + jnp.arange(nb_max - n_tok // MOE_BLOCK, dtype=i32)
    zero_blocks = jnp.concatenate([last_blk, jnp.where(spare < nb_max, spare, -1)]).astype(i32)
    return pos.astype(i32), plan, zero_blocks, nb_max


def kernel(x, meta_tokens, mix_norm, w_in, gla_w_a2, gla_b_a, gla_out_norm, mla_q_norm, mla_w_qb, mla_kv_norm,
           mla_w_kvb, w_out, ffn_norm, router_group_w, router_group_b, router_expert_w, router_expert_b,
           expert_w_gate, expert_w_up, expert_w_down, final_norm):
    batch, seq, d = x.shape
    assert PREP_TILE == ATT_TILE
    assert d == D_MODEL and seq % max(PREP_TILE, GLA_TILE, ATT_TILE) == 0
    assert (batch * seq) % max(OUT_TILE, SCATTER_TILE, FINAL_TILE) == 0 and batch % GLA_BATCH == 0
    n_tok = batch * seq
    x2d = x.reshape(n_tok, d)

    w_in_r, w_qb_r, w_kvb_r = _relayout_weights(w_in[0], mla_w_qb[0], mla_w_kvb[0])
    mixg = mix_norm[0].reshape(1, d)
    qn = mla_q_norm[0].reshape(1, MLA_Q_RANK)
    kvn = mla_kv_norm[0].reshape(1, MLA_KV_RANK)
    ct_m, st_m = _rope_tables(jnp.arange(META_TILE))
    ct_x, st_x = _rope_tables(N_META + jnp.arange(seq))

    x_meta = jnp.pad(meta_tokens.astype(f32), ((0, META_TILE - N_META), (0, 0)))
    _, kg_m, vg_m, _, a_m, _, km_m, vmt_m = _prep_call(
        x_meta, META_TILE, META_TILE, mixg, w_in_r, qn, w_qb_r, kvn, w_kvb_r, ct_m, st_m)
    qg, kg, vg, rg, ag, qm, km, vmt = _prep_call(
        x2d, seq, PREP_TILE, mixg, w_in_r, qn, w_qb_r, kvn, w_kvb_r, ct_x, st_x)

    def chunk0(a):
        return jnp.pad(a[:N_META], ((CHUNK - N_META, 0), (0, 0)))

    wa2_p = jnp.pad(gla_w_a2[0], ((0, LANE - GLA_GATE_RANK), (0, 0))).astype(bf16)
    y_gla = _gla_call(qg, kg, vg, rg, ag, chunk0(kg_m), chunk0(vg_m), chunk0(a_m),
                      wa2_p, gla_b_a[0].reshape(1, GLA_QK), gla_out_norm[0].reshape(1, GLA_VW), batch, seq)
    y_mla = _mla_call(qm, km, vmt, km_m[:N_META], vmt_m[0, :, :N_META], batch, seq)

    wo = w_out[0].astype(bf16)
    rw = jnp.concatenate([router_group_w[0], router_expert_w[0],
                          jnp.zeros((d, LANE - N_GROUPS - N_EXPERTS), f32)], axis=1)
    rb = jnp.concatenate([router_group_b[0], router_expert_b[0],
                          jnp.zeros((LANE - N_GROUPS - N_EXPERTS,), f32)]).reshape(1, LANE)
    ffn_gain = ffn_norm[0].reshape(1, d)
    ux, cnt, routes = _outproj_call(x2d, y_gla, y_mla, wo[:GLA_VW], wo[GLA_VW:], ffn_gain,
                                      rw.T.astype(bf16), rb.reshape(LANE, 1))

    counts = cnt.reshape(-1, BUCKET_LANES)[:, :N_BUCKETS].astype(i32)
    tok_bucket = routes[:, 0, :].reshape(-1).astype(i32)
    tok_rank = routes[:, 1, :].reshape(-1).astype(i32)
    pos, plan, zero_blocks, nb_max = _route_plan(counts, tok_bucket, tok_rank, n_tok)
    n_slots = nb_max * MOE_BLOCK
    hs = _scatter_call(pos, zero_blocks, ux, n_slots)
    y = _moe_call(plan, hs, expert_w_gate[0], expert_w_up[0], expert_w_down[0], ffn_gain)
    out = _final_call(pos, final_norm.reshape(1, d), y)
    return out.reshape(batch, seq, d)
```

```python
import functools

import numpy as np
import jax
import jax.numpy as jnp
from jax import lax
from jax.experimental import pallas as pl
from jax.experimental.pallas import tpu as pltpu

f32 = jnp.float32
bf16 = jnp.bfloat16
i32 = jnp.int32

D_MODEL = 1024
CHUNK = 64
N_META = 16
EPS = 1e-6
GLA_HEADS = 4
GLA_DK = 64
GLA_DV = 128
GLA_GATE_RANK = 16
GLA_TAU = 16.0
GLA_QK = GLA_HEADS * GLA_DK
GLA_VW = GLA_HEADS * GLA_DV
MLA_HEADS = 4
MLA_Q_RANK = 256
MLA_KV_RANK = 128
MLA_NOPE = 128
MLA_ROPE = 64
MLA_V = 128
MLA_OUT = MLA_HEADS * MLA_V
MLA_QK_PAD = 256
MLA_VA = MLA_V + 16
LOG2_E = 1.4426950408889634
ROPE_BASE = 10000.0
N_GROUPS = 8
EXPERTS_PER_GROUP = 8
N_EXPERTS = N_GROUPS * EXPERTS_PER_GROUP
D_EXPERT = 512
N_PAIRS = EXPERTS_PER_GROUP * (EXPERTS_PER_GROUP - 1) // 2
N_BUCKETS = N_GROUPS * N_PAIRS
BUCKET_LANES = 256
LANE = 128
SUBLANES = 8
META_W = LANE
ROW_W = D_MODEL + META_W

PREP_TILE = 512
GLA_TILE = 512
GLA_BATCH = 4
ATT_TILE = 512
ATT_HEADS = 4
META_TILE = 128
OUT_TILE = 1024
ROUTE_ROWS = 256
ROUTE_GROUP = 1024
SCATTER_TILE = 2048
FINAL_TILE = 1024
ISSUE_UNROLL = 128
MOE_BLOCK = 32
MOE_GROUP = 16
VMEM_LIMIT = 56 * 1024 * 1024

C_Q, C_K, C_V, C_R = 0, 256, 512, 1024
C_QLAT, C_KVLAT, C_KROPE, C_A, C_END = 1536, 1792, 1920, 2048, 2176

_TILE_POS = np.arange(GLA_TILE)
_CHUNK_PREFIX = ((_TILE_POS[:, None] // CHUNK == _TILE_POS[None, :] // CHUNK)
                 & (_TILE_POS[None, :] <= _TILE_POS[:, None])).astype(np.float32)
_PAIR_LO = np.array([lo for lo in range(8) for hi in range(lo + 1, 8)], np.int32)
_PAIR_HI = np.array([hi for lo in range(8) for hi in range(lo + 1, 8)], np.int32)


def _dot(a, b):
    return jnp.dot(a, b, preferred_element_type=f32)


def _dot_nt(a, b):
    return lax.dot_general(a, b, (((1,), (1,)), ((), ())), preferred_element_type=f32)


def _dot_tn(a, b):
    return lax.dot_general(a, b, (((0,), (0,)), ((), ())), preferred_element_type=f32)


def _rms(x, gain):
    return x * lax.rsqrt(jnp.mean(x * x, axis=-1, keepdims=True) + EPS) * gain


def _split3(x):
    hi = x.astype(bf16)
    r1 = x - hi.astype(f32)
    mid = r1.astype(bf16)
    lo = (r1 - mid.astype(f32)).astype(bf16)
    return hi, mid, lo


def _prep_kernel(x_ref, g_ref, win_ref, qn_ref, wqb_ref, kvn_ref, wkvb_ref, ct_ref, st_ref,
                 qg_ref, kg_ref, vg_ref, rg_ref, a_ref, qm_ref, km_ref, vmt_ref):
    u = _rms(x_ref[...], g_ref[...]).astype(bf16)

    def proj(lo, hi):
        return _dot(u, win_ref[:, lo:hi])

    qg_ref[...] = proj(C_Q, C_K).astype(bf16)
    kg_ref[...] = proj(C_K, C_V).astype(bf16)
    vg_ref[...] = proj(C_V, C_R).astype(bf16)
    rg_ref[...] = proj(C_R, C_QLAT).astype(bf16)
    z = proj(C_QLAT, C_END)
    a_ref[...] = z[:, C_A - C_QLAT:].astype(bf16)
    ctab = ct_ref[...]
    stab = st_ref[...]

    def rope(seg):
        return seg * ctab + pltpu.roll(seg, 64, axis=1) * stab

    k_rope = rope(z[:, C_KROPE - C_QLAT:C_A - C_QLAT]).astype(bf16)
    qn = _rms(z[:, 0:MLA_Q_RANK], qn_ref[...]).astype(bf16)
    kvn = _rms(z[:, MLA_Q_RANK:MLA_Q_RANK + MLA_KV_RANK], kvn_ref[...]).astype(bf16)
    scale = (MLA_NOPE + MLA_ROPE) ** -0.5 * LOG2_E
    qf = _dot(qn, wqb_ref[...])
    kvf = _dot(kvn, wkvb_ref[...])
    for h in range(MLA_HEADS):
        c = h * MLA_QK_PAD
        qm_ref[:, c:c + LANE] = (qf[:, c:c + LANE] * scale).astype(bf16)
        qm_ref[:, c + LANE:c + 2 * LANE] = (rope(qf[:, c + LANE:c + 2 * LANE]) * scale).astype(bf16)
        km_ref[:, c:c + LANE] = kvf[:, h * LANE:(h + 1) * LANE].astype(bf16)
        km_ref[:, c + LANE:c + 2 * LANE] = k_rope
    vt = kvf[:, MLA_HEADS * MLA_NOPE:].T
    for h in range(MLA_HEADS):
        vmt_ref[h * MLA_VA:h * MLA_VA + MLA_V, :] = vt[h * MLA_V:(h + 1) * MLA_V].astype(bf16)
        vmt_ref[h * MLA_VA + MLA_V:(h + 1) * MLA_VA, :] = jnp.ones((MLA_VA - MLA_V, vt.shape[1]), bf16)


def _prep_call(x2d, rows_per_seq, tile, gain, w_in_r, q_norm, w_qb_r, kv_norm, w_kvb_r, ctab, stab):
    t = x2d.shape[0]
    nj = rows_per_seq // tile
    grid = (t // rows_per_seq, nj)

    def row(b, j):
        return (b * nj + j, 0)

    def const(b, j):
        return (0, 0)

    def tab(b, j):
        return (j, 0)

    widths = (GLA_QK, GLA_QK, GLA_VW, GLA_VW, LANE, MLA_HEADS * MLA_QK_PAD, MLA_HEADS * MLA_QK_PAD)
    return pl.pallas_call(
        _prep_kernel,
        grid=grid,
        in_specs=[
            pl.BlockSpec((tile, D_MODEL), row),
            pl.BlockSpec((1, D_MODEL), const),
            pl.BlockSpec((D_MODEL, C_END), const),
            pl.BlockSpec((1, MLA_Q_RANK), const),
            pl.BlockSpec((MLA_Q_RANK, MLA_HEADS * MLA_QK_PAD), const),
            pl.BlockSpec((1, MLA_KV_RANK), const),
            pl.BlockSpec((MLA_KV_RANK, 2 * MLA_OUT), const),
            pl.BlockSpec((tile, LANE), tab),
            pl.BlockSpec((tile, LANE), tab),
        ],
        out_specs=[pl.BlockSpec((tile, w), row) for w in widths]
        + [pl.BlockSpec((None, MLA_HEADS * MLA_VA, tile), lambda b, j: (b * nj + j, 0, 0))],
        out_shape=[jax.ShapeDtypeStruct((t, w), bf16) for w in widths]
        + [jax.ShapeDtypeStruct((t // tile, MLA_HEADS * MLA_VA, tile), bf16)],
        compiler_params=pltpu.CompilerParams(
            dimension_semantics=("parallel", "parallel"), vmem_limit_bytes=VMEM_LIMIT),
        name="prep",
    )(x2d, gain, w_in_r, q_norm, w_qb_r, kv_norm, w_kvb_r, ctab, stab)


def _gla_log_decay(a, wa2_ref, ba_ref):
    s = _dot(a, wa2_ref[...]) + ba_ref[...]
    return (jnp.minimum(s, 0.0) - jnp.log(1.0 + jnp.exp(-jnp.abs(s)))) * (1.0 / GLA_TAU)


def _gla_front(q, k, v, la, tri, want_out):
    nc = la.shape[0] // CHUNK
    hi, mid, lo = _split3(la)
    b = _dot(tri, hi) + _dot(tri, mid) + _dot(tri, lo)
    b_last = [b[(c + 1) * CHUNK - 1:(c + 1) * CHUNK, :] for c in range(nc)]
    b_last_full = jnp.concatenate([jnp.broadcast_to(bl, (CHUNK, GLA_QK)) for bl in b_last], axis=0)
    kf = k.astype(f32)
    front = dict(v=v, b_last=b_last, kd=(kf * jnp.exp(b_last_full - b)).astype(bf16))
    if want_out:
        front.update(qe=(q.astype(f32) * (GLA_DK ** -0.5) * jnp.exp(b)).astype(bf16),
                     ke=kf * jnp.exp(-b), vf=v.astype(f32))
    return front


def _gla_chunks(front, st_ref, want_out):
    v, kd, b_last = front["v"], front["kd"], front["b_last"]
    rr = lax.broadcasted_iota(i32, (GLA_VW, GLA_QK), 0) // GLA_DV
    cc = lax.broadcasted_iota(i32, (GLA_VW, GLA_QK), 1) // GLA_DK
    if want_out:
        qe, ke, vf = front["qe"], front["ke"], front["vf"]
        lane_h = lax.broadcasted_iota(i32, (CHUNK, GLA_QK), 1) // GLA_DK
        vlane_h = lax.broadcasted_iota(i32, (CHUNK, GLA_VW), 1) // GLA_DV
        a_row = lax.broadcasted_iota(i32, (CHUNK, GLA_QK), 0)
        a_col = lax.broadcasted_iota(i32, (CHUNK, GLA_QK), 1) % CHUNK
    outs = []
    st = st_ref[...]
    for c in range(len(b_last)):
        rows = slice(c * CHUNK, (c + 1) * CHUNK)
        upd = jnp.where(rr == cc, _dot_tn(v[rows], kd[rows]), 0.0)
        if want_out:
            kbd = jnp.concatenate(
                [jnp.where(lane_h == h, ke[rows], 0.0) for h in range(GLA_HEADS)], axis=0).astype(bf16)
            att = jnp.where(a_col <= a_row, _dot_nt(qe[rows], kbd), 0.0).astype(bf16)
            vbd = jnp.concatenate(
                [jnp.where(vlane_h == h, vf[rows], 0.0) for h in range(GLA_HEADS)], axis=0).astype(bf16)
            outs.append(_dot(att, vbd) + _dot_nt(qe[rows], st.astype(bf16)))
        st = st * jnp.exp(b_last[c]) + upd
    st_ref[...] = st
    return jnp.concatenate(outs, axis=0) if want_out else None


def _gla_kernel(q_ref, k_ref, v_ref, r_ref, a_ref, km_ref, vm_ref, am_ref, wa2_ref, ba_ref, gain_ref, tri_ref,
                y_ref, st_ref):
    j = pl.program_id(1)

    @pl.when(j == 0)
    def _():
        st_ref[...] = jnp.zeros_like(st_ref)
        la = _gla_log_decay(am_ref[...], wa2_ref, ba_ref)
        row = lax.broadcasted_iota(i32, la.shape, 0)
        la = jnp.where(row >= CHUNK - N_META, la, 0.0)
        front = _gla_front(None, km_ref[...], vm_ref[...], la, tri_ref[0:CHUNK, 0:CHUNK], False)
        _gla_chunks(front, st_ref.at[0], False)
        for bb in range(1, GLA_BATCH):
            st_ref[bb] = st_ref[0]

    fronts = [_gla_front(q_ref[bb], k_ref[bb], v_ref[bb], _gla_log_decay(a_ref[bb], wa2_ref, ba_ref),
                         tri_ref[...], True) for bb in range(GLA_BATCH)]
    for bb in range(GLA_BATCH):
        o = _gla_chunks(fronts[bb], st_ref.at[bb], True)
        r = r_ref[bb].astype(f32)
        outs = []
        for h in range(GLA_HEADS):
            oh = o[:, h * GLA_DV:(h + 1) * GLA_DV]
            outs.append(oh * lax.rsqrt(jnp.mean(oh * oh, axis=-1, keepdims=True) + EPS))
        on = jnp.concatenate(outs, axis=1) * gain_ref[...]
        y_ref[bb] = (on * (r * jax.nn.sigmoid(r))).astype(bf16)


def _gla_call(qg, kg, vg, rg, ag, km, vm, am, wa2_p, b_a, gain, batch, seq):
    nj = seq // GLA_TILE

    def row(b, j):
        return (b, j, 0)

    def const(b, j):
        return (0, 0)

    def seqs(a):
        return a.reshape(batch, seq, a.shape[-1])

    out = pl.pallas_call(
        _gla_kernel,
        grid=(batch // GLA_BATCH, nj),
        in_specs=[
            pl.BlockSpec((GLA_BATCH, GLA_TILE, GLA_QK), row),
            pl.BlockSpec((GLA_BATCH, GLA_TILE, GLA_QK), row),
            pl.BlockSpec((GLA_BATCH, GLA_TILE, GLA_VW), row),
            pl.BlockSpec((GLA_BATCH, GLA_TILE, GLA_VW), row),
            pl.BlockSpec((GLA_BATCH, GLA_TILE, LANE), row),
            pl.BlockSpec((CHUNK, GLA_QK), const),
            pl.BlockSpec((CHUNK, GLA_VW), const),
            pl.BlockSpec((CHUNK, LANE), const),
            pl.BlockSpec((LANE, GLA_QK), const),
            pl.BlockSpec((1, GLA_QK), const),
            pl.BlockSpec((1, GLA_VW), const),
            pl.BlockSpec((GLA_TILE, GLA_TILE), const),
        ],
        out_specs=pl.BlockSpec((GLA_BATCH, GLA_TILE, GLA_VW), row),
        out_shape=jax.ShapeDtypeStruct((batch, seq, GLA_VW), bf16),
        scratch_shapes=[pltpu.VMEM((GLA_BATCH, GLA_VW, GLA_QK), f32)],
        compiler_params=pltpu.CompilerParams(
            dimension_semantics=("parallel", "arbitrary"), vmem_limit_bytes=VMEM_LIMIT),
        name="gla",
    )(seqs(qg), seqs(kg), seqs(vg), seqs(rg), seqs(ag), km, vm, am, wa2_p, b_a, gain,
      jnp.asarray(_CHUNK_PREFIX, bf16))
    return out.reshape(batch * seq, GLA_VW)


def _mla_kernel(q_ref, k_ref, vt_ref, km_ref, vmt_ref, o_ref, sa_ref, sb_ref):
    i = pl.program_id(2)
    tq = ATT_TILE
    w = MLA_QK_PAD
    va = MLA_VA
    heads = range(ATT_HEADS)

    def scores(h, blk):
        rows = pl.ds(pl.multiple_of(blk * tq, tq), tq)
        return _dot_nt(k_ref[rows, h * w:(h + 1) * w], q_ref[:, h * w:(h + 1) * w])

    def soft(s, vtb, carry, mask=None):
        m, acc = carry
        if mask is not None:
            s = jnp.where(mask, s, -1e30)
        m_new = jnp.maximum(m, jnp.max(s, axis=0, keepdims=True))
        p = jnp.exp2(s - m_new).astype(bf16)
        return m_new, jnp.exp2(m - m_new) * acc + _dot(vtb, p)

    def vt(h, blk):
        return vt_ref[blk, h * va:(h + 1) * va, :]

    def finish(carries):
        ss = [_dot_nt(km_ref[:, h * w:(h + 1) * w], q_ref[:, h * w:(h + 1) * w]) for h in heads]
        accs = [soft(ss[h], vmt_ref[h * va:(h + 1) * va, :], carries[h])[1] for h in heads]
        for h in heads:
            acc = accs[h]
            o_ref[:, h * MLA_V:(h + 1) * MLA_V] = (acc[:MLA_V] * (1.0 / acc[MLA_V:MLA_V + 1])).T.astype(bf16)

    kc = lax.broadcasted_iota(i32, (tq, tq), 0) // CHUNK
    qc = lax.broadcasted_iota(i32, (tq, tq), 1) // CHUNK
    mask = kc <= qc

    for h in heads:
        sa_ref[h] = scores(h, 0)

    def pair(p, carries):
        b0 = 2 * p
        for h in heads:
            sb_ref[h] = scores(h, b0 + 1)
        carries = [soft(sa_ref[h], vt(h, b0), carries[h]) for h in heads]
        for h in heads:
            sa_ref[h] = scores(h, b0 + 2)
        return tuple(soft(sb_ref[h], vt(h, b0 + 1), carries[h]) for h in heads)

    init = tuple((jnp.full((1, tq), -1e30, f32), jnp.zeros((va, tq), f32)) for _ in heads)
    carries = lax.fori_loop(0, i // 2, pair, init)

    @pl.when(i % 2 == 1)
    def _():
        for h in heads:
            sb_ref[h] = scores(h, i)
        c1 = [soft(sa_ref[h], vt(h, i - 1), carries[h]) for h in heads]
        finish([soft(sb_ref[h], vt(h, i), c1[h], mask) for h in heads])

    @pl.when(i % 2 == 0)
    def _():
        finish([soft(sa_ref[h], vt(h, i), carries[h], mask) for h in heads])


def _mla_call(qm, km, vmt, km_meta, vmt_meta, batch, seq):
    nq = seq // ATT_TILE
    nh = ATT_HEADS
    qm3 = qm.reshape(batch, seq, MLA_HEADS * MLA_QK_PAD)
    km3 = km.reshape(batch, seq, MLA_HEADS * MLA_QK_PAD)
    vt4 = vmt.reshape(batch, nq, MLA_HEADS * MLA_VA, ATT_TILE)
    out = pl.pallas_call(
        _mla_kernel,
        grid=(batch, MLA_HEADS // nh, nq),
        in_specs=[
            pl.BlockSpec((None, ATT_TILE, nh * MLA_QK_PAD), lambda b, h, i: (b, i, h)),
            pl.BlockSpec((None, seq, nh * MLA_QK_PAD), lambda b, h, i: (b, 0, h)),
            pl.BlockSpec((None, nq, nh * MLA_VA, ATT_TILE), lambda b, h, i: (b, 0, h, 0)),
            pl.BlockSpec((N_META, nh * MLA_QK_PAD), lambda b, h, i: (0, h)),
            pl.BlockSpec((nh * MLA_VA, N_META), lambda b, h, i: (h, 0)),
        ],
        out_specs=pl.BlockSpec((None, ATT_TILE, nh * MLA_V), lambda b, h, i: (b, i, h)),
        out_shape=jax.ShapeDtypeStruct((batch, seq, MLA_OUT), bf16),
        scratch_shapes=[pltpu.VMEM((nh, ATT_TILE, ATT_TILE), f32), pltpu.VMEM((nh, ATT_TILE, ATT_TILE), f32)],
        compiler_params=pltpu.CompilerParams(
            dimension_semantics=("parallel", "parallel", "arbitrary"), vmem_limit_bytes=VMEM_LIMIT),
        name="mla",
    )(qm3, km3, vt4, km_meta, vmt_meta)
    return out.reshape(batch * seq, MLA_OUT)


def _route_cols(lt):
    r = lt.shape[1]
    neg = -1e30
    gl = lt[0:N_GROUPS, :]
    gsub = lax.broadcasted_iota(i32, (N_GROUPS, r), 0)
    gmax = jnp.max(gl, axis=0, keepdims=True)
    g_p = 1.0 / jnp.sum(jnp.exp(gl - gmax), axis=0, keepdims=True)
    g_idx = jnp.min(jnp.where(gl == gmax, gsub, N_GROUPS), axis=0, keepdims=True)
    el_all = lt[N_GROUPS:N_GROUPS + N_EXPERTS, :]
    esub = lax.broadcasted_iota(i32, (N_EXPERTS, r), 0)
    base = g_idx * EXPERTS_PER_GROUP
    e_mask = (esub >= base) & (esub < base + EXPERTS_PER_GROUP)
    el = jnp.where(e_mask, el_all, neg)
    m1 = jnp.max(el, axis=0, keepdims=True)
    i1 = jnp.min(jnp.where(e_mask & (el == m1), esub, N_EXPERTS), axis=0, keepdims=True)
    el2 = jnp.where(esub == i1, neg, el)
    m2 = jnp.max(el2, axis=0, keepdims=True)
    i2 = jnp.min(jnp.where(e_mask & (esub != i1) & (el2 == m2), esub, N_EXPERTS), axis=0, keepdims=True)
    rr = jnp.exp(m2 - m1)
    ga = g_p / (1.0 + rr)
    gb = g_p * rr / (1.0 + rr)
    la_ = i1 - base
    lb_ = i2 - base
    lo = jnp.minimum(la_, lb_)
    hi = jnp.maximum(la_, lb_)
    g_lo = jnp.where(la_ < lb_, ga, gb)
    g_hi = jnp.where(la_ < lb_, gb, ga)
    pidx = ((lo * (2 * EXPERTS_PER_GROUP - 1 - lo)) >> 1) + (hi - lo - 1)
    bucket = g_idx * N_PAIRS + pidx
    bsub = lax.broadcasted_iota(i32, (BUCKET_LANES, r), 0)
    oht = jnp.where(bsub == bucket, 1.0, 0.0)
    ohb = oht.astype(bf16)
    ri = lax.broadcasted_iota(i32, (ROUTE_ROWS, ROUTE_ROWS), 0)
    ci = lax.broadcasted_iota(i32, (ROUTE_ROWS, ROUTE_ROWS), 1)
    before = jnp.where(ri < ci, 1.0, 0.0).astype(bf16)
    ones = jnp.ones((SUBLANES, ROUTE_ROWS), bf16)
    subs = [slice(i * ROUTE_ROWS, (i + 1) * ROUTE_ROWS) for i in range(r // ROUTE_ROWS)]
    cum = jnp.concatenate([_dot(ohb[:, sl], before) for sl in subs], axis=1)
    rank = jnp.sum(oht * cum, axis=0, keepdims=True)
    counts = [_dot_nt(ones, ohb[:, sl])[0:1, :] for sl in subs]
    msub = lax.broadcasted_iota(i32, (LANE, r), 0)
    meta_t = jnp.where(msub == 0, bucket.astype(f32),
                       jnp.where(msub == 1, rank,
                                 jnp.where(msub == 2, g_lo, jnp.where(msub == 3, g_hi, 0.0))))
    return meta_t.T, counts, meta_t[0:SUBLANES, :]


def _outproj_kernel(x_ref, yg_ref, ym_ref, wog_ref, wom_ref, gain_ref, wrt_ref, rb_ref,
                    ux_ref, cnt_ref, rt_ref):
    per_group = ROUTE_GROUP // ROUTE_ROWS
    for grp in range(OUT_TILE // ROUTE_GROUP):
        rows = slice(grp * ROUTE_GROUP, (grp + 1) * ROUTE_GROUP)
        h1 = x_ref[rows, :] + _dot(yg_ref[rows, :], wog_ref[...]) + _dot(ym_ref[rows, :], wom_ref[...])
        ux_ref[rows, 0:D_MODEL] = h1
        u2 = _rms(h1, gain_ref[...])
        lt = _dot_nt(wrt_ref[...], u2.astype(bf16)) + rb_ref[...]
        meta, counts, routes = _route_cols(lt)
        ux_ref[rows, D_MODEL:ROW_W] = meta
        for i in range(per_group):
            cnt_ref[grp * per_group + i] = counts[i]
            rt_ref[grp * per_group + i] = routes[:, i * ROUTE_ROWS:(i + 1) * ROUTE_ROWS]


def _outproj_call(x2d, yg, ym, wo_g, wo_m, gain, w_r, rbias):
    t = x2d.shape[0]
    nt = t // OUT_TILE

    def row(i):
        return (i, 0)

    def const(i):
        return (0, 0)

    return pl.pallas_call(
        _outproj_kernel,
        grid=(nt,),
        in_specs=[
            pl.BlockSpec((OUT_TILE, D_MODEL), row),
            pl.BlockSpec((OUT_TILE, GLA_VW), row),
            pl.BlockSpec((OUT_TILE, MLA_OUT), row),
            pl.BlockSpec((GLA_VW, D_MODEL), const),
            pl.BlockSpec((MLA_OUT, D_MODEL), const),
            pl.BlockSpec((1, D_MODEL), const),
            pl.BlockSpec((LANE, D_MODEL), const),
            pl.BlockSpec((LANE, 1), const),
        ],
        out_specs=[
            pl.BlockSpec((OUT_TILE, ROW_W), row),
            pl.BlockSpec((OUT_TILE // ROUTE_ROWS, 1, BUCKET_LANES), lambda i: (i, 0, 0)),
            pl.BlockSpec((OUT_TILE // ROUTE_ROWS, SUBLANES, ROUTE_ROWS), lambda i: (i, 0, 0)),
        ],
        out_shape=[
            jax.ShapeDtypeStruct((t, ROW_W), f32),
            jax.ShapeDtypeStruct((nt * (OUT_TILE // ROUTE_ROWS), 1, BUCKET_LANES), f32),
            jax.ShapeDtypeStruct((nt * (OUT_TILE // ROUTE_ROWS), SUBLANES, ROUTE_ROWS), f32),
        ],
        compiler_params=pltpu.CompilerParams(
            dimension_semantics=("parallel",), vmem_limit_bytes=VMEM_LIMIT),
        name="outproj",
    )(x2d, yg, ym, wo_g, wo_m, gain, w_r, rbias)


def _scatter_kernel(pos_ref, zb_ref, ux_ref, hs_ref, zbuf, sem, zsem):
    @pl.when(pl.program_id(0) == 0)
    def _():
        zbuf[...] = jnp.zeros_like(zbuf)

        def zero_copy(j):
            rows = pl.ds(pl.multiple_of(zb_ref[j] * MOE_BLOCK, MOE_BLOCK), MOE_BLOCK)
            return pltpu.make_async_copy(zbuf, hs_ref.at[rows], zsem)

        def zstart(j, c):
            @pl.when(zb_ref[j] >= 0)
            def _():
                zero_copy(j).start()
            return c

        def zwait(j, c):
            @pl.when(zb_ref[j] >= 0)
            def _():
                zero_copy(j).wait()
            return c

        lax.fori_loop(0, zb_ref.shape[0], zstart, 0)
        lax.fori_loop(0, zb_ref.shape[0], zwait, 0)

    def start(io, c):
        for r in range(ISSUE_UNROLL):
            ii = io * (ISSUE_UNROLL // SUBLANES) + r // SUBLANES
            pltpu.make_async_copy(ux_ref.at[ii, pl.ds(r % SUBLANES, 1)],
                                  hs_ref.at[pl.ds(pos_ref[io * ISSUE_UNROLL + r], 1)], sem).start()
        return c

    lax.fori_loop(0, SCATTER_TILE // ISSUE_UNROLL, start, 0)
    pltpu.make_async_copy(hs_ref.at[pl.ds(0, SCATTER_TILE)], hs_ref.at[pl.ds(0, SCATTER_TILE)], sem).wait()


def _scatter_call(pos, zero_blocks, ux, n_slots):
    t = ux.shape[0]
    nz = zero_blocks.shape[0]
    return pl.pallas_call(
        _scatter_kernel,
        grid=(t // SCATTER_TILE,),
        in_specs=[
            pl.BlockSpec((SCATTER_TILE,), lambda i: (i,), memory_space=pltpu.SMEM),
            pl.BlockSpec((nz,), lambda i: (0,), memory_space=pltpu.SMEM),
            pl.BlockSpec((SCATTER_TILE // SUBLANES, SUBLANES, ROW_W), lambda i: (i, 0, 0)),
        ],
        out_specs=pl.BlockSpec(memory_space=pl.ANY),
        out_shape=jax.ShapeDtypeStruct((n_slots, ROW_W), f32),
        scratch_shapes=[pltpu.VMEM((MOE_BLOCK, ROW_W), f32), pltpu.SemaphoreType.DMA(()),
                        pltpu.SemaphoreType.DMA(())],
        compiler_params=pltpu.CompilerParams(
            dimension_semantics=("arbitrary",), vmem_limit_bytes=VMEM_LIMIT),
        name="scatter",
    )(pos, zero_blocks, ux.reshape(t // SUBLANES, SUBLANES, ROW_W))


def _moe_kernel(se_ref, sf_ref, sk_ref, sp_ref, sn_ref, sb_ref, sr_ref, si_ref, so_ref,
                hs_hbm, wg_hbm, wu_hbm, wd_hbm, fg_ref, y_hbm,
                xbuf, obuf, wg_buf, wu_buf, wd_buf, wgu_s, wd_s, in_sem, out_sem, w_sem):
    s = pl.program_id(0)
    ns = pl.num_programs(0)
    cur = s % 2
    g_n = MOE_GROUP

    def in_copy(step, g, buf):
        rows = pl.ds(pl.multiple_of(sb_ref[step * g_n + g] * MOE_BLOCK, MOE_BLOCK), MOE_BLOCK)
        return pltpu.make_async_copy(
            hs_hbm.at[rows], xbuf.at[buf, pl.ds(g * MOE_BLOCK, MOE_BLOCK)], in_sem.at[buf])

    def out_copy(step, g, buf):
        rows = pl.ds(pl.multiple_of(sb_ref[step * g_n + g] * MOE_BLOCK, MOE_BLOCK), MOE_BLOCK)
        cols = pl.ds(pl.multiple_of(sr_ref[step * g_n + g] * D_MODEL, D_MODEL), D_MODEL)
        return pltpu.make_async_copy(
            obuf.at[buf, pl.ds(g * MOE_BLOCK, MOE_BLOCK)], y_hbm.at[rows, cols], out_sem.at[buf])

    def for_slots(step, flags_ref, fn):
        for g in range(g_n):
            @pl.when(flags_ref[step * g_n + g] == 1)
            def _():
                fn(g)

    @pl.when(s == 0)
    def _():
        xbuf[...] = jnp.zeros_like(xbuf)
        for_slots(0, si_ref, lambda g: in_copy(0, g, 0).start())

    @pl.when(s + 1 < ns)
    def _():
        for_slots(s + 1, si_ref, lambda g: in_copy(s + 1, g, 1 - cur).start())

    for_slots(s, si_ref, lambda g: in_copy(s, g, cur).wait())

    @pl.when(s >= 2)
    def _():
        for_slots(s - 2, so_ref, lambda g: out_copy(s - 2, g, cur).wait())

    def w_copies(expert, slot):
        return (pltpu.make_async_copy(wg_hbm.at[expert], wg_buf.at[slot], w_sem.at[slot]),
                pltpu.make_async_copy(wu_hbm.at[expert], wu_buf.at[slot], w_sem.at[slot]),
                pltpu.make_async_copy(wd_hbm.at[expert], wd_buf.at[slot], w_sem.at[slot]))

    @pl.when(s == 0)
    def _():
        for c in w_copies(se_ref[0], sp_ref[0]):
            c.start()

    @pl.when(sf_ref[s] == 1)
    def _():
        slot = sp_ref[s]
        for c in w_copies(se_ref[s], slot):
            c.wait()
        wgu_s[:, 0:D_EXPERT] = wg_buf[slot].astype(bf16)
        wgu_s[:, D_EXPERT:2 * D_EXPERT] = wu_buf[slot].astype(bf16)
        wd_s[...] = wd_buf[slot].astype(bf16)

        @pl.when(sn_ref[s] >= 0)
        def _():
            for c in w_copies(sn_ref[s], 1 - slot):
                c.start()

    @pl.when(sk_ref[s] == 1)
    def _():
        h1 = xbuf[cur, :, 0:D_MODEL]
        meta = xbuf[cur, :, D_MODEL:ROW_W]
        u = _rms(h1, fg_ref[...]).astype(bf16)
        ones = jnp.ones((MOE_BLOCK, 1), f32)
        role0 = [sr_ref[s * g_n + g] == 0 for g in range(g_n)]
        gate = jnp.concatenate(
            [jnp.where(role0[g], meta[g * MOE_BLOCK:(g + 1) * MOE_BLOCK, 2:3],
                       meta[g * MOE_BLOCK:(g + 1) * MOE_BLOCK, 3:4]) for g in range(g_n)], axis=0)
        keep = jnp.concatenate([jnp.where(role0[g], ones, 0.0) for g in range(g_n)], axis=0)
        gu = _dot(u, wgu_s[...])
        gt = gu[:, 0:D_EXPERT]
        hdn = (gt * jax.nn.sigmoid(gt) * gu[:, D_EXPERT:]).astype(bf16)
        obuf[cur] = _dot(hdn, wd_s[...]) * gate + h1 * keep

    @pl.when(sk_ref[s] == 0)
    def _():
        obuf[cur] = jnp.zeros(obuf.shape[1:], f32)

    for_slots(s, so_ref, lambda g: out_copy(s, g, cur).start())

    @pl.when(s == ns - 1)
    def _():
        for_slots(s, so_ref, lambda g: out_copy(s, g, cur).wait())

        @pl.when(s >= 1)
        def _():
            for_slots(s - 1, so_ref, lambda g: out_copy(s - 1, g, 1 - cur).wait())


def _moe_call(plan, hs, w_gate, w_up, w_down, ffn_gain):
    n_steps = plan[0].shape[0]
    n_slots = hs.shape[0]
    rows = MOE_GROUP * MOE_BLOCK

    grid_spec = pltpu.PrefetchScalarGridSpec(
        num_scalar_prefetch=9,
        grid=(n_steps,),
        in_specs=[
            pl.BlockSpec(memory_space=pl.ANY),
            pl.BlockSpec(memory_space=pl.ANY),
            pl.BlockSpec(memory_space=pl.ANY),
            pl.BlockSpec(memory_space=pl.ANY),
            pl.BlockSpec((1, D_MODEL), lambda s, *_: (0, 0)),
        ],
        out_specs=pl.BlockSpec(memory_space=pl.ANY),
        scratch_shapes=[
            pltpu.VMEM((2, rows, ROW_W), f32),
            pltpu.VMEM((2, rows, D_MODEL), f32),
            pltpu.VMEM((2, D_MODEL, D_EXPERT), f32),
            pltpu.VMEM((2, D_MODEL, D_EXPERT), f32),
            pltpu.VMEM((2, D_EXPERT, D_MODEL), f32),
            pltpu.VMEM((D_MODEL, 2 * D_EXPERT), bf16),
            pltpu.VMEM((D_EXPERT, D_MODEL), bf16),
            pltpu.SemaphoreType.DMA((2,)),
            pltpu.SemaphoreType.DMA((2,)),
            pltpu.SemaphoreType.DMA((2,)),
        ],
    )
    return pl.pallas_call(
        _moe_kernel,
        grid_spec=grid_spec,
        out_shape=jax.ShapeDtypeStruct((n_slots, 2 * D_MODEL), f32),
        compiler_params=pltpu.CompilerParams(
            dimension_semantics=("arbitrary",), vmem_limit_bytes=VMEM_LIMIT),
        name="moe",
    )(*plan, hs, w_gate, w_up, w_down, ffn_gain)


def _final_kernel(posc_ref, posn_ref, gain_ref, y_hbm, o_ref, ybuf, sem):
    i = pl.program_id(0)
    cur = i % 2

    def issue(pos_ref, buf):
        def start(io, c):
            for r in range(ISSUE_UNROLL):
                ii = io * (ISSUE_UNROLL // SUBLANES) + r // SUBLANES
                pltpu.make_async_copy(y_hbm.at[pl.ds(pos_ref[io * ISSUE_UNROLL + r], 1)],
                                      ybuf.at[buf, ii, pl.ds(r % SUBLANES, 1)], sem.at[buf]).start()
            return c

        lax.fori_loop(0, FINAL_TILE // ISSUE_UNROLL, start, 0)

    @pl.when(i == 0)
    def _():
        issue(posc_ref, 0)

    @pl.when(i + 1 < pl.num_programs(0))
    def _():
        issue(posn_ref, 1 - cur)

    pltpu.make_async_copy(ybuf.at[cur], ybuf.at[cur], sem.at[cur]).wait()
    h = ybuf[cur, :, :, 0:D_MODEL] + ybuf[cur, :, :, D_MODEL:2 * D_MODEL]
    o_ref[...] = _rms(h, gain_ref[...])


def _final_call(pos, gain, y):
    t = pos.shape[0]
    n = t // FINAL_TILE
    rows = FINAL_TILE // SUBLANES
    out = pl.pallas_call(
        _final_kernel,
        grid=(n,),
        in_specs=[
            pl.BlockSpec((FINAL_TILE,), lambda i: (i,), memory_space=pltpu.SMEM),
            pl.BlockSpec((FINAL_TILE,), lambda i: (jnp.minimum(i + 1, n - 1),), memory_space=pltpu.SMEM),
            pl.BlockSpec((1, 1, D_MODEL), lambda i: (0, 0, 0)),
            pl.BlockSpec(memory_space=pl.ANY),
        ],
        out_specs=pl.BlockSpec((rows, SUBLANES, D_MODEL), lambda i: (i, 0, 0)),
        out_shape=jax.ShapeDtypeStruct((t // SUBLANES, SUBLANES, D_MODEL), f32),
        scratch_shapes=[pltpu.VMEM((2, rows, SUBLANES, 2 * D_MODEL), f32), pltpu.SemaphoreType.DMA((2,))],
        compiler_params=pltpu.CompilerParams(
            dimension_semantics=("arbitrary",), vmem_limit_bytes=VMEM_LIMIT),
        name="final",
    )(pos, pos, gain.reshape(1, 1, D_MODEL), y)
    return out.reshape(t, D_MODEL)


def _rope_tables(pos):
    inv = ROPE_BASE ** (-jnp.arange(0, MLA_ROPE, 2, dtype=f32) / MLA_ROPE)
    ang = pos.astype(f32)[:, None] * inv[None, :]
    cos, sin = jnp.cos(ang), jnp.sin(ang)
    z = jnp.zeros((pos.shape[0], LANE - MLA_ROPE), f32)
    return jnp.concatenate([cos, cos, z], axis=1), jnp.concatenate([-sin, sin, z], axis=1)


def _relayout_weights(w_in, w_qb, w_kvb):
    half = MLA_ROPE // 2
    perm = (np.arange(MLA_ROPE) + half) % MLA_ROPE
    pts = np.cumsum((GLA_QK, GLA_QK, GLA_VW, GLA_VW, GLA_GATE_RANK, MLA_Q_RANK, MLA_KV_RANK, MLA_ROPE))
    q_g, k_g, v_g, r_g, a_l, q_lat, kv_lat, k_rope = jnp.split(w_in, pts[:-1], axis=1)
    a_seg = jnp.pad(a_l, ((0, 0), (0, LANE - GLA_GATE_RANK)))
    w_in_r = jnp.concatenate(
        [q_g, k_g, v_g, r_g, q_lat, kv_lat, k_rope, k_rope[:, perm], a_seg], axis=1).astype(bf16)
    qcols, kcols, vcols = [], [], []
    for h in range(MLA_HEADS):
        c = h * (MLA_NOPE + MLA_ROPE)
        rope = w_qb[:, c + MLA_NOPE:c + MLA_NOPE + MLA_ROPE]
        qcols += [w_qb[:, c:c + MLA_NOPE], rope, rope[:, perm]]
        c2 = h * (MLA_NOPE + MLA_V)
        kcols.append(w_kvb[:, c2:c2 + MLA_NOPE])
        vcols.append(w_kvb[:, c2 + MLA_NOPE:c2 + MLA_NOPE + MLA_V])
    return w_in_r, jnp.concatenate(qcols, axis=1).astype(bf16), jnp.concatenate(kcols + vcols, axis=1).astype(bf16)


_BUCKET_GROUP = np.arange(N_BUCKETS) // N_PAIRS
_RUN_EXPERT = np.concatenate([_BUCKET_GROUP * EXPERTS_PER_GROUP + _PAIR_LO[np.arange(N_BUCKETS) % N_PAIRS],
                              _BUCKET_GROUP * EXPERTS_PER_GROUP + _PAIR_HI[np.arange(N_BUCKETS) % N_PAIRS]])
_RUN_IS_EXPERT = (_RUN_EXPERT[:, None] == np.arange(N_EXPERTS)[None, :]).astype(np.int32)
_RUN_BEFORE = ((_RUN_EXPERT[:, None] == _RUN_EXPERT[None, :])
               & (np.arange(2 * N_BUCKETS)[None, :] < np.arange(2 * N_BUCKETS)[:, None])).astype(np.int32)


def _route_plan(counts, bucket, rank, n_tok):
    nt = counts.shape[0]
    g_n = MOE_GROUP
    tot = counts.sum(axis=0)
    nblk = (tot + MOE_BLOCK - 1) // MOE_BLOCK
    bstart_blk = jnp.cumsum(nblk) - nblk
    n_blocks = jnp.sum(nblk)
    tile_base = bstart_blk[None, :] * MOE_BLOCK + jnp.cumsum(counts, axis=0) - counts
    hit = bucket.reshape(nt, -1, 1) == jnp.arange(N_BUCKETS, dtype=i32)
    pos = jnp.sum(jnp.where(hit, tile_base[:, None, :], 0), axis=-1).reshape(-1) + rank
    nb_max = (n_tok + N_BUCKETS * (MOE_BLOCK - 1)) // MOE_BLOCK

    n_run = jnp.concatenate([nblk, nblk])
    b0_run = jnp.concatenate([bstart_blk, bstart_blk])
    c_e = jnp.sum(n_run[:, None] * _RUN_IS_EXPERT, axis=0)
    g_e = (c_e + g_n - 1) // g_n
    gend = jnp.cumsum(g_e)
    gstart = gend - g_e
    n_compute = gend[-1]
    off_run = jnp.sum(_RUN_BEFORE * n_run[None, :], axis=1)
    f_run = jnp.sum(_RUN_IS_EXPERT * gstart[None, :], axis=1) * g_n + off_run

    n_steps = (2 * nb_max + N_EXPERTS * (g_n - 1) + g_n - 1) // g_n + 1
    f = jnp.arange(n_steps * g_n, dtype=i32)
    in_run = (f[:, None] >= f_run[None, :]) & (f[:, None] < (f_run + n_run)[None, :])
    valid_c = jnp.any(in_run, axis=1)
    block_c = jnp.sum(jnp.where(in_run, b0_run[None, :] + f[:, None] - f_run[None, :], 0), axis=1)
    role_c = jnp.sum(jnp.where(in_run[:, N_BUCKETS:], 1, 0), axis=1)
    u_idx = f - n_compute * g_n
    valid_f = (u_idx >= 0) & (u_idx < 2 * (nb_max - n_blocks))
    slot_block = jnp.where(valid_c, block_c, jnp.where(valid_f, n_blocks + u_idx // 2, 0))
    slot_role = jnp.where(valid_c, role_c, jnp.where(valid_f, u_idx % 2, 0))

    step = jnp.arange(n_steps, dtype=i32)
    e_of_step = jnp.minimum(jnp.sum(gend[None, :] <= step[:, None], axis=1), N_EXPERTS - 1)
    is_compute = step < n_compute
    last_e = jnp.max(jnp.where(is_compute, e_of_step, 0))
    step_expert = jnp.where(is_compute, e_of_step, last_e)
    step_first = jnp.concatenate([jnp.ones((1,), bool), step_expert[1:] != step_expert[:-1]])
    ordinal = jnp.cumsum(step_first.astype(i32)) - 1
    ords = jnp.arange(N_EXPERTS + 1, dtype=i32)
    expert_of_ord = jnp.sum(jnp.where(step_first[:, None] & (ordinal[:, None] == ords[None, :]),
                                      step_expert[:, None], 0), axis=0)
    has_next = ordinal + 1 <= ordinal[-1]
    next_expert = jnp.sum(jnp.where(ords[None, :] == ordinal[:, None] + 1, expert_of_ord[None, :], 0), axis=1)
    step_next = jnp.where(step_first & has_next, next_expert, -1)
    plan = tuple(a.astype(i32) for a in
                 (step_expert, step_first, is_compute, ordinal % 2, step_next,
                  slot_block, slot_role, valid_c, valid_c | valid_f))
    last_blk = jnp.where(nblk > 0, bstart_blk + nblk - 1, -1)
    spare = n_blocks + jnp.arange(nb_max - n_tok // MOE_BLOCK, dtype=i32)
    zero_blocks = jnp.concatenate([last_blk, jnp.where(spare < nb_max, spare, -1)]).astype(i32)
    return pos.astype(i32), plan, zero_blocks, nb_max


def kernel(x, meta_tokens, mix_norm, w_in, gla_w_a2, gla_b_a, gla_out_norm, mla_q_norm, mla_w_qb, mla_kv_norm,
           mla_w_kvb, w_out, ffn_norm, router_group_w, router_group_b, router_expert_w, router_expert_b,
           expert_w_gate, expert_w_up, expert_w_down, final_norm):
    batch, seq, d = x.shape
    assert PREP_TILE == ATT_TILE
    assert d == D_MODEL and seq % max(PREP_TILE, GLA_TILE, ATT_TILE) == 0
    assert (batch * seq) % max(OUT_TILE, SCATTER_TILE, FINAL_TILE) == 0 and batch % GLA_BATCH == 0
    n_tok = batch * seq
    x2d = x.reshape(n_tok, d)

    w_in_r, w_qb_r, w_kvb_r = _relayout_weights(w_in[0], mla_w_qb[0], mla_w_kvb[0])
    mixg = mix_norm[0].reshape(1, d)
    qn = mla_q_norm[0].reshape(1, MLA_Q_RANK)
    kvn = mla_kv_norm[0].reshape(1, MLA_KV_RANK)
    ct_m, st_m = _rope_tables(jnp.arange(META_TILE))
    ct_x, st_x = _rope_tables(N_META + jnp.arange(seq))

    x_meta = jnp.pad(meta_tokens.astype(f32), ((0, META_TILE - N_META), (0, 0)))
    _, kg_m, vg_m, _, a_m, _, km_m, vmt_m = _prep_call(
        x_meta, META_TILE, META_TILE, mixg, w_in_r, qn, w_qb_r, kvn, w_kvb_r, ct_m, st_m)
    qg, kg, vg, rg, ag, qm, km, vmt = _prep_call(
        x2d, seq, PREP_TILE, mixg, w_in_r, qn, w_qb_r, kvn, w_kvb_r, ct_x, st_x)

    def chunk0(a):
        return jnp.pad(a[:N_META], ((CHUNK - N_META, 0), (0, 0)))

    wa2_p = jnp.pad(gla_w_a2[0], ((0, LANE - GLA_GATE_RANK), (0, 0))).astype(bf16)
    y_gla = _gla_call(qg, kg, vg, rg, ag, chunk0(kg_m), chunk0(vg_m), chunk0(a_m),
                      wa2_p, gla_b_a[0].reshape(1, GLA_QK), gla_out_norm[0].reshape(1, GLA_VW), batch, seq)
    y_mla = _mla_call(qm, km, vmt, km_m[:N_META], vmt_m[0, :, :N_META], batch, seq)

    wo = w_out[0].astype(bf16)
    rw = jnp.concatenate([router_group_w[0], router_expert_w[0],
                          jnp.zeros((d, LANE - N_GROUPS - N_EXPERTS), f32)], axis=1)
    rb = jnp.concatenate([router_group_b[0], router_expert_b[0],
                          jnp.zeros((LANE - N_GROUPS - N_EXPERTS,), f32)]).reshape(1, LANE)
    ffn_gain = ffn_norm[0].reshape(1, d)
    ux, cnt, routes = _outproj_call(x2d, y_gla, y_mla, wo[:GLA_VW], wo[GLA_VW:], ffn_gain,
                                      rw.T.astype(bf16), rb.reshape(LANE, 1))

    counts = cnt.reshape(-1, BUCKET_LANES)[:, :N_BUCKETS].astype(i32)
    tok_bucket = routes[:, 0, :].reshape(-1).astype(i32)
    tok_rank = routes[:, 1, :].reshape(-1).astype(i32)
    pos, plan, zero_blocks, nb_max = _route_plan(counts, tok_bucket, tok_rank, n_tok)
    n_slots = nb_max * MOE_BLOCK
    hs = _scatter_call(pos, zero_blocks, ux, n_slots)
    y = _moe_call(plan, hs, expert_w_gate[0], expert_w_up[0], expert_w_down[0], ffn_gain)
    out = _final_call(pos, final_norm.reshape(1, d), y)
    return out.reshape(batch, seq, d)
```

```python
import functools

import numpy as np
import jax
import jax.numpy as jnp
from jax import lax
from jax.experimental import pallas as pl
from jax.experimental.pallas import tpu as pltpu

f32 = jnp.float32
bf16 = jnp.bfloat16
i32 = jnp.int32

D_MODEL = 1024
CHUNK = 64
N_META = 16
EPS = 1e-6
GLA_HEADS = 4
GLA_DK = 64
GLA_DV = 128
GLA_GATE_RANK = 16
GLA_TAU = 16.0
GLA_QK = GLA_HEADS * GLA_DK
GLA_VW = GLA_HEADS * GLA_DV
MLA_HEADS = 4
MLA_Q_RANK = 256
MLA_KV_RANK = 128
MLA_NOPE = 128
MLA_ROPE = 64
MLA_V = 128
MLA_OUT = MLA_HEADS * MLA_V
MLA_QK_PAD = 256
MLA_VA = MLA_V + 16
LOG2_E = 1.4426950408889634
ROPE_BASE = 10000.0
N_GROUPS = 8
EXPERTS_PER_GROUP = 8
N_EXPERTS = N_GROUPS * EXPERTS_PER_GROUP
D_EXPERT = 512
N_PAIRS = EXPERTS_PER_GROUP * (EXPERTS_PER_GROUP - 1) // 2
N_BUCKETS = N_GROUPS * N_PAIRS
BUCKET_LANES = 256
LANE = 128
SUBLANES = 8
MLA_K_W = MLA_HEADS * MLA_NOPE + LANE
META_W = LANE
ROW_W = D_MODEL + META_W

PREP_TILE = 512
GLA_TILE = 512
GLA_BATCH = 4
ATT_TILE = 512
ATT_HEADS = 4
META_TILE = 128
OUT_TILE = 1024
ROUTE_ROWS = 256
ROUTE_GROUP = 1024
SCATTER_TILE = 2048
FINAL_TILE = 1024
ISSUE_UNROLL = 128
MOE_BLOCK = 32
MOE_GROUP = 16
VMEM_LIMIT = 56 * 1024 * 1024

C_Q, C_K, C_V, C_R = 0, 256, 512, 1024
C_QLAT, C_KVLAT, C_KROPE, C_A, C_END = 1536, 1792, 1920, 2048, 2176

_TILE_POS = np.arange(GLA_TILE)
_CHUNK_PREFIX = ((_TILE_POS[:, None] // CHUNK == _TILE_POS[None, :] // CHUNK)
                 & (_TILE_POS[None, :] <= _TILE_POS[:, None])).astype(np.float32)
_PAIR_LO = np.array([lo for lo in range(8) for hi in range(lo + 1, 8)], np.int32)
_PAIR_HI = np.array([hi for lo in range(8) for hi in range(lo + 1, 8)], np.int32)


def _dot(a, b):
    return jnp.dot(a, b, preferred_element_type=f32)


def _dot_nt(a, b):
    return lax.dot_general(a, b, (((1,), (1,)), ((), ())), preferred_element_type=f32)


def _dot_tn(a, b):
    return lax.dot_general(a, b, (((0,), (0,)), ((), ())), preferred_element_type=f32)


def _rms(x, gain):
    return x * lax.rsqrt(jnp.mean(x * x, axis=-1, keepdims=True) + EPS) * gain


def _split3(x):
    hi = x.astype(bf16)
    r1 = x - hi.astype(f32)
    mid = r1.astype(bf16)
    lo = (r1 - mid.astype(f32)).astype(bf16)
    return hi, mid, lo


def _prep_kernel(x_ref, g_ref, win_ref, qn_ref, wqb_ref, kvn_ref, wkvb_ref, ct_ref, st_ref,
                 qg_ref, kg_ref, vg_ref, rg_ref, a_ref, qm_ref, km_ref, vmt_ref):
    u = _rms(x_ref[...], g_ref[...]).astype(bf16)

    def proj(lo, hi):
        return _dot(u, win_ref[:, lo:hi])

    qg_ref[...] = proj(C_Q, C_K).astype(bf16)
    kg_ref[...] = proj(C_K, C_V).astype(bf16)
    vg_ref[...] = proj(C_V, C_R).astype(bf16)
    rg_ref[...] = proj(C_R, C_QLAT).astype(bf16)
    z = proj(C_QLAT, C_END)
    a_ref[...] = z[:, C_A - C_QLAT:].astype(bf16)
    ctab = ct_ref[...]
    stab = st_ref[...]

    def rope(seg):
        return seg * ctab + pltpu.roll(seg, 64, axis=1) * stab

    k_rope = rope(z[:, C_KROPE - C_QLAT:C_A - C_QLAT]).astype(bf16)
    qn = _rms(z[:, 0:MLA_Q_RANK], qn_ref[...]).astype(bf16)
    kvn = _rms(z[:, MLA_Q_RANK:MLA_Q_RANK + MLA_KV_RANK], kvn_ref[...]).astype(bf16)
    scale = (MLA_NOPE + MLA_ROPE) ** -0.5 * LOG2_E
    qf = _dot(qn, wqb_ref[...])
    kvf = _dot(kvn, wkvb_ref[...])
    for h in range(MLA_HEADS):
        c = h * MLA_QK_PAD
        qm_ref[:, c:c + LANE] = (qf[:, c:c + LANE] * scale).astype(bf16)
        qm_ref[:, c + LANE:c + 2 * LANE] = (rope(qf[:, c + LANE:c + 2 * LANE]) * scale).astype(bf16)
        km_ref[:, h * LANE:(h + 1) * LANE] = kvf[:, h * LANE:(h + 1) * LANE].astype(bf16)
    km_ref[:, MLA_HEADS * MLA_NOPE:MLA_K_W] = k_rope
    vt = kvf[:, MLA_HEADS * MLA_NOPE:].T
    for h in range(MLA_HEADS):
        vmt_ref[h * MLA_VA:h * MLA_VA + MLA_V, :] = vt[h * MLA_V:(h + 1) * MLA_V].astype(bf16)
        vmt_ref[h * MLA_VA + MLA_V:(h + 1) * MLA_VA, :] = jnp.ones((MLA_VA - MLA_V, vt.shape[1]), bf16)


def _prep_call(x2d, rows_per_seq, tile, gain, w_in_r, q_norm, w_qb_r, kv_norm, w_kvb_r, ctab, stab):
    t = x2d.shape[0]
    nj = rows_per_seq // tile
    grid = (t // rows_per_seq, nj)

    def row(b, j):
        return (b * nj + j, 0)

    def const(b, j):
        return (0, 0)

    def tab(b, j):
        return (j, 0)

    widths = (GLA_QK, GLA_QK, GLA_VW, GLA_VW, LANE, MLA_HEADS * MLA_QK_PAD, MLA_K_W)
    return pl.pallas_call(
        _prep_kernel,
        grid=grid,
        in_specs=[
            pl.BlockSpec((tile, D_MODEL), row),
            pl.BlockSpec((1, D_MODEL), const),
            pl.BlockSpec((D_MODEL, C_END), const),
            pl.BlockSpec((1, MLA_Q_RANK), const),
            pl.BlockSpec((MLA_Q_RANK, MLA_HEADS * MLA_QK_PAD), const),
            pl.BlockSpec((1, MLA_KV_RANK), const),
            pl.BlockSpec((MLA_KV_RANK, 2 * MLA_OUT), const),
            pl.BlockSpec((tile, LANE), tab),
            pl.BlockSpec((tile, LANE), tab),
        ],
        out_specs=[pl.BlockSpec((tile, w), row) for w in widths]
        + [pl.BlockSpec((None, MLA_HEADS * MLA_VA, tile), lambda b, j: (b * nj + j, 0, 0))],
        out_shape=[jax.ShapeDtypeStruct((t, w), bf16) for w in widths]
        + [jax.ShapeDtypeStruct((t // tile, MLA_HEADS * MLA_VA, tile), bf16)],
        compiler_params=pltpu.CompilerParams(
            dimension_semantics=("parallel", "parallel"), vmem_limit_bytes=VMEM_LIMIT),
        name="prep",
    )(x2d, gain, w_in_r, q_norm, w_qb_r, kv_norm, w_kvb_r, ctab, stab)


def _gla_log_decay(a, wa2_ref, ba_ref):
    s = _dot(a, wa2_ref[...]) + ba_ref[...]
    return (jnp.minimum(s, 0.0) - jnp.log(1.0 + jnp.exp(-jnp.abs(s)))) * (1.0 / GLA_TAU)


def _gla_front(q, k, v, la, tri, want_out):
    nc = la.shape[0] // CHUNK
    hi, mid, lo = _split3(la)
    b = _dot(tri, hi) + _dot(tri, mid) + _dot(tri, lo)
    b_last = [b[(c + 1) * CHUNK - 1:(c + 1) * CHUNK, :] for c in range(nc)]
    b_last_full = jnp.concatenate([jnp.broadcast_to(bl, (CHUNK, GLA_QK)) for bl in b_last], axis=0)
    kf = k.astype(f32)
    front = dict(v=v, b_last=b_last, kd=(kf * jnp.exp(b_last_full - b)).astype(bf16))
    if want_out:
        front.update(qe=(q.astype(f32) * (GLA_DK ** -0.5) * jnp.exp(b)).astype(bf16),
                     ke=kf * jnp.exp(-b), vf=v.astype(f32))
    return front


def _gla_chunks(front, st_ref, want_out):
    v, kd, b_last = front["v"], front["kd"], front["b_last"]
    rr = lax.broadcasted_iota(i32, (GLA_VW, GLA_QK), 0) // GLA_DV
    cc = lax.broadcasted_iota(i32, (GLA_VW, GLA_QK), 1) // GLA_DK
    if want_out:
        qe, ke, vf = front["qe"], front["ke"], front["vf"]
        lane_h = lax.broadcasted_iota(i32, (CHUNK, GLA_QK), 1) // GLA_DK
        vlane_h = lax.broadcasted_iota(i32, (CHUNK, GLA_VW), 1) // GLA_DV
        a_row = lax.broadcasted_iota(i32, (CHUNK, GLA_QK), 0)
        a_col = lax.broadcasted_iota(i32, (CHUNK, GLA_QK), 1) % CHUNK
    outs = []
    st = st_ref[...]
    for c in range(len(b_last)):
        rows = slice(c * CHUNK, (c + 1) * CHUNK)
        upd = jnp.where(rr == cc, _dot_tn(v[rows], kd[rows]), 0.0)
        if want_out:
            kbd = jnp.concatenate(
                [jnp.where(lane_h == h, ke[rows], 0.0) for h in range(GLA_HEADS)], axis=0).astype(bf16)
            att = jnp.where(a_col <= a_row, _dot_nt(qe[rows], kbd), 0.0).astype(bf16)
            vbd = jnp.concatenate(
                [jnp.where(vlane_h == h, vf[rows], 0.0) for h in range(GLA_HEADS)], axis=0).astype(bf16)
            outs.append(_dot(att, vbd) + _dot_nt(qe[rows], st.astype(bf16)))
        st = st * jnp.exp(b_last[c]) + upd
    st_ref[...] = st
    return jnp.concatenate(outs, axis=0) if want_out else None


def _gla_kernel(q_ref, k_ref, v_ref, r_ref, a_ref, km_ref, vm_ref, am_ref, wa2_ref, ba_ref, gain_ref, tri_ref,
                y_ref, st_ref):
    j = pl.program_id(1)

    @pl.when(j == 0)
    def _():
        st_ref[...] = jnp.zeros_like(st_ref)
        la = _gla_log_decay(am_ref[...], wa2_ref, ba_ref)
        row = lax.broadcasted_iota(i32, la.shape, 0)
        la = jnp.where(row >= CHUNK - N_META, la, 0.0)
        front = _gla_front(None, km_ref[...], vm_ref[...], la, tri_ref[0:CHUNK, 0:CHUNK], False)
        _gla_chunks(front, st_ref.at[0], False)
        for bb in range(1, GLA_BATCH):
            st_ref[bb] = st_ref[0]

    fronts = [_gla_front(q_ref[bb], k_ref[bb], v_ref[bb], _gla_log_decay(a_ref[bb], wa2_ref, ba_ref),
                         tri_ref[...], True) for bb in range(GLA_BATCH)]
    for bb in range(GLA_BATCH):
        o = _gla_chunks(fronts[bb], st_ref.at[bb], True)
        r = r_ref[bb].astype(f32)
        outs = []
        for h in range(GLA_HEADS):
            oh = o[:, h * GLA_DV:(h + 1) * GLA_DV]
            outs.append(oh * lax.rsqrt(jnp.mean(oh * oh, axis=-1, keepdims=True) + EPS))
        on = jnp.concatenate(outs, axis=1) * gain_ref[...]
        y_ref[bb] = (on * (r * jax.nn.sigmoid(r))).astype(bf16)


def _gla_call(qg, kg, vg, rg, ag, km, vm, am, wa2_p, b_a, gain, batch, seq):
    nj = seq // GLA_TILE

    def row(b, j):
        return (b, j, 0)

    def const(b, j):
        return (0, 0)

    def seqs(a):
        return a.reshape(batch, seq, a.shape[-1])

    out = pl.pallas_call(
        _gla_kernel,
        grid=(batch // GLA_BATCH, nj),
        in_specs=[
            pl.BlockSpec((GLA_BATCH, GLA_TILE, GLA_QK), row),
            pl.BlockSpec((GLA_BATCH, GLA_TILE, GLA_QK), row),
            pl.BlockSpec((GLA_BATCH, GLA_TILE, GLA_VW), row),
            pl.BlockSpec((GLA_BATCH, GLA_TILE, GLA_VW), row),
            pl.BlockSpec((GLA_BATCH, GLA_TILE, LANE), row),
            pl.BlockSpec((CHUNK, GLA_QK), const),
            pl.BlockSpec((CHUNK, GLA_VW), const),
            pl.BlockSpec((CHUNK, LANE), const),
            pl.BlockSpec((LANE, GLA_QK), const),
            pl.BlockSpec((1, GLA_QK), const),
            pl.BlockSpec((1, GLA_VW), const),
            pl.BlockSpec((GLA_TILE, GLA_TILE), const),
        ],
        out_specs=pl.BlockSpec((GLA_BATCH, GLA_TILE, GLA_VW), row),
        out_shape=jax.ShapeDtypeStruct((batch, seq, GLA_VW), bf16),
        scratch_shapes=[pltpu.VMEM((GLA_BATCH, GLA_VW, GLA_QK), f32)],
        compiler_params=pltpu.CompilerParams(
            dimension_semantics=("parallel", "arbitrary"), vmem_limit_bytes=VMEM_LIMIT),
        name="gla",
    )(seqs(qg), seqs(kg), seqs(vg), seqs(rg), seqs(ag), km, vm, am, wa2_p, b_a, gain,
      jnp.asarray(_CHUNK_PREFIX, bf16))
    return out.reshape(batch * seq, GLA_VW)


def _mla_kernel(q_ref, k_ref, vt_ref, km_ref, vmt_ref, o_ref, sa_ref, sb_ref):
    i = pl.program_id(2)
    tq = ATT_TILE
    w = MLA_QK_PAD
    va = MLA_VA
    heads = range(ATT_HEADS)

    def keys(ref, rows, h):
        return jnp.concatenate([ref[rows, h * MLA_NOPE:(h + 1) * MLA_NOPE], ref[rows, MLA_HEADS * MLA_NOPE:MLA_K_W]],
                               axis=1)

    def scores(h, blk):
        rows = pl.ds(pl.multiple_of(blk * tq, tq), tq)
        return _dot_nt(keys(k_ref, rows, h), q_ref[:, h * w:(h + 1) * w])

    def soft(s, vtb, carry, mask=None):
        m, acc = carry
        if mask is not None:
            s = jnp.where(mask, s, -1e30)
        m_new = jnp.maximum(m, jnp.max(s, axis=0, keepdims=True))
        p = jnp.exp2(s - m_new).astype(bf16)
        return m_new, jnp.exp2(m - m_new) * acc + _dot(vtb, p)

    def vt(h, blk):
        return vt_ref[blk, h * va:(h + 1) * va, :]

    def finish(carries):
        ss = [_dot_nt(keys(km_ref, slice(None), h), q_ref[:, h * w:(h + 1) * w]) for h in heads]
        accs = [soft(ss[h], vmt_ref[h * va:(h + 1) * va, :], carries[h])[1] for h in heads]
        for h in heads:
            acc = accs[h]
            o_ref[:, h * MLA_V:(h + 1) * MLA_V] = (acc[:MLA_V] * (1.0 / acc[MLA_V:MLA_V + 1])).T.astype(bf16)

    kc = lax.broadcasted_iota(i32, (tq, tq), 0) // CHUNK
    qc = lax.broadcasted_iota(i32, (tq, tq), 1) // CHUNK
    mask = kc <= qc

    for h in heads:
        sa_ref[h] = scores(h, 0)

    def pair(p, carries):
        b0 = 2 * p
        for h in heads:
            sb_ref[h] = scores(h, b0 + 1)
        carries = [soft(sa_ref[h], vt(h, b0), carries[h]) for h in heads]
        for h in heads:
            sa_ref[h] = scores(h, b0 + 2)
        return tuple(soft(sb_ref[h], vt(h, b0 + 1), carries[h]) for h in heads)

    init = tuple((jnp.full((1, tq), -1e30, f32), jnp.zeros((va, tq), f32)) for _ in heads)
    carries = lax.fori_loop(0, i // 2, pair, init)

    @pl.when(i % 2 == 1)
    def _():
        for h in heads:
            sb_ref[h] = scores(h, i)
        c1 = [soft(sa_ref[h], vt(h, i - 1), carries[h]) for h in heads]
        finish([soft(sb_ref[h], vt(h, i), c1[h], mask) for h in heads])

    @pl.when(i % 2 == 0)
    def _():
        finish([soft(sa_ref[h], vt(h, i), carries[h], mask) for h in heads])


def _mla_call(qm, km, vmt, km_meta, vmt_meta, batch, seq):
    nq = seq // ATT_TILE
    nh = ATT_HEADS
    qm3 = qm.reshape(batch, seq, MLA_HEADS * MLA_QK_PAD)
    km3 = km.reshape(batch, seq, MLA_K_W)
    assert nh == MLA_HEADS
    vt4 = vmt.reshape(batch, nq, MLA_HEADS * MLA_VA, ATT_TILE)
    out = pl.pallas_call(
        _mla_kernel,
        grid=(batch, MLA_HEADS // nh, nq),
        in_specs=[
            pl.BlockSpec((None, ATT_TILE, nh * MLA_QK_PAD), lambda b, h, i: (b, i, h)),
            pl.BlockSpec((None, seq, MLA_K_W), lambda b, h, i: (b, 0, 0)),
            pl.BlockSpec((None, nq, nh * MLA_VA, ATT_TILE), lambda b, h, i: (b, 0, h, 0)),
            pl.BlockSpec((N_META, MLA_K_W), lambda b, h, i: (0, 0)),
            pl.BlockSpec((nh * MLA_VA, N_META), lambda b, h, i: (h, 0)),
        ],
        out_specs=pl.BlockSpec((None, ATT_TILE, nh * MLA_V), lambda b, h, i: (b, i, h)),
        out_shape=jax.ShapeDtypeStruct((batch, seq, MLA_OUT), bf16),
        scratch_shapes=[pltpu.VMEM((nh, ATT_TILE, ATT_TILE), f32), pltpu.VMEM((nh, ATT_TILE, ATT_TILE), f32)],
        compiler_params=pltpu.CompilerParams(
            dimension_semantics=("parallel", "parallel", "arbitrary"), vmem_limit_bytes=VMEM_LIMIT),
        name="mla",
    )(qm3, km3, vt4, km_meta, vmt_meta)
    return out.reshape(batch * seq, MLA_OUT)


def _route_cols(lt):
    r = lt.shape[1]
    neg = -1e30
    gl = lt[0:N_GROUPS, :]
    gsub = lax.broadcasted_iota(i32, (N_GROUPS, r), 0)
    gmax = jnp.max(gl, axis=0, keepdims=True)
    g_p = 1.0 / jnp.sum(jnp.exp(gl - gmax), axis=0, keepdims=True)
    g_idx = jnp.min(jnp.where(gl == gmax, gsub, N_GROUPS), axis=0, keepdims=True)
    el_all = lt[N_GROUPS:N_GROUPS + N_EXPERTS, :]
    esub = lax.broadcasted_iota(i32, (N_EXPERTS, r), 0)
    base = g_idx * EXPERTS_PER_GROUP
    e_mask = (esub >= base) & (esub < base + EXPERTS_PER_GROUP)
    el = jnp.where(e_mask, el_all, neg)
    m1 = jnp.max(el, axis=0, keepdims=True)
    i1 = jnp.min(jnp.where(e_mask & (el == m1), esub, N_EXPERTS), axis=0, keepdims=True)
    el2 = jnp.where(esub == i1, neg, el)
    m2 = jnp.max(el2, axis=0, keepdims=True)
    i2 = jnp.min(jnp.where(e_mask & (esub != i1) & (el2 == m2), esub, N_EXPERTS), axis=0, keepdims=True)
    rr = jnp.exp(m2 - m1)
    ga = g_p / (1.0 + rr)
    gb = g_p * rr / (1.0 + rr)
    la_ = i1 - base
    lb_ = i2 - base
    lo = jnp.minimum(la_, lb_)
    hi = jnp.maximum(la_, lb_)
    g_lo = jnp.where(la_ < lb_, ga, gb)
    g_hi = jnp.where(la_ < lb_, gb, ga)
    pidx = ((lo * (2 * EXPERTS_PER_GROUP - 1 - lo)) >> 1) + (hi - lo - 1)
    bucket = g_idx * N_PAIRS + pidx
    bsub = lax.broadcasted_iota(i32, (BUCKET_LANES, r), 0)
    oht = jnp.where(bsub == bucket, 1.0, 0.0)
    ohb = oht.astype(bf16)
    ri = lax.broadcasted_iota(i32, (ROUTE_ROWS, ROUTE_ROWS), 0)
    ci = lax.broadcasted_iota(i32, (ROUTE_ROWS, ROUTE_ROWS), 1)
    before = jnp.where(ri < ci, 1.0, 0.0).astype(bf16)
    ones = jnp.ones((SUBLANES, ROUTE_ROWS), bf16)
    subs = [slice(i * ROUTE_ROWS, (i + 1) * ROUTE_ROWS) for i in range(r // ROUTE_ROWS)]
    cum = jnp.concatenate([_dot(ohb[:, sl], before) for sl in subs], axis=1)
    rank = jnp.sum(oht * cum, axis=0, keepdims=True)
    counts = [_dot_nt(ones, ohb[:, sl])[0:1, :] for sl in subs]
    msub = lax.broadcasted_iota(i32, (LANE, r), 0)
    meta_t = jnp.where(msub == 0, bucket.astype(f32),
                       jnp.where(msub == 1, rank,
                                 jnp.where(msub == 2, g_lo, jnp.where(msub == 3, g_hi, 0.0))))
    return meta_t.T, counts, meta_t[0:SUBLANES, :]


def _outproj_kernel(x_ref, yg_ref, ym_ref, wog_ref, wom_ref, gain_ref, wrt_ref, rb_ref,
                    ux_ref, cnt_ref, rt_ref):
    per_group = ROUTE_GROUP // ROUTE_ROWS
    for grp in range(OUT_TILE // ROUTE_GROUP):
        rows = slice(grp * ROUTE_GROUP, (grp + 1) * ROUTE_GROUP)
        h1 = x_ref[rows, :] + _dot(yg_ref[rows, :], wog_ref[...]) + _dot(ym_ref[rows, :], wom_ref[...])
        ux_ref[rows, 0:D_MODEL] = h1
        u2 = _rms(h1, gain_ref[...])
        lt = _dot_nt(wrt_ref[...], u2.astype(bf16)) + rb_ref[...]
        meta, counts, routes = _route_cols(lt)
        ux_ref[rows, D_MODEL:ROW_W] = meta
        for i in range(per_group):
            cnt_ref[grp * per_group + i] = counts[i]
            rt_ref[grp * per_group + i] = routes[:, i * ROUTE_ROWS:(i + 1) * ROUTE_ROWS]


def _outproj_call(x2d, yg, ym, wo_g, wo_m, gain, w_r, rbias):
    t = x2d.shape[0]
    nt = t // OUT_TILE

    def row(i):
        return (i, 0)

    def const(i):
        return (0, 0)

    return pl.pallas_call(
        _outproj_kernel,
        grid=(nt,),
        in_specs=[
            pl.BlockSpec((OUT_TILE, D_MODEL), row),
            pl.BlockSpec((OUT_TILE, GLA_VW), row),
            pl.BlockSpec((OUT_TILE, MLA_OUT), row),
            pl.BlockSpec((GLA_VW, D_MODEL), const),
            pl.BlockSpec((MLA_OUT, D_MODEL), const),
            pl.BlockSpec((1, D_MODEL), const),
            pl.BlockSpec((LANE, D_MODEL), const),
            pl.BlockSpec((LANE, 1), const),
        ],
        out_specs=[
            pl.BlockSpec((OUT_TILE, ROW_W), row),
            pl.BlockSpec((OUT_TILE // ROUTE_ROWS, 1, BUCKET_LANES), lambda i: (i, 0, 0)),
            pl.BlockSpec((OUT_TILE // ROUTE_ROWS, SUBLANES, ROUTE_ROWS), lambda i: (i, 0, 0)),
        ],
        out_shape=[
            jax.ShapeDtypeStruct((t, ROW_W), f32),
            jax.ShapeDtypeStruct((nt * (OUT_TILE // ROUTE_ROWS), 1, BUCKET_LANES), f32),
            jax.ShapeDtypeStruct((nt * (OUT_TILE // ROUTE_ROWS), SUBLANES, ROUTE_ROWS), f32),
        ],
        compiler_params=pltpu.CompilerParams(
            dimension_semantics=("parallel",), vmem_limit_bytes=VMEM_LIMIT),
        name="outproj",
    )(x2d, yg, ym, wo_g, wo_m, gain, w_r, rbias)


def _scatter_kernel(pos_ref, zb_ref, ux_ref, hs_ref, zbuf, sem, zsem):
    @pl.when(pl.program_id(0) == 0)
    def _():
        zbuf[...] = jnp.zeros_like(zbuf)

        def zero_copy(j):
            rows = pl.ds(pl.multiple_of(zb_ref[j] * MOE_BLOCK, MOE_BLOCK), MOE_BLOCK)
            return pltpu.make_async_copy(zbuf, hs_ref.at[rows], zsem)

        def zstart(j, c):
            @pl.when(zb_ref[j] >= 0)
            def _():
                zero_copy(j).start()
            return c

        def zwait(j, c):
            @pl.when(zb_ref[j] >= 0)
            def _():
                zero_copy(j).wait()
            return c

        lax.fori_loop(0, zb_ref.shape[0], zstart, 0)
        lax.fori_loop(0, zb_ref.shape[0], zwait, 0)

    def start(io, c):
        for r in range(ISSUE_UNROLL):
            ii = io * (ISSUE_UNROLL // SUBLANES) + r // SUBLANES
            pltpu.make_async_copy(ux_ref.at[ii, pl.ds(r % SUBLANES, 1)],
                                  hs_ref.at[pl.ds(pos_ref[io * ISSUE_UNROLL + r], 1)], sem).start()
        return c

    lax.fori_loop(0, SCATTER_TILE // ISSUE_UNROLL, start, 0)
    pltpu.make_async_copy(hs_ref.at[pl.ds(0, SCATTER_TILE)], hs_ref.at[pl.ds(0, SCATTER_TILE)], sem).wait()


def _scatter_call(pos, zero_blocks, ux, n_slots):
    t = ux.shape[0]
    nz = zero_blocks.shape[0]
    return pl.pallas_call(
        _scatter_kernel,
        grid=(t // SCATTER_TILE,),
        in_specs=[
            pl.BlockSpec((SCATTER_TILE,), lambda i: (i,), memory_space=pltpu.SMEM),
            pl.BlockSpec((nz,), lambda i: (0,), memory_space=pltpu.SMEM),
            pl.BlockSpec((SCATTER_TILE // SUBLANES, SUBLANES, ROW_W), lambda i: (i, 0, 0)),
        ],
        out_specs=pl.BlockSpec(memory_space=pl.ANY),
        out_shape=jax.ShapeDtypeStruct((n_slots, ROW_W), f32),
        scratch_shapes=[pltpu.VMEM((MOE_BLOCK, ROW_W), f32), pltpu.SemaphoreType.DMA(()),
                        pltpu.SemaphoreType.DMA(())],
        compiler_params=pltpu.CompilerParams(
            dimension_semantics=("arbitrary",), vmem_limit_bytes=VMEM_LIMIT),
        name="scatter",
    )(pos, zero_blocks, ux.reshape(t // SUBLANES, SUBLANES, ROW_W))


def _moe_kernel(se_ref, sf_ref, sk_ref, sp_ref, sn_ref, sb_ref, sr_ref, si_ref, so_ref,
                hs_hbm, wg_hbm, wu_hbm, wd_hbm, fg_ref, y_hbm,
                xbuf, obuf, wg_buf, wu_buf, wd_buf, wgu_s, wd_s, in_sem, out_sem, w_sem):
    s = pl.program_id(0)
    ns = pl.num_programs(0)
    cur = s % 2
    g_n = MOE_GROUP

    def in_copy(step, g, buf):
        rows = pl.ds(pl.multiple_of(sb_ref[step * g_n + g] * MOE_BLOCK, MOE_BLOCK), MOE_BLOCK)
        return pltpu.make_async_copy(
            hs_hbm.at[rows], xbuf.at[buf, pl.ds(g * MOE_BLOCK, MOE_BLOCK)], in_sem.at[buf])

    def out_copy(step, g, buf):
        rows = pl.ds(pl.multiple_of(sb_ref[step * g_n + g] * MOE_BLOCK, MOE_BLOCK), MOE_BLOCK)
        cols = pl.ds(pl.multiple_of(sr_ref[step * g_n + g] * D_MODEL, D_MODEL), D_MODEL)
        return pltpu.make_async_copy(
            obuf.at[buf, pl.ds(g * MOE_BLOCK, MOE_BLOCK)], y_hbm.at[rows, cols], out_sem.at[buf])

    def for_slots(step, flags_ref, fn):
        for g in range(g_n):
            @pl.when(flags_ref[step * g_n + g] == 1)
            def _():
                fn(g)

    @pl.when(s == 0)
    def _():
        xbuf[...] = jnp.zeros_like(xbuf)
        for_slots(0, si_ref, lambda g: in_copy(0, g, 0).start())

    @pl.when(s + 1 < ns)
    def _():
        for_slots(s + 1, si_ref, lambda g: in_copy(s + 1, g, 1 - cur).start())

    for_slots(s, si_ref, lambda g: in_copy(s, g, cur).wait())

    @pl.when(s >= 2)
    def _():
        for_slots(s - 2, so_ref, lambda g: out_copy(s - 2, g, cur).wait())

    def w_copies(expert, slot):
        return (pltpu.make_async_copy(wg_hbm.at[expert], wg_buf.at[slot], w_sem.at[slot]),
                pltpu.make_async_copy(wu_hbm.at[expert], wu_buf.at[slot], w_sem.at[slot]),
                pltpu.make_async_copy(wd_hbm.at[expert], wd_buf.at[slot], w_sem.at[slot]))

    @pl.when(s == 0)
    def _():
        for c in w_copies(se_ref[0], sp_ref[0]):
            c.start()

    @pl.when(sf_ref[s] == 1)
    def _():
        slot = sp_ref[s]
        for c in w_copies(se_ref[s], slot):
            c.wait()
        wgu_s[:, 0:D_EXPERT] = wg_buf[slot].astype(bf16)
        wgu_s[:, D_EXPERT:2 * D_EXPERT] = wu_buf[slot].astype(bf16)
        wd_s[...] = wd_buf[slot].astype(bf16)

        @pl.when(sn_ref[s] >= 0)
        def _():
            for c in w_copies(sn_ref[s], 1 - slot):
                c.start()

    @pl.when(sk_ref[s] == 1)
    def _():
        h1 = xbuf[cur, :, 0:D_MODEL]
        meta = xbuf[cur, :, D_MODEL:ROW_W]
        u = _rms(h1, fg_ref[...]).astype(bf16)
        ones = jnp.ones((MOE_BLOCK, 1), f32)
        role0 = [sr_ref[s * g_n + g] == 0 for g in range(g_n)]
        gate = jnp.concatenate(
            [jnp.where(role0[g], meta[g * MOE_BLOCK:(g + 1) * MOE_BLOCK, 2:3],
                       meta[g * MOE_BLOCK:(g + 1) * MOE_BLOCK, 3:4]) for g in range(g_n)], axis=0)
        keep = jnp.concatenate([jnp.where(role0[g], ones, 0.0) for g in range(g_n)], axis=0)
        gu = _dot(u, wgu_s[...])
        gt = gu[:, 0:D_EXPERT]
        hdn = (gt * jax.nn.sigmoid(gt) * gu[:, D_EXPERT:]).astype(bf16)
        obuf[cur] = _dot(hdn, wd_s[...]) * gate + h1 * keep

    @pl.when(sk_ref[s] == 0)
    def _():
        obuf[cur] = jnp.zeros(obuf.shape[1:], f32)

    for_slots(s, so_ref, lambda g: out_copy(s, g, cur).start())

    @pl.when(s == ns - 1)
    def _():
        for_slots(s, so_ref, lambda g: out_copy(s, g, cur).wait())

        @pl.when(s >= 1)
        def _():
            for_slots(s - 1, so_ref, lambda g: out_copy(s - 1, g, 1 - cur).wait())


def _moe_call(plan, hs, w_gate, w_up, w_down, ffn_gain):
    n_steps = plan[0].shape[0]
    n_slots = hs.shape[0]
    rows = MOE_GROUP * MOE_BLOCK

    grid_spec = pltpu.PrefetchScalarGridSpec(
        num_scalar_prefetch=9,
        grid=(n_steps,),
        in_specs=[
            pl.BlockSpec(memory_space=pl.ANY),
            pl.BlockSpec(memory_space=pl.ANY),
            pl.BlockSpec(memory_space=pl.ANY),
            pl.BlockSpec(memory_space=pl.ANY),
            pl.BlockSpec((1, D_MODEL), lambda s, *_: (0, 0)),
        ],
        out_specs=pl.BlockSpec(memory_space=pl.ANY),
        scratch_shapes=[
            pltpu.VMEM((2, rows, ROW_W), f32),
            pltpu.VMEM((2, rows, D_MODEL), f32),
            pltpu.VMEM((2, D_MODEL, D_EXPERT), f32),
            pltpu.VMEM((2, D_MODEL, D_EXPERT), f32),
            pltpu.VMEM((2, D_EXPERT, D_MODEL), f32),
            pltpu.VMEM((D_MODEL, 2 * D_EXPERT), bf16),
            pltpu.VMEM((D_EXPERT, D_MODEL), bf16),
            pltpu.SemaphoreType.DMA((2,)),
            pltpu.SemaphoreType.DMA((2,)),
            pltpu.SemaphoreType.DMA((2,)),
        ],
    )
    return pl.pallas_call(
        _moe_kernel,
        grid_spec=grid_spec,
        out_shape=jax.ShapeDtypeStruct((n_slots, 2 * D_MODEL), f32),
        compiler_params=pltpu.CompilerParams(
            dimension_semantics=("arbitrary",), vmem_limit_bytes=VMEM_LIMIT),
        name="moe",
    )(*plan, hs, w_gate, w_up, w_down, ffn_gain)


def _final_kernel(posc_ref, posn_ref, gain_ref, y_hbm, o_ref, ybuf, sem):
    i = pl.program_id(0)
    cur = i % 2

    def issue(pos_ref, buf):
        def start(io, c):
            for r in range(ISSUE_UNROLL):
                ii = io * (ISSUE_UNROLL // SUBLANES) + r // SUBLANES
                pltpu.make_async_copy(y_hbm.at[pl.ds(pos_ref[io * ISSUE_UNROLL + r], 1)],
                                      ybuf.at[buf, ii, pl.ds(r % SUBLANES, 1)], sem.at[buf]).start()
            return c

        lax.fori_loop(0, FINAL_TILE // ISSUE_UNROLL, start, 0)

    @pl.when(i == 0)
    def _():
        issue(posc_ref, 0)

    @pl.when(i + 1 < pl.num_programs(0))
    def _():
        issue(posn_ref, 1 - cur)

    pltpu.make_async_copy(ybuf.at[cur], ybuf.at[cur], sem.at[cur]).wait()
    h = ybuf[cur, :, :, 0:D_MODEL] + ybuf[cur, :, :, D_MODEL:2 * D_MODEL]
    o_ref[...] = _rms(h, gain_ref[...])


def _final_call(pos, gain, y):
    t = pos.shape[0]
    n = t // FINAL_TILE
    rows = FINAL_TILE // SUBLANES
    out = pl.pallas_call(
        _final_kernel,
        grid=(n,),
        in_specs=[
            pl.BlockSpec((FINAL_TILE,), lambda i: (i,), memory_space=pltpu.SMEM),
            pl.BlockSpec((FINAL_TILE,), lambda i: (jnp.minimum(i + 1, n - 1),), memory_space=pltpu.SMEM),
            pl.BlockSpec((1, 1, D_MODEL), lambda i: (0, 0, 0)),
            pl.BlockSpec(memory_space=pl.ANY),
        ],
        out_specs=pl.BlockSpec((rows, SUBLANES, D_MODEL), lambda i: (i, 0, 0)),
        out_shape=jax.ShapeDtypeStruct((t // SUBLANES, SUBLANES, D_MODEL), f32),
        scratch_shapes=[pltpu.VMEM((2, rows, SUBLANES, 2 * D_MODEL), f32), pltpu.SemaphoreType.DMA((2,))],
        compiler_params=pltpu.CompilerParams(
            dimension_semantics=("arbitrary",), vmem_limit_bytes=VMEM_LIMIT),
        name="final",
    )(pos, pos, gain.reshape(1, 1, D_MODEL), y)
    return out.reshape(t, D_MODEL)


def _rope_tables(pos):
    inv = ROPE_BASE ** (-jnp.arange(0, MLA_ROPE, 2, dtype=f32) / MLA_ROPE)
    ang = pos.astype(f32)[:, None] * inv[None, :]
    cos, sin = jnp.cos(ang), jnp.sin(ang)
    z = jnp.zeros((pos.shape[0], LANE - MLA_ROPE), f32)
    return jnp.concatenate([cos, cos, z], axis=1), jnp.concatenate([-sin, sin, z], axis=1)


def _relayout_weights(w_in, w_qb, w_kvb):
    half = MLA_ROPE // 2
    perm = (np.arange(MLA_ROPE) + half) % MLA_ROPE
    pts = np.cumsum((GLA_QK, GLA_QK, GLA_VW, GLA_VW, GLA_GATE_RANK, MLA_Q_RANK, MLA_KV_RANK, MLA_ROPE))
    q_g, k_g, v_g, r_g, a_l, q_lat, kv_lat, k_rope = jnp.split(w_in, pts[:-1], axis=1)
    a_seg = jnp.pad(a_l, ((0, 0), (0, LANE - GLA_GATE_RANK)))
    w_in_r = jnp.concatenate(
        [q_g, k_g, v_g, r_g, q_lat, kv_lat, k_rope, k_rope[:, perm], a_seg], axis=1).astype(bf16)
    qcols, kcols, vcols = [], [], []
    for h in range(MLA_HEADS):
        c = h * (MLA_NOPE + MLA_ROPE)
        rope = w_qb[:, c + MLA_NOPE:c + MLA_NOPE + MLA_ROPE]
        qcols += [w_qb[:, c:c + MLA_NOPE], rope, rope[:, perm]]
        c2 = h * (MLA_NOPE + MLA_V)
        kcols.append(w_kvb[:, c2:c2 + MLA_NOPE])
        vcols.append(w_kvb[:, c2 + MLA_NOPE:c2 + MLA_NOPE + MLA_V])
    return w_in_r, jnp.concatenate(qcols, axis=1).astype(bf16), jnp.concatenate(kcols + vcols, axis=1).astype(bf16)


_BUCKET_GROUP = np.arange(N_BUCKETS) // N_PAIRS
_RUN_EXPERT = np.concatenate([_BUCKET_GROUP * EXPERTS_PER_GROUP + _PAIR_LO[np.arange(N_BUCKETS) % N_PAIRS],
                              _BUCKET_GROUP * EXPERTS_PER_GROUP + _PAIR_HI[np.arange(N_BUCKETS) % N_PAIRS]])
_RUN_IS_EXPERT = (_RUN_EXPERT[:, None] == np.arange(N_EXPERTS)[None, :]).astype(np.int32)
_RUN_BEFORE = ((_RUN_EXPERT[:, None] == _RUN_EXPERT[None, :])
               & (np.arange(2 * N_BUCKETS)[None, :] < np.arange(2 * N_BUCKETS)[:, None])).astype(np.int32)


def _route_plan(counts, bucket, rank, n_tok):
    nt = counts.shape[0]
    g_n = MOE_GROUP
    tot = counts.sum(axis=0)
    nblk = (tot + MOE_BLOCK - 1) // MOE_BLOCK
    bstart_blk = jnp.cumsum(nblk) - nblk
    n_blocks = jnp.sum(nblk)
    tile_base = bstart_blk[None, :] * MOE_BLOCK + jnp.cumsum(counts, axis=0) - counts
    hit = bucket.reshape(nt, -1, 1) == jnp.arange(N_BUCKETS, dtype=i32)
    pos = jnp.sum(jnp.where(hit, tile_base[:, None, :], 0), axis=-1).reshape(-1) + rank
    nb_max = (n_tok + N_BUCKETS * (MOE_BLOCK - 1)) // MOE_BLOCK

    n_run = jnp.concatenate([nblk, nblk])
    b0_run = jnp.concatenate([bstart_blk, bstart_blk])
    c_e = jnp.sum(n_run[:, None] * _RUN_IS_EXPERT, axis=0)
    g_e = (c_e + g_n - 1) // g_n
    gend = jnp.cumsum(g_e)
    gstart = gend - g_e
    n_compute = gend[-1]
    off_run = jnp.sum(_RUN_BEFORE * n_run[None, :], axis=1)
    f_run = jnp.sum(_RUN_IS_EXPERT * gstart[None, :], axis=1) * g_n + off_run

    n_steps = (2 * nb_max + N_EXPERTS * (g_n - 1) + g_n - 1) // g_n + 1
    f = jnp.arange(n_steps * g_n, dtype=i32)
    in_run = (f[:, None] >= f_run[None, :]) & (f[:, None] < (f_run + n_run)[None, :])
    valid_c = jnp.any(in_run, axis=1)
    block_c = jnp.sum(jnp.where(in_run, b0_run[None, :] + f[:, None] - f_run[None, :], 0), axis=1)
    role_c = jnp.sum(jnp.where(in_run[:, N_BUCKETS:], 1, 0), axis=1)
    u_idx = f - n_compute * g_n
    valid_f = (u_idx >= 0) & (u_idx < 2 * (nb_max - n_blocks))
    slot_block = jnp.where(valid_c, block_c, jnp.where(valid_f, n_blocks + u_idx // 2, 0))
    slot_role = jnp.where(valid_c, role_c, jnp.where(valid_f, u_idx % 2, 0))

    step = jnp.arange(n_steps, dtype=i32)
    e_of_step = jnp.minimum(jnp.sum(gend[None, :] <= step[:, None], axis=1), N_EXPERTS - 1)
    is_compute = step < n_compute
    last_e = jnp.max(jnp.where(is_compute, e_of_step, 0))
    step_expert = jnp.where(is_compute, e_of_step, last_e)
    step_first = jnp.concatenate([jnp.ones((1,), bool), step_expert[1:] != step_expert[:-1]])
    ordinal = jnp.cumsum(step_first.astype(i32)) - 1
    ords = jnp.arange(N_EXPERTS + 1, dtype=i32)
    expert_of_ord = jnp.sum(jnp.where(step_first[:, None] & (ordinal[:, None] == ords[None, :]),
                                      step_expert[:, None], 0), axis=0)
    has_next = ordinal + 1 <= ordinal[-1]
    next_expert = jnp.sum(jnp.where(ords[None, :] == ordinal[:, None] + 1, expert_of_ord[None, :], 0), axis=1)
    step_next = jnp.where(step_first & has_next, next_expert, -1)
    plan = tuple(a.astype(i32) for a in
                 (step_expert, step_first, is_compute, ordinal % 2, step_next,
                  slot_block, slot_role, valid_c, valid_c | valid_f))
    last_blk = jnp.where(nblk > 0, bstart_blk + nblk - 1, -1)
    spare = n_blocks + jnp.arange(nb_max - n_tok // MOE_BLOCK, dtype=i32)
    zero_blocks = jnp.concatenate([last_blk, jnp.where(spare < nb_max, spare, -1)]).astype(i32)
    return pos.astype(i32), plan, zero_blocks, nb_max


def kernel(x, meta_tokens, mix_norm, w_in, gla_w_a2, gla_b_a, gla_out_norm, mla_q_norm, mla_w_qb, mla_kv_norm,
           mla_w_kvb, w_out, ffn_norm, router_group_w, router_group_b, router_expert_w, router_expert_b,
           expert_w_gate, expert_w_up, expert_w_down, final_norm):
    batch, seq, d = x.shape
    assert PREP_TILE == ATT_TILE
    assert d == D_MODEL and seq % max(PREP_TILE, GLA_TILE, ATT_TILE) == 0
    assert (batch * seq) % max(OUT_TILE, SCATTER_TILE, FINAL_TILE) == 0 and batch % GLA_BATCH == 0
    n_tok = batch * seq
    x2d = x.reshape(n_tok, d)

    w_in_r, w_qb_r, w_kvb_r = _relayout_weights(w_in[0], mla_w_qb[0], mla_w_kvb[0])
    mixg = mix_norm[0].reshape(1, d)
    qn = mla_q_norm[0].reshape(1, MLA_Q_RANK)
    kvn = mla_kv_norm[0].reshape(1, MLA_KV_RANK)
    ct_m, st_m = _rope_tables(jnp.arange(META_TILE))
    ct_x, st_x = _rope_tables(N_META + jnp.arange(seq))

    x_meta = jnp.pad(meta_tokens.astype(f32), ((0, META_TILE - N_META), (0, 0)))
    _, kg_m, vg_m, _, a_m, _, km_m, vmt_m = _prep_call(
        x_meta, META_TILE, META_TILE, mixg, w_in_r, qn, w_qb_r, kvn, w_kvb_r, ct_m, st_m)
    qg, kg, vg, rg, ag, qm, km, vmt = _prep_call(
        x2d, seq, PREP_TILE, mixg, w_in_r, qn, w_qb_r, kvn, w_kvb_r, ct_x, st_x)

    def chunk0(a):
        return jnp.pad(a[:N_META], ((CHUNK - N_META, 0), (0, 0)))

    wa2_p = jnp.pad(gla_w_a2[0], ((0, LANE - GLA_GATE_RANK), (0, 0))).astype(bf16)
    y_gla = _gla_call(qg, kg, vg, rg, ag, chunk0(kg_m), chunk0(vg_m), chunk0(a_m),
                      wa2_p, gla_b_a[0].reshape(1, GLA_QK), gla_out_norm[0].reshape(1, GLA_VW), batch, seq)
    y_mla = _mla_call(qm, km, vmt, km_m[:N_META], vmt_m[0, :, :N_META], batch, seq)

    wo = w_out[0].astype(bf16)
    rw = jnp.concatenate([router_group_w[0], router_expert_w[0],
                          jnp.zeros((d, LANE - N_GROUPS - N_EXPERTS), f32)], axis=1)
    rb = jnp.concatenate([router_group_b[0], router_expert_b[0],
                          jnp.zeros((LANE - N_GROUPS - N_EXPERTS,), f32)]).reshape(1, LANE)
    ffn_gain = ffn_norm[0].reshape(1, d)
    ux, cnt, routes = _outproj_call(x2d, y_gla, y_mla, wo[:GLA_VW], wo[GLA_VW:], ffn_gain,
                                      rw.T.astype(bf16), rb.reshape(LANE, 1))

    counts = cnt.reshape(-1, BUCKET_LANES)[:, :N_BUCKETS].astype(i32)
    tok_bucket = routes[:, 0, :].reshape(-1).astype(i32)
    tok_rank = routes[:, 1, :].reshape(-1).astype(i32)
    pos, plan, zero_blocks, nb_max = _route_plan(counts, tok_bucket, tok_rank, n_tok)
    n_slots = nb_max * MOE_BLOCK
    hs = _scatter_call(pos, zero_blocks, ux, n_slots)
    y = _moe_call(plan, hs, expert_w_gate[0], expert_w_up[0], expert_w_down[0], ffn_gain)
    out = _final_call(pos, final_norm.reshape(1, d), y)
    return out.reshape(batch, seq, d)
```

```python
import functools

import numpy as np
import jax
import jax.numpy as jnp
from jax import lax
from jax.experimental import pallas as pl
from jax.experimental.pallas import tpu as pltpu

f32 = jnp.float32
bf16 = jnp.bfloat16
i32 = jnp.int32

D_MODEL = 1024
CHUNK = 64
N_META = 16
EPS = 1e-6
GLA_HEADS = 4
GLA_DK = 64
GLA_DV = 128
GLA_GATE_RANK = 16
GLA_TAU = 16.0
GLA_QK = GLA_HEADS * GLA_DK
GLA_VW = GLA_HEADS * GLA_DV
MLA_HEADS = 4
MLA_Q_RANK = 256
MLA_KV_RANK = 128
MLA_NOPE = 128
MLA_ROPE = 64
MLA_V = 128
MLA_OUT = MLA_HEADS * MLA_V
MLA_QK_PAD = 256
MLA_VA = MLA_V + 16
LOG2_E = 1.4426950408889634
ROPE_BASE = 10000.0
N_GROUPS = 8
EXPERTS_PER_GROUP = 8
N_EXPERTS = N_GROUPS * EXPERTS_PER_GROUP
D_EXPERT = 512
N_PAIRS = EXPERTS_PER_GROUP * (EXPERTS_PER_GROUP - 1) // 2
N_BUCKETS = N_GROUPS * N_PAIRS
BUCKET_LANES = 256
LANE = 128
SUBLANES = 8
MLA_K_W = MLA_HEADS * MLA_NOPE + LANE
META_W = LANE
ROW_W = D_MODEL + META_W

PREP_TILE = 512
GLA_TILE = 512
GLA_BATCH = 4
ATT_TILE = 512
ATT_HEADS = 4
META_TILE = 128
OUT_TILE = 1024
ROUTE_ROWS = 256
ROUTE_GROUP = 1024
SCATTER_TILE = 4096
FINAL_TILE = 1024
ISSUE_UNROLL = 128
MOE_BLOCK = 32
MOE_GROUP = 16
VMEM_LIMIT = 56 * 1024 * 1024

C_Q, C_K, C_V, C_R = 0, 256, 512, 1024
C_QLAT, C_KVLAT, C_KROPE, C_A, C_END = 1536, 1792, 1920, 2048, 2176

_TILE_POS = np.arange(GLA_TILE)
_CHUNK_PREFIX = ((_TILE_POS[:, None] // CHUNK == _TILE_POS[None, :] // CHUNK)
                 & (_TILE_POS[None, :] <= _TILE_POS[:, None])).astype(np.float32)
_PAIR_LO = np.array([lo for lo in range(8) for hi in range(lo + 1, 8)], np.int32)
_PAIR_HI = np.array([hi for lo in range(8) for hi in range(lo + 1, 8)], np.int32)


def _dot(a, b):
    return jnp.dot(a, b, preferred_element_type=f32)


def _dot_nt(a, b):
    return lax.dot_general(a, b, (((1,), (1,)), ((), ())), preferred_element_type=f32)


def _dot_tn(a, b):
    return lax.dot_general(a, b, (((0,), (0,)), ((), ())), preferred_element_type=f32)


def _rms(x, gain):
    return x * lax.rsqrt(jnp.mean(x * x, axis=-1, keepdims=True) + EPS) * gain


def _split3(x):
    hi = x.astype(bf16)
    r1 = x - hi.astype(f32)
    mid = r1.astype(bf16)
    lo = (r1 - mid.astype(f32)).astype(bf16)
    return hi, mid, lo


def _prep_kernel(x_ref, g_ref, win_ref, qn_ref, wqb_ref, kvn_ref, wkvb_ref, ct_ref, st_ref,
                 qg_ref, kg_ref, vg_ref, rg_ref, a_ref, qm_ref, km_ref, vmt_ref):
    u = _rms(x_ref[...], g_ref[...]).astype(bf16)

    def proj(lo, hi):
        return _dot(u, win_ref[:, lo:hi])

    qg_ref[...] = proj(C_Q, C_K).astype(bf16)
    kg_ref[...] = proj(C_K, C_V).astype(bf16)
    vg_ref[...] = proj(C_V, C_R).astype(bf16)
    rg_ref[...] = proj(C_R, C_QLAT).astype(bf16)
    z = proj(C_QLAT, C_END)
    a_ref[...] = z[:, C_A - C_QLAT:].astype(bf16)
    ctab = ct_ref[...]
    stab = st_ref[...]

    def rope(seg):
        return seg * ctab + pltpu.roll(seg, 64, axis=1) * stab

    k_rope = rope(z[:, C_KROPE - C_QLAT:C_A - C_QLAT]).astype(bf16)
    qn = _rms(z[:, 0:MLA_Q_RANK], qn_ref[...]).astype(bf16)
    kvn = _rms(z[:, MLA_Q_RANK:MLA_Q_RANK + MLA_KV_RANK], kvn_ref[...]).astype(bf16)
    scale = (MLA_NOPE + MLA_ROPE) ** -0.5 * LOG2_E
    qf = _dot(qn, wqb_ref[...])
    kvf = _dot(kvn, wkvb_ref[...])
    for h in range(MLA_HEADS):
        c = h * MLA_QK_PAD
        qm_ref[:, c:c + LANE] = (qf[:, c:c + LANE] * scale).astype(bf16)
        qm_ref[:, c + LANE:c + 2 * LANE] = (rope(qf[:, c + LANE:c + 2 * LANE]) * scale).astype(bf16)
        km_ref[:, h * LANE:(h + 1) * LANE] = kvf[:, h * LANE:(h + 1) * LANE].astype(bf16)
    km_ref[:, MLA_HEADS * MLA_NOPE:MLA_K_W] = k_rope
    vt = kvf[:, MLA_HEADS * MLA_NOPE:].T
    for h in range(MLA_HEADS):
        vmt_ref[h * MLA_VA:h * MLA_VA + MLA_V, :] = vt[h * MLA_V:(h + 1) * MLA_V].astype(bf16)
        vmt_ref[h * MLA_VA + MLA_V:(h + 1) * MLA_VA, :] = jnp.ones((MLA_VA - MLA_V, vt.shape[1]), bf16)


def _prep_call(x2d, rows_per_seq, tile, gain, w_in_r, q_norm, w_qb_r, kv_norm, w_kvb_r, ctab, stab):
    t = x2d.shape[0]
    nj = rows_per_seq // tile
    grid = (t // rows_per_seq, nj)

    def row(b, j):
        return (b * nj + j, 0)

    def const(b, j):
        return (0, 0)

    def tab(b, j):
        return (j, 0)

    widths = (GLA_QK, GLA_QK, GLA_VW, GLA_VW, LANE, MLA_HEADS * MLA_QK_PAD, MLA_K_W)
    return pl.pallas_call(
        _prep_kernel,
        grid=grid,
        in_specs=[
            pl.BlockSpec((tile, D_MODEL), row),
            pl.BlockSpec((1, D_MODEL), const),
            pl.BlockSpec((D_MODEL, C_END), const),
            pl.BlockSpec((1, MLA_Q_RANK), const),
            pl.BlockSpec((MLA_Q_RANK, MLA_HEADS * MLA_QK_PAD), const),
            pl.BlockSpec((1, MLA_KV_RANK), const),
            pl.BlockSpec((MLA_KV_RANK, 2 * MLA_OUT), const),
            pl.BlockSpec((tile, LANE), tab),
            pl.BlockSpec((tile, LANE), tab),
        ],
        out_specs=[pl.BlockSpec((tile, w), row) for w in widths]
        + [pl.BlockSpec((None, MLA_HEADS * MLA_VA, tile), lambda b, j: (b * nj + j, 0, 0))],
        out_shape=[jax.ShapeDtypeStruct((t, w), bf16) for w in widths]
        + [jax.ShapeDtypeStruct((t // tile, MLA_HEADS * MLA_VA, tile), bf16)],
        compiler_params=pltpu.CompilerParams(
            dimension_semantics=("parallel", "parallel"), vmem_limit_bytes=VMEM_LIMIT),
        name="prep",
    )(x2d, gain, w_in_r, q_norm, w_qb_r, kv_norm, w_kvb_r, ctab, stab)


def _gla_log_decay(a, wa2_ref, ba_ref):
    s = _dot(a, wa2_ref[...]) + ba_ref[...]
    return (jnp.minimum(s, 0.0) - jnp.log(1.0 + jnp.exp(-jnp.abs(s)))) * (1.0 / GLA_TAU)


def _gla_front(q, k, v, la, tri, want_out):
    nc = la.shape[0] // CHUNK
    hi, mid, lo = _split3(la)
    b = _dot(tri, hi) + _dot(tri, mid) + _dot(tri, lo)
    b_last = [b[(c + 1) * CHUNK - 1:(c + 1) * CHUNK, :] for c in range(nc)]
    b_last_full = jnp.concatenate([jnp.broadcast_to(bl, (CHUNK, GLA_QK)) for bl in b_last], axis=0)
    kf = k.astype(f32)
    front = dict(v=v, b_last=b_last, kd=(kf * jnp.exp(b_last_full - b)).astype(bf16))
    if want_out:
        front.update(qe=(q.astype(f32) * (GLA_DK ** -0.5) * jnp.exp(b)).astype(bf16),
                     ke=kf * jnp.exp(-b), vf=v.astype(f32))
    return front


def _gla_chunks(front, st_ref, want_out):
    v, kd, b_last = front["v"], front["kd"], front["b_last"]
    rr = lax.broadcasted_iota(i32, (GLA_VW, GLA_QK), 0) // GLA_DV
    cc = lax.broadcasted_iota(i32, (GLA_VW, GLA_QK), 1) // GLA_DK
    if want_out:
        qe, ke, vf = front["qe"], front["ke"], front["vf"]
        lane_h = lax.broadcasted_iota(i32, (CHUNK, GLA_QK), 1) // GLA_DK
        vlane_h = lax.broadcasted_iota(i32, (CHUNK, GLA_VW), 1) // GLA_DV
        a_row = lax.broadcasted_iota(i32, (CHUNK, GLA_QK), 0)
        a_col = lax.broadcasted_iota(i32, (CHUNK, GLA_QK), 1) % CHUNK
    outs = []
    st = st_ref[...]
    for c in range(len(b_last)):
        rows = slice(c * CHUNK, (c + 1) * CHUNK)
        upd = jnp.where(rr == cc, _dot_tn(v[rows], kd[rows]), 0.0)
        if want_out:
            kbd = jnp.concatenate(
                [jnp.where(lane_h == h, ke[rows], 0.0) for h in range(GLA_HEADS)], axis=0).astype(bf16)
            att = jnp.where(a_col <= a_row, _dot_nt(qe[rows], kbd), 0.0).astype(bf16)
            vbd = jnp.concatenate(
                [jnp.where(vlane_h == h, vf[rows], 0.0) for h in range(GLA_HEADS)], axis=0).astype(bf16)
            outs.append(_dot(att, vbd) + _dot_nt(qe[rows], st.astype(bf16)))
        st = st * jnp.exp(b_last[c]) + upd
    st_ref[...] = st
    return jnp.concatenate(outs, axis=0) if want_out else None


def _gla_kernel(q_ref, k_ref, v_ref, r_ref, a_ref, km_ref, vm_ref, am_ref, wa2_ref, ba_ref, gain_ref, tri_ref,
                y_ref, st_ref):
    j = pl.program_id(1)

    @pl.when(j == 0)
    def _():
        st_ref[...] = jnp.zeros_like(st_ref)
        la = _gla_log_decay(am_ref[...], wa2_ref, ba_ref)
        row = lax.broadcasted_iota(i32, la.shape, 0)
        la = jnp.where(row >= CHUNK - N_META, la, 0.0)
        front = _gla_front(None, km_ref[...], vm_ref[...], la, tri_ref[0:CHUNK, 0:CHUNK], False)
        _gla_chunks(front, st_ref.at[0], False)
        for bb in range(1, GLA_BATCH):
            st_ref[bb] = st_ref[0]

    fronts = [_gla_front(q_ref[bb], k_ref[bb], v_ref[bb], _gla_log_decay(a_ref[bb], wa2_ref, ba_ref),
                         tri_ref[...], True) for bb in range(GLA_BATCH)]
    for bb in range(GLA_BATCH):
        o = _gla_chunks(fronts[bb], st_ref.at[bb], True)
        r = r_ref[bb].astype(f32)
        outs = []
        for h in range(GLA_HEADS):
            oh = o[:, h * GLA_DV:(h + 1) * GLA_DV]
            outs.append(oh * lax.rsqrt(jnp.mean(oh * oh, axis=-1, keepdims=True) + EPS))
        on = jnp.concatenate(outs, axis=1) * gain_ref[...]
        y_ref[bb] = (on * (r * jax.nn.sigmoid(r))).astype(bf16)


def _gla_call(qg, kg, vg, rg, ag, km, vm, am, wa2_p, b_a, gain, batch, seq):
    nj = seq // GLA_TILE

    def row(b, j):
        return (b, j, 0)

    def const(b, j):
        return (0, 0)

    def seqs(a):
        return a.reshape(batch, seq, a.shape[-1])

    out = pl.pallas_call(
        _gla_kernel,
        grid=(batch // GLA_BATCH, nj),
        in_specs=[
            pl.BlockSpec((GLA_BATCH, GLA_TILE, GLA_QK), row),
            pl.BlockSpec((GLA_BATCH, GLA_TILE, GLA_QK), row),
            pl.BlockSpec((GLA_BATCH, GLA_TILE, GLA_VW), row),
            pl.BlockSpec((GLA_BATCH, GLA_TILE, GLA_VW), row),
            pl.BlockSpec((GLA_BATCH, GLA_TILE, LANE), row),
            pl.BlockSpec((CHUNK, GLA_QK), const),
            pl.BlockSpec((CHUNK, GLA_VW), const),
            pl.BlockSpec((CHUNK, LANE), const),
            pl.BlockSpec((LANE, GLA_QK), const),
            pl.BlockSpec((1, GLA_QK), const),
            pl.BlockSpec((1, GLA_VW), const),
            pl.BlockSpec((GLA_TILE, GLA_TILE), const),
        ],
        out_specs=pl.BlockSpec((GLA_BATCH, GLA_TILE, GLA_VW), row),
        out_shape=jax.ShapeDtypeStruct((batch, seq, GLA_VW), bf16),
        scratch_shapes=[pltpu.VMEM((GLA_BATCH, GLA_VW, GLA_QK), f32)],
        compiler_params=pltpu.CompilerParams(
            dimension_semantics=("parallel", "arbitrary"), vmem_limit_bytes=VMEM_LIMIT),
        name="gla",
    )(seqs(qg), seqs(kg), seqs(vg), seqs(rg), seqs(ag), km, vm, am, wa2_p, b_a, gain,
      jnp.asarray(_CHUNK_PREFIX, bf16))
    return out.reshape(batch * seq, GLA_VW)


def _mla_kernel(q_ref, k_ref, vt_ref, km_ref, vmt_ref, o_ref, sa_ref, sb_ref):
    i = pl.program_id(2)
    tq = ATT_TILE
    w = MLA_QK_PAD
    va = MLA_VA
    heads = range(ATT_HEADS)

    def keys(ref, rows, h):
        return jnp.concatenate([ref[rows, h * MLA_NOPE:(h + 1) * MLA_NOPE], ref[rows, MLA_HEADS * MLA_NOPE:MLA_K_W]],
                               axis=1)

    def scores(h, blk):
        rows = pl.ds(pl.multiple_of(blk * tq, tq), tq)
        return _dot_nt(keys(k_ref, rows, h), q_ref[:, h * w:(h + 1) * w])

    def soft(s, vtb, carry, mask=None):
        m, acc = carry
        if mask is not None:
            s = jnp.where(mask, s, -1e30)
        m_new = jnp.maximum(m, jnp.max(s, axis=0, keepdims=True))
        p = jnp.exp2(s - m_new).astype(bf16)
        return m_new, jnp.exp2(m - m_new) * acc + _dot(vtb, p)

    def vt(h, blk):
        return vt_ref[blk, h * va:(h + 1) * va, :]

    def finish(carries):
        ss = [_dot_nt(keys(km_ref, slice(None), h), q_ref[:, h * w:(h + 1) * w]) for h in heads]
        accs = [soft(ss[h], vmt_ref[h * va:(h + 1) * va, :], carries[h])[1] for h in heads]
        for h in heads:
            acc = accs[h]
            o_ref[:, h * MLA_V:(h + 1) * MLA_V] = (acc[:MLA_V] * (1.0 / acc[MLA_V:MLA_V + 1])).T.astype(bf16)

    kc = lax.broadcasted_iota(i32, (tq, tq), 0) // CHUNK
    qc = lax.broadcasted_iota(i32, (tq, tq), 1) // CHUNK
    mask = kc <= qc

    for h in heads:
        sa_ref[h] = scores(h, 0)

    def pair(p, carries):
        b0 = 2 * p
        for h in heads:
            sb_ref[h] = scores(h, b0 + 1)
        carries = [soft(sa_ref[h], vt(h, b0), carries[h]) for h in heads]
        for h in heads:
            sa_ref[h] = scores(h, b0 + 2)
        return tuple(soft(sb_ref[h], vt(h, b0 + 1), carries[h]) for h in heads)

    init = tuple((jnp.full((1, tq), -1e30, f32), jnp.zeros((va, tq), f32)) for _ in heads)
    carries = lax.fori_loop(0, i // 2, pair, init)

    @pl.when(i % 2 == 1)
    def _():
        for h in heads:
            sb_ref[h] = scores(h, i)
        c1 = [soft(sa_ref[h], vt(h, i - 1), carries[h]) for h in heads]
        finish([soft(sb_ref[h], vt(h, i), c1[h], mask) for h in heads])

    @pl.when(i % 2 == 0)
    def _():
        finish([soft(sa_ref[h], vt(h, i), carries[h], mask) for h in heads])


def _mla_call(qm, km, vmt, km_meta, vmt_meta, batch, seq):
    nq = seq // ATT_TILE
    nh = ATT_HEADS
    qm3 = qm.reshape(batch, seq, MLA_HEADS * MLA_QK_PAD)
    km3 = km.reshape(batch, seq, MLA_K_W)
    assert nh == MLA_HEADS
    vt4 = vmt.reshape(batch, nq, MLA_HEADS * MLA_VA, ATT_TILE)
    out = pl.pallas_call(
        _mla_kernel,
        grid=(batch, MLA_HEADS // nh, nq),
        in_specs=[
            pl.BlockSpec((None, ATT_TILE, nh * MLA_QK_PAD), lambda b, h, i: (b, i, h)),
            pl.BlockSpec((None, seq, MLA_K_W), lambda b, h, i: (b, 0, 0)),
            pl.BlockSpec((None, nq, nh * MLA_VA, ATT_TILE), lambda b, h, i: (b, 0, h, 0)),
            pl.BlockSpec((N_META, MLA_K_W), lambda b, h, i: (0, 0)),
            pl.BlockSpec((nh * MLA_VA, N_META), lambda b, h, i: (h, 0)),
        ],
        out_specs=pl.BlockSpec((None, ATT_TILE, nh * MLA_V), lambda b, h, i: (b, i, h)),
        out_shape=jax.ShapeDtypeStruct((batch, seq, MLA_OUT), bf16),
        scratch_shapes=[pltpu.VMEM((nh, ATT_TILE, ATT_TILE), f32), pltpu.VMEM((nh, ATT_TILE, ATT_TILE), f32)],
        compiler_params=pltpu.CompilerParams(
            dimension_semantics=("parallel", "parallel", "arbitrary"), vmem_limit_bytes=VMEM_LIMIT),
        name="mla",
    )(qm3, km3, vt4, km_meta, vmt_meta)
    return out.reshape(batch * seq, MLA_OUT)


def _route_cols(lt):
    r = lt.shape[1]
    neg = -1e30
    gl = lt[0:N_GROUPS, :]
    gsub = lax.broadcasted_iota(i32, (N_GROUPS, r), 0)
    gmax = jnp.max(gl, axis=0, keepdims=True)
    g_p = 1.0 / jnp.sum(jnp.exp(gl - gmax), axis=0, keepdims=True)
    g_idx = jnp.min(jnp.where(gl == gmax, gsub, N_GROUPS), axis=0, keepdims=True)
    el_all = lt[N_GROUPS:N_GROUPS + N_EXPERTS, :]
    esub = lax.broadcasted_iota(i32, (N_EXPERTS, r), 0)
    base = g_idx * EXPERTS_PER_GROUP
    e_mask = (esub >= base) & (esub < base + EXPERTS_PER_GROUP)
    el = jnp.where(e_mask, el_all, neg)
    m1 = jnp.max(el, axis=0, keepdims=True)
    i1 = jnp.min(jnp.where(e_mask & (el == m1), esub, N_EXPERTS), axis=0, keepdims=True)
    el2 = jnp.where(esub == i1, neg, el)
    m2 = jnp.max(el2, axis=0, keepdims=True)
    i2 = jnp.min(jnp.where(e_mask & (esub != i1) & (el2 == m2), esub, N_EXPERTS), axis=0, keepdims=True)
    rr = jnp.exp(m2 - m1)
    ga = g_p / (1.0 + rr)
    gb = g_p * rr / (1.0 + rr)
    la_ = i1 - base
    lb_ = i2 - base
    lo = jnp.minimum(la_, lb_)
    hi = jnp.maximum(la_, lb_)
    g_lo = jnp.where(la_ < lb_, ga, gb)
    g_hi = jnp.where(la_ < lb_, gb, ga)
    pidx = ((lo * (2 * EXPERTS_PER_GROUP - 1 - lo)) >> 1) + (hi - lo - 1)
    bucket = g_idx * N_PAIRS + pidx
    bsub = lax.broadcasted_iota(i32, (BUCKET_LANES, r), 0)
    oht = jnp.where(bsub == bucket, 1.0, 0.0)
    ohb = oht.astype(bf16)
    ri = lax.broadcasted_iota(i32, (ROUTE_ROWS, ROUTE_ROWS), 0)
    ci = lax.broadcasted_iota(i32, (ROUTE_ROWS, ROUTE_ROWS), 1)
    before = jnp.where(ri < ci, 1.0, 0.0).astype(bf16)
    ones = jnp.ones((SUBLANES, ROUTE_ROWS), bf16)
    subs = [slice(i * ROUTE_ROWS, (i + 1) * ROUTE_ROWS) for i in range(r // ROUTE_ROWS)]
    cum = jnp.concatenate([_dot(ohb[:, sl], before) for sl in subs], axis=1)
    rank = jnp.sum(oht * cum, axis=0, keepdims=True)
    counts = [_dot_nt(ones, ohb[:, sl])[0:1, :] for sl in subs]
    msub = lax.broadcasted_iota(i32, (LANE, r), 0)
    meta_t = jnp.where(msub == 0, bucket.astype(f32),
                       jnp.where(msub == 1, rank,
                                 jnp.where(msub == 2, g_lo, jnp.where(msub == 3, g_hi, 0.0))))
    return meta_t.T, counts, meta_t[0:SUBLANES, :]


def _outproj_kernel(x_ref, yg_ref, ym_ref, wog_ref, wom_ref, gain_ref, wrt_ref, rb_ref,
                    ux_ref, cnt_ref, rt_ref):
    per_group = ROUTE_GROUP // ROUTE_ROWS
    for grp in range(OUT_TILE // ROUTE_GROUP):
        rows = slice(grp * ROUTE_GROUP, (grp + 1) * ROUTE_GROUP)
        h1 = x_ref[rows, :] + _dot(yg_ref[rows, :], wog_ref[...]) + _dot(ym_ref[rows, :], wom_ref[...])
        ux_ref[rows, 0:D_MODEL] = h1
        u2 = _rms(h1, gain_ref[...])
        lt = _dot_nt(wrt_ref[...], u2.astype(bf16)) + rb_ref[...]
        meta, counts, routes = _route_cols(lt)
        ux_ref[rows, D_MODEL:ROW_W] = meta
        for i in range(per_group):
            cnt_ref[grp * per_group + i] = counts[i]
            rt_ref[grp * per_group + i] = routes[:, i * ROUTE_ROWS:(i + 1) * ROUTE_ROWS]


def _outproj_call(x2d, yg, ym, wo_g, wo_m, gain, w_r, rbias):
    t = x2d.shape[0]
    nt = t // OUT_TILE

    def row(i):
        return (i, 0)

    def const(i):
        return (0, 0)

    return pl.pallas_call(
        _outproj_kernel,
        grid=(nt,),
        in_specs=[
            pl.BlockSpec((OUT_TILE, D_MODEL), row),
            pl.BlockSpec((OUT_TILE, GLA_VW), row),
            pl.BlockSpec((OUT_TILE, MLA_OUT), row),
            pl.BlockSpec((GLA_VW, D_MODEL), const),
            pl.BlockSpec((MLA_OUT, D_MODEL), const),
            pl.BlockSpec((1, D_MODEL), const),
            pl.BlockSpec((LANE, D_MODEL), const),
            pl.BlockSpec((LANE, 1), const),
        ],
        out_specs=[
            pl.BlockSpec((OUT_TILE, ROW_W), row),
            pl.BlockSpec((OUT_TILE // ROUTE_ROWS, 1, BUCKET_LANES), lambda i: (i, 0, 0)),
            pl.BlockSpec((OUT_TILE // ROUTE_ROWS, SUBLANES, ROUTE_ROWS), lambda i: (i, 0, 0)),
        ],
        out_shape=[
            jax.ShapeDtypeStruct((t, ROW_W), f32),
            jax.ShapeDtypeStruct((nt * (OUT_TILE // ROUTE_ROWS), 1, BUCKET_LANES), f32),
            jax.ShapeDtypeStruct((nt * (OUT_TILE // ROUTE_ROWS), SUBLANES, ROUTE_ROWS), f32),
        ],
        compiler_params=pltpu.CompilerParams(
            dimension_semantics=("parallel",), vmem_limit_bytes=VMEM_LIMIT),
        name="outproj",
    )(x2d, yg, ym, wo_g, wo_m, gain, w_r, rbias)


def _scatter_kernel(pos_ref, zb_ref, ux_ref, hs_ref, zbuf, sem, zsem):
    @pl.when(pl.program_id(0) == 0)
    def _():
        zbuf[...] = jnp.zeros_like(zbuf)

        def zero_copy(j):
            rows = pl.ds(pl.multiple_of(zb_ref[j] * MOE_BLOCK, MOE_BLOCK), MOE_BLOCK)
            return pltpu.make_async_copy(zbuf, hs_ref.at[rows], zsem)

        def zstart(j, c):
            @pl.when(zb_ref[j] >= 0)
            def _():
                zero_copy(j).start()
            return c

        def zwait(j, c):
            @pl.when(zb_ref[j] >= 0)
            def _():
                zero_copy(j).wait()
            return c

        lax.fori_loop(0, zb_ref.shape[0], zstart, 0)
        lax.fori_loop(0, zb_ref.shape[0], zwait, 0)

    def start(io, c):
        for r in range(ISSUE_UNROLL):
            ii = io * (ISSUE_UNROLL // SUBLANES) + r // SUBLANES
            pltpu.make_async_copy(ux_ref.at[ii, pl.ds(r % SUBLANES, 1)],
                                  hs_ref.at[pl.ds(pos_ref[io * ISSUE_UNROLL + r], 1)], sem).start()
        return c

    lax.fori_loop(0, SCATTER_TILE // ISSUE_UNROLL, start, 0)
    pltpu.make_async_copy(hs_ref.at[pl.ds(0, SCATTER_TILE)], hs_ref.at[pl.ds(0, SCATTER_TILE)], sem).wait()


def _scatter_call(pos, zero_blocks, ux, n_slots):
    t = ux.shape[0]
    nz = zero_blocks.shape[0]
    return pl.pallas_call(
        _scatter_kernel,
        grid=(t // SCATTER_TILE,),
        in_specs=[
            pl.BlockSpec((SCATTER_TILE,), lambda i: (i,), memory_space=pltpu.SMEM),
            pl.BlockSpec((nz,), lambda i: (0,), memory_space=pltpu.SMEM),
            pl.BlockSpec((SCATTER_TILE // SUBLANES, SUBLANES, ROW_W), lambda i: (i, 0, 0)),
        ],
        out_specs=pl.BlockSpec(memory_space=pl.ANY),
        out_shape=jax.ShapeDtypeStruct((n_slots, ROW_W), f32),
        scratch_shapes=[pltpu.VMEM((MOE_BLOCK, ROW_W), f32), pltpu.SemaphoreType.DMA(()),
                        pltpu.SemaphoreType.DMA(())],
        compiler_params=pltpu.CompilerParams(
            dimension_semantics=("arbitrary",), vmem_limit_bytes=VMEM_LIMIT),
        name="scatter",
    )(pos, zero_blocks, ux.reshape(t // SUBLANES, SUBLANES, ROW_W))


def _moe_kernel(se_ref, sf_ref, sk_ref, sp_ref, sn_ref, sb_ref, sr_ref, si_ref, so_ref,
                hs_hbm, wg_hbm, wu_hbm, wd_hbm, fg_ref, y_hbm,
                xbuf, obuf, wg_buf, wu_buf, wd_buf, wgu_s, wd_s, in_sem, out_sem, w_sem):
    s = pl.program_id(0)
    ns = pl.num_programs(0)
    cur = s % 2
    g_n = MOE_GROUP

    def in_copy(step, g, buf):
        rows = pl.ds(pl.multiple_of(sb_ref[step * g_n + g] * MOE_BLOCK, MOE_BLOCK), MOE_BLOCK)
        return pltpu.make_async_copy(
            hs_hbm.at[rows], xbuf.at[buf, pl.ds(g * MOE_BLOCK, MOE_BLOCK)], in_sem.at[buf])

    def out_copy(step, g, buf):
        rows = pl.ds(pl.multiple_of(sb_ref[step * g_n + g] * MOE_BLOCK, MOE_BLOCK), MOE_BLOCK)
        cols = pl.ds(pl.multiple_of(sr_ref[step * g_n + g] * D_MODEL, D_MODEL), D_MODEL)
        return pltpu.make_async_copy(
            obuf.at[buf, pl.ds(g * MOE_BLOCK, MOE_BLOCK)], y_hbm.at[rows, cols], out_sem.at[buf])

    def for_slots(step, flags_ref, fn):
        for g in range(g_n):
            @pl.when(flags_ref[step * g_n + g] == 1)
            def _():
                fn(g)

    @pl.when(s == 0)
    def _():
        xbuf[...] = jnp.zeros_like(xbuf)
        for_slots(0, si_ref, lambda g: in_copy(0, g, 0).start())

    @pl.when(s + 1 < ns)
    def _():
        for_slots(s + 1, si_ref, lambda g: in_copy(s + 1, g, 1 - cur).start())

    for_slots(s, si_ref, lambda g: in_copy(s, g, cur).wait())

    @pl.when(s >= 2)
    def _():
        for_slots(s - 2, so_ref, lambda g: out_copy(s - 2, g, cur).wait())

    def w_copies(expert, slot):
        return (pltpu.make_async_copy(wg_hbm.at[expert], wg_buf.at[slot], w_sem.at[slot]),
                pltpu.make_async_copy(wu_hbm.at[expert], wu_buf.at[slot], w_sem.at[slot]),
                pltpu.make_async_copy(wd_hbm.at[expert], wd_buf.at[slot], w_sem.at[slot]))

    @pl.when(s == 0)
    def _():
        for c in w_copies(se_ref[0], sp_ref[0]):
            c.start()

    @pl.when(sf_ref[s] == 1)
    def _():
        slot = sp_ref[s]
        for c in w_copies(se_ref[s], slot):
            c.wait()
        wgu_s[:, 0:D_EXPERT] = wg_buf[slot].astype(bf16)
        wgu_s[:, D_EXPERT:2 * D_EXPERT] = wu_buf[slot].astype(bf16)
        wd_s[...] = wd_buf[slot].astype(bf16)

        @pl.when(sn_ref[s] >= 0)
        def _():
            for c in w_copies(sn_ref[s], 1 - slot):
                c.start()

    @pl.when(sk_ref[s] == 1)
    def _():
        h1 = xbuf[cur, :, 0:D_MODEL]
        meta = xbuf[cur, :, D_MODEL:ROW_W]
        u = _rms(h1, fg_ref[...]).astype(bf16)
        ones = jnp.ones((MOE_BLOCK, 1), f32)
        role0 = [sr_ref[s * g_n + g] == 0 for g in range(g_n)]
        gate = jnp.concatenate(
            [jnp.where(role0[g], meta[g * MOE_BLOCK:(g + 1) * MOE_BLOCK, 2:3],
                       meta[g * MOE_BLOCK:(g + 1) * MOE_BLOCK, 3:4]) for g in range(g_n)], axis=0)
        keep = jnp.concatenate([jnp.where(role0[g], ones, 0.0) for g in range(g_n)], axis=0)
        gu = _dot(u, wgu_s[...])
        gt = gu[:, 0:D_EXPERT]
        hdn = (gt * jax.nn.sigmoid(gt) * gu[:, D_EXPERT:]).astype(bf16)
        obuf[cur] = _dot(hdn, wd_s[...]) * gate + h1 * keep

    @pl.when(sk_ref[s] == 0)
    def _():
        obuf[cur] = jnp.zeros(obuf.shape[1:], f32)

    for_slots(s, so_ref, lambda g: out_copy(s, g, cur).start())

    @pl.when(s == ns - 1)
    def _():
        for_slots(s, so_ref, lambda g: out_copy(s, g, cur).wait())

        @pl.when(s >= 1)
        def _():
            for_slots(s - 1, so_ref, lambda g: out_copy(s - 1, g, 1 - cur).wait())


def _moe_call(plan, hs, w_gate, w_up, w_down, ffn_gain):
    n_steps = plan[0].shape[0]
    n_slots = hs.shape[0]
    rows = MOE_GROUP * MOE_BLOCK

    grid_spec = pltpu.PrefetchScalarGridSpec(
        num_scalar_prefetch=9,
        grid=(n_steps,),
        in_specs=[
            pl.BlockSpec(memory_space=pl.ANY),
            pl.BlockSpec(memory_space=pl.ANY),
            pl.BlockSpec(memory_space=pl.ANY),
            pl.BlockSpec(memory_space=pl.ANY),
            pl.BlockSpec((1, D_MODEL), lambda s, *_: (0, 0)),
        ],
        out_specs=pl.BlockSpec(memory_space=pl.ANY),
        scratch_shapes=[
            pltpu.VMEM((2, rows, ROW_W), f32),
            pltpu.VMEM((2, rows, D_MODEL), f32),
            pltpu.VMEM((2, D_MODEL, D_EXPERT), f32),
            pltpu.VMEM((2, D_MODEL, D_EXPERT), f32),
            pltpu.VMEM((2, D_EXPERT, D_MODEL), f32),
            pltpu.VMEM((D_MODEL, 2 * D_EXPERT), bf16),
            pltpu.VMEM((D_EXPERT, D_MODEL), bf16),
            pltpu.SemaphoreType.DMA((2,)),
            pltpu.SemaphoreType.DMA((2,)),
            pltpu.SemaphoreType.DMA((2,)),
        ],
    )
    return pl.pallas_call(
        _moe_kernel,
        grid_spec=grid_spec,
        out_shape=jax.ShapeDtypeStruct((n_slots, 2 * D_MODEL), f32),
        compiler_params=pltpu.CompilerParams(
            dimension_semantics=("arbitrary",), vmem_limit_bytes=VMEM_LIMIT),
        name="moe",
    )(*plan, hs, w_gate, w_up, w_down, ffn_gain)


def _final_kernel(posc_ref, posn_ref, gain_ref, y_hbm, o_ref, ybuf, sem):
    i = pl.program_id(0)
    cur = i % 2

    def issue(pos_ref, buf):
        def start(io, c):
            for r in range(ISSUE_UNROLL):
                ii = io * (ISSUE_UNROLL // SUBLANES) + r // SUBLANES
                pltpu.make_async_copy(y_hbm.at[pl.ds(pos_ref[io * ISSUE_UNROLL + r], 1)],
                                      ybuf.at[buf, ii, pl.ds(r % SUBLANES, 1)], sem.at[buf]).start()
            return c

        lax.fori_loop(0, FINAL_TILE // ISSUE_UNROLL, start, 0)

    @pl.when(i == 0)
    def _():
        issue(posc_ref, 0)

    @pl.when(i + 1 < pl.num_programs(0))
    def _():
        issue(posn_ref, 1 - cur)

    pltpu.make_async_copy(ybuf.at[cur], ybuf.at[cur], sem.at[cur]).wait()
    h = ybuf[cur, :, :, 0:D_MODEL] + ybuf[cur, :, :, D_MODEL:2 * D_MODEL]
    o_ref[...] = _rms(h, gain_ref[...])


def _final_call(pos, gain, y):
    t = pos.shape[0]
    n = t // FINAL_TILE
    rows = FINAL_TILE // SUBLANES
    out = pl.pallas_call(
        _final_kernel,
        grid=(n,),
        in_specs=[
            pl.BlockSpec((FINAL_TILE,), lambda i: (i,), memory_space=pltpu.SMEM),
            pl.BlockSpec((FINAL_TILE,), lambda i: (jnp.minimum(i + 1, n - 1),), memory_space=pltpu.SMEM),
            pl.BlockSpec((1, 1, D_MODEL), lambda i: (0, 0, 0)),
            pl.BlockSpec(memory_space=pl.ANY),
        ],
        out_specs=pl.BlockSpec((rows, SUBLANES, D_MODEL), lambda i: (i, 0, 0)),
        out_shape=jax.ShapeDtypeStruct((t // SUBLANES, SUBLANES, D_MODEL), f32),
        scratch_shapes=[pltpu.VMEM((2, rows, SUBLANES, 2 * D_MODEL), f32), pltpu.SemaphoreType.DMA((2,))],
        compiler_params=pltpu.CompilerParams(
            dimension_semantics=("arbitrary",), vmem_limit_bytes=VMEM_LIMIT),
        name="final",
    )(pos, pos, gain.reshape(1, 1, D_MODEL), y)
    return out.reshape(t, D_MODEL)


def _rope_tables(pos):
    inv = ROPE_BASE ** (-jnp.arange(0, MLA_ROPE, 2, dtype=f32) / MLA_ROPE)
    ang = pos.astype(f32)[:, None] * inv[None, :]
    cos, sin = jnp.cos(ang), jnp.sin(ang)
    z = jnp.zeros((pos.shape[0], LANE - MLA_ROPE), f32)
    return jnp.concatenate([cos, cos, z], axis=1), jnp.concatenate([-sin, sin, z], axis=1)


def _relayout_weights(w_in, w_qb, w_kvb):
    half = MLA_ROPE // 2
    perm = (np.arange(MLA_ROPE) + half) % MLA_ROPE
    pts = np.cumsum((GLA_QK, GLA_QK, GLA_VW, GLA_VW, GLA_GATE_RANK, MLA_Q_RANK, MLA_KV_RANK, MLA_ROPE))
    q_g, k_g, v_g, r_g, a_l, q_lat, kv_lat, k_rope = jnp.split(w_in, pts[:-1], axis=1)
    a_seg = jnp.pad(a_l, ((0, 0), (0, LANE - GLA_GATE_RANK)))
    w_in_r = jnp.concatenate(
        [q_g, k_g, v_g, r_g, q_lat, kv_lat, k_rope, k_rope[:, perm], a_seg], axis=1).astype(bf16)
    qcols, kcols, vcols = [], [], []
    for h in range(MLA_HEADS):
        c = h * (MLA_NOPE + MLA_ROPE)
        rope = w_qb[:, c + MLA_NOPE:c + MLA_NOPE + MLA_ROPE]
        qcols += [w_qb[:, c:c + MLA_NOPE], rope, rope[:, perm]]
        c2 = h * (MLA_NOPE + MLA_V)
        kcols.append(w_kvb[:, c2:c2 + MLA_NOPE])
        vcols.append(w_kvb[:, c2 + MLA_NOPE:c2 + MLA_NOPE + MLA_V])
    return w_in_r, jnp.concatenate(qcols, axis=1).astype(bf16), jnp.concatenate(kcols + vcols, axis=1).astype(bf16)


_BUCKET_GROUP = np.arange(N_BUCKETS) // N_PAIRS
_RUN_EXPERT = np.concatenate([_BUCKET_GROUP * EXPERTS_PER_GROUP + _PAIR_LO[np.arange(N_BUCKETS) % N_PAIRS],
                              _BUCKET_GROUP * EXPERTS_PER_GROUP + _PAIR_HI[np.arange(N_BUCKETS) % N_PAIRS]])
_RUN_IS_EXPERT = (_RUN_EXPERT[:, None] == np.arange(N_EXPERTS)[None, :]).astype(np.int32)
_RUN_BEFORE = ((_RUN_EXPERT[:, None] == _RUN_EXPERT[None, :])
               & (np.arange(2 * N_BUCKETS)[None, :] < np.arange(2 * N_BUCKETS)[:, None])).astype(np.int32)


def _route_plan(counts, bucket, rank, n_tok):
    nt = counts.shape[0]
    g_n = MOE_GROUP
    tot = counts.sum(axis=0)
    nblk = (tot + MOE_BLOCK - 1) // MOE_BLOCK
    bstart_blk = jnp.cumsum(nblk) - nblk
    n_blocks = jnp.sum(nblk)
    tile_base = bstart_blk[None, :] * MOE_BLOCK + jnp.cumsum(counts, axis=0) - counts
    hit = bucket.reshape(nt, -1, 1) == jnp.arange(N_BUCKETS, dtype=i32)
    pos = jnp.sum(jnp.where(hit, tile_base[:, None, :], 0), axis=-1).reshape(-1) + rank
    nb_max = (n_tok + N_BUCKETS * (MOE_BLOCK - 1)) // MOE_BLOCK

    n_run = jnp.concatenate([nblk, nblk])
    b0_run = jnp.concatenate([bstart_blk, bstart_blk])
    c_e = jnp.sum(n_run[:, None] * _RUN_IS_EXPERT, axis=0)
    g_e = (c_e + g_n - 1) // g_n
    gend = jnp.cumsum(g_e)
    gstart = gend - g_e
    n_compute = gend[-1]
    off_run = jnp.sum(_RUN_BEFORE * n_run[None, :], axis=1)
    f_run = jnp.sum(_RUN_IS_EXPERT * gstart[None, :], axis=1) * g_n + off_run

    n_steps = (2 * nb_max + N_EXPERTS * (g_n - 1) + g_n - 1) // g_n + 1
    f = jnp.arange(n_steps * g_n, dtype=i32)
    in_run = (f[:, None] >= f_run[None, :]) & (f[:, None] < (f_run + n_run)[None, :])
    valid_c = jnp.any(in_run, axis=1)
    block_c = jnp.sum(jnp.where(in_run, b0_run[None, :] + f[:, None] - f_run[None, :], 0), axis=1)
    role_c = jnp.sum(jnp.where(in_run[:, N_BUCKETS:], 1, 0), axis=1)
    u_idx = f - n_compute * g_n
    valid_f = (u_idx >= 0) & (u_idx < 2 * (nb_max - n_blocks))
    slot_block = jnp.where(valid_c, block_c, jnp.where(valid_f, n_blocks + u_idx // 2, 0))
    slot_role = jnp.where(valid_c, role_c, jnp.where(valid_f, u_idx % 2, 0))

    step = jnp.arange(n_steps, dtype=i32)
    e_of_step = jnp.minimum(jnp.sum(gend[None, :] <= step[:, None], axis=1), N_EXPERTS - 1)
    is_compute = step < n_compute
    last_e = jnp.max(jnp.where(is_compute, e_of_step, 0))
    step_expert = jnp.where(is_compute, e_of_step, last_e)
    step_first = jnp.concatenate([jnp.ones((1,), bool), step_expert[1:] != step_expert[:-1]])
    ordinal = jnp.cumsum(step_first.astype(i32)) - 1
    ords = jnp.arange(N_EXPERTS + 1, dtype=i32)
    expert_of_ord = jnp.sum(jnp.where(step_first[:, None] & (ordinal[:, None] == ords[None, :]),
                                      step_expert[:, None], 0), axis=0)
    has_next = ordinal + 1 <= ordinal[-1]
    next_expert = jnp.sum(jnp.where(ords[None, :] == ordinal[:, None] + 1, expert_of_ord[None, :], 0), axis=1)
    step_next = jnp.where(step_first & has_next, next_expert, -1)
    plan = tuple(a.astype(i32) for a in
                 (step_expert, step_first, is_compute, ordinal % 2, step_next,
                  slot_block, slot_role, valid_c, valid_c | valid_f))
    last_blk = jnp.where(nblk > 0, bstart_blk + nblk - 1, -1)
    spare = n_blocks + jnp.arange(nb_max - n_tok // MOE_BLOCK, dtype=i32)
    zero_blocks = jnp.concatenate([last_blk, jnp.where(spare < nb_max, spare, -1)]).astype(i32)
    return pos.astype(i32), plan, zero_blocks, nb_max


def kernel(x, meta_tokens, mix_norm, w_in, gla_w_a2, gla_b_a, gla_out_norm, mla_q_norm, mla_w_qb, mla_kv_norm,
           mla_w_kvb, w_out, ffn_norm, router_group_w, router_group_b, router_expert_w, router_expert_b,
           expert_w_gate, expert_w_up, expert_w_down, final_norm):
    batch, seq, d = x.shape
    assert PREP_TILE == ATT_TILE
    assert d == D_MODEL and seq % max(PREP_TILE, GLA_TILE, ATT_TILE) == 0
    assert (batch * seq) % max(OUT_TILE, SCATTER_TILE, FINAL_TILE) == 0 and batch % GLA_BATCH == 0
    n_tok = batch * seq
    x2d = x.reshape(n_tok, d)

    w_in_r, w_qb_r, w_kvb_r = _relayout_weights(w_in[0], mla_w_qb[0], mla_w_kvb[0])
    mixg = mix_norm[0].reshape(1, d)
    qn = mla_q_norm[0].reshape(1, MLA_Q_RANK)
    kvn = mla_kv_norm[0].reshape(1, MLA_KV_RANK)
    ct_m, st_m = _rope_tables(jnp.arange(META_TILE))
    ct_x, st_x = _rope_tables(N_META + jnp.arange(seq))

    x_meta = jnp.pad(meta_tokens.astype(f32), ((0, META_TILE - N_META), (0, 0)))
    _, kg_m, vg_m, _, a_m, _, km_m, vmt_m = _prep_call(
        x_meta, META_TILE, META_TILE, mixg, w_in_r, qn, w_qb_r, kvn, w_kvb_r, ct_m, st_m)
    qg, kg, vg, rg, ag, qm, km, vmt = _prep_call(
        x2d, seq, PREP_TILE, mixg, w_in_r, qn, w_qb_r, kvn, w_kvb_r, ct_x, st_x)

    def chunk0(a):
        return jnp.pad(a[:N_META], ((CHUNK - N_META, 0), (0, 0)))

    wa2_p = jnp.pad(gla_w_a2[0], ((0, LANE - GLA_GATE_RANK), (0, 0))).astype(bf16)
    y_gla = _gla_call(qg, kg, vg, rg, ag, chunk0(kg_m), chunk0(vg_m), chunk0(a_m),
                      wa2_p, gla_b_a[0].reshape(1, GLA_QK), gla_out_norm[0].reshape(1, GLA_VW), batch, seq)
    y_mla = _mla_call(qm, km, vmt, km_m[:N_META], vmt_m[0, :, :N_META], batch, seq)

    wo = w_out[0].astype(bf16)
    rw = jnp.concatenate([router_group_w[0], router_expert_w[0],
                          jnp.zeros((d, LANE - N_GROUPS - N_EXPERTS), f32)], axis=1)
    rb = jnp.concatenate([router_group_b[0], router_expert_b[0],
                          jnp.zeros((LANE - N_GROUPS - N_EXPERTS,), f32)]).reshape(1, LANE)
    ffn_gain = ffn_norm[0].reshape(1, d)
    ux, cnt, routes = _outproj_call(x2d, y_gla, y_mla, wo[:GLA_VW], wo[GLA_VW:], ffn_gain,
                                      rw.T.astype(bf16), rb.reshape(LANE, 1))

    counts = cnt.reshape(-1, BUCKET_LANES)[:, :N_BUCKETS].astype(i32)
    tok_bucket = routes[:, 0, :].reshape(-1).astype(i32)
    tok_rank = routes[:, 1, :].reshape(-1).astype(i32)
    pos, plan, zero_blocks, nb_max = _route_plan(counts, tok_bucket, tok_rank, n_tok)
    n_slots = nb_max * MOE_BLOCK
    hs = _scatter_call(pos, zero_blocks, ux, n_slots)
    y = _moe_call(plan, hs, expert_w_gate[0], expert_w_up[0], expert_w_down[0], ffn_gain)
    out = _final_call(pos, final_norm.reshape(1, d), y)
    return out.reshape(batch, seq, d)
```

```python
import functools

import numpy as np
import jax
import jax.numpy as jnp
from jax import lax
from jax.experimental import pallas as pl
from jax.experimental.pallas import tpu as pltpu

f32 = jnp.float32
bf16 = jnp.bfloat16
i32 = jnp.int32

D_MODEL = 1024
CHUNK = 64
N_META = 16
EPS = 1e-6
GLA_HEADS = 4
GLA_DK = 64
GLA_DV = 128
GLA_GATE_RANK = 16
GLA_TAU = 16.0
GLA_QK = GLA_HEADS * GLA_DK
GLA_VW = GLA_HEADS * GLA_DV
MLA_HEADS = 4
MLA_Q_RANK = 256
MLA_KV_RANK = 128
MLA_NOPE = 128
MLA_ROPE = 64
MLA_V = 128
MLA_OUT = MLA_HEADS * MLA_V
MLA_QK_PAD = 256
MLA_VA = MLA_V + 16
LOG2_E = 1.4426950408889634
ROPE_BASE = 10000.0
N_GROUPS = 8
EXPERTS_PER_GROUP = 8
N_EXPERTS = N_GROUPS * EXPERTS_PER_GROUP
D_EXPERT = 512
N_PAIRS = EXPERTS_PER_GROUP * (EXPERTS_PER_GROUP - 1) // 2
N_BUCKETS = N_GROUPS * N_PAIRS
BUCKET_LANES = 256
LANE = 128
SUBLANES = 8
MLA_K_W = MLA_HEADS * MLA_NOPE + LANE
META_W = LANE
ROW_W = D_MODEL + META_W

PREP_TILE = 512
GLA_TILE = 512
GLA_BATCH = 4
ATT_TILE = 512
ATT_HEADS = 4
META_TILE = 128
OUT_TILE = 1024
ROUTE_ROWS = 256
ROUTE_GROUP = 1024
SCATTER_TILE = 2048
FINAL_TILE = 1024
ISSUE_UNROLL = 128
MOE_BLOCK = 32
MOE_GROUP = 16
VMEM_LIMIT = 56 * 1024 * 1024

C_Q, C_K, C_V, C_R = 0, 256, 512, 1024
C_QLAT, C_KVLAT, C_KROPE, C_A, C_END = 1536, 1792, 1920, 2048, 2176

_TILE_POS = np.arange(GLA_TILE)
_CHUNK_PREFIX = ((_TILE_POS[:, None] // CHUNK == _TILE_POS[None, :] // CHUNK)
                 & (_TILE_POS[None, :] <= _TILE_POS[:, None])).astype(np.float32)
_PAIR_LO = np.array([lo for lo in range(8) for hi in range(lo + 1, 8)], np.int32)
_PAIR_HI = np.array([hi for lo in range(8) for hi in range(lo + 1, 8)], np.int32)


def _dot(a, b):
    return jnp.dot(a, b, preferred_element_type=f32)


def _dot_nt(a, b):
    return lax.dot_general(a, b, (((1,), (1,)), ((), ())), preferred_element_type=f32)


def _dot_tn(a, b):
    return lax.dot_general(a, b, (((0,), (0,)), ((), ())), preferred_element_type=f32)


def _rms(x, gain):
    return x * lax.rsqrt(jnp.mean(x * x, axis=-1, keepdims=True) + EPS) * gain


def _split3(x):
    hi = x.astype(bf16)
    r1 = x - hi.astype(f32)
    mid = r1.astype(bf16)
    lo = (r1 - mid.astype(f32)).astype(bf16)
    return hi, mid, lo


def _prep_kernel(x_ref, g_ref, win_ref, qn_ref, wqb_ref, kvn_ref, wkvb_ref, ct_ref, st_ref,
                 qg_ref, kg_ref, vg_ref, rg_ref, a_ref, qm_ref, km_ref, vmt_ref):
    u = _rms(x_ref[...], g_ref[...]).astype(bf16)

    def proj(lo, hi):
        return _dot(u, win_ref[:, lo:hi])

    qg_ref[...] = proj(C_Q, C_K).astype(bf16)
    kg_ref[...] = proj(C_K, C_V).astype(bf16)
    vg_ref[...] = proj(C_V, C_R).astype(bf16)
    rg_ref[...] = proj(C_R, C_QLAT).astype(bf16)
    z = proj(C_QLAT, C_END)
    a_ref[...] = z[:, C_A - C_QLAT:].astype(bf16)
    ctab = ct_ref[...]
    stab = st_ref[...]

    def rope(seg):
        return seg * ctab + pltpu.roll(seg, 64, axis=1) * stab

    k_rope = rope(z[:, C_KROPE - C_QLAT:C_A - C_QLAT]).astype(bf16)
    qn = _rms(z[:, 0:MLA_Q_RANK], qn_ref[...]).astype(bf16)
    kvn = _rms(z[:, MLA_Q_RANK:MLA_Q_RANK + MLA_KV_RANK], kvn_ref[...]).astype(bf16)
    scale = (MLA_NOPE + MLA_ROPE) ** -0.5 * LOG2_E
    qf = _dot(qn, wqb_ref[...])
    kvf = _dot(kvn, wkvb_ref[...])
    for h in range(MLA_HEADS):
        c = h * MLA_QK_PAD
        qm_ref[:, c:c + LANE] = (qf[:, c:c + LANE] * scale).astype(bf16)
        qm_ref[:, c + LANE:c + 2 * LANE] = (rope(qf[:, c + LANE:c + 2 * LANE]) * scale).astype(bf16)
        km_ref[:, h * LANE:(h + 1) * LANE] = kvf[:, h * LANE:(h + 1) * LANE].astype(bf16)
    km_ref[:, MLA_HEADS * MLA_NOPE:MLA_K_W] = k_rope
    vt = kvf[:, MLA_HEADS * MLA_NOPE:].T
    for h in range(MLA_HEADS):
        vmt_ref[h * MLA_VA:h * MLA_VA + MLA_V, :] = vt[h * MLA_V:(h + 1) * MLA_V].astype(bf16)
        vmt_ref[h * MLA_VA + MLA_V:(h + 1) * MLA_VA, :] = jnp.ones((MLA_VA - MLA_V, vt.shape[1]), bf16)


def _prep_call(x2d, rows_per_seq, tile, gain, w_in_r, q_norm, w_qb_r, kv_norm, w_kvb_r, ctab, stab):
    t = x2d.shape[0]
    nj = rows_per_seq // tile
    grid = (t // rows_per_seq, nj)

    def row(b, j):
        return (b * nj + j, 0)

    def const(b, j):
        return (0, 0)

    def tab(b, j):
        return (j, 0)

    widths = (GLA_QK, GLA_QK, GLA_VW, GLA_VW, LANE, MLA_HEADS * MLA_QK_PAD, MLA_K_W)
    return pl.pallas_call(
        _prep_kernel,
        grid=grid,
        in_specs=[
            pl.BlockSpec((tile, D_MODEL), row),
            pl.BlockSpec((1, D_MODEL), const),
            pl.BlockSpec((D_MODEL, C_END), const),
            pl.BlockSpec((1, MLA_Q_RANK), const),
            pl.BlockSpec((MLA_Q_RANK, MLA_HEADS * MLA_QK_PAD), const),
            pl.BlockSpec((1, MLA_KV_RANK), const),
            pl.BlockSpec((MLA_KV_RANK, 2 * MLA_OUT), const),
            pl.BlockSpec((tile, LANE), tab),
            pl.BlockSpec((tile, LANE), tab),
        ],
        out_specs=[pl.BlockSpec((tile, w), row) for w in widths]
        + [pl.BlockSpec((None, MLA_HEADS * MLA_VA, tile), lambda b, j: (b * nj + j, 0, 0))],
        out_shape=[jax.ShapeDtypeStruct((t, w), bf16) for w in widths]
        + [jax.ShapeDtypeStruct((t // tile, MLA_HEADS * MLA_VA, tile), bf16)],
        compiler_params=pltpu.CompilerParams(
            dimension_semantics=("parallel", "parallel"), vmem_limit_bytes=VMEM_LIMIT),
        name="prep",
    )(x2d, gain, w_in_r, q_norm, w_qb_r, kv_norm, w_kvb_r, ctab, stab)


def _gla_log_decay(a, wa2_ref, ba_ref):
    s = _dot(a, wa2_ref[...]) + ba_ref[...]
    return (jnp.minimum(s, 0.0) - jnp.log(1.0 + jnp.exp(-jnp.abs(s)))) * (1.0 / GLA_TAU)


def _gla_front(q, k, v, la, tri, want_out):
    nc = la.shape[0] // CHUNK
    hi, mid, lo = _split3(la)
    b = _dot(tri, hi) + _dot(tri, mid) + _dot(tri, lo)
    b_last = [b[(c + 1) * CHUNK - 1:(c + 1) * CHUNK, :] for c in range(nc)]
    b_last_full = jnp.concatenate([jnp.broadcast_to(bl, (CHUNK, GLA_QK)) for bl in b_last], axis=0)
    kf = k.astype(f32)
    front = dict(v=v, b_last=b_last, kd=(kf * jnp.exp(b_last_full - b)).astype(bf16))
    if want_out:
        front.update(qe=(q.astype(f32) * (GLA_DK ** -0.5) * jnp.exp(b)).astype(bf16),
                     ke=kf * jnp.exp(-b), vf=v.astype(f32))
    return front


def _gla_chunks(front, st_ref, want_out):
    v, kd, b_last = front["v"], front["kd"], front["b_last"]
    rr = lax.broadcasted_iota(i32, (GLA_VW, GLA_QK), 0) // GLA_DV
    cc = lax.broadcasted_iota(i32, (GLA_VW, GLA_QK), 1) // GLA_DK
    if want_out:
        qe, ke, vf = front["qe"], front["ke"], front["vf"]
        lane_h = lax.broadcasted_iota(i32, (CHUNK, GLA_QK), 1) // GLA_DK
        vlane_h = lax.broadcasted_iota(i32, (CHUNK, GLA_VW), 1) // GLA_DV
        a_row = lax.broadcasted_iota(i32, (CHUNK, GLA_QK), 0)
        a_col = lax.broadcasted_iota(i32, (CHUNK, GLA_QK), 1) % CHUNK
    outs = []
    st = st_ref[...]
    for c in range(len(b_last)):
        rows = slice(c * CHUNK, (c + 1) * CHUNK)
        upd = jnp.where(rr == cc, _dot_tn(v[rows], kd[rows]), 0.0)
        if want_out:
            kbd = jnp.concatenate(
                [jnp.where(lane_h == h, ke[rows], 0.0) for h in range(GLA_HEADS)], axis=0).astype(bf16)
            att = jnp.where(a_col <= a_row, _dot_nt(qe[rows], kbd), 0.0).astype(bf16)
            vbd = jnp.concatenate(
                [jnp.where(vlane_h == h, vf[rows], 0.0) for h in range(GLA_HEADS)], axis=0).astype(bf16)
            outs.append(_dot(att, vbd) + _dot_nt(qe[rows], st.astype(bf16)))
        st = st * jnp.exp(b_last[c]) + upd
    st_ref[...] = st
    return jnp.concatenate(outs, axis=0) if want_out else None


def _gla_kernel(q_ref, k_ref, v_ref, r_ref, a_ref, km_ref, vm_ref, am_ref, wa2_ref, ba_ref, gain_ref, tri_ref,
                y_ref, st_ref):
    j = pl.program_id(1)

    @pl.when(j == 0)
    def _():
        st_ref[...] = jnp.zeros_like(st_ref)
        la = _gla_log_decay(am_ref[...], wa2_ref, ba_ref)
        row = lax.broadcasted_iota(i32, la.shape, 0)
        la = jnp.where(row >= CHUNK - N_META, la, 0.0)
        front = _gla_front(None, km_ref[...], vm_ref[...], la, tri_ref[0:CHUNK, 0:CHUNK], False)
        _gla_chunks(front, st_ref.at[0], False)
        for bb in range(1, GLA_BATCH):
            st_ref[bb] = st_ref[0]

    fronts = [_gla_front(q_ref[bb], k_ref[bb], v_ref[bb], _gla_log_decay(a_ref[bb], wa2_ref, ba_ref),
                         tri_ref[...], True) for bb in range(GLA_BATCH)]
    for bb in range(GLA_BATCH):
        o = _gla_chunks(fronts[bb], st_ref.at[bb], True)
        r = r_ref[bb].astype(f32)
        outs = []
        for h in range(GLA_HEADS):
            oh = o[:, h * GLA_DV:(h + 1) * GLA_DV]
            outs.append(oh * lax.rsqrt(jnp.mean(oh * oh, axis=-1, keepdims=True) + EPS))
        on = jnp.concatenate(outs, axis=1) * gain_ref[...]
        y_ref[bb] = (on * (r * jax.nn.sigmoid(r))).astype(bf16)


def _gla_call(qg, kg, vg, rg, ag, km, vm, am, wa2_p, b_a, gain, batch, seq):
    nj = seq // GLA_TILE

    def row(b, j):
        return (b, j, 0)

    def const(b, j):
        return (0, 0)

    def seqs(a):
        return a.reshape(batch, seq, a.shape[-1])

    out = pl.pallas_call(
        _gla_kernel,
        grid=(batch // GLA_BATCH, nj),
        in_specs=[
            pl.BlockSpec((GLA_BATCH, GLA_TILE, GLA_QK), row),
            pl.BlockSpec((GLA_BATCH, GLA_TILE, GLA_QK), row),
            pl.BlockSpec((GLA_BATCH, GLA_TILE, GLA_VW), row),
            pl.BlockSpec((GLA_BATCH, GLA_TILE, GLA_VW), row),
            pl.BlockSpec((GLA_BATCH, GLA_TILE, LANE), row),
            pl.BlockSpec((CHUNK, GLA_QK), const),
            pl.BlockSpec((CHUNK, GLA_VW), const),
            pl.BlockSpec((CHUNK, LANE), const),
            pl.BlockSpec((LANE, GLA_QK), const),
            pl.BlockSpec((1, GLA_QK), const),
            pl.BlockSpec((1, GLA_VW), const),
            pl.BlockSpec((GLA_TILE, GLA_TILE), const),
        ],
        out_specs=pl.BlockSpec((GLA_BATCH, GLA_TILE, GLA_VW), row),
        out_shape=jax.ShapeDtypeStruct((batch, seq, GLA_VW), bf16),
        scratch_shapes=[pltpu.VMEM((GLA_BATCH, GLA_VW, GLA_QK), f32)],
        compiler_params=pltpu.CompilerParams(
            dimension_semantics=("parallel", "arbitrary"), vmem_limit_bytes=VMEM_LIMIT),
        name="gla",
    )(seqs(qg), seqs(kg), seqs(vg), seqs(rg), seqs(ag), km, vm, am, wa2_p, b_a, gain,
      jnp.asarray(_CHUNK_PREFIX, bf16))
    return out.reshape(batch * seq, GLA_VW)


def _mla_kernel(q_ref, k_ref, vt_ref, km_ref, vmt_ref, o_ref, sa_ref, sb_ref):
    i = pl.program_id(2)
    tq = ATT_TILE
    w = MLA_QK_PAD
    va = MLA_VA
    heads = range(ATT_HEADS)

    def keys(ref, rows, h):
        return jnp.concatenate([ref[rows, h * MLA_NOPE:(h + 1) * MLA_NOPE], ref[rows, MLA_HEADS * MLA_NOPE:MLA_K_W]],
                               axis=1)

    def scores(h, blk):
        rows = pl.ds(pl.multiple_of(blk * tq, tq), tq)
        return _dot_nt(keys(k_ref, rows, h), q_ref[:, h * w:(h + 1) * w])

    def soft(s, vtb, carry, mask=None):
        m, acc = carry
        if mask is not None:
            s = jnp.where(mask, s, -1e30)
        m_new = jnp.maximum(m, jnp.max(s, axis=0, keepdims=True))
        p = jnp.exp2(s - m_new).astype(bf16)
        return m_new, jnp.exp2(m - m_new) * acc + _dot(vtb, p)

    def vt(h, blk):
        return vt_ref[blk, h * va:(h + 1) * va, :]

    def finish(carries):
        ss = [_dot_nt(keys(km_ref, slice(None), h), q_ref[:, h * w:(h + 1) * w]) for h in heads]
        accs = [soft(ss[h], vmt_ref[h * va:(h + 1) * va, :], carries[h])[1] for h in heads]
        for h in heads:
            acc = accs[h]
            o_ref[:, h * MLA_V:(h + 1) * MLA_V] = (acc[:MLA_V] * (1.0 / acc[MLA_V:MLA_V + 1])).T.astype(bf16)

    kc = lax.broadcasted_iota(i32, (tq, tq), 0) // CHUNK
    qc = lax.broadcasted_iota(i32, (tq, tq), 1) // CHUNK
    mask = kc <= qc

    for h in heads:
        sa_ref[h] = scores(h, 0)

    def pair(p, carries):
        b0 = 2 * p
        for h in heads:
            sb_ref[h] = scores(h, b0 + 1)
        carries = [soft(sa_ref[h], vt(h, b0), carries[h]) for h in heads]
        for h in heads:
            sa_ref[h] = scores(h, b0 + 2)
        return tuple(soft(sb_ref[h], vt(h, b0 + 1), carries[h]) for h in heads)

    init = tuple((jnp.full((1, tq), -1e30, f32), jnp.zeros((va, tq), f32)) for _ in heads)
    carries = lax.fori_loop(0, i // 2, pair, init)

    @pl.when(i % 2 == 1)
    def _():
        for h in heads:
            sb_ref[h] = scores(h, i)
        c1 = [soft(sa_ref[h], vt(h, i - 1), carries[h]) for h in heads]
        finish([soft(sb_ref[h], vt(h, i), c1[h], mask) for h in heads])

    @pl.when(i % 2 == 0)
    def _():
        finish([soft(sa_ref[h], vt(h, i), carries[h], mask) for h in heads])


def _mla_call(qm, km, vmt, km_meta, vmt_meta, batch, seq):
    nq = seq // ATT_TILE
    nh = ATT_HEADS
    qm3 = qm.reshape(batch, seq, MLA_HEADS * MLA_QK_PAD)
    km3 = km.reshape(batch, seq, MLA_K_W)
    assert nh == MLA_HEADS
    vt4 = vmt.reshape(batch, nq, MLA_HEADS * MLA_VA, ATT_TILE)
    out = pl.pallas_call(
        _mla_kernel,
        grid=(batch, MLA_HEADS // nh, nq),
        in_specs=[
            pl.BlockSpec((None, ATT_TILE, nh * MLA_QK_PAD), lambda b, h, i: (b, i, h)),
            pl.BlockSpec((None, seq, MLA_K_W), lambda b, h, i: (b, 0, 0)),
            pl.BlockSpec((None, nq, nh * MLA_VA, ATT_TILE), lambda b, h, i: (b, 0, h, 0)),
            pl.BlockSpec((N_META, MLA_K_W), lambda b, h, i: (0, 0)),
            pl.BlockSpec((nh * MLA_VA, N_META), lambda b, h, i: (h, 0)),
        ],
        out_specs=pl.BlockSpec((None, ATT_TILE, nh * MLA_V), lambda b, h, i: (b, i, h)),
        out_shape=jax.ShapeDtypeStruct((batch, seq, MLA_OUT), bf16),
        scratch_shapes=[pltpu.VMEM((nh, ATT_TILE, ATT_TILE), f32), pltpu.VMEM((nh, ATT_TILE, ATT_TILE), f32)],
        compiler_params=pltpu.CompilerParams(
            dimension_semantics=("parallel", "parallel", "arbitrary"), vmem_limit_bytes=VMEM_LIMIT),
        name="mla",
    )(qm3, km3, vt4, km_meta, vmt_meta)
    return out.reshape(batch * seq, MLA_OUT)


def _route_cols(lt):
    r = lt.shape[1]
    neg = -1e30
    gl = lt[0:N_GROUPS, :]
    gsub = lax.broadcasted_iota(i32, (N_GROUPS, r), 0)
    gmax = jnp.max(gl, axis=0, keepdims=True)
    g_p = 1.0 / jnp.sum(jnp.exp(gl - gmax), axis=0, keepdims=True)
    g_idx = jnp.min(jnp.where(gl == gmax, gsub, N_GROUPS), axis=0, keepdims=True)
    el_all = lt[N_GROUPS:N_GROUPS + N_EXPERTS, :]
    esub = lax.broadcasted_iota(i32, (N_EXPERTS, r), 0)
    base = g_idx * EXPERTS_PER_GROUP
    e_mask = (esub >= base) & (esub < base + EXPERTS_PER_GROUP)
    el = jnp.where(e_mask, el_all, neg)
    m1 = jnp.max(el, axis=0, keepdims=True)
    i1 = jnp.min(jnp.where(e_mask & (el == m1), esub, N_EXPERTS), axis=0, keepdims=True)
    el2 = jnp.where(esub == i1, neg, el)
    m2 = jnp.max(el2, axis=0, keepdims=True)
    i2 = jnp.min(jnp.where(e_mask & (esub != i1) & (el2 == m2), esub, N_EXPERTS), axis=0, keepdims=True)
    rr = jnp.exp(m2 - m1)
    ga = g_p / (1.0 + rr)
    gb = g_p * rr / (1.0 + rr)
    la_ = i1 - base
    lb_ = i2 - base
    lo = jnp.minimum(la_, lb_)
    hi = jnp.maximum(la_, lb_)
    g_lo = jnp.where(la_ < lb_, ga, gb)
    g_hi = jnp.where(la_ < lb_, gb, ga)
    pidx = ((lo * (2 * EXPERTS_PER_GROUP - 1 - lo)) >> 1) + (hi - lo - 1)
    bucket = g_idx * N_PAIRS + pidx
    bsub = lax.broadcasted_iota(i32, (BUCKET_LANES, r), 0)
    oht = jnp.where(bsub == bucket, 1.0, 0.0)
    ohb = oht.astype(bf16)
    ri = lax.broadcasted_iota(i32, (ROUTE_ROWS, ROUTE_ROWS), 0)
    ci = lax.broadcasted_iota(i32, (ROUTE_ROWS, ROUTE_ROWS), 1)
    before = jnp.where(ri < ci, 1.0, 0.0).astype(bf16)
    ones = jnp.ones((SUBLANES, ROUTE_ROWS), bf16)
    subs = [slice(i * ROUTE_ROWS, (i + 1) * ROUTE_ROWS) for i in range(r // ROUTE_ROWS)]
    cum = jnp.concatenate([_dot(ohb[:, sl], before) for sl in subs], axis=1)
    rank = jnp.sum(oht * cum, axis=0, keepdims=True)
    counts = [_dot_nt(ones, ohb[:, sl])[0:1, :] for sl in subs]
    msub = lax.broadcasted_iota(i32, (LANE, r), 0)
    meta_t = jnp.where(msub == 0, bucket.astype(f32),
                       jnp.where(msub == 1, rank,
                                 jnp.where(msub == 2, g_lo, jnp.where(msub == 3, g_hi, 0.0))))
    return meta_t.T, counts, meta_t[0:SUBLANES, :]


def _outproj_kernel(x_ref, yg_ref, ym_ref, wog_ref, wom_ref, gain_ref, wrt_ref, rb_ref,
                    ux_ref, cnt_ref, rt_ref):
    per_group = ROUTE_GROUP // ROUTE_ROWS
    for grp in range(OUT_TILE // ROUTE_GROUP):
        rows = slice(grp * ROUTE_GROUP, (grp + 1) * ROUTE_GROUP)
        h1 = x_ref[rows, :] + _dot(yg_ref[rows, :], wog_ref[...]) + _dot(ym_ref[rows, :], wom_ref[...])
        ux_ref[rows, 0:D_MODEL] = h1
        u2 = _rms(h1, gain_ref[...])
        lt = _dot_nt(wrt_ref[...], u2.astype(bf16)) + rb_ref[...]
        meta, counts, routes = _route_cols(lt)
        ux_ref[rows, D_MODEL:ROW_W] = meta
        for i in range(per_group):
            cnt_ref[grp * per_group + i] = counts[i]
            rt_ref[grp * per_group + i] = routes[:, i * ROUTE_ROWS:(i + 1) * ROUTE_ROWS]


def _outproj_call(x2d, yg, ym, wo_g, wo_m, gain, w_r, rbias):
    t = x2d.shape[0]
    nt = t // OUT_TILE

    def row(i):
        return (i, 0)

    def const(i):
        return (0, 0)

    return pl.pallas_call(
        _outproj_kernel,
        grid=(nt,),
        in_specs=[
            pl.BlockSpec((OUT_TILE, D_MODEL), row),
            pl.BlockSpec((OUT_TILE, GLA_VW), row),
            pl.BlockSpec((OUT_TILE, MLA_OUT), row),
            pl.BlockSpec((GLA_VW, D_MODEL), const),
            pl.BlockSpec((MLA_OUT, D_MODEL), const),
            pl.BlockSpec((1, D_MODEL), const),
            pl.BlockSpec((LANE, D_MODEL), const),
            pl.BlockSpec((LANE, 1), const),
        ],
        out_specs=[
            pl.BlockSpec((OUT_TILE, ROW_W), row),
            pl.BlockSpec((OUT_TILE // ROUTE_ROWS, 1, BUCKET_LANES), lambda i: (i, 0, 0)),
            pl.BlockSpec((OUT_TILE // ROUTE_ROWS, SUBLANES, ROUTE_ROWS), lambda i: (i, 0, 0)),
        ],
        out_shape=[
            jax.ShapeDtypeStruct((t, ROW_W), f32),
            jax.ShapeDtypeStruct((nt * (OUT_TILE // ROUTE_ROWS), 1, BUCKET_LANES), f32),
            jax.ShapeDtypeStruct((nt * (OUT_TILE // ROUTE_ROWS), SUBLANES, ROUTE_ROWS), f32),
        ],
        compiler_params=pltpu.CompilerParams(
            dimension_semantics=("parallel",), vmem_limit_bytes=VMEM_LIMIT),
        name="outproj",
    )(x2d, yg, ym, wo_g, wo_m, gain, w_r, rbias)


def _scatter_kernel(pos_ref, zb_ref, ux_ref, hs_ref, zbuf, sem, zsem):
    @pl.when(pl.program_id(0) == 0)
    def _():
        zbuf[...] = jnp.zeros_like(zbuf)

        def zero_copy(j):
            rows = pl.ds(pl.multiple_of(zb_ref[j] * MOE_BLOCK, MOE_BLOCK), MOE_BLOCK)
            return pltpu.make_async_copy(zbuf, hs_ref.at[rows], zsem)

        def zstart(j, c):
            @pl.when(zb_ref[j] >= 0)
            def _():
                zero_copy(j).start()
            return c

        def zwait(j, c):
            @pl.when(zb_ref[j] >= 0)
            def _():
                zero_copy(j).wait()
            return c

        lax.fori_loop(0, zb_ref.shape[0], zstart, 0)
        lax.fori_loop(0, zb_ref.shape[0], zwait, 0)

    def start(io, c):
        for r in range(ISSUE_UNROLL):
            ii = io * (ISSUE_UNROLL // SUBLANES) + r // SUBLANES
            pltpu.make_async_copy(ux_ref.at[ii, pl.ds(r % SUBLANES, 1)],
                                  hs_ref.at[pl.ds(pos_ref[io * ISSUE_UNROLL + r], 1)], sem).start(priority=r % 2)
        return c

    lax.fori_loop(0, SCATTER_TILE // ISSUE_UNROLL, start, 0)
    pltpu.make_async_copy(hs_ref.at[pl.ds(0, SCATTER_TILE)], hs_ref.at[pl.ds(0, SCATTER_TILE)], sem).wait()


def _scatter_call(pos, zero_blocks, ux, n_slots):
    t = ux.shape[0]
    nz = zero_blocks.shape[0]
    return pl.pallas_call(
        _scatter_kernel,
        grid=(t // SCATTER_TILE,),
        in_specs=[
            pl.BlockSpec((SCATTER_TILE,), lambda i: (i,), memory_space=pltpu.SMEM),
            pl.BlockSpec((nz,), lambda i: (0,), memory_space=pltpu.SMEM),
            pl.BlockSpec((SCATTER_TILE // SUBLANES, SUBLANES, ROW_W), lambda i: (i, 0, 0)),
        ],
        out_specs=pl.BlockSpec(memory_space=pl.ANY),
        out_shape=jax.ShapeDtypeStruct((n_slots, ROW_W), f32),
        scratch_shapes=[pltpu.VMEM((MOE_BLOCK, ROW_W), f32), pltpu.SemaphoreType.DMA(()),
                        pltpu.SemaphoreType.DMA(())],
        compiler_params=pltpu.CompilerParams(
            dimension_semantics=("arbitrary",), vmem_limit_bytes=VMEM_LIMIT),
        name="scatter",
    )(pos, zero_blocks, ux.reshape(t // SUBLANES, SUBLANES, ROW_W))


def _moe_kernel(se_ref, sf_ref, sk_ref, sp_ref, sn_ref, sb_ref, sr_ref, si_ref, so_ref,
                hs_hbm, wg_hbm, wu_hbm, wd_hbm, fg_ref, y_hbm,
                xbuf, obuf, wg_buf, wu_buf, wd_buf, wgu_s, wd_s, in_sem, out_sem, w_sem):
    s = pl.program_id(0)
    ns = pl.num_programs(0)
    cur = s % 2
    g_n = MOE_GROUP

    def in_copy(step, g, buf):
        rows = pl.ds(pl.multiple_of(sb_ref[step * g_n + g] * MOE_BLOCK, MOE_BLOCK), MOE_BLOCK)
        return pltpu.make_async_copy(
            hs_hbm.at[rows], xbuf.at[buf, pl.ds(g * MOE_BLOCK, MOE_BLOCK)], in_sem.at[buf])

    def out_copy(step, g, buf):
        rows = pl.ds(pl.multiple_of(sb_ref[step * g_n + g] * MOE_BLOCK, MOE_BLOCK), MOE_BLOCK)
        cols = pl.ds(pl.multiple_of(sr_ref[step * g_n + g] * D_MODEL, D_MODEL), D_MODEL)
        return pltpu.make_async_copy(
            obuf.at[buf, pl.ds(g * MOE_BLOCK, MOE_BLOCK)], y_hbm.at[rows, cols], out_sem.at[buf])

    def for_slots(step, flags_ref, fn):
        for g in range(g_n):
            @pl.when(flags_ref[step * g_n + g] == 1)
            def _():
                fn(g)

    @pl.when(s == 0)
    def _():
        xbuf[...] = jnp.zeros_like(xbuf)
        for_slots(0, si_ref, lambda g: in_copy(0, g, 0).start())

    @pl.when(s + 1 < ns)
    def _():
        for_slots(s + 1, si_ref, lambda g: in_copy(s + 1, g, 1 - cur).start())

    for_slots(s, si_ref, lambda g: in_copy(s, g, cur).wait())

    @pl.when(s >= 2)
    def _():
        for_slots(s - 2, so_ref, lambda g: out_copy(s - 2, g, cur).wait())

    def w_copies(expert, slot):
        return (pltpu.make_async_copy(wg_hbm.at[expert], wg_buf.at[slot], w_sem.at[slot]),
                pltpu.make_async_copy(wu_hbm.at[expert], wu_buf.at[slot], w_sem.at[slot]),
                pltpu.make_async_copy(wd_hbm.at[expert], wd_buf.at[slot], w_sem.at[slot]))

    @pl.when(s == 0)
    def _():
        for c in w_copies(se_ref[0], sp_ref[0]):
            c.start()

    @pl.when(sf_ref[s] == 1)
    def _():
        slot = sp_ref[s]
        for c in w_copies(se_ref[s], slot):
            c.wait()
        wgu_s[:, 0:D_EXPERT] = wg_buf[slot].astype(bf16)
        wgu_s[:, D_EXPERT:2 * D_EXPERT] = wu_buf[slot].astype(bf16)
        wd_s[...] = wd_buf[slot].astype(bf16)

        @pl.when(sn_ref[s] >= 0)
        def _():
            for c in w_copies(sn_ref[s], 1 - slot):
                c.start()

    @pl.when(sk_ref[s] == 1)
    def _():
        h1 = xbuf[cur, :, 0:D_MODEL]
        meta = xbuf[cur, :, D_MODEL:ROW_W]
        u = _rms(h1, fg_ref[...]).astype(bf16)
        ones = jnp.ones((MOE_BLOCK, 1), f32)
        role0 = [sr_ref[s * g_n + g] == 0 for g in range(g_n)]
        gate = jnp.concatenate(
            [jnp.where(role0[g], meta[g * MOE_BLOCK:(g + 1) * MOE_BLOCK, 2:3],
                       meta[g * MOE_BLOCK:(g + 1) * MOE_BLOCK, 3:4]) for g in range(g_n)], axis=0)
        keep = jnp.concatenate([jnp.where(role0[g], ones, 0.0) for g in range(g_n)], axis=0)
        gu = _dot(u, wgu_s[...])
        gt = gu[:, 0:D_EXPERT]
        hdn = (gt * jax.nn.sigmoid(gt) * gu[:, D_EXPERT:]).astype(bf16)
        obuf[cur] = _dot(hdn, wd_s[...]) * gate + h1 * keep

    @pl.when(sk_ref[s] == 0)
    def _():
        obuf[cur] = jnp.zeros(obuf.shape[1:], f32)

    for_slots(s, so_ref, lambda g: out_copy(s, g, cur).start())

    @pl.when(s == ns - 1)
    def _():
        for_slots(s, so_ref, lambda g: out_copy(s, g, cur).wait())

        @pl.when(s >= 1)
        def _():
            for_slots(s - 1, so_ref, lambda g: out_copy(s - 1, g, 1 - cur).wait())


def _moe_call(plan, hs, w_gate, w_up, w_down, ffn_gain):
    n_steps = plan[0].shape[0]
    n_slots = hs.shape[0]
    rows = MOE_GROUP * MOE_BLOCK

    grid_spec = pltpu.PrefetchScalarGridSpec(
        num_scalar_prefetch=9,
        grid=(n_steps,),
        in_specs=[
            pl.BlockSpec(memory_space=pl.ANY),
            pl.BlockSpec(memory_space=pl.ANY),
            pl.BlockSpec(memory_space=pl.ANY),
            pl.BlockSpec(memory_space=pl.ANY),
            pl.BlockSpec((1, D_MODEL), lambda s, *_: (0, 0)),
        ],
        out_specs=pl.BlockSpec(memory_space=pl.ANY),
        scratch_shapes=[
            pltpu.VMEM((2, rows, ROW_W), f32),
            pltpu.VMEM((2, rows, D_MODEL), f32),
            pltpu.VMEM((2, D_MODEL, D_EXPERT), f32),
            pltpu.VMEM((2, D_MODEL, D_EXPERT), f32),
            pltpu.VMEM((2, D_EXPERT, D_MODEL), f32),
            pltpu.VMEM((D_MODEL, 2 * D_EXPERT), bf16),
            pltpu.VMEM((D_EXPERT, D_MODEL), bf16),
            pltpu.SemaphoreType.DMA((2,)),
            pltpu.SemaphoreType.DMA((2,)),
            pltpu.SemaphoreType.DMA((2,)),
        ],
    )
    return pl.pallas_call(
        _moe_kernel,
        grid_spec=grid_spec,
        out_shape=jax.ShapeDtypeStruct((n_slots, 2 * D_MODEL), f32),
        compiler_params=pltpu.CompilerParams(
            dimension_semantics=("arbitrary",), vmem_limit_bytes=VMEM_LIMIT),
        name="moe",
    )(*plan, hs, w_gate, w_up, w_down, ffn_gain)


def _final_kernel(posc_ref, posn_ref, gain_ref, y_hbm, o_ref, ybuf, sem):
    i = pl.program_id(0)
    cur = i % 2

    def issue(pos_ref, buf):
        def start(io, c):
            for r in range(ISSUE_UNROLL):
                ii = io * (ISSUE_UNROLL // SUBLANES) + r // SUBLANES
                pltpu.make_async_copy(y_hbm.at[pl.ds(pos_ref[io * ISSUE_UNROLL + r], 1)],
                                      ybuf.at[buf, ii, pl.ds(r % SUBLANES, 1)], sem.at[buf]).start(priority=r % 2)
            return c

        lax.fori_loop(0, FINAL_TILE // ISSUE_UNROLL, start, 0)

    @pl.when(i == 0)
    def _():
        issue(posc_ref, 0)

    @pl.when(i + 1 < pl.num_programs(0))
    def _():
        issue(posn_ref, 1 - cur)

    pltpu.make_async_copy(ybuf.at[cur], ybuf.at[cur], sem.at[cur]).wait()
    h = ybuf[cur, :, :, 0:D_MODEL] + ybuf[cur, :, :, D_MODEL:2 * D_MODEL]
    o_ref[...] = _rms(h, gain_ref[...])


def _final_call(pos, gain, y):
    t = pos.shape[0]
    n = t // FINAL_TILE
    rows = FINAL_TILE // SUBLANES
    out = pl.pallas_call(
        _final_kernel,
        grid=(n,),
        in_specs=[
            pl.BlockSpec((FINAL_TILE,), lambda i: (i,), memory_space=pltpu.SMEM),
            pl.BlockSpec((FINAL_TILE,), lambda i: (jnp.minimum(i + 1, n - 1),), memory_space=pltpu.SMEM),
            pl.BlockSpec((1, 1, D_MODEL), lambda i: (0, 0, 0)),
            pl.BlockSpec(memory_space=pl.ANY),
        ],
        out_specs=pl.BlockSpec((rows, SUBLANES, D_MODEL), lambda i: (i, 0, 0)),
        out_shape=jax.ShapeDtypeStruct((t // SUBLANES, SUBLANES, D_MODEL), f32),
        scratch_shapes=[pltpu.VMEM((2, rows, SUBLANES, 2 * D_MODEL), f32), pltpu.SemaphoreType.DMA((2,))],
        compiler_params=pltpu.CompilerParams(
            dimension_semantics=("arbitrary",), vmem_limit_bytes=VMEM_LIMIT),
        name="final",
    )(pos, pos, gain.reshape(1, 1, D_MODEL), y)
    return out.reshape(t, D_MODEL)


def _rope_tables(pos):
    inv = ROPE_BASE ** (-jnp.arange(0, MLA_ROPE, 2, dtype=f32) / MLA_ROPE)
    ang = pos.astype(f32)[:, None] * inv[None, :]
    cos, sin = jnp.cos(ang), jnp.sin(ang)
    z = jnp.zeros((pos.shape[0], LANE - MLA_ROPE), f32)
    return jnp.concatenate([cos, cos, z], axis=1), jnp.concatenate([-sin, sin, z], axis=1)


def _relayout_weights(w_in, w_qb, w_kvb):
    half = MLA_ROPE // 2
    perm = (np.arange(MLA_ROPE) + half) % MLA_ROPE
    pts = np.cumsum((GLA_QK, GLA_QK, GLA_VW, GLA_VW, GLA_GATE_RANK, MLA_Q_RANK, MLA_KV_RANK, MLA_ROPE))
    q_g, k_g, v_g, r_g, a_l, q_lat, kv_lat, k_rope = jnp.split(w_in, pts[:-1], axis=1)
    a_seg = jnp.pad(a_l, ((0, 0), (0, LANE - GLA_GATE_RANK)))
    w_in_r = jnp.concatenate(
        [q_g, k_g, v_g, r_g, q_lat, kv_lat, k_rope, k_rope[:, perm], a_seg], axis=1).astype(bf16)
    qcols, kcols, vcols = [], [], []
    for h in range(MLA_HEADS):
        c = h * (MLA_NOPE + MLA_ROPE)
        rope = w_qb[:, c + MLA_NOPE:c + MLA_NOPE + MLA_ROPE]
        qcols += [w_qb[:, c:c + MLA_NOPE], rope, rope[:, perm]]
        c2 = h * (MLA_NOPE + MLA_V)
        kcols.append(w_kvb[:, c2:c2 + MLA_NOPE])
        vcols.append(w_kvb[:, c2 + MLA_NOPE:c2 + MLA_NOPE + MLA_V])
    return w_in_r, jnp.concatenate(qcols, axis=1).astype(bf16), jnp.concatenate(kcols + vcols, axis=1).astype(bf16)


_BUCKET_GROUP = np.arange(N_BUCKETS) // N_PAIRS
_RUN_EXPERT = np.concatenate([_BUCKET_GROUP * EXPERTS_PER_GROUP + _PAIR_LO[np.arange(N_BUCKETS) % N_PAIRS],
                              _BUCKET_GROUP * EXPERTS_PER_GROUP + _PAIR_HI[np.arange(N_BUCKETS) % N_PAIRS]])
_RUN_IS_EXPERT = (_RUN_EXPERT[:, None] == np.arange(N_EXPERTS)[None, :]).astype(np.int32)
_RUN_BEFORE = ((_RUN_EXPERT[:, None] == _RUN_EXPERT[None, :])
               & (np.arange(2 * N_BUCKETS)[None, :] < np.arange(2 * N_BUCKETS)[:, None])).astype(np.int32)


def _route_plan(counts, bucket, rank, n_tok):
    nt = counts.shape[0]
    g_n = MOE_GROUP
    tot = counts.sum(axis=0)
    nblk = (tot + MOE_BLOCK - 1) // MOE_BLOCK
    bstart_blk = jnp.cumsum(nblk) - nblk
    n_blocks = jnp.sum(nblk)
    tile_base = bstart_blk[None, :] * MOE_BLOCK + jnp.cumsum(counts, axis=0) - counts
    hit = bucket.reshape(nt, -1, 1) == jnp.arange(N_BUCKETS, dtype=i32)
    pos = jnp.sum(jnp.where(hit, tile_base[:, None, :], 0), axis=-1).reshape(-1) + rank
    nb_max = (n_tok + N_BUCKETS * (MOE_BLOCK - 1)) // MOE_BLOCK

    n_run = jnp.concatenate([nblk, nblk])
    b0_run = jnp.concatenate([bstart_blk, bstart_blk])
    c_e = jnp.sum(n_run[:, None] * _RUN_IS_EXPERT, axis=0)
    g_e = (c_e + g_n - 1) // g_n
    gend = jnp.cumsum(g_e)
    gstart = gend - g_e
    n_compute = gend[-1]
    off_run = jnp.sum(_RUN_BEFORE * n_run[None, :], axis=1)
    f_run = jnp.sum(_RUN_IS_EXPERT * gstart[None, :], axis=1) * g_n + off_run

    n_steps = (2 * nb_max + N_EXPERTS * (g_n - 1) + g_n - 1) // g_n + 1
    f = jnp.arange(n_steps * g_n, dtype=i32)
    in_run = (f[:, None] >= f_run[None, :]) & (f[:, None] < (f_run + n_run)[None, :])
    valid_c = jnp.any(in_run, axis=1)
    block_c = jnp.sum(jnp.where(in_run, b0_run[None, :] + f[:, None] - f_run[None, :], 0), axis=1)
    role_c = jnp.sum(jnp.where(in_run[:, N_BUCKETS:], 1, 0), axis=1)
    u_idx = f - n_compute * g_n
    valid_f = (u_idx >= 0) & (u_idx < 2 * (nb_max - n_blocks))
    slot_block = jnp.where(valid_c, block_c, jnp.where(valid_f, n_blocks + u_idx // 2, 0))
    slot_role = jnp.where(valid_c, role_c, jnp.where(valid_f, u_idx % 2, 0))

    step = jnp.arange(n_steps, dtype=i32)
    e_of_step = jnp.minimum(jnp.sum(gend[None, :] <= step[:, None], axis=1), N_EXPERTS - 1)
    is_compute = step < n_compute
    last_e = jnp.max(jnp.where(is_compute, e_of_step, 0))
    step_expert = jnp.where(is_compute, e_of_step, last_e)
    step_first = jnp.concatenate([jnp.ones((1,), bool), step_expert[1:] != step_expert[:-1]])
    ordinal = jnp.cumsum(step_first.astype(i32)) - 1
    ords = jnp.arange(N_EXPERTS + 1, dtype=i32)
    expert_of_ord = jnp.sum(jnp.where(step_first[:, None] & (ordinal[:, None] == ords[None, :]),
                                      step_expert[:, None], 0), axis=0)
    has_next = ordinal + 1 <= ordinal[-1]
    next_expert = jnp.sum(jnp.where(ords[None, :] == ordinal[:, None] + 1, expert_of_ord[None, :], 0), axis=1)
    step_next = jnp.where(step_first & has_next, next_expert, -1)
    plan = tuple(a.astype(i32) for a in
                 (step_expert, step_first, is_compute, ordinal % 2, step_next,
                  slot_block, slot_role, valid_c, valid_c | valid_f))
    last_blk = jnp.where(nblk > 0, bstart_blk + nblk - 1, -1)
    spare = n_blocks + jnp.arange(nb_max - n_tok // MOE_BLOCK, dtype=i32)
    zero_blocks = jnp.concatenate([last_blk, jnp.where(spare < nb_max, spare, -1)]).astype(i32)
    return pos.astype(i32), plan, zero_blocks, nb_max


def kernel(x, meta_tokens, mix_norm, w_in, gla_w_a2, gla_b_a, gla_out_norm, mla_q_norm, mla_w_qb, mla_kv_norm,
           mla_w_kvb, w_out, ffn_norm, router_group_w, router_group_b, router_expert_w, router_expert_b,
           expert_w_gate, expert_w_up, expert_w_down, final_norm):
    batch, seq, d = x.shape
    assert PREP_TILE == ATT_TILE
    assert d == D_MODEL and seq % max(PREP_TILE, GLA_TILE, ATT_TILE) == 0
    assert (batch * seq) % max(OUT_TILE, SCATTER_TILE, FINAL_TILE) == 0 and batch % GLA_BATCH == 0
    n_tok = batch * seq
    x2d = x.reshape(n_tok, d)

    w_in_r, w_qb_r, w_kvb_r = _relayout_weights(w_in[0], mla_w_qb[0], mla_w_kvb[0])
    mixg = mix_norm[0].reshape(1, d)
    qn = mla_q_norm[0].reshape(1, MLA_Q_RANK)
    kvn = mla_kv_norm[0].reshape(1, MLA_KV_RANK)
    ct_m, st_m = _rope_tables(jnp.arange(META_TILE))
    ct_x, st_x = _rope_tables(N_META + jnp.arange(seq))

    x_meta = jnp.pad(meta_tokens.astype(f32), ((0, META_TILE - N_META), (0, 0)))
    _, kg_m, vg_m, _, a_m, _, km_m, vmt_m = _prep_call(
        x_meta, META_TILE, META_TILE, mixg, w_in_r, qn, w_qb_r, kvn, w_kvb_r, ct_m, st_m)
    qg, kg, vg, rg, ag, qm, km, vmt = _prep_call(
        x2d, seq, PREP_TILE, mixg, w_in_r, qn, w_qb_r, kvn, w_kvb_r, ct_x, st_x)

    def chunk0(a):
        return jnp.pad(a[:N_META], ((CHUNK - N_META, 0), (0, 0)))

    wa2_p = jnp.pad(gla_w_a2[0], ((0, LANE - GLA_GATE_RANK), (0, 0))).astype(bf16)
    y_gla = _gla_call(qg, kg, vg, rg, ag, chunk0(kg_m), chunk0(vg_m), chunk0(a_m),
                      wa2_p, gla_b_a[0].reshape(1, GLA_QK), gla_out_norm[0].reshape(1, GLA_VW), batch, seq)
    y_mla = _mla_call(qm, km, vmt, km_m[:N_META], vmt_m[0, :, :N_META], batch, seq)

    wo = w_out[0].astype(bf16)
    rw = jnp.concatenate([router_group_w[0], router_expert_w[0],
                          jnp.zeros((d, LANE - N_GROUPS - N_EXPERTS), f32)], axis=1)
    rb = jnp.concatenate([router_group_b[0], router_expert_b[0],
                          jnp.zeros((LANE - N_GROUPS - N_EXPERTS,), f32)]).reshape(1, LANE)
    ffn_gain = ffn_norm[0].reshape(1, d)
    ux, cnt, routes = _outproj_call(x2d, y_gla, y_mla, wo[:GLA_VW], wo[GLA_VW:], ffn_gain,
                                      rw.T.astype(bf16), rb.reshape(LANE, 1))

    counts = cnt.reshape(-1, BUCKET_LANES)[:, :N_BUCKETS].astype(i32)
    tok_bucket = routes[:, 0, :].reshape(-1).astype(i32)
    tok_rank = routes[:, 1, :].reshape(-1).astype(i32)
    pos, plan, zero_blocks, nb_max = _route_plan(counts, tok_bucket, tok_rank, n_tok)
    n_slots = nb_max * MOE_BLOCK
    hs = _scatter_call(pos, zero_blocks, ux, n_slots)
    y = _moe_call(plan, hs, expert_w_gate[0], expert_w_up[0], expert_w_down[0], ffn_gain)
    out = _final_call(pos, final_norm.reshape(1, d), y)
    return out.reshape(batch, seq, d)
```

```python
import functools

import numpy as np
import jax
import jax.numpy as jnp
from jax import lax
from jax.experimental import pallas as pl
from jax.experimental.pallas import tpu as pltpu

f32 = jnp.float32
bf16 = jnp.bfloat16
i32 = jnp.int32

D_MODEL = 1024
CHUNK = 64
N_META = 16
EPS = 1e-6
GLA_HEADS = 4
GLA_DK = 64
GLA_DV = 128
GLA_GATE_RANK = 16
GLA_TAU = 16.0
GLA_QK = GLA_HEADS * GLA_DK
GLA_VW = GLA_HEADS * GLA_DV
MLA_HEADS = 4
MLA_Q_RANK = 256
MLA_KV_RANK = 128
MLA_NOPE = 128
MLA_ROPE = 64
MLA_V = 128
MLA_OUT = MLA_HEADS * MLA_V
MLA_QK_PAD = 256
MLA_VA = MLA_V + 16
LOG2_E = 1.4426950408889634
ROPE_BASE = 10000.0
N_GROUPS = 8
EXPERTS_PER_GROUP = 8
N_EXPERTS = N_GROUPS * EXPERTS_PER_GROUP
D_EXPERT = 512
N_PAIRS = EXPERTS_PER_GROUP * (EXPERTS_PER_GROUP - 1) // 2
N_BUCKETS = N_GROUPS * N_PAIRS
BUCKET_LANES = 256
LANE = 128
SUBLANES = 8
MLA_K_W = MLA_HEADS * MLA_NOPE + LANE
META_W = LANE
ROW_W = D_MODEL + META_W

PREP_TILE = 512
GLA_TILE = 512
GLA_BATCH = 4
ATT_TILE = 512
ATT_HEADS = 4
META_TILE = 128
OUT_TILE = 1024
ROUTE_ROWS = 256
ROUTE_GROUP = 1024
SCATTER_TILE = 2048
FINAL_TILE = 1024
ISSUE_UNROLL = 128
MOE_BLOCK = 32
MOE_GROUP = 16
VMEM_LIMIT = 56 * 1024 * 1024

C_Q, C_K, C_V, C_R = 0, 256, 512, 1024
C_QLAT, C_KVLAT, C_KROPE, C_A, C_END = 1536, 1792, 1920, 2048, 2176

_TILE_POS = np.arange(GLA_TILE)
_CHUNK_PREFIX = ((_TILE_POS[:, None] // CHUNK == _TILE_POS[None, :] // CHUNK)
                 & (_TILE_POS[None, :] <= _TILE_POS[:, None])).astype(np.float32)
_PAIR_LO = np.array([lo for lo in range(8) for hi in range(lo + 1, 8)], np.int32)
_PAIR_HI = np.array([hi for lo in range(8) for hi in range(lo + 1, 8)], np.int32)


def _dot(a, b):
    return jnp.dot(a, b, preferred_element_type=f32)


def _dot_nt(a, b):
    return lax.dot_general(a, b, (((1,), (1,)), ((), ())), preferred_element_type=f32)


def _dot_tn(a, b):
    return lax.dot_general(a, b, (((0,), (0,)), ((), ())), preferred_element_type=f32)


def _rms(x, gain):
    return x * lax.rsqrt(jnp.mean(x * x, axis=-1, keepdims=True) + EPS) * gain


def _split3(x):
    hi = x.astype(bf16)
    r1 = x - hi.astype(f32)
    mid = r1.astype(bf16)
    lo = (r1 - mid.astype(f32)).astype(bf16)
    return hi, mid, lo


def _prep_kernel(x_ref, g_ref, win_ref, qn_ref, wqb_ref, kvn_ref, wkvb_ref, ct_ref, st_ref,
                 qg_ref, kg_ref, vg_ref, rg_ref, a_ref, qm_ref, km_ref, vmt_ref):
    u = _rms(x_ref[...], g_ref[...]).astype(bf16)

    def proj(lo, hi):
        return _dot(u, win_ref[:, lo:hi])

    qg_ref[...] = proj(C_Q, C_K).astype(bf16)
    kg_ref[...] = proj(C_K, C_V).astype(bf16)
    vg_ref[...] = proj(C_V, C_R).astype(bf16)
    rg_ref[...] = proj(C_R, C_QLAT).astype(bf16)
    z = proj(C_QLAT, C_END)
    a_ref[...] = z[:, C_A - C_QLAT:].astype(bf16)
    ctab = ct_ref[...]
    stab = st_ref[...]

    def rope(seg):
        return seg * ctab + pltpu.roll(seg, 64, axis=1) * stab

    k_rope = rope(z[:, C_KROPE - C_QLAT:C_A - C_QLAT]).astype(bf16)
    qn = _rms(z[:, 0:MLA_Q_RANK], qn_ref[...]).astype(bf16)
    kvn = _rms(z[:, MLA_Q_RANK:MLA_Q_RANK + MLA_KV_RANK], kvn_ref[...]).astype(bf16)
    scale = (MLA_NOPE + MLA_ROPE) ** -0.5 * LOG2_E
    qf = _dot(qn, wqb_ref[...])
    kvf = _dot(kvn, wkvb_ref[...])
    for h in range(MLA_HEADS):
        c = h * MLA_QK_PAD
        qm_ref[:, c:c + LANE] = (qf[:, c:c + LANE] * scale).astype(bf16)
        qm_ref[:, c + LANE:c + 2 * LANE] = (rope(qf[:, c + LANE:c + 2 * LANE]) * scale).astype(bf16)
        km_ref[:, h * LANE:(h + 1) * LANE] = kvf[:, h * LANE:(h + 1) * LANE].astype(bf16)
    km_ref[:, MLA_HEADS * MLA_NOPE:MLA_K_W] = k_rope
    vt = kvf[:, MLA_HEADS * MLA_NOPE:].T
    for h in range(MLA_HEADS):
        vmt_ref[h * MLA_VA:h * MLA_VA + MLA_V, :] = vt[h * MLA_V:(h + 1) * MLA_V].astype(bf16)
        vmt_ref[h * MLA_VA + MLA_V:(h + 1) * MLA_VA, :] = jnp.ones((MLA_VA - MLA_V, vt.shape[1]), bf16)


def _prep_call(x2d, rows_per_seq, tile, gain, w_in_r, q_norm, w_qb_r, kv_norm, w_kvb_r, ctab, stab):
    t = x2d.shape[0]
    nj = rows_per_seq // tile
    grid = (t // rows_per_seq, nj)

    def row(b, j):
        return (b * nj + j, 0)

    def const(b, j):
        return (0, 0)

    def tab(b, j):
        return (j, 0)

    widths = (GLA_QK, GLA_QK, GLA_VW, GLA_VW, LANE, MLA_HEADS * MLA_QK_PAD, MLA_K_W)
    return pl.pallas_call(
        _prep_kernel,
        grid=grid,
        in_specs=[
            pl.BlockSpec((tile, D_MODEL), row),
            pl.BlockSpec((1, D_MODEL), const),
            pl.BlockSpec((D_MODEL, C_END), const),
            pl.BlockSpec((1, MLA_Q_RANK), const),
            pl.BlockSpec((MLA_Q_RANK, MLA_HEADS * MLA_QK_PAD), const),
            pl.BlockSpec((1, MLA_KV_RANK), const),
            pl.BlockSpec((MLA_KV_RANK, 2 * MLA_OUT), const),
            pl.BlockSpec((tile, LANE), tab),
            pl.BlockSpec((tile, LANE), tab),
        ],
        out_specs=[pl.BlockSpec((tile, w), row) for w in widths]
        + [pl.BlockSpec((None, MLA_HEADS * MLA_VA, tile), lambda b, j: (b * nj + j, 0, 0))],
        out_shape=[jax.ShapeDtypeStruct((t, w), bf16) for w in widths]
        + [jax.ShapeDtypeStruct((t // tile, MLA_HEADS * MLA_VA, tile), bf16)],
        compiler_params=pltpu.CompilerParams(
            dimension_semantics=("parallel", "parallel"), vmem_limit_bytes=VMEM_LIMIT),
        name="prep",
    )(x2d, gain, w_in_r, q_norm, w_qb_r, kv_norm, w_kvb_r, ctab, stab)


def _gla_log_decay(a, wa2_ref, ba_ref):
    s = _dot(a, wa2_ref[...]) + ba_ref[...]
    return (jnp.minimum(s, 0.0) - jnp.log(1.0 + jnp.exp(-jnp.abs(s)))) * (1.0 / GLA_TAU)


def _gla_front(q, k, v, la, tri, want_out):
    nc = la.shape[0] // CHUNK
    hi, mid, lo = _split3(la)
    b = _dot(tri, hi) + _dot(tri, mid) + _dot(tri, lo)
    b_last = [b[(c + 1) * CHUNK - 1:(c + 1) * CHUNK, :] for c in range(nc)]
    b_last_full = jnp.concatenate([jnp.broadcast_to(bl, (CHUNK, GLA_QK)) for bl in b_last], axis=0)
    kf = k.astype(f32)
    front = dict(v=v, b_last=b_last, kd=(kf * jnp.exp(b_last_full - b)).astype(bf16))
    if want_out:
        front.update(qe=(q.astype(f32) * (GLA_DK ** -0.5) * jnp.exp(b)).astype(bf16),
                     ke=kf * jnp.exp(-b), vf=v.astype(f32))
    return front


def _gla_chunks(front, st_ref, want_out):
    v, kd, b_last = front["v"], front["kd"], front["b_last"]
    rr = lax.broadcasted_iota(i32, (GLA_VW, GLA_QK), 0) // GLA_DV
    cc = lax.broadcasted_iota(i32, (GLA_VW, GLA_QK), 1) // GLA_DK
    if want_out:
        qe, ke, vf = front["qe"], front["ke"], front["vf"]
        lane_h = lax.broadcasted_iota(i32, (CHUNK, GLA_QK), 1) // GLA_DK
        vlane_h = lax.broadcasted_iota(i32, (CHUNK, GLA_VW), 1) // GLA_DV
        a_row = lax.broadcasted_iota(i32, (CHUNK, GLA_QK), 0)
        a_col = lax.broadcasted_iota(i32, (CHUNK, GLA_QK), 1) % CHUNK
    outs = []
    st = st_ref[...]
    for c in range(len(b_last)):
        rows = slice(c * CHUNK, (c + 1) * CHUNK)
        upd = jnp.where(rr == cc, _dot_tn(v[rows], kd[rows]), 0.0)
        if want_out:
            kbd = jnp.concatenate(
                [jnp.where(lane_h == h, ke[rows], 0.0) for h in range(GLA_HEADS)], axis=0).astype(bf16)
            att = jnp.where(a_col <= a_row, _dot_nt(qe[rows], kbd), 0.0).astype(bf16)
            vbd = jnp.concatenate(
                [jnp.where(vlane_h == h, vf[rows], 0.0) for h in range(GLA_HEADS)], axis=0).astype(bf16)
            outs.append(_dot(att, vbd) + _dot_nt(qe[rows], st.astype(bf16)))
        st = st * jnp.exp(b_last[c]) + upd
    st_ref[...] = st
    return jnp.concatenate(outs, axis=0) if want_out else None


def _gla_kernel(q_ref, k_ref, v_ref, r_ref, a_ref, km_ref, vm_ref, am_ref, wa2_ref, ba_ref, gain_ref, tri_ref,
                y_ref, st_ref):
    j = pl.program_id(1)

    @pl.when(j == 0)
    def _():
        st_ref[...] = jnp.zeros_like(st_ref)
        la = _gla_log_decay(am_ref[...], wa2_ref, ba_ref)
        row = lax.broadcasted_iota(i32, la.shape, 0)
        la = jnp.where(row >= CHUNK - N_META, la, 0.0)
        front = _gla_front(None, km_ref[...], vm_ref[...], la, tri_ref[0:CHUNK, 0:CHUNK], False)
        _gla_chunks(front, st_ref.at[0], False)
        for bb in range(1, GLA_BATCH):
            st_ref[bb] = st_ref[0]

    fronts = [_gla_front(q_ref[bb], k_ref[bb], v_ref[bb], _gla_log_decay(a_ref[bb], wa2_ref, ba_ref),
                         tri_ref[...], True) for bb in range(GLA_BATCH)]
    for bb in range(GLA_BATCH):
        o = _gla_chunks(fronts[bb], st_ref.at[bb], True)
        r = r_ref[bb].astype(f32)
        outs = []
        for h in range(GLA_HEADS):
            oh = o[:, h * GLA_DV:(h + 1) * GLA_DV]
            outs.append(oh * lax.rsqrt(jnp.mean(oh * oh, axis=-1, keepdims=True) + EPS))
        on = jnp.concatenate(outs, axis=1) * gain_ref[...]
        y_ref[bb] = (on * (r * jax.nn.sigmoid(r))).astype(bf16)


def _gla_call(qg, kg, vg, rg, ag, km, vm, am, wa2_p, b_a, gain, batch, seq):
    nj = seq // GLA_TILE

    def row(b, j):
        return (b, j, 0)

    def const(b, j):
        return (0, 0)

    def seqs(a):
        return a.reshape(batch, seq, a.shape[-1])

    out = pl.pallas_call(
        _gla_kernel,
        grid=(batch // GLA_BATCH, nj),
        in_specs=[
            pl.BlockSpec((GLA_BATCH, GLA_TILE, GLA_QK), row),
            pl.BlockSpec((GLA_BATCH, GLA_TILE, GLA_QK), row),
            pl.BlockSpec((GLA_BATCH, GLA_TILE, GLA_VW), row),
            pl.BlockSpec((GLA_BATCH, GLA_TILE, GLA_VW), row),
            pl.BlockSpec((GLA_BATCH, GLA_TILE, LANE), row),
            pl.BlockSpec((CHUNK, GLA_QK), const),
            pl.BlockSpec((CHUNK, GLA_VW), const),
            pl.BlockSpec((CHUNK, LANE), const),
            pl.BlockSpec((LANE, GLA_QK), const),
            pl.BlockSpec((1, GLA_QK), const),
            pl.BlockSpec((1, GLA_VW), const),
            pl.BlockSpec((GLA_TILE, GLA_TILE), const),
        ],
        out_specs=pl.BlockSpec((GLA_BATCH, GLA_TILE, GLA_VW), row),
        out_shape=jax.ShapeDtypeStruct((batch, seq, GLA_VW), bf16),
        scratch_shapes=[pltpu.VMEM((GLA_BATCH, GLA_VW, GLA_QK), f32)],
        compiler_params=pltpu.CompilerParams(
            dimension_semantics=("parallel", "arbitrary"), vmem_limit_bytes=VMEM_LIMIT),
        name="gla",
    )(seqs(qg), seqs(kg), seqs(vg), seqs(rg), seqs(ag), km, vm, am, wa2_p, b_a, gain,
      jnp.asarray(_CHUNK_PREFIX, bf16))
    return out.reshape(batch * seq, GLA_VW)


def _mla_kernel(q_ref, k_ref, vt_ref, km_ref, vmt_ref, o_ref, sa_ref, sb_ref):
    i = pl.program_id(2)
    tq = ATT_TILE
    w = MLA_QK_PAD
    va = MLA_VA
    heads = range(ATT_HEADS)

    def keys(ref, rows, h):
        return jnp.concatenate([ref[rows, h * MLA_NOPE:(h + 1) * MLA_NOPE], ref[rows, MLA_HEADS * MLA_NOPE:MLA_K_W]],
                               axis=1)

    def scores(h, blk):
        rows = pl.ds(pl.multiple_of(blk * tq, tq), tq)
        return _dot_nt(keys(k_ref, rows, h), q_ref[:, h * w:(h + 1) * w])

    def soft(s, vtb, carry, mask=None):
        m, acc = carry
        if mask is not None:
            s = jnp.where(mask, s, -1e30)
        m_new = jnp.maximum(m, jnp.max(s, axis=0, keepdims=True))
        p = jnp.exp2(s - m_new).astype(bf16)
        return m_new, jnp.exp2(m - m_new) * acc + _dot(vtb, p)

    def vt(h, blk):
        return vt_ref[blk, h * va:(h + 1) * va, :]

    def finish(carries):
        ss = [_dot_nt(keys(km_ref, slice(None), h), q_ref[:, h * w:(h + 1) * w]) for h in heads]
        accs = [soft(ss[h], vmt_ref[h * va:(h + 1) * va, :], carries[h])[1] for h in heads]
        for h in heads:
            acc = accs[h]
            o_ref[:, h * MLA_V:(h + 1) * MLA_V] = (acc[:MLA_V] * (1.0 / acc[MLA_V:MLA_V + 1])).T.astype(bf16)

    kc = lax.broadcasted_iota(i32, (tq, tq), 0) // CHUNK
    qc = lax.broadcasted_iota(i32, (tq, tq), 1) // CHUNK
    mask = kc <= qc

    for h in heads:
        sa_ref[h] = scores(h, 0)

    def pair(p, carries):
        b0 = 2 * p
        for h in heads:
            sb_ref[h] = scores(h, b0 + 1)
        carries = [soft(sa_ref[h], vt(h, b0), carries[h]) for h in heads]
        for h in heads:
            sa_ref[h] = scores(h, b0 + 2)
        return tuple(soft(sb_ref[h], vt(h, b0 + 1), carries[h]) for h in heads)

    init = tuple((jnp.full((1, tq), -1e30, f32), jnp.zeros((va, tq), f32)) for _ in heads)
    carries = lax.fori_loop(0, i // 2, pair, init)

    @pl.when(i % 2 == 1)
    def _():
        for h in heads:
            sb_ref[h] = scores(h, i)
        c1 = [soft(sa_ref[h], vt(h, i - 1), carries[h]) for h in heads]
        finish([soft(sb_ref[h], vt(h, i), c1[h], mask) for h in heads])

    @pl.when(i % 2 == 0)
    def _():
        finish([soft(sa_ref[h], vt(h, i), carries[h], mask) for h in heads])


def _mla_call(qm, km, vmt, km_meta, vmt_meta, batch, seq):
    nq = seq // ATT_TILE
    nh = ATT_HEADS
    qm3 = qm.reshape(batch, seq, MLA_HEADS * MLA_QK_PAD)
    km3 = km.reshape(batch, seq, MLA_K_W)
    assert nh == MLA_HEADS
    vt4 = vmt.reshape(batch, nq, MLA_HEADS * MLA_VA, ATT_TILE)
    out = pl.pallas_call(
        _mla_kernel,
        grid=(batch, MLA_HEADS // nh, nq),
        in_specs=[
            pl.BlockSpec((None, ATT_TILE, nh * MLA_QK_PAD), lambda b, h, i: (b, i, h)),
            pl.BlockSpec((None, seq, MLA_K_W), lambda b, h, i: (b, 0, 0)),
            pl.BlockSpec((None, nq, nh * MLA_VA, ATT_TILE), lambda b, h, i: (b, 0, h, 0)),
            pl.BlockSpec((N_META, MLA_K_W), lambda b, h, i: (0, 0)),
            pl.BlockSpec((nh * MLA_VA, N_META), lambda b, h, i: (h, 0)),
        ],
        out_specs=pl.BlockSpec((None, ATT_TILE, nh * MLA_V), lambda b, h, i: (b, i, h)),
        out_shape=jax.ShapeDtypeStruct((batch, seq, MLA_OUT), bf16),
        scratch_shapes=[pltpu.VMEM((nh, ATT_TILE, ATT_TILE), f32), pltpu.VMEM((nh, ATT_TILE, ATT_TILE), f32)],
        compiler_params=pltpu.CompilerParams(
            dimension_semantics=("parallel", "parallel", "arbitrary"), vmem_limit_bytes=VMEM_LIMIT),
        name="mla",
    )(qm3, km3, vt4, km_meta, vmt_meta)
    return out.reshape(batch * seq, MLA_OUT)


def _route_cols(lt):
    r = lt.shape[1]
    neg = -1e30
    gl = lt[0:N_GROUPS, :]
    gsub = lax.broadcasted_iota(i32, (N_GROUPS, r), 0)
    gmax = jnp.max(gl, axis=0, keepdims=True)
    g_p = 1.0 / jnp.sum(jnp.exp(gl - gmax), axis=0, keepdims=True)
    g_idx = jnp.min(jnp.where(gl == gmax, gsub, N_GROUPS), axis=0, keepdims=True)
    el_all = lt[N_GROUPS:N_GROUPS + N_EXPERTS, :]
    esub = lax.broadcasted_iota(i32, (N_EXPERTS, r), 0)
    base = g_idx * EXPERTS_PER_GROUP
    e_mask = (esub >= base) & (esub < base + EXPERTS_PER_GROUP)
    el = jnp.where(e_mask, el_all, neg)
    m1 = jnp.max(el, axis=0, keepdims=True)
    i1 = jnp.min(jnp.where(e_mask & (el == m1), esub, N_EXPERTS), axis=0, keepdims=True)
    el2 = jnp.where(esub == i1, neg, el)
    m2 = jnp.max(el2, axis=0, keepdims=True)
    i2 = jnp.min(jnp.where(e_mask & (esub != i1) & (el2 == m2), esub, N_EXPERTS), axis=0, keepdims=True)
    rr = jnp.exp(m2 - m1)
    ga = g_p / (1.0 + rr)
    gb = g_p * rr / (1.0 + rr)
    la_ = i1 - base
    lb_ = i2 - base
    lo = jnp.minimum(la_, lb_)
    hi = jnp.maximum(la_, lb_)
    g_lo = jnp.where(la_ < lb_, ga, gb)
    g_hi = jnp.where(la_ < lb_, gb, ga)
    pidx = ((lo * (2 * EXPERTS_PER_GROUP - 1 - lo)) >> 1) + (hi - lo - 1)
    bucket = g_idx * N_PAIRS + pidx
    bsub = lax.broadcasted_iota(i32, (BUCKET_LANES, r), 0)
    oht = jnp.where(bsub == bucket, 1.0, 0.0)
    ohb = oht.astype(bf16)
    ri = lax.broadcasted_iota(i32, (ROUTE_ROWS, ROUTE_ROWS), 0)
    ci = lax.broadcasted_iota(i32, (ROUTE_ROWS, ROUTE_ROWS), 1)
    before = jnp.where(ri < ci, 1.0, 0.0).astype(bf16)
    ones = jnp.ones((SUBLANES, ROUTE_ROWS), bf16)
    subs = [slice(i * ROUTE_ROWS, (i + 1) * ROUTE_ROWS) for i in range(r // ROUTE_ROWS)]
    cum = jnp.concatenate([_dot(ohb[:, sl], before) for sl in subs], axis=1)
    rank = jnp.sum(oht * cum, axis=0, keepdims=True)
    counts = [_dot_nt(ones, ohb[:, sl])[0:1, :] for sl in subs]
    msub = lax.broadcasted_iota(i32, (LANE, r), 0)
    meta_t = jnp.where(msub == 0, bucket.astype(f32),
                       jnp.where(msub == 1, rank,
                                 jnp.where(msub == 2, g_lo, jnp.where(msub == 3, g_hi, 0.0))))
    return meta_t.T, counts, meta_t[0:SUBLANES, :]


def _outproj_kernel(x_ref, yg_ref, ym_ref, wog_ref, wom_ref, gain_ref, wrt_ref, rb_ref,
                    ux_ref, cnt_ref, rt_ref):
    per_group = ROUTE_GROUP // ROUTE_ROWS
    for grp in range(OUT_TILE // ROUTE_GROUP):
        rows = slice(grp * ROUTE_GROUP, (grp + 1) * ROUTE_GROUP)
        h1 = x_ref[rows, :] + _dot(yg_ref[rows, :], wog_ref[...]) + _dot(ym_ref[rows, :], wom_ref[...])
        ux_ref[rows, 0:D_MODEL] = h1
        u2 = _rms(h1, gain_ref[...])
        lt = _dot_nt(wrt_ref[...], u2.astype(bf16)) + rb_ref[...]
        meta, counts, routes = _route_cols(lt)
        ux_ref[rows, D_MODEL:ROW_W] = meta
        for i in range(per_group):
            cnt_ref[grp * per_group + i] = counts[i]
            rt_ref[grp * per_group + i] = routes[:, i * ROUTE_ROWS:(i + 1) * ROUTE_ROWS]


def _outproj_call(x2d, yg, ym, wo_g, wo_m, gain, w_r, rbias):
    t = x2d.shape[0]
    nt = t // OUT_TILE

    def row(i):
        return (i, 0)

    def const(i):
        return (0, 0)

    return pl.pallas_call(
        _outproj_kernel,
        grid=(nt,),
        in_specs=[
            pl.BlockSpec((OUT_TILE, D_MODEL), row),
            pl.BlockSpec((OUT_TILE, GLA_VW), row),
            pl.BlockSpec((OUT_TILE, MLA_OUT), row),
            pl.BlockSpec((GLA_VW, D_MODEL), const),
            pl.BlockSpec((MLA_OUT, D_MODEL), const),
            pl.BlockSpec((1, D_MODEL), const),
            pl.BlockSpec((LANE, D_MODEL), const),
            pl.BlockSpec((LANE, 1), const),
        ],
        out_specs=[
            pl.BlockSpec((OUT_TILE, ROW_W), row),
            pl.BlockSpec((OUT_TILE // ROUTE_ROWS, 1, BUCKET_LANES), lambda i: (i, 0, 0)),
            pl.BlockSpec((OUT_TILE // ROUTE_ROWS, SUBLANES, ROUTE_ROWS), lambda i: (i, 0, 0)),
        ],
        out_shape=[
            jax.ShapeDtypeStruct((t, ROW_W), f32),
            jax.ShapeDtypeStruct((nt * (OUT_TILE // ROUTE_ROWS), 1, BUCKET_LANES), f32),
            jax.ShapeDtypeStruct((nt * (OUT_TILE // ROUTE_ROWS), SUBLANES, ROUTE_ROWS), f32),
        ],
        compiler_params=pltpu.CompilerParams(
            dimension_semantics=("parallel",), vmem_limit_bytes=VMEM_LIMIT),
        name="outproj",
    )(x2d, yg, ym, wo_g, wo_m, gain, w_r, rbias)


def _scatter_kernel(pos_ref, zb_ref, ux_ref, hs_ref, zbuf, sem, zsem):
    @pl.when(pl.program_id(0) == 0)
    def _():
        zbuf[...] = jnp.zeros_like(zbuf)

        def zero_copy(j):
            rows = pl.ds(pl.multiple_of(zb_ref[j] * MOE_BLOCK, MOE_BLOCK), MOE_BLOCK)
            return pltpu.make_async_copy(zbuf, hs_ref.at[rows], zsem)

        def zstart(j, c):
            @pl.when(zb_ref[j] >= 0)
            def _():
                zero_copy(j).start()
            return c

        def zwait(j, c):
            @pl.when(zb_ref[j] >= 0)
            def _():
                zero_copy(j).wait()
            return c

        lax.fori_loop(0, zb_ref.shape[0], zstart, 0)
        lax.fori_loop(0, zb_ref.shape[0], zwait, 0)

    def start(io, c):
        for r in range(ISSUE_UNROLL):
            ii = io * (ISSUE_UNROLL // SUBLANES) + r // SUBLANES
            pltpu.make_async_copy(ux_ref.at[ii, pl.ds(r % SUBLANES, 1)],
                                  hs_ref.at[pl.ds(pos_ref[io * ISSUE_UNROLL + r], 1)], sem).start(priority=r % 2)
        return c

    lax.fori_loop(0, SCATTER_TILE // ISSUE_UNROLL, start, 0)
    pltpu.make_async_copy(hs_ref.at[pl.ds(0, SCATTER_TILE)], hs_ref.at[pl.ds(0, SCATTER_TILE)], sem).wait()


def _scatter_call(pos, zero_blocks, ux, n_slots):
    t = ux.shape[0]
    nz = zero_blocks.shape[0]
    return pl.pallas_call(
        _scatter_kernel,
        grid=(t // SCATTER_TILE,),
        in_specs=[
            pl.BlockSpec((SCATTER_TILE,), lambda i: (i,), memory_space=pltpu.SMEM),
            pl.BlockSpec((nz,), lambda i: (0,), memory_space=pltpu.SMEM),
            pl.BlockSpec((SCATTER_TILE // SUBLANES, SUBLANES, ROW_W), lambda i: (i, 0, 0)),
        ],
        out_specs=pl.BlockSpec(memory_space=pl.ANY),
        out_shape=jax.ShapeDtypeStruct((n_slots, ROW_W), f32),
        scratch_shapes=[pltpu.VMEM((MOE_BLOCK, ROW_W), f32), pltpu.SemaphoreType.DMA(()),
                        pltpu.SemaphoreType.DMA(())],
        compiler_params=pltpu.CompilerParams(
            dimension_semantics=("arbitrary",), vmem_limit_bytes=VMEM_LIMIT),
        name="scatter",
    )(pos, zero_blocks, ux.reshape(t // SUBLANES, SUBLANES, ROW_W))


def _moe_kernel(se_ref, sf_ref, sk_ref, sp_ref, sn_ref, sb_ref, sr_ref, si_ref, so_ref,
                hs_hbm, wg_hbm, wu_hbm, wd_hbm, fg_ref, y_hbm,
                xbuf, obuf, wg_buf, wu_buf, wd_buf, wgu_s, wd_s, in_sem, out_sem, w_sem):
    s = pl.program_id(0)
    ns = pl.num_programs(0)
    cur = s % 2
    g_n = MOE_GROUP

    def in_copy(step, g, buf):
        rows = pl.ds(pl.multiple_of(sb_ref[step * g_n + g] * MOE_BLOCK, MOE_BLOCK), MOE_BLOCK)
        return pltpu.make_async_copy(
            hs_hbm.at[rows], xbuf.at[buf, pl.ds(g * MOE_BLOCK, MOE_BLOCK)], in_sem.at[buf])

    def out_copy(step, g, buf):
        rows = pl.ds(pl.multiple_of(sb_ref[step * g_n + g] * MOE_BLOCK, MOE_BLOCK), MOE_BLOCK)
        cols = pl.ds(pl.multiple_of(sr_ref[step * g_n + g] * D_MODEL, D_MODEL), D_MODEL)
        return pltpu.make_async_copy(
            obuf.at[buf, pl.ds(g * MOE_BLOCK, MOE_BLOCK)], y_hbm.at[rows, cols], out_sem.at[buf])

    def for_slots(step, flags_ref, fn):
        for g in range(g_n):
            @pl.when(flags_ref[step * g_n + g] == 1)
            def _():
                fn(g)

    @pl.when(s == 0)
    def _():
        xbuf[...] = jnp.zeros_like(xbuf)
        for_slots(0, si_ref, lambda g: in_copy(0, g, 0).start())

    @pl.when(s + 1 < ns)
    def _():
        for_slots(s + 1, si_ref, lambda g: in_copy(s + 1, g, 1 - cur).start(priority=g % 2))

    for_slots(s, si_ref, lambda g: in_copy(s, g, cur).wait())

    @pl.when(s >= 2)
    def _():
        for_slots(s - 2, so_ref, lambda g: out_copy(s - 2, g, cur).wait())

    def w_copies(expert, slot):
        return (pltpu.make_async_copy(wg_hbm.at[expert], wg_buf.at[slot], w_sem.at[slot]),
                pltpu.make_async_copy(wu_hbm.at[expert], wu_buf.at[slot], w_sem.at[slot]),
                pltpu.make_async_copy(wd_hbm.at[expert], wd_buf.at[slot], w_sem.at[slot]))

    @pl.when(s == 0)
    def _():
        for c in w_copies(se_ref[0], sp_ref[0]):
            c.start()

    @pl.when(sf_ref[s] == 1)
    def _():
        slot = sp_ref[s]
        for c in w_copies(se_ref[s], slot):
            c.wait()
        wgu_s[:, 0:D_EXPERT] = wg_buf[slot].astype(bf16)
        wgu_s[:, D_EXPERT:2 * D_EXPERT] = wu_buf[slot].astype(bf16)
        wd_s[...] = wd_buf[slot].astype(bf16)

        @pl.when(sn_ref[s] >= 0)
        def _():
            for c in w_copies(sn_ref[s], 1 - slot):
                c.start()

    @pl.when(sk_ref[s] == 1)
    def _():
        h1 = xbuf[cur, :, 0:D_MODEL]
        meta = xbuf[cur, :, D_MODEL:ROW_W]
        u = _rms(h1, fg_ref[...]).astype(bf16)
        ones = jnp.ones((MOE_BLOCK, 1), f32)
        role0 = [sr_ref[s * g_n + g] == 0 for g in range(g_n)]
        gate = jnp.concatenate(
            [jnp.where(role0[g], meta[g * MOE_BLOCK:(g + 1) * MOE_BLOCK, 2:3],
                       meta[g * MOE_BLOCK:(g + 1) * MOE_BLOCK, 3:4]) for g in range(g_n)], axis=0)
        keep = jnp.concatenate([jnp.where(role0[g], ones, 0.0) for g in range(g_n)], axis=0)
        gu = _dot(u, wgu_s[...])
        gt = gu[:, 0:D_EXPERT]
        hdn = (gt * jax.nn.sigmoid(gt) * gu[:, D_EXPERT:]).astype(bf16)
        obuf[cur] = _dot(hdn, wd_s[...]) * gate + h1 * keep

    @pl.when(sk_ref[s] == 0)
    def _():
        obuf[cur] = jnp.zeros(obuf.shape[1:], f32)

    for_slots(s, so_ref, lambda g: out_copy(s, g, cur).start(priority=g % 2))

    @pl.when(s == ns - 1)
    def _():
        for_slots(s, so_ref, lambda g: out_copy(s, g, cur).wait())

        @pl.when(s >= 1)
        def _():
            for_slots(s - 1, so_ref, lambda g: out_copy(s - 1, g, 1 - cur).wait())


def _moe_call(plan, hs, w_gate, w_up, w_down, ffn_gain):
    n_steps = plan[0].shape[0]
    n_slots = hs.shape[0]
    rows = MOE_GROUP * MOE_BLOCK

    grid_spec = pltpu.PrefetchScalarGridSpec(
        num_scalar_prefetch=9,
        grid=(n_steps,),
        in_specs=[
            pl.BlockSpec(memory_space=pl.ANY),
            pl.BlockSpec(memory_space=pl.ANY),
            pl.BlockSpec(memory_space=pl.ANY),
            pl.BlockSpec(memory_space=pl.ANY),
            pl.BlockSpec((1, D_MODEL), lambda s, *_: (0, 0)),
        ],
        out_specs=pl.BlockSpec(memory_space=pl.ANY),
        scratch_shapes=[
            pltpu.VMEM((2, rows, ROW_W), f32),
            pltpu.VMEM((2, rows, D_MODEL), f32),
            pltpu.VMEM((2, D_MODEL, D_EXPERT), f32),
            pltpu.VMEM((2, D_MODEL, D_EXPERT), f32),
            pltpu.VMEM((2, D_EXPERT, D_MODEL), f32),
            pltpu.VMEM((D_MODEL, 2 * D_EXPERT), bf16),
            pltpu.VMEM((D_EXPERT, D_MODEL), bf16),
            pltpu.SemaphoreType.DMA((2,)),
            pltpu.SemaphoreType.DMA((2,)),
            pltpu.SemaphoreType.DMA((2,)),
        ],
    )
    return pl.pallas_call(
        _moe_kernel,
        grid_spec=grid_spec,
        out_shape=jax.ShapeDtypeStruct((n_slots, 2 * D_MODEL), f32),
        compiler_params=pltpu.CompilerParams(
            dimension_semantics=("arbitrary",), vmem_limit_bytes=VMEM_LIMIT),
        name="moe",
    )(*plan, hs, w_gate, w_up, w_down, ffn_gain)


def _final_kernel(posc_ref, posn_ref, gain_ref, y_hbm, o_ref, ybuf, sem):
    i = pl.program_id(0)
    cur = i % 2

    def issue(pos_ref, buf):
        def start(io, c):
            for r in range(ISSUE_UNROLL):
                ii = io * (ISSUE_UNROLL // SUBLANES) + r // SUBLANES
                pltpu.make_async_copy(y_hbm.at[pl.ds(pos_ref[io * ISSUE_UNROLL + r], 1)],
                                      ybuf.at[buf, ii, pl.ds(r % SUBLANES, 1)], sem.at[buf]).start(priority=r % 2)
            return c

        lax.fori_loop(0, FINAL_TILE // ISSUE_UNROLL, start, 0)

    @pl.when(i == 0)
    def _():
        issue(posc_ref, 0)

    @pl.when(i + 1 < pl.num_programs(0))
    def _():
        issue(posn_ref, 1 - cur)

    pltpu.make_async_copy(ybuf.at[cur], ybuf.at[cur], sem.at[cur]).wait()
    h = ybuf[cur, :, :, 0:D_MODEL] + ybuf[cur, :, :, D_MODEL:2 * D_MODEL]
    o_ref[...] = _rms(h, gain_ref[...])


def _final_call(pos, gain, y):
    t = pos.shape[0]
    n = t // FINAL_TILE
    rows = FINAL_TILE // SUBLANES
    out = pl.pallas_call(
        _final_kernel,
        grid=(n,),
        in_specs=[
            pl.BlockSpec((FINAL_TILE,), lambda i: (i,), memory_space=pltpu.SMEM),
            pl.BlockSpec((FINAL_TILE,), lambda i: (jnp.minimum(i + 1, n - 1),), memory_space=pltpu.SMEM),
            pl.BlockSpec((1, 1, D_MODEL), lambda i: (0, 0, 0)),
            pl.BlockSpec(memory_space=pl.ANY),
        ],
        out_specs=pl.BlockSpec((rows, SUBLANES, D_MODEL), lambda i: (i, 0, 0)),
        out_shape=jax.ShapeDtypeStruct((t // SUBLANES, SUBLANES, D_MODEL), f32),
        scratch_shapes=[pltpu.VMEM((2, rows, SUBLANES, 2 * D_MODEL), f32), pltpu.SemaphoreType.DMA((2,))],
        compiler_params=pltpu.CompilerParams(
            dimension_semantics=("arbitrary",), vmem_limit_bytes=VMEM_LIMIT),
        name="final",
    )(pos, pos, gain.reshape(1, 1, D_MODEL), y)
    return out.reshape(t, D_MODEL)


def _rope_tables(pos):
    inv = ROPE_BASE ** (-jnp.arange(0, MLA_ROPE, 2, dtype=f32) / MLA_ROPE)
    ang = pos.astype(f32)[:, None] * inv[None, :]
    cos, sin = jnp.cos(ang), jnp.sin(ang)
    z = jnp.zeros((pos.shape[0], LANE - MLA_ROPE), f32)
    return jnp.concatenate([cos, cos, z], axis=1), jnp.concatenate([-sin, sin, z], axis=1)


def _relayout_weights(w_in, w_qb, w_kvb):
    half = MLA_ROPE // 2
    perm = (np.arange(MLA_ROPE) + half) % MLA_ROPE
    pts = np.cumsum((GLA_QK, GLA_QK, GLA_VW, GLA_VW, GLA_GATE_RANK, MLA_Q_RANK, MLA_KV_RANK, MLA_ROPE))
    q_g, k_g, v_g, r_g, a_l, q_lat, kv_lat, k_rope = jnp.split(w_in, pts[:-1], axis=1)
    a_seg = jnp.pad(a_l, ((0, 0), (0, LANE - GLA_GATE_RANK)))
    w_in_r = jnp.concatenate(
        [q_g, k_g, v_g, r_g, q_lat, kv_lat, k_rope, k_rope[:, perm], a_seg], axis=1).astype(bf16)
    qcols, kcols, vcols = [], [], []
    for h in range(MLA_HEADS):
        c = h * (MLA_NOPE + MLA_ROPE)
        rope = w_qb[:, c + MLA_NOPE:c + MLA_NOPE + MLA_ROPE]
        qcols += [w_qb[:, c:c + MLA_NOPE], rope, rope[:, perm]]
        c2 = h * (MLA_NOPE + MLA_V)
        kcols.append(w_kvb[:, c2:c2 + MLA_NOPE])
        vcols.append(w_kvb[:, c2 + MLA_NOPE:c2 + MLA_NOPE + MLA_V])
    return w_in_r, jnp.concatenate(qcols, axis=1).astype(bf16), jnp.concatenate(kcols + vcols, axis=1).astype(bf16)


_BUCKET_GROUP = np.arange(N_BUCKETS) // N_PAIRS
_RUN_EXPERT = np.concatenate([_BUCKET_GROUP * EXPERTS_PER_GROUP + _PAIR_LO[np.arange(N_BUCKETS) % N_PAIRS],
                              _BUCKET_GROUP * EXPERTS_PER_GROUP + _PAIR_HI[np.arange(N_BUCKETS) % N_PAIRS]])
_RUN_IS_EXPERT = (_RUN_EXPERT[:, None] == np.arange(N_EXPERTS)[None, :]).astype(np.int32)
_RUN_BEFORE = ((_RUN_EXPERT[:, None] == _RUN_EXPERT[None, :])
               & (np.arange(2 * N_BUCKETS)[None, :] < np.arange(2 * N_BUCKETS)[:, None])).astype(np.int32)


def _route_plan(counts, bucket, rank, n_tok):
    nt = counts.shape[0]
    g_n = MOE_GROUP
    tot = counts.sum(axis=0)
    nblk = (tot + MOE_BLOCK - 1) // MOE_BLOCK
    bstart_blk = jnp.cumsum(nblk) - nblk
    n_blocks = jnp.sum(nblk)
    tile_base = bstart_blk[None, :] * MOE_BLOCK + jnp.cumsum(counts, axis=0) - counts
    hit = bucket.reshape(nt, -1, 1) == jnp.arange(N_BUCKETS, dtype=i32)
    pos = jnp.sum(jnp.where(hit, tile_base[:, None, :], 0), axis=-1).reshape(-1) + rank
    nb_max = (n_tok + N_BUCKETS * (MOE_BLOCK - 1)) // MOE_BLOCK

    n_run = jnp.concatenate([nblk, nblk])
    b0_run = jnp.concatenate([bstart_blk, bstart_blk])
    c_e = jnp.sum(n_run[:, None] * _RUN_IS_EXPERT, axis=0)
    g_e = (c_e + g_n - 1) // g_n
    gend = jnp.cumsum(g_e)
    gstart = gend - g_e
    n_compute = gend[-1]
    off_run = jnp.sum(_RUN_BEFORE * n_run[None, :], axis=1)
    f_run = jnp.sum(_RUN_IS_EXPERT * gstart[None, :], axis=1) * g_n + off_run

    n_steps = (2 * nb_max + N_EXPERTS * (g_n - 1) + g_n - 1) // g_n + 1
    f = jnp.arange(n_steps * g_n, dtype=i32)
    in_run = (f[:, None] >= f_run[None, :]) & (f[:, None] < (f_run + n_run)[None, :])
    valid_c = jnp.any(in_run, axis=1)
    block_c = jnp.sum(jnp.where(in_run, b0_run[None, :] + f[:, None] - f_run[None, :], 0), axis=1)
    role_c = jnp.sum(jnp.where(in_run[:, N_BUCKETS:], 1, 0), axis=1)
    u_idx = f - n_compute * g_n
    valid_f = (u_idx >= 0) & (u_idx < 2 * (nb_max - n_blocks))
    slot_block = jnp.where(valid_c, block_c, jnp.where(valid_f, n_blocks + u_idx // 2, 0))
    slot_role = jnp.where(valid_c, role_c, jnp.where(valid_f, u_idx % 2, 0))

    step = jnp.arange(n_steps, dtype=i32)
    e_of_step = jnp.minimum(jnp.sum(gend[None, :] <= step[:, None], axis=1), N_EXPERTS - 1)
    is_compute = step < n_compute
    last_e = jnp.max(jnp.where(is_compute, e_of_step, 0))
    step_expert = jnp.where(is_compute, e_of_step, last_e)
    step_first = jnp.concatenate([jnp.ones((1,), bool), step_expert[1:] != step_expert[:-1]])
    ordinal = jnp.cumsum(step_first.astype(i32)) - 1
    ords = jnp.arange(N_EXPERTS + 1, dtype=i32)
    expert_of_ord = jnp.sum(jnp.where(step_first[:, None] & (ordinal[:, None] == ords[None, :]),
                                      step_expert[:, None], 0), axis=0)
    has_next = ordinal + 1 <= ordinal[-1]
    next_expert = jnp.sum(jnp.where(ords[None, :] == ordinal[:, None] + 1, expert_of_ord[None, :], 0), axis=1)
    step_next = jnp.where(step_first & has_next, next_expert, -1)
    plan = tuple(a.astype(i32) for a in
                 (step_expert, step_first, is_compute, ordinal % 2, step_next,
                  slot_block, slot_role, valid_c, valid_c | valid_f))
    last_blk = jnp.where(nblk > 0, bstart_blk + nblk - 1, -1)
    spare = n_blocks + jnp.arange(nb_max - n_tok // MOE_BLOCK, dtype=i32)
    zero_blocks = jnp.concatenate([last_blk, jnp.where(spare < nb_max, spare, -1)]).astype(i32)
    return pos.astype(i32), plan, zero_blocks, nb_max


def kernel(x, meta_tokens, mix_norm, w_in, gla_w_a2, gla_b_a, gla_out_norm, mla_q_norm, mla_w_qb, mla_kv_norm,
           mla_w_kvb, w_out, ffn_norm, router_group_w, router_group_b, router_expert_w, router_expert_b,
           expert_w_gate, expert_w_up, expert_w_down, final_norm):
    batch, seq, d = x.shape
    assert PREP_TILE == ATT_TILE
    assert d == D_MODEL and seq % max(PREP_TILE, GLA_TILE, ATT_TILE) == 0
    assert (batch * seq) % max(OUT_TILE, SCATTER_TILE, FINAL_TILE) == 0 and batch % GLA_BATCH == 0
    n_tok = batch * seq
    x2d = x.reshape(n_tok, d)

    w_in_r, w_qb_r, w_kvb_r = _relayout_weights(w_in[0], mla_w_qb[0], mla_w_kvb[0])
    mixg = mix_norm[0].reshape(1, d)
    qn = mla_q_norm[0].reshape(1, MLA_Q_RANK)
    kvn = mla_kv_norm[0].reshape(1, MLA_KV_RANK)
    ct_m, st_m = _rope_tables(jnp.arange(META_TILE))
    ct_x, st_x = _rope_tables(N_META + jnp.arange(seq))

    x_meta = jnp.pad(meta_tokens.astype(f32), ((0, META_TILE - N_META), (0, 0)))
    _, kg_m, vg_m, _, a_m, _, km_m, vmt_m = _prep_call(
        x_meta, META_TILE, META_TILE, mixg, w_in_r, qn, w_qb_r, kvn, w_kvb_r, ct_m, st_m)
    qg, kg, vg, rg, ag, qm, km, vmt = _prep_call(
        x2d, seq, PREP_TILE, mixg, w_in_r, qn, w_qb_r, kvn, w_kvb_r, ct_x, st_x)

    def chunk0(a):
        return jnp.pad(a[:N_META], ((CHUNK - N_META, 0), (0, 0)))

    wa2_p = jnp.pad(gla_w_a2[0], ((0, LANE - GLA_GATE_RANK), (0, 0))).astype(bf16)
    y_gla = _gla_call(qg, kg, vg, rg, ag, chunk0(kg_m), chunk0(vg_m), chunk0(a_m),
                      wa2_p, gla_b_a[0].reshape(1, GLA_QK), gla_out_norm[0].reshape(1, GLA_VW), batch, seq)
    y_mla = _mla_call(qm, km, vmt, km_m[:N_META], vmt_m[0, :, :N_META], batch, seq)

    wo = w_out[0].astype(bf16)
    rw = jnp.concatenate([router_group_w[0], router_expert_w[0],
                          jnp.zeros((d, LANE - N_GROUPS - N_EXPERTS), f32)], axis=1)
    rb = jnp.concatenate([router_group_b[0], router_expert_b[0],
                          jnp.zeros((LANE - N_GROUPS - N_EXPERTS,), f32)]).reshape(1, LANE)
    ffn_gain = ffn_norm[0].reshape(1, d)
    ux, cnt, routes = _outproj_call(x2d, y_gla, y_mla, wo[:GLA_VW], wo[GLA_VW:], ffn_gain,
                                      rw.T.astype(bf16), rb.reshape(LANE, 1))

    counts = cnt.reshape(-1, BUCKET_LANES)[:, :N_BUCKETS].astype(i32)
    tok_bucket = routes[:, 0, :].reshape(-1).astype(i32)
    tok_rank = routes[:, 1, :].reshape(-1).astype(i32)
    pos, plan, zero_blocks, nb_max = _route_plan(counts, tok_bucket, tok_rank, n_tok)
    n_slots = nb_max * MOE_BLOCK
    hs = _scatter_call(pos, zero_blocks, ux, n_slots)
    y = _moe_call(plan, hs, expert_w_gate[0], expert_w_up[0], expert_w_down[0], ffn_gain)
    out = _final_call(pos, final_norm.reshape(1, d), y)
    return out.reshape(batch, seq, d)
```

```python
import functools

import numpy as np
import jax
import jax.numpy as jnp
from jax import lax
from jax.experimental import pallas as pl
from jax.experimental.pallas import tpu as pltpu

f32 = jnp.float32
bf16 = jnp.bfloat16
i32 = jnp.int32

D_MODEL = 1024
CHUNK = 64
N_META = 16
EPS = 1e-6
GLA_HEADS = 4
GLA_DK = 64
GLA_DV = 128
GLA_GATE_RANK = 16
GLA_TAU = 16.0
GLA_QK = GLA_HEADS * GLA_DK
GLA_VW = GLA_HEADS * GLA_DV
MLA_HEADS = 4
MLA_Q_RANK = 256
MLA_KV_RANK = 128
MLA_NOPE = 128
MLA_ROPE = 64
MLA_V = 128
MLA_OUT = MLA_HEADS * MLA_V
MLA_QK_PAD = 256
MLA_VA = MLA_V + 16
LOG2_E = 1.4426950408889634
ROPE_BASE = 10000.0
N_GROUPS = 8
EXPERTS_PER_GROUP = 8
N_EXPERTS = N_GROUPS * EXPERTS_PER_GROUP
D_EXPERT = 512
N_PAIRS = EXPERTS_PER_GROUP * (EXPERTS_PER_GROUP - 1) // 2
N_BUCKETS = N_GROUPS * N_PAIRS
BUCKET_LANES = 256
LANE = 128
SUBLANES = 8
MLA_K_W = MLA_HEADS * MLA_NOPE + LANE
META_W = LANE
ROW_W = D_MODEL + META_W

PREP_TILE = 512
GLA_TILE = 512
GLA_BATCH = 4
ATT_TILE = 512
ATT_HEADS = 4
META_TILE = 128
OUT_TILE = 1024
ROUTE_ROWS = 256
ROUTE_GROUP = 1024
SCATTER_TILE = 2048
FINAL_TILE = 1024
ISSUE_UNROLL = 128
MOE_BLOCK = 64
MOE_GROUP = 8
VMEM_LIMIT = 56 * 1024 * 1024

C_Q, C_K, C_V, C_R = 0, 256, 512, 1024
C_QLAT, C_KVLAT, C_KROPE, C_A, C_END = 1536, 1792, 1920, 2048, 2176

_TILE_POS = np.arange(GLA_TILE)
_CHUNK_PREFIX = ((_TILE_POS[:, None] // CHUNK == _TILE_POS[None, :] // CHUNK)
                 & (_TILE_POS[None, :] <= _TILE_POS[:, None])).astype(np.float32)
_PAIR_LO = np.array([lo for lo in range(8) for hi in range(lo + 1, 8)], np.int32)
_PAIR_HI = np.array([hi for lo in range(8) for hi in range(lo + 1, 8)], np.int32)


def _dot(a, b):
    return jnp.dot(a, b, preferred_element_type=f32)


def _dot_nt(a, b):
    return lax.dot_general(a, b, (((1,), (1,)), ((), ())), preferred_element_type=f32)


def _dot_tn(a, b):
    return lax.dot_general(a, b, (((0,), (0,)), ((), ())), preferred_element_type=f32)


def _rms(x, gain):
    return x * lax.rsqrt(jnp.mean(x * x, axis=-1, keepdims=True) + EPS) * gain


def _split3(x):
    hi = x.astype(bf16)
    r1 = x - hi.astype(f32)
    mid = r1.astype(bf16)
    lo = (r1 - mid.astype(f32)).astype(bf16)
    return hi, mid, lo


def _prep_kernel(x_ref, g_ref, win_ref, qn_ref, wqb_ref, kvn_ref, wkvb_ref, ct_ref, st_ref,
                 qg_ref, kg_ref, vg_ref, rg_ref, a_ref, qm_ref, km_ref, vmt_ref):
    u = _rms(x_ref[...], g_ref[...]).astype(bf16)

    def proj(lo, hi):
        return _dot(u, win_ref[:, lo:hi])

    qg_ref[...] = proj(C_Q, C_K).astype(bf16)
    kg_ref[...] = proj(C_K, C_V).astype(bf16)
    vg_ref[...] = proj(C_V, C_R).astype(bf16)
    rg_ref[...] = proj(C_R, C_QLAT).astype(bf16)
    z = proj(C_QLAT, C_END)
    a_ref[...] = z[:, C_A - C_QLAT:].astype(bf16)
    ctab = ct_ref[...]
    stab = st_ref[...]

    def rope(seg):
        return seg * ctab + pltpu.roll(seg, 64, axis=1) * stab

    k_rope = rope(z[:, C_KROPE - C_QLAT:C_A - C_QLAT]).astype(bf16)
    qn = _rms(z[:, 0:MLA_Q_RANK], qn_ref[...]).astype(bf16)
    kvn = _rms(z[:, MLA_Q_RANK:MLA_Q_RANK + MLA_KV_RANK], kvn_ref[...]).astype(bf16)
    scale = (MLA_NOPE + MLA_ROPE) ** -0.5 * LOG2_E
    qf = _dot(qn, wqb_ref[...])
    kvf = _dot(kvn, wkvb_ref[...])
    for h in range(MLA_HEADS):
        c = h * MLA_QK_PAD
        qm_ref[:, c:c + LANE] = (qf[:, c:c + LANE] * scale).astype(bf16)
        qm_ref[:, c + LANE:c + 2 * LANE] = (rope(qf[:, c + LANE:c + 2 * LANE]) * scale).astype(bf16)
        km_ref[:, h * LANE:(h + 1) * LANE] = kvf[:, h * LANE:(h + 1) * LANE].astype(bf16)
    km_ref[:, MLA_HEADS * MLA_NOPE:MLA_K_W] = k_rope
    vt = kvf[:, MLA_HEADS * MLA_NOPE:].T
    for h in range(MLA_HEADS):
        vmt_ref[h * MLA_VA:h * MLA_VA + MLA_V, :] = vt[h * MLA_V:(h + 1) * MLA_V].astype(bf16)
        vmt_ref[h * MLA_VA + MLA_V:(h + 1) * MLA_VA, :] = jnp.ones((MLA_VA - MLA_V, vt.shape[1]), bf16)


def _prep_call(x2d, rows_per_seq, tile, gain, w_in_r, q_norm, w_qb_r, kv_norm, w_kvb_r, ctab, stab):
    t = x2d.shape[0]
    nj = rows_per_seq // tile
    grid = (t // rows_per_seq, nj)

    def row(b, j):
        return (b * nj + j, 0)

    def const(b, j):
        return (0, 0)

    def tab(b, j):
        return (j, 0)

    widths = (GLA_QK, GLA_QK, GLA_VW, GLA_VW, LANE, MLA_HEADS * MLA_QK_PAD, MLA_K_W)
    return pl.pallas_call(
        _prep_kernel,
        grid=grid,
        in_specs=[
            pl.BlockSpec((tile, D_MODEL), row),
            pl.BlockSpec((1, D_MODEL), const),
            pl.BlockSpec((D_MODEL, C_END), const),
            pl.BlockSpec((1, MLA_Q_RANK), const),
            pl.BlockSpec((MLA_Q_RANK, MLA_HEADS * MLA_QK_PAD), const),
            pl.BlockSpec((1, MLA_KV_RANK), const),
            pl.BlockSpec((MLA_KV_RANK, 2 * MLA_OUT), const),
            pl.BlockSpec((tile, LANE), tab),
            pl.BlockSpec((tile, LANE), tab),
        ],
        out_specs=[pl.BlockSpec((tile, w), row) for w in widths]
        + [pl.BlockSpec((None, MLA_HEADS * MLA_VA, tile), lambda b, j: (b * nj + j, 0, 0))],
        out_shape=[jax.ShapeDtypeStruct((t, w), bf16) for w in widths]
        + [jax.ShapeDtypeStruct((t // tile, MLA_HEADS * MLA_VA, tile), bf16)],
        compiler_params=pltpu.CompilerParams(
            dimension_semantics=("parallel", "parallel"), vmem_limit_bytes=VMEM_LIMIT),
        name="prep",
    )(x2d, gain, w_in_r, q_norm, w_qb_r, kv_norm, w_kvb_r, ctab, stab)


def _gla_log_decay(a, wa2_ref, ba_ref):
    s = _dot(a, wa2_ref[...]) + ba_ref[...]
    return (jnp.minimum(s, 0.0) - jnp.log(1.0 + jnp.exp(-jnp.abs(s)))) * (1.0 / GLA_TAU)


def _gla_front(q, k, v, la, tri, want_out):
    nc = la.shape[0] // CHUNK
    hi, mid, lo = _split3(la)
    b = _dot(tri, hi) + _dot(tri, mid) + _dot(tri, lo)
    b_last = [b[(c + 1) * CHUNK - 1:(c + 1) * CHUNK, :] for c in range(nc)]
    b_last_full = jnp.concatenate([jnp.broadcast_to(bl, (CHUNK, GLA_QK)) for bl in b_last], axis=0)
    kf = k.astype(f32)
    front = dict(v=v, b_last=b_last, kd=(kf * jnp.exp(b_last_full - b)).astype(bf16))
    if want_out:
        front.update(qe=(q.astype(f32) * (GLA_DK ** -0.5) * jnp.exp(b)).astype(bf16),
                     ke=kf * jnp.exp(-b), vf=v.astype(f32))
    return front


def _gla_chunks(front, st_ref, want_out):
    v, kd, b_last = front["v"], front["kd"], front["b_last"]
    rr = lax.broadcasted_iota(i32, (GLA_VW, GLA_QK), 0) // GLA_DV
    cc = lax.broadcasted_iota(i32, (GLA_VW, GLA_QK), 1) // GLA_DK
    if want_out:
        qe, ke, vf = front["qe"], front["ke"], front["vf"]
        lane_h = lax.broadcasted_iota(i32, (CHUNK, GLA_QK), 1) // GLA_DK
        vlane_h = lax.broadcasted_iota(i32, (CHUNK, GLA_VW), 1) // GLA_DV
        a_row = lax.broadcasted_iota(i32, (CHUNK, GLA_QK), 0)
        a_col = lax.broadcasted_iota(i32, (CHUNK, GLA_QK), 1) % CHUNK
    outs = []
    st = st_ref[...]
    for c in range(len(b_last)):
        rows = slice(c * CHUNK, (c + 1) * CHUNK)
        upd = jnp.where(rr == cc, _dot_tn(v[rows], kd[rows]), 0.0)
        if want_out:
            kbd = jnp.concatenate(
                [jnp.where(lane_h == h, ke[rows], 0.0) for h in range(GLA_HEADS)], axis=0).astype(bf16)
            att = jnp.where(a_col <= a_row, _dot_nt(qe[rows], kbd), 0.0).astype(bf16)
            vbd = jnp.concatenate(
                [jnp.where(vlane_h == h, vf[rows], 0.0) for h in range(GLA_HEADS)], axis=0).astype(bf16)
            outs.append(_dot(att, vbd) + _dot_nt(qe[rows], st.astype(bf16)))
        st = st * jnp.exp(b_last[c]) + upd
    st_ref[...] = st
    return jnp.concatenate(outs, axis=0) if want_out else None


def _gla_kernel(q_ref, k_ref, v_ref, r_ref, a_ref, km_ref, vm_ref, am_ref, wa2_ref, ba_ref, gain_ref, tri_ref,
                y_ref, st_ref):
    j = pl.program_id(1)

    @pl.when(j == 0)
    def _():
        st_ref[...] = jnp.zeros_like(st_ref)
        la = _gla_log_decay(am_ref[...], wa2_ref, ba_ref)
        row = lax.broadcasted_iota(i32, la.shape, 0)
        la = jnp.where(row >= CHUNK - N_META, la, 0.0)
        front = _gla_front(None, km_ref[...], vm_ref[...], la, tri_ref[0:CHUNK, 0:CHUNK], False)
        _gla_chunks(front, st_ref.at[0], False)
        for bb in range(1, GLA_BATCH):
            st_ref[bb] = st_ref[0]

    fronts = [_gla_front(q_ref[bb], k_ref[bb], v_ref[bb], _gla_log_decay(a_ref[bb], wa2_ref, ba_ref),
                         tri_ref[...], True) for bb in range(GLA_BATCH)]
    for bb in range(GLA_BATCH):
        o = _gla_chunks(fronts[bb], st_ref.at[bb], True)
        r = r_ref[bb].astype(f32)
        outs = []
        for h in range(GLA_HEADS):
            oh = o[:, h * GLA_DV:(h + 1) * GLA_DV]
            outs.append(oh * lax.rsqrt(jnp.mean(oh * oh, axis=-1, keepdims=True) + EPS))
        on = jnp.concatenate(outs, axis=1) * gain_ref[...]
        y_ref[bb] = (on * (r * jax.nn.sigmoid(r))).astype(bf16)


def _gla_call(qg, kg, vg, rg, ag, km, vm, am, wa2_p, b_a, gain, batch, seq):
    nj = seq // GLA_TILE

    def row(b, j):
        return (b, j, 0)

    def const(b, j):
        return (0, 0)

    def seqs(a):
        return a.reshape(batch, seq, a.shape[-1])

    out = pl.pallas_call(
        _gla_kernel,
        grid=(batch // GLA_BATCH, nj),
        in_specs=[
            pl.BlockSpec((GLA_BATCH, GLA_TILE, GLA_QK), row),
            pl.BlockSpec((GLA_BATCH, GLA_TILE, GLA_QK), row),
            pl.BlockSpec((GLA_BATCH, GLA_TILE, GLA_VW), row),
            pl.BlockSpec((GLA_BATCH, GLA_TILE, GLA_VW), row),
            pl.BlockSpec((GLA_BATCH, GLA_TILE, LANE), row),
            pl.BlockSpec((CHUNK, GLA_QK), const),
            pl.BlockSpec((CHUNK, GLA_VW), const),
            pl.BlockSpec((CHUNK, LANE), const),
            pl.BlockSpec((LANE, GLA_QK), const),
            pl.BlockSpec((1, GLA_QK), const),
            pl.BlockSpec((1, GLA_VW), const),
            pl.BlockSpec((GLA_TILE, GLA_TILE), const),
        ],
        out_specs=pl.BlockSpec((GLA_BATCH, GLA_TILE, GLA_VW), row),
        out_shape=jax.ShapeDtypeStruct((batch, seq, GLA_VW), bf16),
        scratch_shapes=[pltpu.VMEM((GLA_BATCH, GLA_VW, GLA_QK), f32)],
        compiler_params=pltpu.CompilerParams(
            dimension_semantics=("parallel", "arbitrary"), vmem_limit_bytes=VMEM_LIMIT),
        name="gla",
    )(seqs(qg), seqs(kg), seqs(vg), seqs(rg), seqs(ag), km, vm, am, wa2_p, b_a, gain,
      jnp.asarray(_CHUNK_PREFIX, bf16))
    return out.reshape(batch * seq, GLA_VW)


def _mla_kernel(q_ref, k_ref, vt_ref, km_ref, vmt_ref, o_ref, sa_ref, sb_ref):
    i = pl.program_id(2)
    tq = ATT_TILE
    w = MLA_QK_PAD
    va = MLA_VA
    heads = range(ATT_HEADS)

    def keys(ref, rows, h):
        return jnp.concatenate([ref[rows, h * MLA_NOPE:(h + 1) * MLA_NOPE], ref[rows, MLA_HEADS * MLA_NOPE:MLA_K_W]],
                               axis=1)

    def scores(h, blk):
        rows = pl.ds(pl.multiple_of(blk * tq, tq), tq)
        return _dot_nt(keys(k_ref, rows, h), q_ref[:, h * w:(h + 1) * w])

    def soft(s, vtb, carry, mask=None):
        m, acc = carry
        if mask is not None:
            s = jnp.where(mask, s, -1e30)
        m_new = jnp.maximum(m, jnp.max(s, axis=0, keepdims=True))
        p = jnp.exp2(s - m_new).astype(bf16)
        return m_new, jnp.exp2(m - m_new) * acc + _dot(vtb, p)

    def vt(h, blk):
        return vt_ref[blk, h * va:(h + 1) * va, :]

    def finish(carries):
        ss = [_dot_nt(keys(km_ref, slice(None), h), q_ref[:, h * w:(h + 1) * w]) for h in heads]
        accs = [soft(ss[h], vmt_ref[h * va:(h + 1) * va, :], carries[h])[1] for h in heads]
        for h in heads:
            acc = accs[h]
            o_ref[:, h * MLA_V:(h + 1) * MLA_V] = (acc[:MLA_V] * (1.0 / acc[MLA_V:MLA_V + 1])).T.astype(bf16)

    kc = lax.broadcasted_iota(i32, (tq, tq), 0) // CHUNK
    qc = lax.broadcasted_iota(i32, (tq, tq), 1) // CHUNK
    mask = kc <= qc

    for h in heads:
        sa_ref[h] = scores(h, 0)

    def pair(p, carries):
        b0 = 2 * p
        for h in heads:
            sb_ref[h] = scores(h, b0 + 1)
        carries = [soft(sa_ref[h], vt(h, b0), carries[h]) for h in heads]
        for h in heads:
            sa_ref[h] = scores(h, b0 + 2)
        return tuple(soft(sb_ref[h], vt(h, b0 + 1), carries[h]) for h in heads)

    init = tuple((jnp.full((1, tq), -1e30, f32), jnp.zeros((va, tq), f32)) for _ in heads)
    carries = lax.fori_loop(0, i // 2, pair, init)

    @pl.when(i % 2 == 1)
    def _():
        for h in heads:
            sb_ref[h] = scores(h, i)
        c1 = [soft(sa_ref[h], vt(h, i - 1), carries[h]) for h in heads]
        finish([soft(sb_ref[h], vt(h, i), c1[h], mask) for h in heads])

    @pl.when(i % 2 == 0)
    def _():
        finish([soft(sa_ref[h], vt(h, i), carries[h], mask) for h in heads])


def _mla_call(qm, km, vmt, km_meta, vmt_meta, batch, seq):
    nq = seq // ATT_TILE
    nh = ATT_HEADS
    qm3 = qm.reshape(batch, seq, MLA_HEADS * MLA_QK_PAD)
    km3 = km.reshape(batch, seq, MLA_K_W)
    assert nh == MLA_HEADS
    vt4 = vmt.reshape(batch, nq, MLA_HEADS * MLA_VA, ATT_TILE)
    out = pl.pallas_call(
        _mla_kernel,
        grid=(batch, MLA_HEADS // nh, nq),
        in_specs=[
            pl.BlockSpec((None, ATT_TILE, nh * MLA_QK_PAD), lambda b, h, i: (b, i, h)),
            pl.BlockSpec((None, seq, MLA_K_W), lambda b, h, i: (b, 0, 0)),
            pl.BlockSpec((None, nq, nh * MLA_VA, ATT_TILE), lambda b, h, i: (b, 0, h, 0)),
            pl.BlockSpec((N_META, MLA_K_W), lambda b, h, i: (0, 0)),
            pl.BlockSpec((nh * MLA_VA, N_META), lambda b, h, i: (h, 0)),
        ],
        out_specs=pl.BlockSpec((None, ATT_TILE, nh * MLA_V), lambda b, h, i: (b, i, h)),
        out_shape=jax.ShapeDtypeStruct((batch, seq, MLA_OUT), bf16),
        scratch_shapes=[pltpu.VMEM((nh, ATT_TILE, ATT_TILE), f32), pltpu.VMEM((nh, ATT_TILE, ATT_TILE), f32)],
        compiler_params=pltpu.CompilerParams(
            dimension_semantics=("parallel", "parallel", "arbitrary"), vmem_limit_bytes=VMEM_LIMIT),
        name="mla",
    )(qm3, km3, vt4, km_meta, vmt_meta)
    return out.reshape(batch * seq, MLA_OUT)


def _route_cols(lt):
    r = lt.shape[1]
    neg = -1e30
    gl = lt[0:N_GROUPS, :]
    gsub = lax.broadcasted_iota(i32, (N_GROUPS, r), 0)
    gmax = jnp.max(gl, axis=0, keepdims=True)
    g_p = 1.0 / jnp.sum(jnp.exp(gl - gmax), axis=0, keepdims=True)
    g_idx = jnp.min(jnp.where(gl == gmax, gsub, N_GROUPS), axis=0, keepdims=True)
    el_all = lt[N_GROUPS:N_GROUPS + N_EXPERTS, :]
    esub = lax.broadcasted_iota(i32, (N_EXPERTS, r), 0)
    base = g_idx * EXPERTS_PER_GROUP
    e_mask = (esub >= base) & (esub < base + EXPERTS_PER_GROUP)
    el = jnp.where(e_mask, el_all, neg)
    m1 = jnp.max(el, axis=0, keepdims=True)
    i1 = jnp.min(jnp.where(e_mask & (el == m1), esub, N_EXPERTS), axis=0, keepdims=True)
    el2 = jnp.where(esub == i1, neg, el)
    m2 = jnp.max(el2, axis=0, keepdims=True)
    i2 = jnp.min(jnp.where(e_mask & (esub != i1) & (el2 == m2), esub, N_EXPERTS), axis=0, keepdims=True)
    rr = jnp.exp(m2 - m1)
    ga = g_p / (1.0 + rr)
    gb = g_p * rr / (1.0 + rr)
    la_ = i1 - base
    lb_ = i2 - base
    lo = jnp.minimum(la_, lb_)
    hi = jnp.maximum(la_, lb_)
    g_lo = jnp.where(la_ < lb_, ga, gb)
    g_hi = jnp.where(la_ < lb_, gb, ga)
    pidx = ((lo * (2 * EXPERTS_PER_GROUP - 1 - lo)) >> 1) + (hi - lo - 1)
    bucket = g_idx * N_PAIRS + pidx
    bsub = lax.broadcasted_iota(i32, (BUCKET_LANES, r), 0)
    oht = jnp.where(bsub == bucket, 1.0, 0.0)
    ohb = oht.astype(bf16)
    ri = lax.broadcasted_iota(i32, (ROUTE_ROWS, ROUTE_ROWS), 0)
    ci = lax.broadcasted_iota(i32, (ROUTE_ROWS, ROUTE_ROWS), 1)
    before = jnp.where(ri < ci, 1.0, 0.0).astype(bf16)
    ones = jnp.ones((SUBLANES, ROUTE_ROWS), bf16)
    subs = [slice(i * ROUTE_ROWS, (i + 1) * ROUTE_ROWS) for i in range(r // ROUTE_ROWS)]
    cum = jnp.concatenate([_dot(ohb[:, sl], before) for sl in subs], axis=1)
    rank = jnp.sum(oht * cum, axis=0, keepdims=True)
    counts = [_dot_nt(ones, ohb[:, sl])[0:1, :] for sl in subs]
    msub = lax.broadcasted_iota(i32, (LANE, r), 0)
    meta_t = jnp.where(msub == 0, bucket.astype(f32),
                       jnp.where(msub == 1, rank,
                                 jnp.where(msub == 2, g_lo, jnp.where(msub == 3, g_hi, 0.0))))
    return meta_t.T, counts, meta_t[0:SUBLANES, :]


def _outproj_kernel(x_ref, yg_ref, ym_ref, wog_ref, wom_ref, gain_ref, wrt_ref, rb_ref,
                    ux_ref, cnt_ref, rt_ref):
    per_group = ROUTE_GROUP // ROUTE_ROWS
    for grp in range(OUT_TILE // ROUTE_GROUP):
        rows = slice(grp * ROUTE_GROUP, (grp + 1) * ROUTE_GROUP)
        h1 = x_ref[rows, :] + _dot(yg_ref[rows, :], wog_ref[...]) + _dot(ym_ref[rows, :], wom_ref[...])
        ux_ref[rows, 0:D_MODEL] = h1
        u2 = _rms(h1, gain_ref[...])
        lt = _dot_nt(wrt_ref[...], u2.astype(bf16)) + rb_ref[...]
        meta, counts, routes = _route_cols(lt)
        ux_ref[rows, D_MODEL:ROW_W] = meta
        for i in range(per_group):
            cnt_ref[grp * per_group + i] = counts[i]
            rt_ref[grp * per_group + i] = routes[:, i * ROUTE_ROWS:(i + 1) * ROUTE_ROWS]


def _outproj_call(x2d, yg, ym, wo_g, wo_m, gain, w_r, rbias):
    t = x2d.shape[0]
    nt = t // OUT_TILE

    def row(i):
        return (i, 0)

    def const(i):
        return (0, 0)

    return pl.pallas_call(
        _outproj_kernel,
        grid=(nt,),
        in_specs=[
            pl.BlockSpec((OUT_TILE, D_MODEL), row),
            pl.BlockSpec((OUT_TILE, GLA_VW), row),
            pl.BlockSpec((OUT_TILE, MLA_OUT), row),
            pl.BlockSpec((GLA_VW, D_MODEL), const),
            pl.BlockSpec((MLA_OUT, D_MODEL), const),
            pl.BlockSpec((1, D_MODEL), const),
            pl.BlockSpec((LANE, D_MODEL), const),
            pl.BlockSpec((LANE, 1), const),
        ],
        out_specs=[
            pl.BlockSpec((OUT_TILE, ROW_W), row),
            pl.BlockSpec((OUT_TILE // ROUTE_ROWS, 1, BUCKET_LANES), lambda i: (i, 0, 0)),
            pl.BlockSpec((OUT_TILE // ROUTE_ROWS, SUBLANES, ROUTE_ROWS), lambda i: (i, 0, 0)),
        ],
        out_shape=[
            jax.ShapeDtypeStruct((t, ROW_W), f32),
            jax.ShapeDtypeStruct((nt * (OUT_TILE // ROUTE_ROWS), 1, BUCKET_LANES), f32),
            jax.ShapeDtypeStruct((nt * (OUT_TILE // ROUTE_ROWS), SUBLANES, ROUTE_ROWS), f32),
        ],
        compiler_params=pltpu.CompilerParams(
            dimension_semantics=("parallel",), vmem_limit_bytes=VMEM_LIMIT),
        name="outproj",
    )(x2d, yg, ym, wo_g, wo_m, gain, w_r, rbias)


def _scatter_kernel(pos_ref, zb_ref, ux_ref, hs_ref, zbuf, sem, zsem):
    @pl.when(pl.program_id(0) == 0)
    def _():
        zbuf[...] = jnp.zeros_like(zbuf)

        def zero_copy(j):
            rows = pl.ds(pl.multiple_of(zb_ref[j] * MOE_BLOCK, MOE_BLOCK), MOE_BLOCK)
            return pltpu.make_async_copy(zbuf, hs_ref.at[rows], zsem)

        def zstart(j, c):
            @pl.when(zb_ref[j] >= 0)
            def _():
                zero_copy(j).start()
            return c

        def zwait(j, c):
            @pl.when(zb_ref[j] >= 0)
            def _():
                zero_copy(j).wait()
            return c

        lax.fori_loop(0, zb_ref.shape[0], zstart, 0)
        lax.fori_loop(0, zb_ref.shape[0], zwait, 0)

    def start(io, c):
        for r in range(ISSUE_UNROLL):
            ii = io * (ISSUE_UNROLL // SUBLANES) + r // SUBLANES
            pltpu.make_async_copy(ux_ref.at[ii, pl.ds(r % SUBLANES, 1)],
                                  hs_ref.at[pl.ds(pos_ref[io * ISSUE_UNROLL + r], 1)], sem).start(priority=r % 2)
        return c

    lax.fori_loop(0, SCATTER_TILE // ISSUE_UNROLL, start, 0)
    pltpu.make_async_copy(hs_ref.at[pl.ds(0, SCATTER_TILE)], hs_ref.at[pl.ds(0, SCATTER_TILE)], sem).wait()


def _scatter_call(pos, zero_blocks, ux, n_slots):
    t = ux.shape[0]
    nz = zero_blocks.shape[0]
    return pl.pallas_call(
        _scatter_kernel,
        grid=(t // SCATTER_TILE,),
        in_specs=[
            pl.BlockSpec((SCATTER_TILE,), lambda i: (i,), memory_space=pltpu.SMEM),
            pl.BlockSpec((nz,), lambda i: (0,), memory_space=pltpu.SMEM),
            pl.BlockSpec((SCATTER_TILE // SUBLANES, SUBLANES, ROW_W), lambda i: (i, 0, 0)),
        ],
        out_specs=pl.BlockSpec(memory_space=pl.ANY),
        out_shape=jax.ShapeDtypeStruct((n_slots, ROW_W), f32),
        scratch_shapes=[pltpu.VMEM((MOE_BLOCK, ROW_W), f32), pltpu.SemaphoreType.DMA(()),
                        pltpu.SemaphoreType.DMA(())],
        compiler_params=pltpu.CompilerParams(
            dimension_semantics=("arbitrary",), vmem_limit_bytes=VMEM_LIMIT),
        name="scatter",
    )(pos, zero_blocks, ux.reshape(t // SUBLANES, SUBLANES, ROW_W))


def _moe_kernel(se_ref, sf_ref, sk_ref, sp_ref, sn_ref, sb_ref, sr_ref, si_ref, so_ref,
                hs_hbm, wg_hbm, wu_hbm, wd_hbm, fg_ref, y_hbm,
                xbuf, obuf, wg_buf, wu_buf, wd_buf, wgu_s, wd_s, in_sem, out_sem, w_sem):
    s = pl.program_id(0)
    ns = pl.num_programs(0)
    cur = s % 2
    g_n = MOE_GROUP

    def in_copy(step, g, buf):
        rows = pl.ds(pl.multiple_of(sb_ref[step * g_n + g] * MOE_BLOCK, MOE_BLOCK), MOE_BLOCK)
        return pltpu.make_async_copy(
            hs_hbm.at[rows], xbuf.at[buf, pl.ds(g * MOE_BLOCK, MOE_BLOCK)], in_sem.at[buf])

    def out_copy(step, g, buf):
        rows = pl.ds(pl.multiple_of(sb_ref[step * g_n + g] * MOE_BLOCK, MOE_BLOCK), MOE_BLOCK)
        cols = pl.ds(pl.multiple_of(sr_ref[step * g_n + g] * D_MODEL, D_MODEL), D_MODEL)
        return pltpu.make_async_copy(
            obuf.at[buf, pl.ds(g * MOE_BLOCK, MOE_BLOCK)], y_hbm.at[rows, cols], out_sem.at[buf])

    def for_slots(step, flags_ref, fn):
        for g in range(g_n):
            @pl.when(flags_ref[step * g_n + g] == 1)
            def _():
                fn(g)

    @pl.when(s == 0)
    def _():
        xbuf[...] = jnp.zeros_like(xbuf)
        for_slots(0, si_ref, lambda g: in_copy(0, g, 0).start())

    @pl.when(s + 1 < ns)
    def _():
        for_slots(s + 1, si_ref, lambda g: in_copy(s + 1, g, 1 - cur).start())

    for_slots(s, si_ref, lambda g: in_copy(s, g, cur).wait())

    @pl.when(s >= 2)
    def _():
        for_slots(s - 2, so_ref, lambda g: out_copy(s - 2, g, cur).wait())

    def w_copies(expert, slot):
        return (pltpu.make_async_copy(wg_hbm.at[expert], wg_buf.at[slot], w_sem.at[slot]),
                pltpu.make_async_copy(wu_hbm.at[expert], wu_buf.at[slot], w_sem.at[slot]),
                pltpu.make_async_copy(wd_hbm.at[expert], wd_buf.at[slot], w_sem.at[slot]))

    @pl.when(s == 0)
    def _():
        for c in w_copies(se_ref[0], sp_ref[0]):
            c.start()

    @pl.when(sf_ref[s] == 1)
    def _():
        slot = sp_ref[s]
        for c in w_copies(se_ref[s], slot):
            c.wait()
        wgu_s[:, 0:D_EXPERT] = wg_buf[slot].astype(bf16)
        wgu_s[:, D_EXPERT:2 * D_EXPERT] = wu_buf[slot].astype(bf16)
        wd_s[...] = wd_buf[slot].astype(bf16)

        @pl.when(sn_ref[s] >= 0)
        def _():
            for c in w_copies(sn_ref[s], 1 - slot):
                c.start()

    @pl.when(sk_ref[s] == 1)
    def _():
        h1 = xbuf[cur, :, 0:D_MODEL]
        meta = xbuf[cur, :, D_MODEL:ROW_W]
        u = _rms(h1, fg_ref[...]).astype(bf16)
        ones = jnp.ones((MOE_BLOCK, 1), f32)
        role0 = [sr_ref[s * g_n + g] == 0 for g in range(g_n)]
        gate = jnp.concatenate(
            [jnp.where(role0[g], meta[g * MOE_BLOCK:(g + 1) * MOE_BLOCK, 2:3],
                       meta[g * MOE_BLOCK:(g + 1) * MOE_BLOCK, 3:4]) for g in range(g_n)], axis=0)
        keep = jnp.concatenate([jnp.where(role0[g], ones, 0.0) for g in range(g_n)], axis=0)
        gu = _dot(u, wgu_s[...])
        gt = gu[:, 0:D_EXPERT]
        hdn = (gt * jax.nn.sigmoid(gt) * gu[:, D_EXPERT:]).astype(bf16)
        obuf[cur] = _dot(hdn, wd_s[...]) * gate + h1 * keep

    @pl.when(sk_ref[s] == 0)
    def _():
        obuf[cur] = jnp.zeros(obuf.shape[1:], f32)

    for_slots(s, so_ref, lambda g: out_copy(s, g, cur).start())

    @pl.when(s == ns - 1)
    def _():
        for_slots(s, so_ref, lambda g: out_copy(s, g, cur).wait())

        @pl.when(s >= 1)
        def _():
            for_slots(s - 1, so_ref, lambda g: out_copy(s - 1, g, 1 - cur).wait())


def _moe_call(plan, hs, w_gate, w_up, w_down, ffn_gain):
    n_steps = plan[0].shape[0]
    n_slots = hs.shape[0]
    rows = MOE_GROUP * MOE_BLOCK

    grid_spec = pltpu.PrefetchScalarGridSpec(
        num_scalar_prefetch=9,
        grid=(n_steps,),
        in_specs=[
            pl.BlockSpec(memory_space=pl.ANY),
            pl.BlockSpec(memory_space=pl.ANY),
            pl.BlockSpec(memory_space=pl.ANY),
            pl.BlockSpec(memory_space=pl.ANY),
            pl.BlockSpec((1, D_MODEL), lambda s, *_: (0, 0)),
        ],
        out_specs=pl.BlockSpec(memory_space=pl.ANY),
        scratch_shapes=[
            pltpu.VMEM((2, rows, ROW_W), f32),
            pltpu.VMEM((2, rows, D_MODEL), f32),
            pltpu.VMEM((2, D_MODEL, D_EXPERT), f32),
            pltpu.VMEM((2, D_MODEL, D_EXPERT), f32),
            pltpu.VMEM((2, D_EXPERT, D_MODEL), f32),
            pltpu.VMEM((D_MODEL, 2 * D_EXPERT), bf16),
            pltpu.VMEM((D_EXPERT, D_MODEL), bf16),
            pltpu.SemaphoreType.DMA((2,)),
            pltpu.SemaphoreType.DMA((2,)),
            pltpu.SemaphoreType.DMA((2,)),
        ],
    )
    return pl.pallas_call(
        _moe_kernel,
        grid_spec=grid_spec,
        out_shape=jax.ShapeDtypeStruct((n_slots, 2 * D_MODEL), f32),
        compiler_params=pltpu.CompilerParams(
            dimension_semantics=("arbitrary",), vmem_limit_bytes=VMEM_LIMIT),
        name="moe",
    )(*plan, hs, w_gate, w_up, w_down, ffn_gain)


def _final_kernel(posc_ref, posn_ref, gain_ref, y_hbm, o_ref, ybuf, sem):
    i = pl.program_id(0)
    cur = i % 2

    def issue(pos_ref, buf):
        def start(io, c):
            for r in range(ISSUE_UNROLL):
                ii = io * (ISSUE_UNROLL // SUBLANES) + r // SUBLANES
                pltpu.make_async_copy(y_hbm.at[pl.ds(pos_ref[io * ISSUE_UNROLL + r], 1)],
                                      ybuf.at[buf, ii, pl.ds(r % SUBLANES, 1)], sem.at[buf]).start(priority=r % 2)
            return c

        lax.fori_loop(0, FINAL_TILE // ISSUE_UNROLL, start, 0)

    @pl.when(i == 0)
    def _():
        issue(posc_ref, 0)

    @pl.when(i + 1 < pl.num_programs(0))
    def _():
        issue(posn_ref, 1 - cur)

    pltpu.make_async_copy(ybuf.at[cur], ybuf.at[cur], sem.at[cur]).wait()
    h = ybuf[cur, :, :, 0:D_MODEL] + ybuf[cur, :, :, D_MODEL:2 * D_MODEL]
    o_ref[...] = _rms(h, gain_ref[...])


def _final_call(pos, gain, y):
    t = pos.shape[0]
    n = t // FINAL_TILE
    rows = FINAL_TILE // SUBLANES
    out = pl.pallas_call(
        _final_kernel,
        grid=(n,),
        in_specs=[
            pl.BlockSpec((FINAL_TILE,), lambda i: (i,), memory_space=pltpu.SMEM),
            pl.BlockSpec((FINAL_TILE,), lambda i: (jnp.minimum(i + 1, n - 1),), memory_space=pltpu.SMEM),
            pl.BlockSpec((1, 1, D_MODEL), lambda i: (0, 0, 0)),
            pl.BlockSpec(memory_space=pl.ANY),
        ],
        out_specs=pl.BlockSpec((rows, SUBLANES, D_MODEL), lambda i: (i, 0, 0)),
        out_shape=jax.ShapeDtypeStruct((t // SUBLANES, SUBLANES, D_MODEL), f32),
        scratch_shapes=[pltpu.VMEM((2, rows, SUBLANES, 2 * D_MODEL), f32), pltpu.SemaphoreType.DMA((2,))],
        compiler_params=pltpu.CompilerParams(
            dimension_semantics=("arbitrary",), vmem_limit_bytes=VMEM_LIMIT),
        name="final",
    )(pos, pos, gain.reshape(1, 1, D_MODEL), y)
    return out.reshape(t, D_MODEL)


def _rope_tables(pos):
    inv = ROPE_BASE ** (-jnp.arange(0, MLA_ROPE, 2, dtype=f32) / MLA_ROPE)
    ang = pos.astype(f32)[:, None] * inv[None, :]
    cos, sin = jnp.cos(ang), jnp.sin(ang)
    z = jnp.zeros((pos.shape[0], LANE - MLA_ROPE), f32)
    return jnp.concatenate([cos, cos, z], axis=1), jnp.concatenate([-sin, sin, z], axis=1)


def _relayout_weights(w_in, w_qb, w_kvb):
    half = MLA_ROPE // 2
    perm = (np.arange(MLA_ROPE) + half) % MLA_ROPE
    pts = np.cumsum((GLA_QK, GLA_QK, GLA_VW, GLA_VW, GLA_GATE_RANK, MLA_Q_RANK, MLA_KV_RANK, MLA_ROPE))
    q_g, k_g, v_g, r_g, a_l, q_lat, kv_lat, k_rope = jnp.split(w_in, pts[:-1], axis=1)
    a_seg = jnp.pad(a_l, ((0, 0), (0, LANE - GLA_GATE_RANK)))
    w_in_r = jnp.concatenate(
        [q_g, k_g, v_g, r_g, q_lat, kv_lat, k_rope, k_rope[:, perm], a_seg], axis=1).astype(bf16)
    qcols, kcols, vcols = [], [], []
    for h in range(MLA_HEADS):
        c = h * (MLA_NOPE + MLA_ROPE)
        rope = w_qb[:, c + MLA_NOPE:c + MLA_NOPE + MLA_ROPE]
        qcols += [w_qb[:, c:c + MLA_NOPE], rope, rope[:, perm]]
        c2 = h * (MLA_NOPE + MLA_V)
        kcols.append(w_kvb[:, c2:c2 + MLA_NOPE])
        vcols.append(w_kvb[:, c2 + MLA_NOPE:c2 + MLA_NOPE + MLA_V])
    return w_in_r, jnp.concatenate(qcols, axis=1).astype(bf16), jnp.concatenate(kcols + vcols, axis=1).astype(bf16)


_BUCKET_GROUP = np.arange(N_BUCKETS) // N_PAIRS
_RUN_EXPERT = np.concatenate([_BUCKET_GROUP * EXPERTS_PER_GROUP + _PAIR_LO[np.arange(N_BUCKETS) % N_PAIRS],
                              _BUCKET_GROUP * EXPERTS_PER_GROUP + _PAIR_HI[np.arange(N_BUCKETS) % N_PAIRS]])
_RUN_IS_EXPERT = (_RUN_EXPERT[:, None] == np.arange(N_EXPERTS)[None, :]).astype(np.int32)
_RUN_BEFORE = ((_RUN_EXPERT[:, None] == _RUN_EXPERT[None, :])
               & (np.arange(2 * N_BUCKETS)[None, :] < np.arange(2 * N_BUCKETS)[:, None])).astype(np.int32)


def _route_plan(counts, bucket, rank, n_tok):
    nt = counts.shape[0]
    g_n = MOE_GROUP
    tot = counts.sum(axis=0)
    nblk = (tot + MOE_BLOCK - 1) // MOE_BLOCK
    bstart_blk = jnp.cumsum(nblk) - nblk
    n_blocks = jnp.sum(nblk)
    tile_base = bstart_blk[None, :] * MOE_BLOCK + jnp.cumsum(counts, axis=0) - counts
    hit = bucket.reshape(nt, -1, 1) == jnp.arange(N_BUCKETS, dtype=i32)
    pos = jnp.sum(jnp.where(hit, tile_base[:, None, :], 0), axis=-1).reshape(-1) + rank
    nb_max = (n_tok + N_BUCKETS * (MOE_BLOCK - 1)) // MOE_BLOCK

    n_run = jnp.concatenate([nblk, nblk])
    b0_run = jnp.concatenate([bstart_blk, bstart_blk])
    c_e = jnp.sum(n_run[:, None] * _RUN_IS_EXPERT, axis=0)
    g_e = (c_e + g_n - 1) // g_n
    gend = jnp.cumsum(g_e)
    gstart = gend - g_e
    n_compute = gend[-1]
    off_run = jnp.sum(_RUN_BEFORE * n_run[None, :], axis=1)
    f_run = jnp.sum(_RUN_IS_EXPERT * gstart[None, :], axis=1) * g_n + off_run

    n_steps = (2 * nb_max + N_EXPERTS * (g_n - 1) + g_n - 1) // g_n + 1
    f = jnp.arange(n_steps * g_n, dtype=i32)
    in_run = (f[:, None] >= f_run[None, :]) & (f[:, None] < (f_run + n_run)[None, :])
    valid_c = jnp.any(in_run, axis=1)
    block_c = jnp.sum(jnp.where(in_run, b0_run[None, :] + f[:, None] - f_run[None, :], 0), axis=1)
    role_c = jnp.sum(jnp.where(in_run[:, N_BUCKETS:], 1, 0), axis=1)
    u_idx = f - n_compute * g_n
    valid_f = (u_idx >= 0) & (u_idx < 2 * (nb_max - n_blocks))
    slot_block = jnp.where(valid_c, block_c, jnp.where(valid_f, n_blocks + u_idx // 2, 0))
    slot_role = jnp.where(valid_c, role_c, jnp.where(valid_f, u_idx % 2, 0))

    step = jnp.arange(n_steps, dtype=i32)
    e_of_step = jnp.minimum(jnp.sum(gend[None, :] <= step[:, None], axis=1), N_EXPERTS - 1)
    is_compute = step < n_compute
    last_e = jnp.max(jnp.where(is_compute, e_of_step, 0))
    step_expert = jnp.where(is_compute, e_of_step, last_e)
    step_first = jnp.concatenate([jnp.ones((1,), bool), step_expert[1:] != step_expert[:-1]])
    ordinal = jnp.cumsum(step_first.astype(i32)) - 1
    ords = jnp.arange(N_EXPERTS + 1, dtype=i32)
    expert_of_ord = jnp.sum(jnp.where(step_first[:, None] & (ordinal[:, None] == ords[None, :]),
                                      step_expert[:, None], 0), axis=0)
    has_next = ordinal + 1 <= ordinal[-1]
    next_expert = jnp.sum(jnp.where(ords[None, :] == ordinal[:, None] + 1, expert_of_ord[None, :], 0), axis=1)
    step_next = jnp.where(step_first & has_next, next_expert, -1)
    plan = tuple(a.astype(i32) for a in
                 (step_expert, step_first, is_compute, ordinal % 2, step_next,
                  slot_block, slot_role, valid_c, valid_c | valid_f))
    last_blk = jnp.where(nblk > 0, bstart_blk + nblk - 1, -1)
    spare = n_blocks + jnp.arange(nb_max - n_tok // MOE_BLOCK, dtype=i32)
    zero_blocks = jnp.concatenate([last_blk, jnp.where(spare < nb_max, spare, -1)]).astype(i32)
    return pos.astype(i32), plan, zero_blocks, nb_max


def kernel(x, meta_tokens, mix_norm, w_in, gla_w_a2, gla_b_a, gla_out_norm, mla_q_norm, mla_w_qb, mla_kv_norm,
           mla_w_kvb, w_out, ffn_norm, router_group_w, router_group_b, router_expert_w, router_expert_b,
           expert_w_gate, expert_w_up, expert_w_down, final_norm):
    batch, seq, d = x.shape
    assert PREP_TILE == ATT_TILE
    assert d == D_MODEL and seq % max(PREP_TILE, GLA_TILE, ATT_TILE) == 0
    assert (batch * seq) % max(OUT_TILE, SCATTER_TILE, FINAL_TILE) == 0 and batch % GLA_BATCH == 0
    n_tok = batch * seq
    x2d = x.reshape(n_tok, d)

    w_in_r, w_qb_r, w_kvb_r = _relayout_weights(w_in[0], mla_w_qb[0], mla_w_kvb[0])
    mixg = mix_norm[0].reshape(1, d)
    qn = mla_q_norm[0].reshape(1, MLA_Q_RANK)
    kvn = mla_kv_norm[0].reshape(1, MLA_KV_RANK)
    ct_m, st_m = _rope_tables(jnp.arange(META_TILE))
    ct_x, st_x = _rope_tables(N_META + jnp.arange(seq))

    x_meta = jnp.pad(meta_tokens.astype(f32), ((0, META_TILE - N_META), (0, 0)))
    _, kg_m, vg_m, _, a_m, _, km_m, vmt_m = _prep_call(
        x_meta, META_TILE, META_TILE, mixg, w_in_r, qn, w_qb_r, kvn, w_kvb_r, ct_m, st_m)
    qg, kg, vg, rg, ag, qm, km, vmt = _prep_call(
        x2d, seq, PREP_TILE, mixg, w_in_r, qn, w_qb_r, kvn, w_kvb_r, ct_x, st_x)

    def chunk0(a):
        return jnp.pad(a[:N_META], ((CHUNK - N_META, 0), (0, 0)))

    wa2_p = jnp.pad(gla_w_a2[0], ((0, LANE - GLA_GATE_RANK), (0, 0))).astype(bf16)
    y_gla = _gla_call(qg, kg, vg, rg, ag, chunk0(kg_m), chunk0(vg_m), chunk0(a_m),
                      wa2_p, gla_b_a[0].reshape(1, GLA_QK), gla_out_norm[0].reshape(1, GLA_VW), batch, seq)
    y_mla = _mla_call(qm, km, vmt, km_m[:N_META], vmt_m[0, :, :N_META], batch, seq)

    wo = w_out[0].astype(bf16)
    rw = jnp.concatenate([router_group_w[0], router_expert_w[0],
                          jnp.zeros((d, LANE - N_GROUPS - N_EXPERTS), f32)], axis=1)
    rb = jnp.concatenate([router_group_b[0], router_expert_b[0],
                          jnp.zeros((LANE - N_GROUPS - N_EXPERTS,), f32)]).reshape(1, LANE)
    ffn_gain = ffn_norm[0].reshape(1, d)
    ux, cnt, routes = _outproj_call(x2d, y_gla, y_mla, wo[:GLA_VW], wo[GLA_VW:], ffn_gain,
                                      rw.T.astype(bf16), rb.reshape(LANE, 1))

    counts = cnt.reshape(-1, BUCKET_LANES)[:, :N_BUCKETS].astype(i32)
    tok_bucket = routes[:, 0, :].reshape(-1).astype(i32)
    tok_rank = routes[:, 1, :].reshape(-1).astype(i32)
    pos, plan, zero_blocks, nb_max = _route_plan(counts, tok_bucket, tok_rank, n_tok)
    n_slots = nb_max * MOE_BLOCK
    hs = _scatter_call(pos, zero_blocks, ux, n_slots)
    y = _moe_call(plan, hs, expert_w_gate[0], expert_w_up[0], expert_w_down[0], ffn_gain)
    out = _final_call(pos, final_norm.reshape(1, d), y)
    return out.reshape(batch, seq, d)
```
